```python
import jax, jax.numpy as jnp
from jax import lax
import numpy as np

D_MODEL = 2048
BATCH = 2
SEQ = 4096
DEPTH = 1

D_MIX = D_MODEL
RET_HEADS = 8
RET_DK = 128
RET_DV = 128
RET_QK_WIDTH = RET_HEADS * RET_DK
RET_WIDTH = RET_HEADS * RET_DV
RET_CHUNK = 128
SWA_HEADS = 16
SWA_KV_HEADS = 2
SWA_HEAD_DIM = 64
SWA_WIDTH = SWA_HEADS * SWA_HEAD_DIM
SWA_KV_WIDTH = SWA_KV_HEADS * SWA_HEAD_DIM
WINDOW = 128
MEM_LEN = 256
XA_HEADS = 4
XA_HEAD_DIM = D_MODEL // XA_HEADS
D_FF = 5632
ROPE_THETA = 10000.0
EPS = 1e-6

IN_SIZES = [RET_QK_WIDTH, RET_QK_WIDTH, RET_WIDTH, RET_WIDTH, SWA_WIDTH, SWA_KV_WIDTH, SWA_KV_WIDTH]
IN_COLS = sum(IN_SIZES)
IN_SPLITS = [int(v) for v in np.cumsum(IN_SIZES)[:-1]]

kernel_name = "hymba_retention_swa_macaron_block"


def rms_norm(x, g):
    xf = x.astype(jnp.float32)
    y = xf * lax.rsqrt(jnp.mean(xf * xf, axis=-1, keepdims=True) + EPS)
    return (y * g.astype(jnp.float32)).astype(x.dtype)


def rope(x, pos):
    d = x.shape[-1]
    inv_freq = ROPE_THETA ** (-jnp.arange(0, d, 2, dtype=jnp.float32) / d)
    ang = pos[:, None] * inv_freq[None, :]
    cos = jnp.cos(ang)[None, :, None, :]
    sin = jnp.sin(ang)[None, :, None, :]
    xf = x.astype(jnp.float32)
    x1, x2 = xf[..., : d // 2], xf[..., d // 2:]
    out = jnp.concatenate([x1 * cos - x2 * sin, x2 * cos + x1 * sin], axis=-1)
    return out.astype(x.dtype)


def swiglu(x, w_gate, w_up, w_down):
    return (jax.nn.silu(x @ w_gate) * (x @ w_up)) @ w_down


def retention(q, k, v):
    B, S, H, dk = q.shape
    dv = v.shape[-1]
    C = RET_CHUNK
    N = S // C
    f32 = jnp.float32
    log_gamma = jnp.log1p(-jnp.exp2(-5.0 - jnp.arange(H, dtype=f32)))
    qc = q.astype(f32).reshape(B, N, C, H, dk)
    kc = (k.astype(f32) * (dk ** -0.5)).reshape(B, N, C, H, dk)
    vc = v.astype(f32).reshape(B, N, C, H, dv)
    idx = jnp.arange(C, dtype=f32)
    diff = idx[:, None] - idx[None, :]
    dmat = jnp.where(diff[None] >= 0,
                     jnp.exp(jnp.maximum(diff, 0.0)[None] * log_gamma[:, None, None]),
                     0.0)
    s = jnp.einsum('bnchd,bnjhd->bnhcj', qc, kc) * dmat[None, None]
    intra = jnp.einsum('bnhcj,bnjhe->bnche', s, vc)
    zeta = jnp.exp((C - 1.0 - idx)[None, :] * log_gamma[:, None])
    kv = jnp.einsum('bnjhd,hj,bnjhe->nbhde', kc, zeta, vc)
    chunk_decay = jnp.exp(C * log_gamma)[None, :, None, None]

    def step(state, kv_n):
        return state * chunk_decay + kv_n, state

    _, prev_states = lax.scan(step, jnp.zeros((B, H, dk, dv), f32), kv)
    xi = jnp.exp((idx + 1.0)[None, :] * log_gamma[:, None])
    cross = jnp.einsum('bnchd,hc,nbhde->bnche', qc, xi, prev_states)
    return (intra + cross).reshape(B, S, H, dv)


def sliding_window_attention(q, k, v, sinks):
    B, S, Hq, d = q.shape
    Hkv = k.shape[2]
    G = Hq // Hkv
    W = WINDOW
    N = S // W
    qb = q.reshape(B, N, W, Hkv, G, d)
    kb = k.reshape(B, N, W, Hkv, d)
    vb = v.reshape(B, N, W, Hkv, d)
    kk = jnp.concatenate([jnp.concatenate([jnp.zeros_like(kb[:, :1]), kb[:, :-1]], axis=1), kb], axis=2)
    vv = jnp.concatenate([jnp.concatenate([jnp.zeros_like(vb[:, :1]), vb[:, :-1]], axis=1), vb], axis=2)
    s = jnp.einsum('bnqhgd,bnkhd->bnhgqk', qb, kk).astype(jnp.float32) * (d ** -0.5)
    qi = jnp.arange(W)[:, None] + W
    ki = jnp.arange(2 * W)[None, :]
    rel = qi - ki
    band = (rel >= 0) & (rel < W)
    valid = band[None] & ((jnp.arange(N)[:, None, None] > 0) | (ki[None] >= W))
    s = jnp.where(valid[None, :, None, None], s, jnp.finfo(jnp.float32).min)
    sink = sinks.astype(jnp.float32).reshape(Hkv, G)[None, None, :, :, None, None]
    m = jnp.maximum(jnp.max(s, axis=-1, keepdims=True), sink)
    p = jnp.exp(s - m)
    probs = p / (jnp.sum(p, axis=-1, keepdims=True) + jnp.exp(sink - m))
    o = jnp.einsum('bnhgqk,bnkhd->bnqhgd', probs.astype(v.dtype), vv)
    return o.reshape(B, S, Hq * d)


def memory_cross_attention(hn, memn, wq, wkv, wo):
    B, S, D = hn.shape
    M = memn.shape[1]
    q = (hn @ wq).reshape(B, S, XA_HEADS, XA_HEAD_DIM)
    k, v = jnp.split(memn @ wkv, 2, axis=-1)
    k = k.reshape(B, M, XA_HEADS, XA_HEAD_DIM)
    v = v.reshape(B, M, XA_HEADS, XA_HEAD_DIM)
    s = jnp.einsum('bshd,bmhd->bhsm', q, k).astype(jnp.float32) * (XA_HEAD_DIM ** -0.5)
    p = jax.nn.softmax(s, axis=-1)
    o = jnp.einsum('bhsm,bmhd->bshd', p.astype(v.dtype), v).reshape(B, S, D)
    return o @ wo


def setup_inputs(seed: int = 0) -> dict:
    key = jax.random.key(seed)
    ks = jax.random.split(key, 24)
    f32 = jnp.float32
    L, D = DEPTH, D_MODEL

    def w(k, shape, fan_in):
        return jax.random.normal(k, shape, f32) * (fan_in ** -0.5)

    def gain(k, shape):
        return 1.0 + 0.02 * jax.random.normal(k, shape, f32)

    return {
        "x": jax.random.normal(ks[0], (BATCH, SEQ, D), f32),
        "mem": jax.random.normal(ks[1], (BATCH, MEM_LEN, D), f32),
        "ffn1_norm": gain(ks[2], (L, D)),
        "ffn1_w_gate": w(ks[3], (L, D, D_FF), D),
        "ffn1_w_up": w(ks[4], (L, D, D_FF), D),
        "ffn1_w_down": w(ks[5], (L, D_FF, D), D_FF),
        "mix_norm": gain(ks[6], (L, D)),
        "w_in": w(ks[7], (L, D, IN_COLS), D),
        "ret_gn_gain": gain(ks[8], (L, RET_WIDTH)),
        "swa_sinks": 0.5 * jax.random.normal(ks[9], (L, SWA_HEADS), f32),
        "w_out": w(ks[10], (L, D_MIX, D), D_MIX),
        "xa_norm": gain(ks[11], (L, D)),
        "mem_norm": gain(ks[12], (L, D)),
        "xa_wq": w(ks[13], (L, D, D), D),
        "xa_wkv": w(ks[14], (L, D, 2 * D), D),
        "xa_wo": w(ks[15], (L, D, D), D),
        "ffn2_norm": gain(ks[16], (L, D)),
        "ffn2_w_gate": w(ks[17], (L, D, D_FF), D),
        "ffn2_w_up": w(ks[18], (L, D, D_FF), D),
        "ffn2_w_down": w(ks[19], (L, D_FF, D), D_FF),
        "final_norm": gain(ks[20], (D,)),
    }


def reference(x, mem, ffn1_norm, ffn1_w_gate, ffn1_w_up, ffn1_w_down, mix_norm, w_in, ret_gn_gain,
              swa_sinks, w_out, xa_norm, mem_norm, xa_wq, xa_wkv, xa_wo, ffn2_norm, ffn2_w_gate,
              ffn2_w_up, ffn2_w_down, final_norm):
    B, S, _ = x.shape
    pos = jnp.arange(S, dtype=jnp.float32)
    h = x
    for l in range(DEPTH):
        h = h + 0.5 * swiglu(rms_norm(h, ffn1_norm[l]), ffn1_w_gate[l], ffn1_w_up[l], ffn1_w_down[l])

        n = rms_norm(h, mix_norm[l])
        rq, rk, rv, rg, sq, sk, sv = jnp.split(n @ w_in[l], IN_SPLITS, axis=-1)

        rq = rope(rq.reshape(B, S, RET_HEADS, RET_DK), pos)
        rk = rope(rk.reshape(B, S, RET_HEADS, RET_DK), pos)
        ret = retention(rq, rk, rv.reshape(B, S, RET_HEADS, RET_DV))
        mu = jnp.mean(ret, axis=-1, keepdims=True)
        var = jnp.mean(jnp.square(ret - mu), axis=-1, keepdims=True)
        ret = (ret - mu) * lax.rsqrt(var + EPS) * ret_gn_gain[l].astype(jnp.float32).reshape(RET_HEADS, RET_DV)
        ret = (jax.nn.silu(rg.astype(jnp.float32)) * ret.reshape(B, S, RET_WIDTH)).astype(h.dtype)

        sq = rope(sq.reshape(B, S, SWA_HEADS, SWA_HEAD_DIM), pos)
        sk = rope(sk.reshape(B, S, SWA_KV_HEADS, SWA_HEAD_DIM), pos)
        swa = sliding_window_attention(sq, sk, sv.reshape(B, S, SWA_KV_HEADS, SWA_HEAD_DIM), swa_sinks[l])

        h = h + jnp.concatenate([ret, swa.astype(h.dtype)], axis=-1) @ w_out[l]

        h = h + memory_cross_attention(rms_norm(h, xa_norm[l]), rms_norm(mem, mem_norm[l]),
                                       xa_wq[l], xa_wkv[l], xa_wo[l])

        h = h + 0.5 * swiglu(rms_norm(h, ffn2_norm[l]), ffn2_w_gate[l], ffn2_w_up[l], ffn2_w_down[l])
    return rms_norm(h, final_norm)
```

```python
import functools

import numpy as np
import jax
import jax.numpy as jnp
from jax import lax
from jax.experimental import pallas as pl
from jax.experimental.pallas import tpu as pltpu

F32 = jnp.float32
BF16 = jnp.bfloat16

D_MODEL = 2048
D_FF = 5632
RET_HEADS = 8
RET_DK = 128
RET_DV = 128
RET_WIDTH = RET_HEADS * RET_DV
RET_CHUNK = 128
SWA_HEADS = 16
SWA_KV_HEADS = 2
SWA_HEAD_DIM = 64
SWA_WIDTH = SWA_HEADS * SWA_HEAD_DIM
SWA_KV_WIDTH = SWA_KV_HEADS * SWA_HEAD_DIM
WINDOW = 128
XA_HEADS = 4
XA_HEAD_DIM = D_MODEL // XA_HEADS
ROPE_THETA = 10000.0
EPS = 1e-6
IN_SIZES = (RET_WIDTH, RET_WIDTH, RET_WIDTH, RET_WIDTH, SWA_WIDTH, SWA_KV_WIDTH, SWA_KV_WIDTH)
IN_COLS = sum(IN_SIZES)

LANES = 128
V7X_VMEM_BYTES = 64 * 1024 * 1024
VMEM_LIMIT_BYTES = 60000 * 1024

FFN_TM = 1024
FFN_TF = 512
ROW_TM = 512
NORM_ROWS = 64
NEG_INF = float(np.finfo(np.float32).min)


def _params(n_axes):
    return pltpu.CompilerParams(
        dimension_semantics=("arbitrary",) * n_axes,
        vmem_limit_bytes=VMEM_LIMIT_BYTES,
    )


def _resident(shape):
    zeros = (0,) * len(shape)
    return pl.BlockSpec(shape, lambda *_: zeros, pipeline_mode=pl.Buffered(1))


def _rms_rows(x, gain):
    ms = jnp.mean(x * x, axis=-1, keepdims=True)
    return x * lax.rsqrt(ms + EPS) * gain


def _dot(a, b):
    return jnp.dot(a, b, preferred_element_type=F32)


def _dot_nt(a, b):
    return lax.dot_general(a, b, (((1,), (1,)), ((), ())), preferred_element_type=F32)


def _dot_tn(a, b):
    return lax.dot_general(a, b, (((0,), (0,)), ((), ())), preferred_element_type=F32)


def _ffn_kernel(h_ref, g_ref, wg_ref, wu_ref, wd_ref, fg_ref, o_ref, xn_ref, *, n_ff, final_norm):
    j = pl.program_id(1)
    tm = h_ref.shape[0]

    @pl.when(j == 0)
    def _():
        def body(r, carry):
            rows = pl.ds(pl.multiple_of(r * NORM_ROWS, NORM_ROWS), NORM_ROWS)
            h = h_ref[rows, :]
            xn_ref[rows, :] = _rms_rows(h, g_ref[...]).astype(BF16)
            o_ref[rows, :] = h
            return carry
        lax.fori_loop(0, tm // NORM_ROWS, body, 0)

    xn = xn_ref[...]
    g = _dot(xn, wg_ref[...])
    u = _dot(xn, wu_ref[...])
    a = (g * (0.5 / (1.0 + jnp.exp(-g))) * u).astype(BF16)
    for c in range(0, o_ref.shape[1], FFN_TF):
        o_ref[:, c:c + FFN_TF] += _dot(a, wd_ref[:, c:c + FFN_TF])

    if final_norm:
        @pl.when(j == n_ff - 1)
        def _():
            def body(r, carry):
                rows = pl.ds(pl.multiple_of(r * NORM_ROWS, NORM_ROWS), NORM_ROWS)
                o_ref[rows, :] = _rms_rows(o_ref[rows, :], fg_ref[...])
                return carry
            lax.fori_loop(0, tm // NORM_ROWS, body, 0)


def _ffn(h, gain, wg, wu, wd, final_gain, *, final_norm):
    t, d = h.shape
    f = wg.shape[1]
    n_ff = f // FFN_TF
    kern = functools.partial(_ffn_kernel, n_ff=n_ff, final_norm=final_norm)
    return pl.pallas_call(
        kern,
        out_shape=jax.ShapeDtypeStruct((t, d), F32),
        grid=(t // FFN_TM, n_ff),
        in_specs=[
            pl.BlockSpec((FFN_TM, d), lambda i, j: (i, 0)),
            pl.BlockSpec((1, d), lambda i, j: (0, 0)),
            pl.BlockSpec((d, FFN_TF), lambda i, j: (0, j)),
            pl.BlockSpec((d, FFN_TF), lambda i, j: (0, j)),
            pl.BlockSpec((FFN_TF, d), lambda i, j: (j, 0)),
            pl.BlockSpec((1, d), lambda i, j: (0, 0)),
        ],
        out_specs=pl.BlockSpec((FFN_TM, d), lambda i, j: (i, 0)),
        scratch_shapes=[pltpu.VMEM((FFN_TM, d), BF16)],
        compiler_params=_params(2),
        name="ffn",
    )(h, gain, wg, wu, wd, final_gain)


def _rope_tables(seq):
    pos = np.arange(seq, dtype=np.float32)

    def angles(d):
        inv = np.float32(ROPE_THETA) ** (-np.arange(0, d, 2, dtype=np.float32) / np.float32(d))
        return (pos[:, None] * inv[None, :].astype(np.float32)).astype(np.float32).astype(np.float64)

    a128 = angles(RET_DK)
    cos_r = np.concatenate([np.cos(a128), np.cos(a128)], -1)
    sin_r = np.concatenate([-np.sin(a128), np.sin(a128)], -1)
    a64 = angles(SWA_HEAD_DIM)
    c, s, z = np.cos(a64), np.sin(a64), np.zeros_like(a64)
    cos_s = np.concatenate([c, c, c, c], -1)
    sin_lo = np.concatenate([-s, z, -s, z], -1)
    sin_hi = np.concatenate([z, s, z, s], -1)
    return [jnp.asarray(v, dtype=F32) for v in (cos_r, sin_r, cos_s, sin_lo, sin_hi)]


def _inproj_kernel(h_ref, g_ref, w_ref, cr_ref, sr_ref, cs_ref, sl_ref, sh_ref,
                   rq_ref, rk_ref, rv_ref, rg_ref, sq_ref, sk_ref, sv_ref, xn_ref):
    tm = h_ref.shape[0]

    def body(r, carry):
        rows = pl.ds(pl.multiple_of(r * NORM_ROWS, NORM_ROWS), NORM_ROWS)
        xn_ref[rows, :] = _rms_rows(h_ref[rows, :], g_ref[...]).astype(BF16)
        return carry
    lax.fori_loop(0, tm // NORM_ROWS, body, 0)

    xn = xn_ref[...]
    cr, sr = cr_ref[...], sr_ref[...]
    cs, sl, sh = cs_ref[...], sl_ref[...], sh_ref[...]
    half = LANES // 2

    def rope_ret(x):
        return x * cr + pltpu.roll(x, half, 1) * sr

    def rope_swa(x):
        return x * cs + pltpu.roll(x, LANES - half // 2, 1) * sl + pltpu.roll(x, half // 2, 1) * sh

    def project(col0, width):
        return _dot(xn, w_ref[:, col0:col0 + width])

    col = 0
    for out_ref, fn in ((rq_ref, rope_ret), (rk_ref, rope_ret), (rv_ref, None), (rg_ref, None)):
        y = project(col, RET_WIDTH)
        for s in range(RET_WIDTH // LANES):
            slab = y[:, s * LANES:(s + 1) * LANES]
            if fn is not None:
                slab = fn(slab)
            out_ref[:, s * LANES:(s + 1) * LANES] = slab.astype(BF16)
        col += RET_WIDTH

    y = project(col, SWA_WIDTH)
    scale = SWA_HEAD_DIM ** -0.5
    for s in range(SWA_WIDTH // LANES):
        slab = rope_swa(y[:, s * LANES:(s + 1) * LANES]) * scale
        sq_ref[:, s * LANES:(s + 1) * LANES] = slab.astype(BF16)
    col += SWA_WIDTH

    y = project(col, 2 * SWA_KV_WIDTH)
    lo = lax.broadcasted_iota(jnp.int32, (tm, LANES), 1) < half
    for out_ref, x in ((sk_ref, rope_swa(y[:, :LANES])), (sv_ref, y[:, LANES:])):
        xr = pltpu.roll(x, half, 1)
        variants = (jnp.where(lo, x, 0.0), jnp.where(lo, 0.0, xr),
                    jnp.where(lo, xr, 0.0), jnp.where(lo, 0.0, x))
        for s, v in enumerate(variants):
            out_ref[:, s * LANES:(s + 1) * LANES] = v.astype(BF16)


def _inproj(h, gain, w_in, seq):
    t, d = h.shape
    tables = _rope_tables(seq)
    tiles_per_seq = seq // ROW_TM
    tab_spec = pl.BlockSpec((ROW_TM, LANES), lambda i: (i % tiles_per_seq, 0))

    def row_spec(width):
        return pl.BlockSpec((ROW_TM, width), lambda i: (i, 0))

    widths = (RET_WIDTH, RET_WIDTH, RET_WIDTH, RET_WIDTH, SWA_WIDTH, 4 * LANES, 4 * LANES)
    return pl.pallas_call(
        _inproj_kernel,
        out_shape=[jax.ShapeDtypeStruct((t, w), BF16) for w in widths],
        grid=(t // ROW_TM,),
        in_specs=[row_spec(d), _resident((1, d)), _resident((d, IN_COLS))] + [tab_spec] * 5,
        out_specs=[row_spec(w) for w in widths],
        scratch_shapes=[pltpu.VMEM((ROW_TM, d), BF16)],
        compiler_params=_params(1),
        name="inproj",
    )(h, gain, w_in, *tables)


def _retention_tables():
    c = RET_CHUNK
    heads = np.arange(RET_HEADS, dtype=np.float64)
    log_gamma = np.log1p(-np.exp2(-5.0 - heads))
    idx = np.arange(c, dtype=np.float64)
    diff = idx[:, None] - idx[None, :]
    scale = RET_DK ** -0.5
    dmat = np.where(diff[None] >= 0, np.exp(np.maximum(diff, 0.0)[None] * log_gamma[:, None, None]), 0.0)
    zeta = np.exp((c - 1.0 - idx)[None, :] * log_gamma[:, None])
    xi = np.exp((idx + 1.0)[None, :] * log_gamma[:, None])
    chunk_decay = tuple(float(v) for v in np.exp(c * log_gamma))
    zeta_b = np.broadcast_to((zeta * scale)[:, :, None], (RET_HEADS, c, RET_DK))
    xi_b = np.broadcast_to(xi[:, :, None], (RET_HEADS, c, RET_DV))
    tabs = [jnp.asarray(v, dtype=F32) for v in (dmat * scale, zeta_b, xi_b)]
    return tabs, chunk_decay


def _ret_kernel(q_ref, k_ref, v_ref, g_ref, dmat_ref, zeta_ref, xi_ref, gain_ref, o_ref, state_ref,
                *, chunk_decay):
    n = pl.program_id(1)

    @pl.when(n == 0)
    def _():
        state_ref[...] = jnp.zeros_like(state_ref)

    for h in range(RET_HEADS):
        hs = slice(h * RET_DK, (h + 1) * RET_DK)
        q, k, v = q_ref[:, hs], k_ref[:, hs], v_ref[:, hs]
        state = state_ref[h]
        s = _dot_nt(q, k) * dmat_ref[h]
        intra = _dot(s.astype(BF16), v)
        cross = _dot(q, state.astype(BF16)) * xi_ref[h]
        kz = (k.astype(F32) * zeta_ref[h]).astype(BF16)
        state_ref[h] = state * chunk_decay[h] + _dot_tn(kz, v)
        ret = intra + cross
        mu = jnp.mean(ret, axis=-1, keepdims=True)
        cen = ret - mu
        var = jnp.mean(cen * cen, axis=-1, keepdims=True)
        y = cen * lax.rsqrt(var + EPS) * gain_ref[:, hs]
        gate = g_ref[:, hs].astype(F32)
        o_ref[:, hs] = (gate * (1.0 / (1.0 + jnp.exp(-gate))) * y).astype(BF16)


def _retention(rq, rk, rv, rg, gn_gain, batch, seq):
    t = rq.shape[0]
    n_chunks = seq // RET_CHUNK
    tabs, chunk_decay = _retention_tables()
    blk = pl.BlockSpec((RET_CHUNK, RET_WIDTH), lambda b, n: (b * n_chunks + n, 0))
    tab_spec = _resident((RET_HEADS, RET_CHUNK, RET_DK))
    return pl.pallas_call(
        functools.partial(_ret_kernel, chunk_decay=chunk_decay),
        out_shape=jax.ShapeDtypeStruct((t, RET_WIDTH), BF16),
        grid=(batch, n_chunks),
        in_specs=[blk, blk, blk, blk, tab_spec, tab_spec, tab_spec, _resident((1, RET_WIDTH))],
        out_specs=blk,
        scratch_shapes=[pltpu.VMEM((RET_HEADS, RET_DK, RET_DV), F32)],
        compiler_params=_params(2),
        name="retention",
    )(rq, rk, rv, rg, *tabs, gn_gain)


def _swa_kernel(sink_ref, q_ref, kp_ref, kc_ref, vp_ref, vc_ref, o_ref):
    n = pl.program_id(1)
    w = WINDOW
    row = lax.broadcasted_iota(jnp.int32, (w, 2 * w), 0)
    col = lax.broadcasted_iota(jnp.int32, (w, 2 * w), 1)
    first_key = jnp.where(n == 0, w, 0)
    valid = (col > row) & (col <= row + w) & (col >= first_key)
    lo = lax.broadcasted_iota(jnp.int32, (w, LANES), 1) < (LANES // 2)
    pairs_per_group = SWA_HEADS // SWA_KV_HEADS // 2

    for g in range(SWA_KV_HEADS):
        ks = [jnp.concatenate([kp_ref[:, c * LANES:(c + 1) * LANES], kc_ref[:, c * LANES:(c + 1) * LANES]], 0)
              for c in (2 * g, 2 * g + 1)]
        vs = [jnp.concatenate([vp_ref[:, c * LANES:(c + 1) * LANES], vc_ref[:, c * LANES:(c + 1) * LANES]], 0)
              for c in (2 * g, 2 * g + 1)]
        for p in range(pairs_per_group):
            slab = g * pairs_per_group + p
            q = q_ref[:, slab * LANES:(slab + 1) * LANES]
            acc = None
            inv = []
            for e in range(2):
                sink = sink_ref[2 * slab + e]
                s = jnp.where(valid, _dot_nt(q, ks[e]), NEG_INF)
                m = jnp.maximum(jnp.max(s, axis=-1, keepdims=True), sink)
                pexp = jnp.exp(s - m)
                denom = jnp.sum(pexp, axis=-1, keepdims=True) + jnp.exp(sink - m)
                inv.append(1.0 / denom)
                pv = _dot(pexp.astype(BF16), vs[e])
                acc = pv if acc is None else acc + pv
            o_ref[:, slab * LANES:(slab + 1) * LANES] = (acc * jnp.where(lo, inv[0], inv[1])).astype(BF16)


def _swa(sq, sk4, sv4, sinks, batch, seq):
    t = sq.shape[0]
    n_blocks = seq // WINDOW
    cur = lambda b, n: (b * n_blocks + n, 0)
    prev = lambda b, n: (b * n_blocks + jnp.maximum(n - 1, 0), 0)
    return pl.pallas_call(
        _swa_kernel,
        out_shape=jax.ShapeDtypeStruct((t, SWA_WIDTH), BF16),
        grid=(batch, n_blocks),
        in_specs=[
            pl.BlockSpec(memory_space=pltpu.SMEM),
            pl.BlockSpec((WINDOW, SWA_WIDTH), cur),
            pl.BlockSpec((WINDOW, 4 * LANES), prev),
            pl.BlockSpec((WINDOW, 4 * LANES), cur),
            pl.BlockSpec((WINDOW, 4 * LANES), prev),
            pl.BlockSpec((WINDOW, 4 * LANES), cur),
        ],
        out_specs=pl.BlockSpec((WINDOW, SWA_WIDTH), cur),
        compiler_params=_params(2),
        name="swa",
    )(sinks, sq, sk4, sk4, sv4, sv4)


def _outproj_kernel(h_ref, ret_ref, swa_ref, w_ref, g_ref, h2_ref, hn_ref):
    tm = h_ref.shape[0]
    y = _dot(ret_ref[...], w_ref[:RET_WIDTH, :]) + _dot(swa_ref[...], w_ref[RET_WIDTH:, :])
    h2_ref[...] = h_ref[...] + y

    def body(r, carry):
        rows = pl.ds(pl.multiple_of(r * NORM_ROWS, NORM_ROWS), NORM_ROWS)
        hn_ref[rows, :] = _rms_rows(h2_ref[rows, :], g_ref[...]).astype(BF16)
        return carry
    lax.fori_loop(0, tm // NORM_ROWS, body, 0)


def _outproj(h, ret, swa, w_out, gain):
    t, d = h.shape
    row = lambda width: pl.BlockSpec((ROW_TM, width), lambda i: (i, 0))
    return pl.pallas_call(
        _outproj_kernel,
        out_shape=[jax.ShapeDtypeStruct((t, d), F32), jax.ShapeDtypeStruct((t, d), BF16)],
        grid=(t // ROW_TM,),
        in_specs=[row(d), row(RET_WIDTH), row(SWA_WIDTH), _resident(w_out.shape), _resident((1, d))],
        out_specs=[row(d), row(d)],
        compiler_params=_params(1),
        name="outproj",
    )(h, ret, swa, w_out, gain)


def _memkv_kernel(m_ref, g_ref, w_ref, o_ref):
    xn = _rms_rows(m_ref[...], g_ref[...]).astype(BF16)
    o_ref[...] = _dot(xn, w_ref[...]).astype(BF16)


def _memkv(mem, gain, wkv):
    rows, d = mem.shape
    n_out = wkv.shape[1]
    tn = 1024
    return pl.pallas_call(
        _memkv_kernel,
        out_shape=jax.ShapeDtypeStruct((rows, n_out), BF16),
        grid=(n_out // tn,),
        in_specs=[_resident((rows, d)), _resident((1, d)), pl.BlockSpec((d, tn), lambda j: (0, j))],
        out_specs=pl.BlockSpec((rows, tn), lambda j: (0, j)),
        compiler_params=_params(1),
        name="memkv",
    )(mem, gain, wkv)


def _xattn_kernel(hn_ref, h_ref, wq_ref, wo_ref, k_ref, v_ref, o_ref, att_ref):
    q = _dot(hn_ref[...], wq_ref[...]).astype(BF16)
    scale = XA_HEAD_DIM ** -0.5
    for hd in range(XA_HEADS):
        hs = slice(hd * XA_HEAD_DIM, (hd + 1) * XA_HEAD_DIM)
        s = _dot_nt(q[:, hs], k_ref[:, hs]) * scale
        m = jnp.max(s, axis=-1, keepdims=True)
        p = jnp.exp(s - m)
        inv = 1.0 / jnp.sum(p, axis=-1, keepdims=True)
        att_ref[:, hs] = (_dot(p.astype(BF16), v_ref[:, hs]) * inv).astype(BF16)
    o_ref[...] = h_ref[...] + _dot(att_ref[...], wo_ref[...])


def _xattn(hn, h, wq, wo, mkv, seq, mem_len):
    t, d = h.shape
    tiles_per_seq = seq // ROW_TM
    row = lambda i: (i, 0)
    return pl.pallas_call(
        _xattn_kernel,
        out_shape=jax.ShapeDtypeStruct((t, d), F32),
        grid=(t // ROW_TM,),
        in_specs=[
            pl.BlockSpec((ROW_TM, d), row),
            pl.BlockSpec((ROW_TM, d), row),
            _resident((d, d)),
            _resident((d, d)),
            pl.BlockSpec((mem_len, d), lambda i: (i // tiles_per_seq, 0)),
            pl.BlockSpec((mem_len, d), lambda i: (i // tiles_per_seq, 1)),
        ],
        out_specs=pl.BlockSpec((ROW_TM, d), row),
        scratch_shapes=[pltpu.VMEM((ROW_TM, d), BF16)],
        compiler_params=_params(1),
        name="xattn",
    )(hn, h, wq, wo, mkv, mkv)


def kernel(x, mem, ffn1_norm, ffn1_w_gate, ffn1_w_up, ffn1_w_down, mix_norm, w_in, ret_gn_gain, swa_sinks,
           w_out, xa_norm, mem_norm, xa_wq, xa_wkv, xa_wo, ffn2_norm, ffn2_w_gate, ffn2_w_up, ffn2_w_down,
           final_norm):
    batch, seq, d = x.shape
    mem_len = mem.shape[1]
    depth = ffn1_norm.shape[0]
    h = x.reshape(batch * seq, d)
    mem2 = mem.reshape(batch * mem_len, d)
    bf = lambda w: w.astype(BF16)
    row = lambda g: g.reshape(1, -1).astype(F32)
    final_gain = row(final_norm)

    for l in range(depth):
        last = l == depth - 1
        h = _ffn(h, row(ffn1_norm[l]), bf(ffn1_w_gate[l]), bf(ffn1_w_up[l]), bf(ffn1_w_down[l]), final_gain,
                 final_norm=False)
        rq, rk, rv, rg, sq, sk4, sv4 = _inproj(h, row(mix_norm[l]), bf(w_in[l]), seq)
        ret = _retention(rq, rk, rv, rg, row(ret_gn_gain[l]), batch, seq)
        swa = _swa(sq, sk4, sv4, swa_sinks[l].astype(F32), batch, seq)
        h, hn = _outproj(h, ret, swa, bf(w_out[l]), row(xa_norm[l]))
        mkv = _memkv(mem2, row(mem_norm[l]), bf(xa_wkv[l]))
        h = _xattn(hn, h, bf(xa_wq[l]), bf(xa_wo[l]), mkv, seq, mem_len)
        h = _ffn(h, row(ffn2_norm[l]), bf(ffn2_w_gate[l]), bf(ffn2_w_up[l]), bf(ffn2_w_down[l]), final_gain,
                 final_norm=last)
    if depth == 0:
        raise ValueError("depth must be at least 1")
    return h.reshape(batch, seq, d)
```

```python
import functools

import numpy as np
import jax
import jax.numpy as jnp
from jax import lax
from jax.experimental import pallas as pl
from jax.experimental.pallas import tpu as pltpu

F32 = jnp.float32
BF16 = jnp.bfloat16

D_MODEL = 2048
D_FF = 5632
RET_HEADS = 8
RET_DK = 128
RET_DV = 128
RET_WIDTH = RET_HEADS * RET_DV
RET_CHUNK = 128
SWA_HEADS = 16
SWA_KV_HEADS = 2
SWA_HEAD_DIM = 64
SWA_WIDTH = SWA_HEADS * SWA_HEAD_DIM
SWA_KV_WIDTH = SWA_KV_HEADS * SWA_HEAD_DIM
WINDOW = 128
XA_HEADS = 4
XA_HEAD_DIM = D_MODEL // XA_HEADS
ROPE_THETA = 10000.0
EPS = 1e-6
IN_SIZES = (RET_WIDTH, RET_WIDTH, RET_WIDTH, RET_WIDTH, SWA_WIDTH, SWA_KV_WIDTH, SWA_KV_WIDTH)
IN_COLS = sum(IN_SIZES)

LANES = 128
V7X_VMEM_BYTES = 64 * 1024 * 1024
VMEM_LIMIT_BYTES = 60000 * 1024

FFN_TM = 1024
FFN_TF = 512
ROW_TM = 512
NORM_ROWS = 64
NEG_INF = float(np.finfo(np.float32).min)


def _params(n_axes):
    return pltpu.CompilerParams(
        dimension_semantics=("arbitrary",) * n_axes,
        vmem_limit_bytes=VMEM_LIMIT_BYTES,
    )


def _resident(shape):
    zeros = (0,) * len(shape)
    return pl.BlockSpec(shape, lambda *_: zeros, pipeline_mode=pl.Buffered(1))


def _rms_rows(x, gain):
    ms = jnp.mean(x * x, axis=-1, keepdims=True)
    return x * lax.rsqrt(ms + EPS) * gain


def _dot(a, b):
    return lax.dot_general(a, b, (((1,), (0,)), ((), ())), preferred_element_type=F32)


def _dot_nt(a, b):
    return lax.dot_general(a, b, (((1,), (1,)), ((), ())), preferred_element_type=F32)


def _dot_tn(a, b):
    return lax.dot_general(a, b, (((0,), (0,)), ((), ())), preferred_element_type=F32)


def _rows_loop(n_rows, fn):
    def body(r, carry):
        fn(pl.ds(pl.multiple_of(r * NORM_ROWS, NORM_ROWS), NORM_ROWS))
        return carry
    lax.fori_loop(0, n_rows // NORM_ROWS, body, 0)


def _ffn_up_kernel(h_ref, g_ref, wg_ref, wu_ref, a_ref, xn_ref):
    @pl.when(pl.program_id(1) == 0)
    def _():
        def norm(rows):
            xn_ref[rows, :] = _rms_rows(h_ref[rows, :], g_ref[...]).astype(BF16)
        _rows_loop(h_ref.shape[0], norm)

    xn = xn_ref[...]
    g = _dot(xn, wg_ref[...].astype(BF16))
    u = _dot(xn, wu_ref[...].astype(BF16))
    a_ref[...] = (g * (0.5 / (1.0 + jnp.exp(-g))) * u).astype(BF16)


def _ffn_down_kernel(h_ref, a_ref, wd_ref, fg_ref, o_ref, *, n_k, final_norm):
    k = pl.program_id(1)

    @pl.when(k == 0)
    def _():
        def copy(rows):
            o_ref[rows, :] = h_ref[rows, :]
        _rows_loop(h_ref.shape[0], copy)

    a = a_ref[...]
    for c in range(0, o_ref.shape[1], FFN_TF):
        o_ref[:, c:c + FFN_TF] += _dot(a, wd_ref[:, c:c + FFN_TF].astype(BF16))

    if final_norm:
        @pl.when(k == n_k - 1)
        def _():
            def norm(rows):
                o_ref[rows, :] = _rms_rows(o_ref[rows, :], fg_ref[...])
            _rows_loop(o_ref.shape[0], norm)


def _ffn(h, gain, wg, wu, wd, final_gain, *, final_norm):
    t, d = h.shape
    f = wg.shape[1]
    n_ff = f // FFN_TF
    grid = (t // FFN_TM, n_ff)
    a = pl.pallas_call(
        _ffn_up_kernel,
        out_shape=jax.ShapeDtypeStruct((t, f), BF16),
        grid=grid,
        in_specs=[
            pl.BlockSpec((FFN_TM, d), lambda i, j: (i, 0)),
            pl.BlockSpec((1, d), lambda i, j: (0, 0)),
            pl.BlockSpec((d, FFN_TF), lambda i, j: (0, j)),
            pl.BlockSpec((d, FFN_TF), lambda i, j: (0, j)),
        ],
        out_specs=pl.BlockSpec((FFN_TM, FFN_TF), lambda i, j: (i, j)),
        scratch_shapes=[pltpu.VMEM((FFN_TM, d), BF16)],
        compiler_params=_params(2),
        name="ffn_up",
    )(h, gain, wg, wu)
    return pl.pallas_call(
        functools.partial(_ffn_down_kernel, n_k=n_ff, final_norm=final_norm),
        out_shape=jax.ShapeDtypeStruct((t, d), F32),
        grid=grid,
        in_specs=[
            pl.BlockSpec((FFN_TM, d), lambda i, k: (i, 0)),
            pl.BlockSpec((FFN_TM, FFN_TF), lambda i, k: (i, k)),
            pl.BlockSpec((FFN_TF, d), lambda i, k: (k, 0)),
            pl.BlockSpec((1, d), lambda i, k: (0, 0)),
        ],
        out_specs=pl.BlockSpec((FFN_TM, d), lambda i, k: (i, 0)),
        compiler_params=_params(2),
        name="ffn_down",
    )(h, a, wd, final_gain)


def _rope_tables(seq):
    pos = np.arange(seq, dtype=np.float32)

    def angles(d):
        inv = np.float32(ROPE_THETA) ** (-np.arange(0, d, 2, dtype=np.float32) / np.float32(d))
        return (pos[:, None] * inv[None, :].astype(np.float32)).astype(np.float32).astype(np.float64)

    a128 = angles(RET_DK)
    cos_r = np.concatenate([np.cos(a128), np.cos(a128)], -1)
    sin_r = np.concatenate([-np.sin(a128), np.sin(a128)], -1)
    a64 = angles(SWA_HEAD_DIM)
    c, s, z = np.cos(a64), np.sin(a64), np.zeros_like(a64)
    cos_s = np.concatenate([c, c, c, c], -1)
    sin_lo = np.concatenate([-s, z, -s, z], -1)
    sin_hi = np.concatenate([z, s, z, s], -1)
    return [jnp.asarray(v, dtype=F32) for v in (cos_r, sin_r, cos_s, sin_lo, sin_hi)]


def _inproj_kernel(h_ref, g_ref, w_ref, cr_ref, sr_ref, cs_ref, sl_ref, sh_ref,
                   rq_ref, rk_ref, rv_ref, rg_ref, sq_ref, sk_ref, sv_ref, xn_ref):
    tm = h_ref.shape[0]

    def body(r, carry):
        rows = pl.ds(pl.multiple_of(r * NORM_ROWS, NORM_ROWS), NORM_ROWS)
        xn_ref[rows, :] = _rms_rows(h_ref[rows, :], g_ref[...]).astype(BF16)
        return carry
    lax.fori_loop(0, tm // NORM_ROWS, body, 0)

    xn = xn_ref[...]
    cr, sr = cr_ref[...], sr_ref[...]
    cs, sl, sh = cs_ref[...], sl_ref[...], sh_ref[...]
    half = LANES // 2

    def rope_ret(x):
        return x * cr + pltpu.roll(x, half, 1) * sr

    def rope_swa(x):
        return x * cs + pltpu.roll(x, LANES - half // 2, 1) * sl + pltpu.roll(x, half // 2, 1) * sh

    def project(col0, width):
        return _dot(xn, w_ref[:, col0:col0 + width])

    col = 0
    for out_ref, fn in ((rq_ref, rope_ret), (rk_ref, rope_ret), (rv_ref, None), (rg_ref, None)):
        y = project(col, RET_WIDTH)
        for s in range(RET_WIDTH // LANES):
            slab = y[:, s * LANES:(s + 1) * LANES]
            if fn is not None:
                slab = fn(slab)
            out_ref[:, s * LANES:(s + 1) * LANES] = slab.astype(BF16)
        col += RET_WIDTH

    y = project(col, SWA_WIDTH)
    scale = SWA_HEAD_DIM ** -0.5
    for s in range(SWA_WIDTH // LANES):
        slab = rope_swa(y[:, s * LANES:(s + 1) * LANES]) * scale
        sq_ref[:, s * LANES:(s + 1) * LANES] = slab.astype(BF16)
    col += SWA_WIDTH

    y = project(col, 2 * SWA_KV_WIDTH)
    lo = lax.broadcasted_iota(jnp.int32, (tm, LANES), 1) < half
    for out_ref, x in ((sk_ref, rope_swa(y[:, :LANES])), (sv_ref, y[:, LANES:])):
        xr = pltpu.roll(x, half, 1)
        variants = (jnp.where(lo, x, 0.0), jnp.where(lo, 0.0, xr),
                    jnp.where(lo, xr, 0.0), jnp.where(lo, 0.0, x))
        for s, v in enumerate(variants):
            out_ref[:, s * LANES:(s + 1) * LANES] = v.astype(BF16)


def _inproj(h, gain, w_in, seq):
    t, d = h.shape
    tables = _rope_tables(seq)
    tiles_per_seq = seq // ROW_TM
    tab_spec = pl.BlockSpec((ROW_TM, LANES), lambda i: (i % tiles_per_seq, 0))

    def row_spec(width):
        return pl.BlockSpec((ROW_TM, width), lambda i: (i, 0))

    widths = (RET_WIDTH, RET_WIDTH, RET_WIDTH, RET_WIDTH, SWA_WIDTH, 4 * LANES, 4 * LANES)
    return pl.pallas_call(
        _inproj_kernel,
        out_shape=[jax.ShapeDtypeStruct((t, w), BF16) for w in widths],
        grid=(t // ROW_TM,),
        in_specs=[row_spec(d), _resident((1, d)), _resident((d, IN_COLS))] + [tab_spec] * 5,
        out_specs=[row_spec(w) for w in widths],
        scratch_shapes=[pltpu.VMEM((ROW_TM, d), BF16)],
        compiler_params=_params(1),
        name="inproj",
    )(h, gain, w_in, *tables)


def _retention_tables():
    c = RET_CHUNK
    heads = np.arange(RET_HEADS, dtype=np.float64)
    log_gamma = np.log1p(-np.exp2(-5.0 - heads))
    idx = np.arange(c, dtype=np.float64)
    diff = idx[:, None] - idx[None, :]
    scale = RET_DK ** -0.5
    dmat = np.where(diff[None] >= 0, np.exp(np.maximum(diff, 0.0)[None] * log_gamma[:, None, None]), 0.0)
    zeta = np.exp((c - 1.0 - idx)[None, :] * log_gamma[:, None])
    xi = np.exp((idx + 1.0)[None, :] * log_gamma[:, None])
    chunk_decay = tuple(float(v) for v in np.exp(c * log_gamma))
    zeta_b = np.broadcast_to((zeta * scale)[:, :, None], (RET_HEADS, c, RET_DK))
    xi_b = np.broadcast_to(xi[:, :, None], (RET_HEADS, c, RET_DV))
    tabs = [jnp.asarray(v, dtype=F32) for v in (dmat * scale, zeta_b, xi_b)]
    return tabs, chunk_decay


def _ret_kernel(q_ref, k_ref, v_ref, g_ref, dmat_ref, zeta_ref, xi_ref, gain_ref, o_ref, state_ref,
                *, chunk_decay):
    n = pl.program_id(1)

    @pl.when(n == 0)
    def _():
        state_ref[...] = jnp.zeros_like(state_ref)

    for h in range(RET_HEADS):
        hs = slice(h * RET_DK, (h + 1) * RET_DK)
        q, k, v = q_ref[:, hs], k_ref[:, hs], v_ref[:, hs]
        state = state_ref[h]
        s = _dot_nt(q, k) * dmat_ref[h]
        intra = _dot(s.astype(BF16), v)
        cross = _dot(q, state.astype(BF16)) * xi_ref[h]
        kz = (k.astype(F32) * zeta_ref[h]).astype(BF16)
        state_ref[h] = state * chunk_decay[h] + _dot_tn(kz, v)
        ret = intra + cross
        mu = jnp.mean(ret, axis=-1, keepdims=True)
        cen = ret - mu
        var = jnp.mean(cen * cen, axis=-1, keepdims=True)
        y = cen * lax.rsqrt(var + EPS) * gain_ref[:, hs]
        gate = g_ref[:, hs].astype(F32)
        o_ref[:, hs] = (gate * (1.0 / (1.0 + jnp.exp(-gate))) * y).astype(BF16)


def _retention(rq, rk, rv, rg, gn_gain, batch, seq):
    t = rq.shape[0]
    n_chunks = seq // RET_CHUNK
    tabs, chunk_decay = _retention_tables()
    blk = pl.BlockSpec((RET_CHUNK, RET_WIDTH), lambda b, n: (b * n_chunks + n, 0))
    tab_spec = _resident((RET_HEADS, RET_CHUNK, RET_DK))
    return pl.pallas_call(
        functools.partial(_ret_kernel, chunk_decay=chunk_decay),
        out_shape=jax.ShapeDtypeStruct((t, RET_WIDTH), BF16),
        grid=(batch, n_chunks),
        in_specs=[blk, blk, blk, blk, tab_spec, tab_spec, tab_spec, _resident((1, RET_WIDTH))],
        out_specs=blk,
        scratch_shapes=[pltpu.VMEM((RET_HEADS, RET_DK, RET_DV), F32)],
        compiler_params=_params(2),
        name="retention",
    )(rq, rk, rv, rg, *tabs, gn_gain)


def _swa_kernel(sink_ref, q_ref, kp_ref, kc_ref, vp_ref, vc_ref, o_ref):
    n = pl.program_id(1)
    w = WINDOW
    row = lax.broadcasted_iota(jnp.int32, (w, 2 * w), 0)
    col = lax.broadcasted_iota(jnp.int32, (w, 2 * w), 1)
    first_key = jnp.where(n == 0, w, 0)
    valid = (col > row) & (col <= row + w) & (col >= first_key)
    lo = lax.broadcasted_iota(jnp.int32, (w, LANES), 1) < (LANES // 2)
    pairs_per_group = SWA_HEADS // SWA_KV_HEADS // 2

    for g in range(SWA_KV_HEADS):
        ks = [jnp.concatenate([kp_ref[:, c * LANES:(c + 1) * LANES], kc_ref[:, c * LANES:(c + 1) * LANES]], 0)
              for c in (2 * g, 2 * g + 1)]
        vs = [jnp.concatenate([vp_ref[:, c * LANES:(c + 1) * LANES], vc_ref[:, c * LANES:(c + 1) * LANES]], 0)
              for c in (2 * g, 2 * g + 1)]
        for p in range(pairs_per_group):
            slab = g * pairs_per_group + p
            q = q_ref[:, slab * LANES:(slab + 1) * LANES]
            acc = None
            inv = []
            for e in range(2):
                sink = sink_ref[2 * slab + e]
                s = jnp.where(valid, _dot_nt(q, ks[e]), NEG_INF)
                m = jnp.maximum(jnp.max(s, axis=-1, keepdims=True), sink)
                pexp = jnp.exp(s - m)
                denom = jnp.sum(pexp, axis=-1, keepdims=True) + jnp.exp(sink - m)
                inv.append(1.0 / denom)
                pv = _dot(pexp.astype(BF16), vs[e])
                acc = pv if acc is None else acc + pv
            o_ref[:, slab * LANES:(slab + 1) * LANES] = (acc * jnp.where(lo, inv[0], inv[1])).astype(BF16)


def _swa(sq, sk4, sv4, sinks, batch, seq):
    t = sq.shape[0]
    n_blocks = seq // WINDOW
    cur = lambda b, n: (b * n_blocks + n, 0)
    prev = lambda b, n: (b * n_blocks + jnp.maximum(n - 1, 0), 0)
    return pl.pallas_call(
        _swa_kernel,
        out_shape=jax.ShapeDtypeStruct((t, SWA_WIDTH), BF16),
        grid=(batch, n_blocks),
        in_specs=[
            pl.BlockSpec(memory_space=pltpu.SMEM),
            pl.BlockSpec((WINDOW, SWA_WIDTH), cur),
            pl.BlockSpec((WINDOW, 4 * LANES), prev),
            pl.BlockSpec((WINDOW, 4 * LANES), cur),
            pl.BlockSpec((WINDOW, 4 * LANES), prev),
            pl.BlockSpec((WINDOW, 4 * LANES), cur),
        ],
        out_specs=pl.BlockSpec((WINDOW, SWA_WIDTH), cur),
        compiler_params=_params(2),
        name="swa",
    )(sinks, sq, sk4, sk4, sv4, sv4)


def _outproj_kernel(h_ref, ret_ref, swa_ref, w_ref, g_ref, h2_ref, hn_ref):
    tm = h_ref.shape[0]
    y = _dot(ret_ref[...], w_ref[:RET_WIDTH, :]) + _dot(swa_ref[...], w_ref[RET_WIDTH:, :])
    h2_ref[...] = h_ref[...] + y

    def body(r, carry):
        rows = pl.ds(pl.multiple_of(r * NORM_ROWS, NORM_ROWS), NORM_ROWS)
        hn_ref[rows, :] = _rms_rows(h2_ref[rows, :], g_ref[...]).astype(BF16)
        return carry
    lax.fori_loop(0, tm // NORM_ROWS, body, 0)


def _outproj(h, ret, swa, w_out, gain):
    t, d = h.shape
    row = lambda width: pl.BlockSpec((ROW_TM, width), lambda i: (i, 0))
    return pl.pallas_call(
        _outproj_kernel,
        out_shape=[jax.ShapeDtypeStruct((t, d), F32), jax.ShapeDtypeStruct((t, d), BF16)],
        grid=(t // ROW_TM,),
        in_specs=[row(d), row(RET_WIDTH), row(SWA_WIDTH), _resident(w_out.shape), _resident((1, d))],
        out_specs=[row(d), row(d)],
        compiler_params=_params(1),
        name="outproj",
    )(h, ret, swa, w_out, gain)


def _memkv_kernel(m_ref, g_ref, w_ref, o_ref):
    xn = _rms_rows(m_ref[...], g_ref[...]).astype(BF16)
    o_ref[...] = _dot(xn, w_ref[...]).astype(BF16)


def _memkv(mem, gain, wkv):
    rows, d = mem.shape
    n_out = wkv.shape[1]
    tn = 1024
    return pl.pallas_call(
        _memkv_kernel,
        out_shape=jax.ShapeDtypeStruct((rows, n_out), BF16),
        grid=(n_out // tn,),
        in_specs=[_resident((rows, d)), _resident((1, d)), pl.BlockSpec((d, tn), lambda j: (0, j))],
        out_specs=pl.BlockSpec((rows, tn), lambda j: (0, j)),
        compiler_params=_params(1),
        name="memkv",
    )(mem, gain, wkv)


def _xattn_kernel(hn_ref, h_ref, wq_ref, wo_ref, k_ref, v_ref, o_ref, att_ref):
    q = _dot(hn_ref[...], wq_ref[...]).astype(BF16)
    scale = XA_HEAD_DIM ** -0.5
    for hd in range(XA_HEADS):
        hs = slice(hd * XA_HEAD_DIM, (hd + 1) * XA_HEAD_DIM)
        s = _dot_nt(q[:, hs], k_ref[:, hs]) * scale
        m = jnp.max(s, axis=-1, keepdims=True)
        p = jnp.exp(s - m)
        inv = 1.0 / jnp.sum(p, axis=-1, keepdims=True)
        att_ref[:, hs] = (_dot(p.astype(BF16), v_ref[:, hs]) * inv).astype(BF16)
    o_ref[...] = h_ref[...] + _dot(att_ref[...], wo_ref[...])


def _xattn(hn, h, wq, wo, mkv, seq, mem_len):
    t, d = h.shape
    tiles_per_seq = seq // ROW_TM
    row = lambda i: (i, 0)
    return pl.pallas_call(
        _xattn_kernel,
        out_shape=jax.ShapeDtypeStruct((t, d), F32),
        grid=(t // ROW_TM,),
        in_specs=[
            pl.BlockSpec((ROW_TM, d), row),
            pl.BlockSpec((ROW_TM, d), row),
            _resident((d, d)),
            _resident((d, d)),
            pl.BlockSpec((mem_len, d), lambda i: (i // tiles_per_seq, 0)),
            pl.BlockSpec((mem_len, d), lambda i: (i // tiles_per_seq, 1)),
        ],
        out_specs=pl.BlockSpec((ROW_TM, d), row),
        scratch_shapes=[pltpu.VMEM((ROW_TM, d), BF16)],
        compiler_params=_params(1),
        name="xattn",
    )(hn, h, wq, wo, mkv, mkv)


def kernel(x, mem, ffn1_norm, ffn1_w_gate, ffn1_w_up, ffn1_w_down, mix_norm, w_in, ret_gn_gain, swa_sinks,
           w_out, xa_norm, mem_norm, xa_wq, xa_wkv, xa_wo, ffn2_norm, ffn2_w_gate, ffn2_w_up, ffn2_w_down,
           final_norm):
    batch, seq, d = x.shape
    mem_len = mem.shape[1]
    depth = ffn1_norm.shape[0]
    h = x.reshape(batch * seq, d)
    mem2 = mem.reshape(batch * mem_len, d)
    bf = lambda w: w.astype(BF16)
    row = lambda g: g.reshape(1, -1).astype(F32)
    final_gain = row(final_norm)

    for l in range(depth):
        last = l == depth - 1
        h = _ffn(h, row(ffn1_norm[l]), ffn1_w_gate[l], ffn1_w_up[l], ffn1_w_down[l], final_gain,
                 final_norm=False)
        rq, rk, rv, rg, sq, sk4, sv4 = _inproj(h, row(mix_norm[l]), bf(w_in[l]), seq)
        ret = _retention(rq, rk, rv, rg, row(ret_gn_gain[l]), batch, seq)
        swa = _swa(sq, sk4, sv4, swa_sinks[l].astype(F32), batch, seq)
        h, hn = _outproj(h, ret, swa, bf(w_out[l]), row(xa_norm[l]))
        mkv = _memkv(mem2, row(mem_norm[l]), bf(xa_wkv[l]))
        h = _xattn(hn, h, bf(xa_wq[l]), bf(xa_wo[l]), mkv, seq, mem_len)
        h = _ffn(h, row(ffn2_norm[l]), ffn2_w_gate[l], ffn2_w_up[l], ffn2_w_down[l], final_gain,
                 final_norm=last)
    if depth == 0:
        raise ValueError("depth must be at least 1")
    return h.reshape(batch, seq, d)
```

```python
import functools

import numpy as np
import jax
import jax.numpy as jnp
from jax import lax
from jax.experimental import pallas as pl
from jax.experimental.pallas import tpu as pltpu

F32 = jnp.float32
BF16 = jnp.bfloat16

D_MODEL = 2048
D_FF = 5632
RET_HEADS = 8
RET_DK = 128
RET_DV = 128
RET_WIDTH = RET_HEADS * RET_DV
RET_CHUNK = 128
SWA_HEADS = 16
SWA_KV_HEADS = 2
SWA_HEAD_DIM = 64
SWA_WIDTH = SWA_HEADS * SWA_HEAD_DIM
SWA_KV_WIDTH = SWA_KV_HEADS * SWA_HEAD_DIM
WINDOW = 128
XA_HEADS = 4
XA_HEAD_DIM = D_MODEL // XA_HEADS
ROPE_THETA = 10000.0
EPS = 1e-6
IN_SIZES = (RET_WIDTH, RET_WIDTH, RET_WIDTH, RET_WIDTH, SWA_WIDTH, SWA_KV_WIDTH, SWA_KV_WIDTH)
IN_COLS = sum(IN_SIZES)

LANES = 128
BF16_SUBLANES = 16
V7X_VMEM_BYTES = 64 * 1024 * 1024
VMEM_LIMIT_BYTES = 60000 * 1024

FFN_TM = 1024
FFN_TF = 512
ROW_TM = 512
NORM_ROWS = 256
NEG_INF = float(np.finfo(np.float32).min)


def _params(n_axes):
    return pltpu.CompilerParams(
        dimension_semantics=("arbitrary",) * n_axes,
        vmem_limit_bytes=VMEM_LIMIT_BYTES,
    )


def _resident(shape):
    zeros = (0,) * len(shape)
    return pl.BlockSpec(shape, lambda *_: zeros, pipeline_mode=pl.Buffered(1))


def _rms_rows(x, gain):
    ms = jnp.mean(x * x, axis=-1, keepdims=True)
    return x * lax.rsqrt(ms + EPS) * gain


def _dot(a, b):
    return lax.dot_general(a, b, (((1,), (0,)), ((), ())), preferred_element_type=F32)


def _dot_nt(a, b):
    return lax.dot_general(a, b, (((1,), (1,)), ((), ())), preferred_element_type=F32)


def _dot_tn(a, b):
    return lax.dot_general(a, b, (((0,), (0,)), ((), ())), preferred_element_type=F32)


def _cast_block(shape, n_steps):
    rows, cols = shape
    for col_splits in (1, 2, 4, 8):
        row_blocks, rem = divmod(n_steps, col_splits)
        if rem or rows % row_blocks or cols % col_splits:
            continue
        br, bc = rows // row_blocks, cols // col_splits
        if br % BF16_SUBLANES == 0 and bc % LANES == 0:
            return br, bc, col_splits
    raise ValueError(f"no aligned {n_steps}-way split of {shape}")


def _with_casts(kernel_fn, n_in, n_out, n_cast):
    def wrapped(*refs):
        ins, rest = refs[:n_in], refs[n_in:]
        cast_in, rest = rest[:n_cast], rest[n_cast:]
        outs, rest = rest[:n_out], rest[n_out:]
        cast_out, scratch = rest[:n_cast], rest[n_cast:]
        kernel_fn(*ins, *outs, *scratch)
        for src, dst in zip(cast_in, cast_out):
            dst[...] = src[...].astype(BF16)
    return wrapped


def _cast_specs(weights, n_outer, n_inner):
    specs, shapes = [], []
    for w in weights:
        br, bc, col_splits = _cast_block(w.shape, n_outer * n_inner)

        def index_map(b, n, col_splits=col_splits):
            step = b * n_inner + n
            return step // col_splits, step % col_splits
        specs.append(pl.BlockSpec((br, bc), index_map))
        shapes.append(jax.ShapeDtypeStruct(w.shape, BF16))
    return specs, shapes


def _rows_loop(n_rows, fn):
    def body(r, carry):
        fn(pl.ds(pl.multiple_of(r * NORM_ROWS, NORM_ROWS), NORM_ROWS))
        return carry
    lax.fori_loop(0, n_rows // NORM_ROWS, body, 0)


def _ffn_kernel(h_ref, g_ref, wg_ref, wu_ref, wd_ref, fg_ref, o_ref, xn_ref, *, n_ff, final_norm):
    j = pl.program_id(1)
    tm = h_ref.shape[0]

    @pl.when(j == 0)
    def _():
        def norm(rows):
            h = h_ref[rows, :]
            xn_ref[rows, :] = _rms_rows(h, g_ref[...]).astype(BF16)
            o_ref[rows, :] = h
        _rows_loop(tm, norm)

    xn = xn_ref[...]
    g = _dot(xn, wg_ref[...])
    u = _dot(xn, wu_ref[...])
    a = (g * (0.5 / (1.0 + jnp.exp(-g))) * u).astype(BF16)
    for c in range(0, o_ref.shape[1], FFN_TF):
        o_ref[:, c:c + FFN_TF] += _dot(a, wd_ref[:, c:c + FFN_TF])

    if final_norm:
        @pl.when(j == n_ff - 1)
        def _():
            def norm(rows):
                o_ref[rows, :] = _rms_rows(o_ref[rows, :], fg_ref[...])
            _rows_loop(tm, norm)


def _ffn(h, gain, wg, wu, wd, final_gain, *, final_norm):
    t, d = h.shape
    f = wg.shape[1]
    n_ff = f // FFN_TF
    return pl.pallas_call(
        functools.partial(_ffn_kernel, n_ff=n_ff, final_norm=final_norm),
        out_shape=jax.ShapeDtypeStruct((t, d), F32),
        grid=(t // FFN_TM, n_ff),
        in_specs=[
            pl.BlockSpec((FFN_TM, d), lambda i, j: (i, 0)),
            pl.BlockSpec((1, d), lambda i, j: (0, 0)),
            pl.BlockSpec((d, FFN_TF), lambda i, j: (0, j)),
            pl.BlockSpec((d, FFN_TF), lambda i, j: (0, j)),
            pl.BlockSpec((FFN_TF, d), lambda i, j: (j, 0)),
            pl.BlockSpec((1, d), lambda i, j: (0, 0)),
        ],
        out_specs=pl.BlockSpec((FFN_TM, d), lambda i, j: (i, 0)),
        scratch_shapes=[pltpu.VMEM((FFN_TM, d), BF16)],
        compiler_params=_params(2),
        name="ffn",
    )(h, gain, wg, wu, wd, final_gain)


def _rope_tables(seq):
    pos = np.arange(seq, dtype=np.float32)

    def angles(d):
        inv = np.float32(ROPE_THETA) ** (-np.arange(0, d, 2, dtype=np.float32) / np.float32(d))
        return (pos[:, None] * inv[None, :].astype(np.float32)).astype(np.float32).astype(np.float64)

    a128 = angles(RET_DK)
    cos_r = np.concatenate([np.cos(a128), np.cos(a128)], -1)
    sin_r = np.concatenate([-np.sin(a128), np.sin(a128)], -1)
    a64 = angles(SWA_HEAD_DIM)
    c, s, z = np.cos(a64), np.sin(a64), np.zeros_like(a64)
    cos_s = np.concatenate([c, c, c, c], -1)
    sin_lo = np.concatenate([-s, z, -s, z], -1)
    sin_hi = np.concatenate([z, s, z, s], -1)
    return [jnp.asarray(v, dtype=F32) for v in (cos_r, sin_r, cos_s, sin_lo, sin_hi)]


def _inproj_kernel(h_ref, g_ref, w_ref, cr_ref, sr_ref, cs_ref, sl_ref, sh_ref,
                   rq_ref, rk_ref, rv_ref, rg_ref, sq_ref, sk_ref, sv_ref, xn_ref):
    tm = h_ref.shape[0]

    def body(r, carry):
        rows = pl.ds(pl.multiple_of(r * NORM_ROWS, NORM_ROWS), NORM_ROWS)
        xn_ref[rows, :] = _rms_rows(h_ref[rows, :], g_ref[...]).astype(BF16)
        return carry
    lax.fori_loop(0, tm // NORM_ROWS, body, 0)

    xn = xn_ref[...]
    cr, sr = cr_ref[...], sr_ref[...]
    cs, sl, sh = cs_ref[...], sl_ref[...], sh_ref[...]
    half = LANES // 2

    def rope_ret(x):
        return x * cr + pltpu.roll(x, half, 1) * sr

    def rope_swa(x):
        return x * cs + pltpu.roll(x, LANES - half // 2, 1) * sl + pltpu.roll(x, half // 2, 1) * sh

    def project(col0, width):
        return _dot(xn, w_ref[:, col0:col0 + width])

    col = 0
    for out_ref, fn in ((rq_ref, rope_ret), (rk_ref, rope_ret), (rv_ref, None), (rg_ref, None)):
        y = project(col, RET_WIDTH)
        for s in range(RET_WIDTH // LANES):
            slab = y[:, s * LANES:(s + 1) * LANES]
            if fn is not None:
                slab = fn(slab)
            out_ref[:, s * LANES:(s + 1) * LANES] = slab.astype(BF16)
        col += RET_WIDTH

    y = project(col, SWA_WIDTH)
    scale = SWA_HEAD_DIM ** -0.5
    for s in range(SWA_WIDTH // LANES):
        slab = rope_swa(y[:, s * LANES:(s + 1) * LANES]) * scale
        sq_ref[:, s * LANES:(s + 1) * LANES] = slab.astype(BF16)
    col += SWA_WIDTH

    y = project(col, 2 * SWA_KV_WIDTH)
    lo = lax.broadcasted_iota(jnp.int32, (tm, LANES), 1) < half
    for out_ref, x in ((sk_ref, rope_swa(y[:, :LANES])), (sv_ref, y[:, LANES:])):
        xr = pltpu.roll(x, half, 1)
        variants = (jnp.where(lo, x, 0.0), jnp.where(lo, 0.0, xr),
                    jnp.where(lo, xr, 0.0), jnp.where(lo, 0.0, x))
        for s, v in enumerate(variants):
            out_ref[:, s * LANES:(s + 1) * LANES] = v.astype(BF16)


def _inproj(h, gain, w_in, seq):
    t, d = h.shape
    tables = _rope_tables(seq)
    tiles_per_seq = seq // ROW_TM
    tab_spec = pl.BlockSpec((ROW_TM, LANES), lambda i: (i % tiles_per_seq, 0))

    def row_spec(width):
        return pl.BlockSpec((ROW_TM, width), lambda i: (i, 0))

    widths = (RET_WIDTH, RET_WIDTH, RET_WIDTH, RET_WIDTH, SWA_WIDTH, 4 * LANES, 4 * LANES)
    return pl.pallas_call(
        _inproj_kernel,
        out_shape=[jax.ShapeDtypeStruct((t, w), BF16) for w in widths],
        grid=(t // ROW_TM,),
        in_specs=[row_spec(d), _resident((1, d)), _resident((d, IN_COLS))] + [tab_spec] * 5,
        out_specs=[row_spec(w) for w in widths],
        scratch_shapes=[pltpu.VMEM((ROW_TM, d), BF16)],
        compiler_params=_params(1),
        name="inproj",
    )(h, gain, w_in, *tables)


def _retention_tables():
    c = RET_CHUNK
    heads = np.arange(RET_HEADS, dtype=np.float64)
    log_gamma = np.log1p(-np.exp2(-5.0 - heads))
    idx = np.arange(c, dtype=np.float64)
    diff = idx[:, None] - idx[None, :]
    scale = RET_DK ** -0.5
    dmat = np.where(diff[None] >= 0, np.exp(np.maximum(diff, 0.0)[None] * log_gamma[:, None, None]), 0.0)
    zeta = np.exp((c - 1.0 - idx)[None, :] * log_gamma[:, None])
    xi = np.exp((idx + 1.0)[None, :] * log_gamma[:, None])
    chunk_decay = tuple(float(v) for v in np.exp(c * log_gamma))
    zeta_b = np.broadcast_to((zeta * scale)[:, :, None], (RET_HEADS, c, RET_DK))
    xi_b = np.broadcast_to(xi[:, :, None], (RET_HEADS, c, RET_DV))
    tabs = [jnp.asarray(v, dtype=F32) for v in (dmat * scale, zeta_b, xi_b)]
    return tabs, chunk_decay


def _ret_kernel(q_ref, k_ref, v_ref, g_ref, dmat_ref, zeta_ref, xi_ref, gain_ref, o_ref, state_ref,
                *, chunk_decay):
    n = pl.program_id(1)

    @pl.when(n == 0)
    def _():
        state_ref[...] = jnp.zeros_like(state_ref)

    for h in range(RET_HEADS):
        hs = slice(h * RET_DK, (h + 1) * RET_DK)
        q, k, v = q_ref[:, hs], k_ref[:, hs], v_ref[:, hs]
        state = state_ref[h]
        s = _dot_nt(q, k) * dmat_ref[h]
        intra = _dot(s.astype(BF16), v)
        cross = _dot(q, state.astype(BF16)) * xi_ref[h]
        kz = (k.astype(F32) * zeta_ref[h]).astype(BF16)
        state_ref[h] = state * chunk_decay[h] + _dot_tn(kz, v)
        ret = intra + cross
        mu = jnp.mean(ret, axis=-1, keepdims=True)
        cen = ret - mu
        var = jnp.mean(cen * cen, axis=-1, keepdims=True)
        y = cen * lax.rsqrt(var + EPS) * gain_ref[:, hs]
        gate = g_ref[:, hs].astype(F32)
        o_ref[:, hs] = (gate * (1.0 / (1.0 + jnp.exp(-gate))) * y).astype(BF16)


def _retention(rq, rk, rv, rg, gn_gain, batch, seq, cast_weights):
    t = rq.shape[0]
    n_chunks = seq // RET_CHUNK
    tabs, chunk_decay = _retention_tables()
    blk = pl.BlockSpec((RET_CHUNK, RET_WIDTH), lambda b, n: (b * n_chunks + n, 0))
    tab_spec = _resident((RET_HEADS, RET_CHUNK, RET_DK))
    cast_specs, cast_shapes = _cast_specs(cast_weights, batch, n_chunks)
    host = functools.partial(_ret_kernel, chunk_decay=chunk_decay)
    out = pl.pallas_call(
        _with_casts(host, 8, 1, len(cast_weights)),
        out_shape=[jax.ShapeDtypeStruct((t, RET_WIDTH), BF16)] + cast_shapes,
        grid=(batch, n_chunks),
        in_specs=[blk, blk, blk, blk, tab_spec, tab_spec, tab_spec, _resident((1, RET_WIDTH))] + cast_specs,
        out_specs=[blk] + cast_specs,
        scratch_shapes=[pltpu.VMEM((RET_HEADS, RET_DK, RET_DV), F32)],
        compiler_params=_params(2),
        name="retention",
    )(rq, rk, rv, rg, *tabs, gn_gain, *cast_weights)
    return out[0], out[1:]


def _swa_kernel(sink_ref, q_ref, kp_ref, kc_ref, vp_ref, vc_ref, o_ref):
    n = pl.program_id(1)
    w = WINDOW
    row = lax.broadcasted_iota(jnp.int32, (w, 2 * w), 0)
    col = lax.broadcasted_iota(jnp.int32, (w, 2 * w), 1)
    first_key = jnp.where(n == 0, w, 0)
    valid = (col > row) & (col <= row + w) & (col >= first_key)
    lo = lax.broadcasted_iota(jnp.int32, (w, LANES), 1) < (LANES // 2)
    pairs_per_group = SWA_HEADS // SWA_KV_HEADS // 2

    for g in range(SWA_KV_HEADS):
        ks = [jnp.concatenate([kp_ref[:, c * LANES:(c + 1) * LANES], kc_ref[:, c * LANES:(c + 1) * LANES]], 0)
              for c in (2 * g, 2 * g + 1)]
        vs = [jnp.concatenate([vp_ref[:, c * LANES:(c + 1) * LANES], vc_ref[:, c * LANES:(c + 1) * LANES]], 0)
              for c in (2 * g, 2 * g + 1)]
        for p in range(pairs_per_group):
            slab = g * pairs_per_group + p
            q = q_ref[:, slab * LANES:(slab + 1) * LANES]
            acc = None
            inv = []
            for e in range(2):
                sink = sink_ref[2 * slab + e]
                s = jnp.where(valid, _dot_nt(q, ks[e]), NEG_INF)
                m = jnp.maximum(jnp.max(s, axis=-1, keepdims=True), sink)
                pexp = jnp.exp(s - m)
                denom = jnp.sum(pexp, axis=-1, keepdims=True) + jnp.exp(sink - m)
                inv.append(1.0 / denom)
                pv = _dot(pexp.astype(BF16), vs[e])
                acc = pv if acc is None else acc + pv
            o_ref[:, slab * LANES:(slab + 1) * LANES] = (acc * jnp.where(lo, inv[0], inv[1])).astype(BF16)


def _swa(sq, sk4, sv4, sinks, batch, seq, cast_weights):
    t = sq.shape[0]
    n_blocks = seq // WINDOW
    cur = lambda b, n: (b * n_blocks + n, 0)
    prev = lambda b, n: (b * n_blocks + jnp.maximum(n - 1, 0), 0)
    cast_specs, cast_shapes = _cast_specs(cast_weights, batch, n_blocks)
    out = pl.pallas_call(
        _with_casts(_swa_kernel, 6, 1, len(cast_weights)),
        out_shape=[jax.ShapeDtypeStruct((t, SWA_WIDTH), BF16)] + cast_shapes,
        grid=(batch, n_blocks),
        in_specs=[
            pl.BlockSpec(memory_space=pltpu.SMEM),
            pl.BlockSpec((WINDOW, SWA_WIDTH), cur),
            pl.BlockSpec((WINDOW, 4 * LANES), prev),
            pl.BlockSpec((WINDOW, 4 * LANES), cur),
            pl.BlockSpec((WINDOW, 4 * LANES), prev),
            pl.BlockSpec((WINDOW, 4 * LANES), cur),
        ] + cast_specs,
        out_specs=[pl.BlockSpec((WINDOW, SWA_WIDTH), cur)] + cast_specs,
        compiler_params=_params(2),
        name="swa",
    )(sinks, sq, sk4, sk4, sv4, sv4, *cast_weights)
    return out[0], out[1:]


def _outproj_kernel(h_ref, ret_ref, swa_ref, w_ref, g_ref, h2_ref, hn_ref):
    tm = h_ref.shape[0]
    y = _dot(ret_ref[...], w_ref[:RET_WIDTH, :]) + _dot(swa_ref[...], w_ref[RET_WIDTH:, :])
    h2_ref[...] = h_ref[...] + y

    def body(r, carry):
        rows = pl.ds(pl.multiple_of(r * NORM_ROWS, NORM_ROWS), NORM_ROWS)
        hn_ref[rows, :] = _rms_rows(h2_ref[rows, :], g_ref[...]).astype(BF16)
        return carry
    lax.fori_loop(0, tm // NORM_ROWS, body, 0)


def _outproj(h, ret, swa, w_out, gain):
    t, d = h.shape
    row = lambda width: pl.BlockSpec((ROW_TM, width), lambda i: (i, 0))
    return pl.pallas_call(
        _outproj_kernel,
        out_shape=[jax.ShapeDtypeStruct((t, d), F32), jax.ShapeDtypeStruct((t, d), BF16)],
        grid=(t // ROW_TM,),
        in_specs=[row(d), row(RET_WIDTH), row(SWA_WIDTH), _resident(w_out.shape), _resident((1, d))],
        out_specs=[row(d), row(d)],
        compiler_params=_params(1),
        name="outproj",
    )(h, ret, swa, w_out, gain)


def _memkv_kernel(m_ref, g_ref, w_ref, o_ref):
    xn = _rms_rows(m_ref[...], g_ref[...]).astype(BF16)
    o_ref[...] = _dot(xn, w_ref[...]).astype(BF16)


def _memkv(mem, gain, wkv):
    rows, d = mem.shape
    n_out = wkv.shape[1]
    tn = 1024
    return pl.pallas_call(
        _memkv_kernel,
        out_shape=jax.ShapeDtypeStruct((rows, n_out), BF16),
        grid=(n_out // tn,),
        in_specs=[_resident((rows, d)), _resident((1, d)), pl.BlockSpec((d, tn), lambda j: (0, j))],
        out_specs=pl.BlockSpec((rows, tn), lambda j: (0, j)),
        compiler_params=_params(1),
        name="memkv",
    )(mem, gain, wkv)


def _xattn_kernel(hn_ref, h_ref, wq_ref, wo_ref, k_ref, v_ref, o_ref, att_ref):
    q = _dot(hn_ref[...], wq_ref[...]).astype(BF16)
    scale = XA_HEAD_DIM ** -0.5
    for hd in range(XA_HEADS):
        hs = slice(hd * XA_HEAD_DIM, (hd + 1) * XA_HEAD_DIM)
        s = _dot_nt(q[:, hs], k_ref[:, hs]) * scale
        m = jnp.max(s, axis=-1, keepdims=True)
        p = jnp.exp(s - m)
        inv = 1.0 / jnp.sum(p, axis=-1, keepdims=True)
        att_ref[:, hs] = (_dot(p.astype(BF16), v_ref[:, hs]) * inv).astype(BF16)
    o_ref[...] = h_ref[...] + _dot(att_ref[...], wo_ref[...])


def _xattn(hn, h, wq, wo, mkv, seq, mem_len):
    t, d = h.shape
    tiles_per_seq = seq // ROW_TM
    row = lambda i: (i, 0)
    return pl.pallas_call(
        _xattn_kernel,
        out_shape=jax.ShapeDtypeStruct((t, d), F32),
        grid=(t // ROW_TM,),
        in_specs=[
            pl.BlockSpec((ROW_TM, d), row),
            pl.BlockSpec((ROW_TM, d), row),
            _resident((d, d)),
            _resident((d, d)),
            pl.BlockSpec((mem_len, d), lambda i: (i // tiles_per_seq, 0)),
            pl.BlockSpec((mem_len, d), lambda i: (i // tiles_per_seq, 1)),
        ],
        out_specs=pl.BlockSpec((ROW_TM, d), row),
        scratch_shapes=[pltpu.VMEM((ROW_TM, d), BF16)],
        compiler_params=_params(1),
        name="xattn",
    )(hn, h, wq, wo, mkv, mkv)


def kernel(x, mem, ffn1_norm, ffn1_w_gate, ffn1_w_up, ffn1_w_down, mix_norm, w_in, ret_gn_gain, swa_sinks,
           w_out, xa_norm, mem_norm, xa_wq, xa_wkv, xa_wo, ffn2_norm, ffn2_w_gate, ffn2_w_up, ffn2_w_down,
           final_norm):
    batch, seq, d = x.shape
    mem_len = mem.shape[1]
    depth = ffn1_norm.shape[0]
    h = x.reshape(batch * seq, d)
    mem2 = mem.reshape(batch * mem_len, d)
    bf = lambda w: w.astype(BF16)
    row = lambda g: g.reshape(1, -1).astype(F32)
    final_gain = row(final_norm)

    for l in range(depth):
        last = l == depth - 1
        h = _ffn(h, row(ffn1_norm[l]), bf(ffn1_w_gate[l]), bf(ffn1_w_up[l]), bf(ffn1_w_down[l]), final_gain,
                 final_norm=False)
        rq, rk, rv, rg, sq, sk4, sv4 = _inproj(h, row(mix_norm[l]), bf(w_in[l]), seq)
        ret, (w2_gate, wkv) = _retention(rq, rk, rv, rg, row(ret_gn_gain[l]), batch, seq,
                                         [ffn2_w_gate[l], xa_wkv[l]])
        swa, (w2_up, w2_down, wo_mix, wq, wo) = _swa(sq, sk4, sv4, swa_sinks[l].astype(F32), batch, seq,
                                                     [ffn2_w_up[l], ffn2_w_down[l], w_out[l], xa_wq[l], xa_wo[l]])
        h, hn = _outproj(h, ret, swa, wo_mix, row(xa_norm[l]))
        mkv = _memkv(mem2, row(mem_norm[l]), wkv)
        h = _xattn(hn, h, wq, wo, mkv, seq, mem_len)
        h = _ffn(h, row(ffn2_norm[l]), w2_gate, w2_up, w2_down, final_gain, final_norm=last)
    if depth == 0:
        raise ValueError("depth must be at least 1")
    return h.reshape(batch, seq, d)
```

```python
import functools

import numpy as np
import jax
import jax.numpy as jnp
from jax import lax
from jax.experimental import pallas as pl
from jax.experimental.pallas import tpu as pltpu

F32 = jnp.float32
BF16 = jnp.bfloat16

D_MODEL = 2048
D_FF = 5632
RET_HEADS = 8
RET_DK = 128
RET_DV = 128
RET_WIDTH = RET_HEADS * RET_DV
RET_CHUNK = 128
SWA_HEADS = 16
SWA_KV_HEADS = 2
SWA_HEAD_DIM = 64
SWA_WIDTH = SWA_HEADS * SWA_HEAD_DIM
SWA_KV_WIDTH = SWA_KV_HEADS * SWA_HEAD_DIM
WINDOW = 128
XA_HEADS = 4
XA_HEAD_DIM = D_MODEL // XA_HEADS
ROPE_THETA = 10000.0
EPS = 1e-6
IN_SIZES = (RET_WIDTH, RET_WIDTH, RET_WIDTH, RET_WIDTH, SWA_WIDTH, SWA_KV_WIDTH, SWA_KV_WIDTH)
IN_COLS = sum(IN_SIZES)

LANES = 128
BF16_SUBLANES = 16
V7X_VMEM_BYTES = 64 * 1024 * 1024
VMEM_LIMIT_BYTES = 60000 * 1024

FFN_TM = 1024
FFN_TF = 512
ROW_TM = 512
NORM_ROWS = 256
NEG_INF = float(np.finfo(np.float32).min)


def _params(n_axes):
    return pltpu.CompilerParams(
        dimension_semantics=("arbitrary",) * n_axes,
        vmem_limit_bytes=VMEM_LIMIT_BYTES,
    )


def _resident(shape):
    zeros = (0,) * len(shape)
    return pl.BlockSpec(shape, lambda *_: zeros, pipeline_mode=pl.Buffered(1))


def _rms_rows(x, gain):
    ms = jnp.mean(x * x, axis=-1, keepdims=True)
    return x * lax.rsqrt(ms + EPS) * gain


def _dot(a, b):
    return lax.dot_general(a, b, (((1,), (0,)), ((), ())), preferred_element_type=F32)


def _dot_nt(a, b):
    return lax.dot_general(a, b, (((1,), (1,)), ((), ())), preferred_element_type=F32)


def _dot_tn(a, b):
    return lax.dot_general(a, b, (((0,), (0,)), ((), ())), preferred_element_type=F32)


def _cast_block(shape, n_steps):
    rows, cols = shape
    for col_splits in (1, 2, 4, 8):
        row_blocks, rem = divmod(n_steps, col_splits)
        if rem or rows % row_blocks or cols % col_splits:
            continue
        br, bc = rows // row_blocks, cols // col_splits
        if br % BF16_SUBLANES == 0 and bc % LANES == 0:
            return br, bc, col_splits
    raise ValueError(f"no aligned {n_steps}-way split of {shape}")


def _with_casts(kernel_fn, n_in, n_out, n_cast):
    def wrapped(*refs):
        ins, rest = refs[:n_in], refs[n_in:]
        cast_in, rest = rest[:n_cast], rest[n_cast:]
        outs, rest = rest[:n_out], rest[n_out:]
        cast_out, scratch = rest[:n_cast], rest[n_cast:]
        kernel_fn(*ins, *outs, *scratch)
        for src, dst in zip(cast_in, cast_out):
            dst[...] = src[...].astype(BF16)
    return wrapped


def _cast_specs(weights, n_outer, n_inner):
    specs, shapes = [], []
    for w in weights:
        br, bc, col_splits = _cast_block(w.shape, n_outer * n_inner)

        def index_map(b, n, col_splits=col_splits):
            step = b * n_inner + n
            return step // col_splits, step % col_splits
        specs.append(pl.BlockSpec((br, bc), index_map))
        shapes.append(jax.ShapeDtypeStruct(w.shape, BF16))
    return specs, shapes


def _rows_loop(n_rows, fn):
    def body(r, carry):
        fn(pl.ds(pl.multiple_of(r * NORM_ROWS, NORM_ROWS), NORM_ROWS))
        return carry
    lax.fori_loop(0, n_rows // NORM_ROWS, body, 0)


def _ffn_kernel(h_ref, g_ref, wg_ref, wu_ref, wd_ref, fg_ref, o_ref, xn_ref, *, n_ff, final_norm):
    j = pl.program_id(1)
    tm = h_ref.shape[0]

    @pl.when(j == 0)
    def _():
        def norm(rows):
            h = h_ref[rows, :]
            xn_ref[rows, :] = _rms_rows(h, g_ref[...]).astype(BF16)
            o_ref[rows, :] = h
        _rows_loop(tm, norm)

    xn = xn_ref[...]
    g = _dot(xn, wg_ref[...])
    u = _dot(xn, wu_ref[...])
    a = (g * (0.5 / (1.0 + jnp.exp(-g))) * u).astype(BF16)
    for c in range(0, o_ref.shape[1], FFN_TF):
        o_ref[:, c:c + FFN_TF] += _dot(a, wd_ref[:, c:c + FFN_TF])

    if final_norm:
        @pl.when(j == n_ff - 1)
        def _():
            def norm(rows):
                o_ref[rows, :] = _rms_rows(o_ref[rows, :], fg_ref[...])
            _rows_loop(tm, norm)


def _ffn(h, gain, wg, wu, wd, final_gain, *, final_norm):
    t, d = h.shape
    f = wg.shape[1]
    n_ff = f // FFN_TF
    return pl.pallas_call(
        functools.partial(_ffn_kernel, n_ff=n_ff, final_norm=final_norm),
        out_shape=jax.ShapeDtypeStruct((t, d), F32),
        grid=(t // FFN_TM, n_ff),
        in_specs=[
            pl.BlockSpec((FFN_TM, d), lambda i, j: (i, 0)),
            pl.BlockSpec((1, d), lambda i, j: (0, 0)),
            pl.BlockSpec((d, FFN_TF), lambda i, j: (0, j)),
            pl.BlockSpec((d, FFN_TF), lambda i, j: (0, j)),
            pl.BlockSpec((FFN_TF, d), lambda i, j: (j, 0)),
            pl.BlockSpec((1, d), lambda i, j: (0, 0)),
        ],
        out_specs=pl.BlockSpec((FFN_TM, d), lambda i, j: (i, 0)),
        scratch_shapes=[pltpu.VMEM((FFN_TM, d), BF16)],
        compiler_params=_params(2),
        name="ffn",
    )(h, gain, wg, wu, wd, final_gain)


def _rope_tables(seq):
    pos = np.arange(seq, dtype=np.float32)

    def angles(d):
        inv = np.float32(ROPE_THETA) ** (-np.arange(0, d, 2, dtype=np.float32) / np.float32(d))
        return (pos[:, None] * inv[None, :].astype(np.float32)).astype(np.float32).astype(np.float64)

    a128 = angles(RET_DK)
    cos_r = np.concatenate([np.cos(a128), np.cos(a128)], -1)
    sin_r = np.concatenate([-np.sin(a128), np.sin(a128)], -1)
    a64 = angles(SWA_HEAD_DIM)
    c, s, z = np.cos(a64), np.sin(a64), np.zeros_like(a64)
    cos_s = np.concatenate([c, c, c, c], -1)
    sin_lo = np.concatenate([-s, z, -s, z], -1)
    sin_hi = np.concatenate([z, s, z, s], -1)
    return [jnp.asarray(v, dtype=F32) for v in (cos_r, sin_r, cos_s, sin_lo, sin_hi)]


def _inproj_kernel(h_ref, g_ref, w_ref, cr_ref, sr_ref, cs_ref, sl_ref, sh_ref,
                   rq_ref, rk_ref, rv_ref, rg_ref, sq_ref, sk_ref, sv_ref, xn_ref):
    tm = h_ref.shape[0]

    def body(r, carry):
        rows = pl.ds(pl.multiple_of(r * NORM_ROWS, NORM_ROWS), NORM_ROWS)
        xn_ref[rows, :] = _rms_rows(h_ref[rows, :], g_ref[...]).astype(BF16)
        return carry
    lax.fori_loop(0, tm // NORM_ROWS, body, 0)

    xn = xn_ref[...]
    cr, sr = cr_ref[...], sr_ref[...]
    cs, sl, sh = cs_ref[...], sl_ref[...], sh_ref[...]
    half = LANES // 2

    def rope_ret(x):
        return x * cr + pltpu.roll(x, half, 1) * sr

    def rope_swa(x):
        return x * cs + pltpu.roll(x, LANES - half // 2, 1) * sl + pltpu.roll(x, half // 2, 1) * sh

    def project(col0, width):
        return _dot(xn, w_ref[:, col0:col0 + width])

    col = 0
    for out_ref, fn in ((rq_ref, rope_ret), (rk_ref, rope_ret), (rv_ref, None), (rg_ref, None)):
        y = project(col, RET_WIDTH)
        for s in range(RET_WIDTH // LANES):
            slab = y[:, s * LANES:(s + 1) * LANES]
            if fn is not None:
                slab = fn(slab)
            out_ref[:, s * LANES:(s + 1) * LANES] = slab.astype(BF16)
        col += RET_WIDTH

    y = project(col, SWA_WIDTH)
    scale = SWA_HEAD_DIM ** -0.5
    for s in range(SWA_WIDTH // LANES):
        slab = rope_swa(y[:, s * LANES:(s + 1) * LANES]) * scale
        sq_ref[:, s * LANES:(s + 1) * LANES] = slab.astype(BF16)
    col += SWA_WIDTH

    y = project(col, 2 * SWA_KV_WIDTH)
    lo = lax.broadcasted_iota(jnp.int32, (tm, LANES), 1) < half
    for transposed, out_ref, x in ((False, sk_ref, rope_swa(y[:, :LANES])), (True, sv_ref, y[:, LANES:])):
        xr = pltpu.roll(x, half, 1)
        variants = (jnp.where(lo, x, 0.0), jnp.where(lo, 0.0, xr),
                    jnp.where(lo, xr, 0.0), jnp.where(lo, 0.0, x))
        for s, v in enumerate(variants):
            if transposed:
                out_ref[s * LANES:(s + 1) * LANES, :] = v.T.astype(BF16)
            else:
                out_ref[:, s * LANES:(s + 1) * LANES] = v.astype(BF16)


def _inproj(h, gain, w_in, seq):
    t, d = h.shape
    tables = _rope_tables(seq)
    tiles_per_seq = seq // ROW_TM
    tab_spec = pl.BlockSpec((ROW_TM, LANES), lambda i: (i % tiles_per_seq, 0))

    def row_spec(width):
        return pl.BlockSpec((ROW_TM, width), lambda i: (i, 0))

    widths = (RET_WIDTH, RET_WIDTH, RET_WIDTH, RET_WIDTH, SWA_WIDTH, 4 * LANES)
    return pl.pallas_call(
        _inproj_kernel,
        out_shape=[jax.ShapeDtypeStruct((t, w), BF16) for w in widths]
        + [jax.ShapeDtypeStruct((4 * LANES, t), BF16)],
        grid=(t // ROW_TM,),
        in_specs=[row_spec(d), _resident((1, d)), _resident((d, IN_COLS))] + [tab_spec] * 5,
        out_specs=[row_spec(w) for w in widths] + [pl.BlockSpec((4 * LANES, ROW_TM), lambda i: (0, i))],
        scratch_shapes=[pltpu.VMEM((ROW_TM, d), BF16)],
        compiler_params=_params(1),
        name="inproj",
    )(h, gain, w_in, *tables)


def _retention_tables():
    c = RET_CHUNK
    heads = np.arange(RET_HEADS, dtype=np.float64)
    log_gamma = np.log1p(-np.exp2(-5.0 - heads))
    idx = np.arange(c, dtype=np.float64)
    diff = idx[:, None] - idx[None, :]
    scale = RET_DK ** -0.5
    dmat = np.where(diff[None] >= 0, np.exp(np.maximum(diff, 0.0)[None] * log_gamma[:, None, None]), 0.0)
    zeta = np.exp((c - 1.0 - idx)[None, :] * log_gamma[:, None])
    xi = np.exp((idx + 1.0)[None, :] * log_gamma[:, None])
    chunk_decay = tuple(float(v) for v in np.exp(c * log_gamma))
    zeta_b = np.broadcast_to((zeta * scale)[:, :, None], (RET_HEADS, c, RET_DK))
    xi_b = np.broadcast_to(xi[:, :, None], (RET_HEADS, c, RET_DV))
    tabs = [jnp.asarray(v, dtype=F32) for v in (dmat * scale, zeta_b, xi_b)]
    return tabs, chunk_decay


def _ret_kernel(q_ref, k_ref, v_ref, g_ref, dmat_ref, zeta_ref, xi_ref, gain_ref, o_ref, state_ref,
                *, chunk_decay):
    n = pl.program_id(1)

    @pl.when(n == 0)
    def _():
        state_ref[...] = jnp.zeros_like(state_ref)

    for h in range(RET_HEADS):
        hs = slice(h * RET_DK, (h + 1) * RET_DK)
        q, k, v = q_ref[:, hs], k_ref[:, hs], v_ref[:, hs]
        state = state_ref[h]
        s = _dot_nt(q, k) * dmat_ref[h]
        intra = _dot(s.astype(BF16), v)
        cross = _dot(q, state.astype(BF16)) * xi_ref[h]
        kz = (k.astype(F32) * zeta_ref[h]).astype(BF16)
        state_ref[h] = state * chunk_decay[h] + _dot_tn(kz, v)
        ret = intra + cross
        mu = jnp.mean(ret, axis=-1, keepdims=True)
        cen = ret - mu
        var = jnp.mean(cen * cen, axis=-1, keepdims=True)
        y = cen * lax.rsqrt(var + EPS) * gain_ref[:, hs]
        gate = g_ref[:, hs].astype(F32)
        o_ref[:, hs] = (gate * (1.0 / (1.0 + jnp.exp(-gate))) * y).astype(BF16)


def _retention(rq, rk, rv, rg, gn_gain, batch, seq, cast_weights):
    t = rq.shape[0]
    n_chunks = seq // RET_CHUNK
    tabs, chunk_decay = _retention_tables()
    blk = pl.BlockSpec((RET_CHUNK, RET_WIDTH), lambda b, n: (b * n_chunks + n, 0))
    tab_spec = _resident((RET_HEADS, RET_CHUNK, RET_DK))
    cast_specs, cast_shapes = _cast_specs(cast_weights, batch, n_chunks)
    host = functools.partial(_ret_kernel, chunk_decay=chunk_decay)
    out = pl.pallas_call(
        _with_casts(host, 8, 1, len(cast_weights)),
        out_shape=[jax.ShapeDtypeStruct((t, RET_WIDTH), BF16)] + cast_shapes,
        grid=(batch, n_chunks),
        in_specs=[blk, blk, blk, blk, tab_spec, tab_spec, tab_spec, _resident((1, RET_WIDTH))] + cast_specs,
        out_specs=[blk] + cast_specs,
        scratch_shapes=[pltpu.VMEM((RET_HEADS, RET_DK, RET_DV), F32)],
        compiler_params=_params(2),
        name="retention",
    )(rq, rk, rv, rg, *tabs, gn_gain, *cast_weights)
    return out[0], out[1:]


def _swa_kernel(sink_ref, q_ref, kp_ref, kc_ref, vp_ref, vc_ref, o_ref):
    n = pl.program_id(1)
    w = WINDOW
    pairs = SWA_HEADS // SWA_KV_HEADS // 2
    key = lax.broadcasted_iota(jnp.int32, (2 * w, pairs * w), 0)
    qry = lax.broadcasted_iota(jnp.int32, (2 * w, pairs * w), 1) % w
    first_key = jnp.where(n == 0, w, 0)
    valid = (key > qry) & (key <= qry + w) & (key >= first_key)

    for g in range(SWA_KV_HEADS):
        slabs = [g * pairs + p for p in range(pairs)]
        q = jnp.concatenate([q_ref[:, sl * LANES:(sl + 1) * LANES] for sl in slabs], 0)
        acc = None
        for e in range(2):
            c = 2 * g + e
            k = jnp.concatenate([kp_ref[:, c * LANES:(c + 1) * LANES], kc_ref[:, c * LANES:(c + 1) * LANES]], 0)
            v_t = jnp.concatenate([vp_ref[c * LANES:(c + 1) * LANES, :], vc_ref[c * LANES:(c + 1) * LANES, :]], 1)
            sink = jnp.concatenate([jnp.full((1, w), sink_ref[2 * sl + e], F32) for sl in slabs], 1)
            s = jnp.where(valid, _dot_nt(k, q), NEG_INF)
            m = jnp.maximum(jnp.max(s, axis=0, keepdims=True), sink)
            p = jnp.exp(s - m)
            inv = 1.0 / (jnp.sum(p, axis=0, keepdims=True) + jnp.exp(sink - m))
            pv = _dot(v_t, (p * inv).astype(BF16))
            acc = pv if acc is None else acc + pv
        for i, sl in enumerate(slabs):
            o_ref[sl * LANES:(sl + 1) * LANES, :] = acc[:, i * w:(i + 1) * w].astype(BF16)


def _swa(sq, sk4, sv4_t, sinks, batch, seq, cast_weights):
    t = sq.shape[0]
    n_blocks = seq // WINDOW
    cur = lambda b, n: (b * n_blocks + n, 0)
    prev = lambda b, n: (b * n_blocks + jnp.maximum(n - 1, 0), 0)
    cur_t = lambda b, n: (0, b * n_blocks + n)
    prev_t = lambda b, n: (0, b * n_blocks + jnp.maximum(n - 1, 0))
    cast_specs, cast_shapes = _cast_specs(cast_weights, batch, n_blocks)
    out = pl.pallas_call(
        _with_casts(_swa_kernel, 6, 1, len(cast_weights)),
        out_shape=[jax.ShapeDtypeStruct((SWA_WIDTH, t), BF16)] + cast_shapes,
        grid=(batch, n_blocks),
        in_specs=[
            pl.BlockSpec(memory_space=pltpu.SMEM),
            pl.BlockSpec((WINDOW, SWA_WIDTH), cur),
            pl.BlockSpec((WINDOW, 4 * LANES), prev),
            pl.BlockSpec((WINDOW, 4 * LANES), cur),
            pl.BlockSpec((4 * LANES, WINDOW), prev_t),
            pl.BlockSpec((4 * LANES, WINDOW), cur_t),
        ] + cast_specs,
        out_specs=[pl.BlockSpec((SWA_WIDTH, WINDOW), cur_t)] + cast_specs,
        compiler_params=_params(2),
        name="swa",
    )(sinks, sq, sk4, sk4, sv4_t, sv4_t, *cast_weights)
    return out[0], out[1:]


def _outproj_kernel(h_ref, ret_ref, swa_ref, w_ref, g_ref, h2_ref, hn_ref):
    tm = h_ref.shape[0]
    y = _dot(ret_ref[...], w_ref[:RET_WIDTH, :]) + _dot_tn(swa_ref[...], w_ref[RET_WIDTH:, :])
    h2_ref[...] = h_ref[...] + y

    def body(r, carry):
        rows = pl.ds(pl.multiple_of(r * NORM_ROWS, NORM_ROWS), NORM_ROWS)
        hn_ref[rows, :] = _rms_rows(h2_ref[rows, :], g_ref[...]).astype(BF16)
        return carry
    lax.fori_loop(0, tm // NORM_ROWS, body, 0)


def _outproj(h, ret, swa, w_out, gain):
    t, d = h.shape
    row = lambda width: pl.BlockSpec((ROW_TM, width), lambda i: (i, 0))
    return pl.pallas_call(
        _outproj_kernel,
        out_shape=[jax.ShapeDtypeStruct((t, d), F32), jax.ShapeDtypeStruct((t, d), BF16)],
        grid=(t // ROW_TM,),
        in_specs=[row(d), row(RET_WIDTH), pl.BlockSpec((SWA_WIDTH, ROW_TM), lambda i: (0, i)),
                  _resident(w_out.shape), _resident((1, d))],
        out_specs=[row(d), row(d)],
        compiler_params=_params(1),
        name="outproj",
    )(h, ret, swa, w_out, gain)


def _memkv_kernel(m_ref, g_ref, w_ref, o_ref):
    xn = _rms_rows(m_ref[...], g_ref[...]).astype(BF16)
    o_ref[...] = _dot(xn, w_ref[...]).astype(BF16)


def _memkv(mem, gain, wkv):
    rows, d = mem.shape
    n_out = wkv.shape[1]
    tn = 1024
    return pl.pallas_call(
        _memkv_kernel,
        out_shape=jax.ShapeDtypeStruct((rows, n_out), BF16),
        grid=(n_out // tn,),
        in_specs=[_resident((rows, d)), _resident((1, d)), pl.BlockSpec((d, tn), lambda j: (0, j))],
        out_specs=pl.BlockSpec((rows, tn), lambda j: (0, j)),
        compiler_params=_params(1),
        name="memkv",
    )(mem, gain, wkv)


def _xattn_kernel(hn_ref, h_ref, wq_ref, wo_ref, k_ref, v_ref, o_ref, att_ref):
    q = _dot(hn_ref[...], wq_ref[...]).astype(BF16)
    scale = XA_HEAD_DIM ** -0.5
    for hd in range(XA_HEADS):
        hs = slice(hd * XA_HEAD_DIM, (hd + 1) * XA_HEAD_DIM)
        s = _dot_nt(q[:, hs], k_ref[:, hs]) * scale
        m = jnp.max(s, axis=-1, keepdims=True)
        p = jnp.exp(s - m)
        inv = 1.0 / jnp.sum(p, axis=-1, keepdims=True)
        att_ref[:, hs] = (_dot(p.astype(BF16), v_ref[:, hs]) * inv).astype(BF16)
    o_ref[...] = h_ref[...] + _dot(att_ref[...], wo_ref[...])


def _xattn(hn, h, wq, wo, mkv, seq, mem_len):
    t, d = h.shape
    tiles_per_seq = seq // ROW_TM
    row = lambda i: (i, 0)
    return pl.pallas_call(
        _xattn_kernel,
        out_shape=jax.ShapeDtypeStruct((t, d), F32),
        grid=(t // ROW_TM,),
        in_specs=[
            pl.BlockSpec((ROW_TM, d), row),
            pl.BlockSpec((ROW_TM, d), row),
            _resident((d, d)),
            _resident((d, d)),
            pl.BlockSpec((mem_len, d), lambda i: (i // tiles_per_seq, 0)),
            pl.BlockSpec((mem_len, d), lambda i: (i // tiles_per_seq, 1)),
        ],
        out_specs=pl.BlockSpec((ROW_TM, d), row),
        scratch_shapes=[pltpu.VMEM((ROW_TM, d), BF16)],
        compiler_params=_params(1),
        name="xattn",
    )(hn, h, wq, wo, mkv, mkv)


def kernel(x, mem, ffn1_norm, ffn1_w_gate, ffn1_w_up, ffn1_w_down, mix_norm, w_in, ret_gn_gain, swa_sinks,
           w_out, xa_norm, mem_norm, xa_wq, xa_wkv, xa_wo, ffn2_norm, ffn2_w_gate, ffn2_w_up, ffn2_w_down,
           final_norm):
    batch, seq, d = x.shape
    mem_len = mem.shape[1]
    depth = ffn1_norm.shape[0]
    h = x.reshape(batch * seq, d)
    mem2 = mem.reshape(batch * mem_len, d)
    bf = lambda w: w.astype(BF16)
    row = lambda g: g.reshape(1, -1).astype(F32)
    final_gain = row(final_norm)

    for l in range(depth):
        last = l == depth - 1
        h = _ffn(h, row(ffn1_norm[l]), bf(ffn1_w_gate[l]), bf(ffn1_w_up[l]), bf(ffn1_w_down[l]), final_gain,
                 final_norm=False)
        rq, rk, rv, rg, sq, sk4, sv4_t = _inproj(h, row(mix_norm[l]), bf(w_in[l]), seq)
        ret, (w2_gate, wkv) = _retention(rq, rk, rv, rg, row(ret_gn_gain[l]), batch, seq,
                                         [ffn2_w_gate[l], xa_wkv[l]])
        swa, (w2_up, w2_down, wo_mix, wq, wo) = _swa(sq, sk4, sv4_t, swa_sinks[l].astype(F32), batch, seq,
                                                     [ffn2_w_up[l], ffn2_w_down[l], w_out[l], xa_wq[l], xa_wo[l]])
        h, hn = _outproj(h, ret, swa, wo_mix, row(xa_norm[l]))
        mkv = _memkv(mem2, row(mem_norm[l]), wkv)
        h = _xattn(hn, h, wq, wo, mkv, seq, mem_len)
        h = _ffn(h, row(ffn2_norm[l]), w2_gate, w2_up, w2_down, final_gain, final_norm=last)
    if depth == 0:
        raise ValueError("depth must be at least 1")
    return h.reshape(batch, seq, d)
```

```python
import functools

import numpy as np
import jax
import jax.numpy as jnp
from jax import lax
from jax.experimental import pallas as pl
from jax.experimental.pallas import tpu as pltpu

F32 = jnp.float32
BF16 = jnp.bfloat16

D_MODEL = 2048
D_FF = 5632
RET_HEADS = 8
RET_DK = 128
RET_DV = 128
RET_WIDTH = RET_HEADS * RET_DV
RET_CHUNK = 128
SWA_HEADS = 16
SWA_KV_HEADS = 2
SWA_HEAD_DIM = 64
SWA_WIDTH = SWA_HEADS * SWA_HEAD_DIM
SWA_KV_WIDTH = SWA_KV_HEADS * SWA_HEAD_DIM
WINDOW = 128
XA_HEADS = 4
XA_HEAD_DIM = D_MODEL // XA_HEADS
ROPE_THETA = 10000.0
EPS = 1e-6
IN_SIZES = (RET_WIDTH, RET_WIDTH, RET_WIDTH, RET_WIDTH, SWA_WIDTH, SWA_KV_WIDTH, SWA_KV_WIDTH)
IN_COLS = sum(IN_SIZES)

LANES = 128
BF16_SUBLANES = 16
V7X_VMEM_BYTES = 64 * 1024 * 1024
VMEM_LIMIT_BYTES = 60000 * 1024

FFN_TM = 1024
FFN_TF = 512
FFN_HEAD_TF = 256
ROW_TM = 512
NORM_ROWS = 256
NEG_INF = float(np.finfo(np.float32).min)


def _params(n_axes):
    return pltpu.CompilerParams(
        dimension_semantics=("arbitrary",) * n_axes,
        vmem_limit_bytes=VMEM_LIMIT_BYTES,
    )


def _resident(shape):
    zeros = (0,) * len(shape)
    return pl.BlockSpec(shape, lambda *_: zeros, pipeline_mode=pl.Buffered(1))


def _rms_rows(x, gain):
    ms = jnp.mean(x * x, axis=-1, keepdims=True)
    return x * lax.rsqrt(ms + EPS) * gain


def _dot(a, b):
    return lax.dot_general(a, b, (((1,), (0,)), ((), ())), preferred_element_type=F32)


def _dot_nt(a, b):
    return lax.dot_general(a, b, (((1,), (1,)), ((), ())), preferred_element_type=F32)


def _dot_tn(a, b):
    return lax.dot_general(a, b, (((0,), (0,)), ((), ())), preferred_element_type=F32)


def _cast_block(shape, n_steps):
    rows, cols = shape
    for col_splits in (1, 2, 4, 8):
        row_blocks, rem = divmod(n_steps, col_splits)
        if rem or rows % row_blocks or cols % col_splits:
            continue
        br, bc = rows // row_blocks, cols // col_splits
        if br % BF16_SUBLANES == 0 and bc % LANES == 0:
            return br, bc, col_splits
    raise ValueError(f"no aligned {n_steps}-way split of {shape}")


def _with_casts(kernel_fn, n_in, n_out, n_cast):
    def wrapped(*refs):
        ins, rest = refs[:n_in], refs[n_in:]
        cast_in, rest = rest[:n_cast], rest[n_cast:]
        outs, rest = rest[:n_out], rest[n_out:]
        cast_out, scratch = rest[:n_cast], rest[n_cast:]
        kernel_fn(*ins, *outs, *scratch)
        for src, dst in zip(cast_in, cast_out):
            dst[...] = src[...].astype(BF16)
    return wrapped


def _cast_specs(weights, n_outer, n_inner):
    specs, shapes = [], []
    for w in weights:
        br, bc, col_splits = _cast_block(w.shape, n_outer * n_inner)

        def index_map(b, n, col_splits=col_splits):
            step = b * n_inner + n
            return step // col_splits, step % col_splits
        specs.append(pl.BlockSpec((br, bc), index_map))
        shapes.append(jax.ShapeDtypeStruct(w.shape, BF16))
    return specs, shapes


def _rows_loop(n_rows, fn):
    def body(r, carry):
        fn(pl.ds(pl.multiple_of(r * NORM_ROWS, NORM_ROWS), NORM_ROWS))
        return carry
    lax.fori_loop(0, n_rows // NORM_ROWS, body, 0)


def _ffn_prologue(h_ref, g_ref, o_ref, xn_ref):
    def norm(rows):
        h = h_ref[rows, :]
        xn_ref[rows, :] = _rms_rows(h, g_ref[...]).astype(BF16)
        o_ref[rows, :] = h
    _rows_loop(h_ref.shape[0], norm)


def _ffn_step(xn_ref, wg, wu, wd, o_ref):
    xn = xn_ref[...]
    g = _dot(xn, wg)
    u = _dot(xn, wu)
    a = (g * (0.5 / (1.0 + jnp.exp(-g))) * u).astype(BF16)
    for c in range(0, o_ref.shape[1], FFN_TF):
        o_ref[:, c:c + FFN_TF] += _dot(a, wd[:, c:c + FFN_TF])


def _ffn_kernel(*refs, n_ff, final_norm, aliased):
    if aliased:
        refs = refs[1:]
    h_ref, g_ref, wg_ref, wu_ref, wd_ref, fg_ref, o_ref, xn_ref = refs
    j = pl.program_id(1)

    @pl.when(j == 0)
    def _():
        _ffn_prologue(h_ref, g_ref, o_ref, xn_ref)

    _ffn_step(xn_ref, wg_ref[...], wu_ref[...], wd_ref[...], o_ref)

    if final_norm:
        @pl.when(j == n_ff - 1)
        def _():
            def norm(rows):
                o_ref[rows, :] = _rms_rows(o_ref[rows, :], fg_ref[...])
            _rows_loop(o_ref.shape[0], norm)


def _ffn(h, gain, wg, wu, wd, final_gain, *, final_norm, first_tile=0, into=None):
    t, d = h.shape
    f = wg.shape[1]
    n_ff = f // FFN_TF
    aliased = into is not None
    tile = lambda i, j: (i + first_tile, 0)
    in_specs = [
        pl.BlockSpec((FFN_TM, d), tile),
        pl.BlockSpec((1, d), lambda i, j: (0, 0)),
        pl.BlockSpec((d, FFN_TF), lambda i, j: (0, j)),
        pl.BlockSpec((d, FFN_TF), lambda i, j: (0, j)),
        pl.BlockSpec((FFN_TF, d), lambda i, j: (j, 0)),
        pl.BlockSpec((1, d), lambda i, j: (0, 0)),
    ]
    args = (h, gain, wg, wu, wd, final_gain)
    if aliased:
        in_specs = [pl.BlockSpec(memory_space=pl.ANY)] + in_specs
        args = (into,) + args
    return pl.pallas_call(
        functools.partial(_ffn_kernel, n_ff=n_ff, final_norm=final_norm, aliased=aliased),
        out_shape=jax.ShapeDtypeStruct((t, d), F32),
        grid=(t // FFN_TM - first_tile, n_ff),
        in_specs=in_specs,
        out_specs=pl.BlockSpec((FFN_TM, d), tile),
        scratch_shapes=[pltpu.VMEM((FFN_TM, d), BF16)],
        input_output_aliases={0: 0} if aliased else {},
        compiler_params=_params(2),
        name="ffn",
    )(*args)


def _ffn_head_kernel(h_ref, g_ref, wg_ref, wu_ref, wd_ref, o_ref, wg_out, wu_out, wd_out, xn_ref):
    @pl.when(pl.program_id(0) == 0)
    def _():
        _ffn_prologue(h_ref, g_ref, o_ref, xn_ref)

    wg, wu, wd = (w[...].astype(BF16) for w in (wg_ref, wu_ref, wd_ref))
    wg_out[...] = wg
    wu_out[...] = wu
    wd_out[...] = wd
    _ffn_step(xn_ref, wg, wu, wd, o_ref)


def _ffn_head(h, gain, wg, wu, wd):
    t, d = h.shape
    f = wg.shape[1]
    tf = FFN_HEAD_TF
    col = pl.BlockSpec((d, tf), lambda j: (0, j))
    row = pl.BlockSpec((tf, d), lambda j: (j, 0))
    tile0 = pl.BlockSpec((FFN_TM, d), lambda j: (0, 0))
    return pl.pallas_call(
        _ffn_head_kernel,
        out_shape=[jax.ShapeDtypeStruct((t, d), F32), jax.ShapeDtypeStruct(wg.shape, BF16),
                   jax.ShapeDtypeStruct(wu.shape, BF16), jax.ShapeDtypeStruct(wd.shape, BF16)],
        grid=(f // tf,),
        in_specs=[tile0, pl.BlockSpec((1, d), lambda j: (0, 0)), col, col, row],
        out_specs=[tile0, col, col, row],
        scratch_shapes=[pltpu.VMEM((FFN_TM, d), BF16)],
        compiler_params=_params(1),
        name="ffn_head",
    )(h, gain, wg, wu, wd)


def _rope_tables(seq):
    pos = np.arange(seq, dtype=np.float32)

    def angles(d):
        inv = np.float32(ROPE_THETA) ** (-np.arange(0, d, 2, dtype=np.float32) / np.float32(d))
        return (pos[:, None] * inv[None, :].astype(np.float32)).astype(np.float32).astype(np.float64)

    a128 = angles(RET_DK)
    cos_r = np.concatenate([np.cos(a128), np.cos(a128)], -1)
    sin_r = np.concatenate([-np.sin(a128), np.sin(a128)], -1)
    a64 = angles(SWA_HEAD_DIM)
    c, s, z = np.cos(a64), np.sin(a64), np.zeros_like(a64)
    cos_s = np.concatenate([c, c, c, c], -1)
    sin_lo = np.concatenate([-s, z, -s, z], -1)
    sin_hi = np.concatenate([z, s, z, s], -1)
    return [jnp.asarray(v, dtype=F32) for v in (cos_r, sin_r, cos_s, sin_lo, sin_hi)]


def _inproj_kernel(h_ref, g_ref, w_ref, cr_ref, sr_ref, cs_ref, sl_ref, sh_ref,
                   rq_ref, rk_ref, rv_ref, rg_ref, sq_ref, sk_ref, sv_ref, xn_ref):
    tm = h_ref.shape[0]

    def body(r, carry):
        rows = pl.ds(pl.multiple_of(r * NORM_ROWS, NORM_ROWS), NORM_ROWS)
        xn_ref[rows, :] = _rms_rows(h_ref[rows, :], g_ref[...]).astype(BF16)
        return carry
    lax.fori_loop(0, tm // NORM_ROWS, body, 0)

    xn = xn_ref[...]
    cr, sr = cr_ref[...], sr_ref[...]
    cs, sl, sh = cs_ref[...], sl_ref[...], sh_ref[...]
    half = LANES // 2

    def rope_ret(x):
        return x * cr + pltpu.roll(x, half, 1) * sr

    def rope_swa(x):
        return x * cs + pltpu.roll(x, LANES - half // 2, 1) * sl + pltpu.roll(x, half // 2, 1) * sh

    def project(col0, width):
        return _dot(xn, w_ref[:, col0:col0 + width])

    col = 0
    for out_ref, fn in ((rq_ref, rope_ret), (rk_ref, rope_ret), (rv_ref, None), (rg_ref, None)):
        y = project(col, RET_WIDTH)
        for s in range(RET_WIDTH // LANES):
            slab = y[:, s * LANES:(s + 1) * LANES]
            if fn is not None:
                slab = fn(slab)
            out_ref[:, s * LANES:(s + 1) * LANES] = slab.astype(BF16)
        col += RET_WIDTH

    y = project(col, SWA_WIDTH)
    scale = SWA_HEAD_DIM ** -0.5
    for s in range(SWA_WIDTH // LANES):
        slab = rope_swa(y[:, s * LANES:(s + 1) * LANES]) * scale
        sq_ref[:, s * LANES:(s + 1) * LANES] = slab.astype(BF16)
    col += SWA_WIDTH

    y = project(col, 2 * SWA_KV_WIDTH)
    lo = lax.broadcasted_iota(jnp.int32, (tm, LANES), 1) < half
    for transposed, out_ref, x in ((False, sk_ref, rope_swa(y[:, :LANES])), (True, sv_ref, y[:, LANES:])):
        xr = pltpu.roll(x, half, 1)
        variants = (jnp.where(lo, x, 0.0), jnp.where(lo, 0.0, xr),
                    jnp.where(lo, xr, 0.0), jnp.where(lo, 0.0, x))
        for s, v in enumerate(variants):
            if transposed:
                out_ref[s * LANES:(s + 1) * LANES, :] = v.T.astype(BF16)
            else:
                out_ref[:, s * LANES:(s + 1) * LANES] = v.astype(BF16)


def _inproj(h, gain, w_in, seq):
    t, d = h.shape
    tables = _rope_tables(seq)
    tiles_per_seq = seq // ROW_TM
    tab_spec = pl.BlockSpec((ROW_TM, LANES), lambda i: (i % tiles_per_seq, 0))

    def row_spec(width):
        return pl.BlockSpec((ROW_TM, width), lambda i: (i, 0))

    widths = (RET_WIDTH, RET_WIDTH, RET_WIDTH, RET_WIDTH, SWA_WIDTH, 4 * LANES)
    return pl.pallas_call(
        _inproj_kernel,
        out_shape=[jax.ShapeDtypeStruct((t, w), BF16) for w in widths]
        + [jax.ShapeDtypeStruct((4 * LANES, t), BF16)],
        grid=(t // ROW_TM,),
        in_specs=[row_spec(d), _resident((1, d)), _resident((d, IN_COLS))] + [tab_spec] * 5,
        out_specs=[row_spec(w) for w in widths] + [pl.BlockSpec((4 * LANES, ROW_TM), lambda i: (0, i))],
        scratch_shapes=[pltpu.VMEM((ROW_TM, d), BF16)],
        compiler_params=_params(1),
        name="inproj",
    )(h, gain, w_in, *tables)


def _retention_tables():
    c = RET_CHUNK
    heads = np.arange(RET_HEADS, dtype=np.float64)
    log_gamma = np.log1p(-np.exp2(-5.0 - heads))
    idx = np.arange(c, dtype=np.float64)
    diff = idx[:, None] - idx[None, :]
    scale = RET_DK ** -0.5
    dmat = np.where(diff[None] >= 0, np.exp(np.maximum(diff, 0.0)[None] * log_gamma[:, None, None]), 0.0)
    zeta = np.exp((c - 1.0 - idx)[None, :] * log_gamma[:, None])
    xi = np.exp((idx + 1.0)[None, :] * log_gamma[:, None])
    chunk_decay = tuple(float(v) for v in np.exp(c * log_gamma))
    zeta_b = np.broadcast_to((zeta * scale)[:, :, None], (RET_HEADS, c, RET_DK))
    xi_b = np.broadcast_to(xi[:, :, None], (RET_HEADS, c, RET_DV))
    tabs = [jnp.asarray(v, dtype=F32) for v in (dmat * scale, zeta_b, xi_b)]
    return tabs, chunk_decay


def _ret_kernel(q_ref, k_ref, v_ref, g_ref, dmat_ref, zeta_ref, xi_ref, gain_ref, o_ref, state_ref,
                *, chunk_decay):
    n = pl.program_id(1)

    @pl.when(n == 0)
    def _():
        state_ref[...] = jnp.zeros_like(state_ref)

    for h in range(RET_HEADS):
        hs = slice(h * RET_DK, (h + 1) * RET_DK)
        q, k, v = q_ref[:, hs], k_ref[:, hs], v_ref[:, hs]
        state = state_ref[h]
        s = _dot_nt(q, k) * dmat_ref[h]
        intra = _dot(s.astype(BF16), v)
        cross = _dot(q, state.astype(BF16)) * xi_ref[h]
        kz = (k.astype(F32) * zeta_ref[h]).astype(BF16)
        state_ref[h] = state * chunk_decay[h] + _dot_tn(kz, v)
        ret = intra + cross
        mu = jnp.mean(ret, axis=-1, keepdims=True)
        cen = ret - mu
        var = jnp.mean(cen * cen, axis=-1, keepdims=True)
        y = cen * lax.rsqrt(var + EPS) * gain_ref[:, hs]
        gate = g_ref[:, hs].astype(F32)
        o_ref[:, hs] = (gate * (1.0 / (1.0 + jnp.exp(-gate))) * y).astype(BF16)


def _retention(rq, rk, rv, rg, gn_gain, batch, seq, cast_weights):
    t = rq.shape[0]
    n_chunks = seq // RET_CHUNK
    tabs, chunk_decay = _retention_tables()
    blk = pl.BlockSpec((RET_CHUNK, RET_WIDTH), lambda b, n: (b * n_chunks + n, 0))
    tab_spec = _resident((RET_HEADS, RET_CHUNK, RET_DK))
    cast_specs, cast_shapes = _cast_specs(cast_weights, batch, n_chunks)
    host = functools.partial(_ret_kernel, chunk_decay=chunk_decay)
    out = pl.pallas_call(
        _with_casts(host, 8, 1, len(cast_weights)),
        out_shape=[jax.ShapeDtypeStruct((t, RET_WIDTH), BF16)] + cast_shapes,
        grid=(batch, n_chunks),
        in_specs=[blk, blk, blk, blk, tab_spec, tab_spec, tab_spec, _resident((1, RET_WIDTH))] + cast_specs,
        out_specs=[blk] + cast_specs,
        scratch_shapes=[pltpu.VMEM((RET_HEADS, RET_DK, RET_DV), F32)],
        compiler_params=_params(2),
        name="retention",
    )(rq, rk, rv, rg, *tabs, gn_gain, *cast_weights)
    return out[0], out[1:]


def _swa_kernel(sink_ref, q_ref, kp_ref, kc_ref, vp_ref, vc_ref, o_ref):
    n = pl.program_id(1)
    w = WINDOW
    pairs = SWA_HEADS // SWA_KV_HEADS // 2
    key = lax.broadcasted_iota(jnp.int32, (2 * w, pairs * w), 0)
    qry = lax.broadcasted_iota(jnp.int32, (2 * w, pairs * w), 1) % w
    first_key = jnp.where(n == 0, w, 0)
    valid = (key > qry) & (key <= qry + w) & (key >= first_key)

    for g in range(SWA_KV_HEADS):
        slabs = [g * pairs + p for p in range(pairs)]
        q = jnp.concatenate([q_ref[:, sl * LANES:(sl + 1) * LANES] for sl in slabs], 0)
        acc = None
        for e in range(2):
            c = 2 * g + e
            k = jnp.concatenate([kp_ref[:, c * LANES:(c + 1) * LANES], kc_ref[:, c * LANES:(c + 1) * LANES]], 0)
            v_t = jnp.concatenate([vp_ref[c * LANES:(c + 1) * LANES, :], vc_ref[c * LANES:(c + 1) * LANES, :]], 1)
            sink = jnp.concatenate([jnp.full((1, w), sink_ref[2 * sl + e], F32) for sl in slabs], 1)
            s = jnp.where(valid, _dot_nt(k, q), NEG_INF)
            m = jnp.maximum(jnp.max(s, axis=0, keepdims=True), sink)
            p = jnp.exp(s - m)
            inv = 1.0 / (jnp.sum(p, axis=0, keepdims=True) + jnp.exp(sink - m))
            pv = _dot(v_t, (p * inv).astype(BF16))
            acc = pv if acc is None else acc + pv
        for i, sl in enumerate(slabs):
            o_ref[sl * LANES:(sl + 1) * LANES, :] = acc[:, i * w:(i + 1) * w].astype(BF16)


def _swa(sq, sk4, sv4_t, sinks, batch, seq, cast_weights):
    t = sq.shape[0]
    n_blocks = seq // WINDOW
    cur = lambda b, n: (b * n_blocks + n, 0)
    prev = lambda b, n: (b * n_blocks + jnp.maximum(n - 1, 0), 0)
    cur_t = lambda b, n: (0, b * n_blocks + n)
    prev_t = lambda b, n: (0, b * n_blocks + jnp.maximum(n - 1, 0))
    cast_specs, cast_shapes = _cast_specs(cast_weights, batch, n_blocks)
    out = pl.pallas_call(
        _with_casts(_swa_kernel, 6, 1, len(cast_weights)),
        out_shape=[jax.ShapeDtypeStruct((SWA_WIDTH, t), BF16)] + cast_shapes,
        grid=(batch, n_blocks),
        in_specs=[
            pl.BlockSpec(memory_space=pltpu.SMEM),
            pl.BlockSpec((WINDOW, SWA_WIDTH), cur),
            pl.BlockSpec((WINDOW, 4 * LANES), prev),
            pl.BlockSpec((WINDOW, 4 * LANES), cur),
            pl.BlockSpec((4 * LANES, WINDOW), prev_t),
            pl.BlockSpec((4 * LANES, WINDOW), cur_t),
        ] + cast_specs,
        out_specs=[pl.BlockSpec((SWA_WIDTH, WINDOW), cur_t)] + cast_specs,
        compiler_params=_params(2),
        name="swa",
    )(sinks, sq, sk4, sk4, sv4_t, sv4_t, *cast_weights)
    return out[0], out[1:]


def _outproj_kernel(h_ref, ret_ref, swa_ref, w_ref, g_ref, h2_ref, hn_ref):
    tm = h_ref.shape[0]
    y = _dot(ret_ref[...], w_ref[:RET_WIDTH, :]) + _dot_tn(swa_ref[...], w_ref[RET_WIDTH:, :])
    h2_ref[...] = h_ref[...] + y

    def body(r, carry):
        rows = pl.ds(pl.multiple_of(r * NORM_ROWS, NORM_ROWS), NORM_ROWS)
        hn_ref[rows, :] = _rms_rows(h2_ref[rows, :], g_ref[...]).astype(BF16)
        return carry
    lax.fori_loop(0, tm // NORM_ROWS, body, 0)


def _outproj(h, ret, swa, w_out, gain):
    t, d = h.shape
    row = lambda width: pl.BlockSpec((ROW_TM, width), lambda i: (i, 0))
    return pl.pallas_call(
        _outproj_kernel,
        out_shape=[jax.ShapeDtypeStruct((t, d), F32), jax.ShapeDtypeStruct((t, d), BF16)],
        grid=(t // ROW_TM,),
        in_specs=[row(d), row(RET_WIDTH), pl.BlockSpec((SWA_WIDTH, ROW_TM), lambda i: (0, i)),
                  _resident(w_out.shape), _resident((1, d))],
        out_specs=[row(d), row(d)],
        compiler_params=_params(1),
        name="outproj",
    )(h, ret, swa, w_out, gain)


def _memkv_kernel(m_ref, g_ref, w_ref, o_ref):
    xn = _rms_rows(m_ref[...], g_ref[...]).astype(BF16)
    o_ref[...] = _dot(xn, w_ref[...].astype(BF16)).astype(BF16)


def _memkv(mem, gain, wkv):
    rows, d = mem.shape
    n_out = wkv.shape[1]
    tn = 1024
    return pl.pallas_call(
        _memkv_kernel,
        out_shape=jax.ShapeDtypeStruct((rows, n_out), BF16),
        grid=(n_out // tn,),
        in_specs=[_resident((rows, d)), _resident((1, d)), pl.BlockSpec((d, tn), lambda j: (0, j))],
        out_specs=pl.BlockSpec((rows, tn), lambda j: (0, j)),
        compiler_params=_params(1),
        name="memkv",
    )(mem, gain, wkv)


def _xattn_kernel(hn_ref, h_ref, wq_ref, wo_ref, k_ref, v_ref, o_ref, att_ref):
    q = _dot(hn_ref[...], wq_ref[...]).astype(BF16)
    scale = XA_HEAD_DIM ** -0.5
    for hd in range(XA_HEADS):
        hs = slice(hd * XA_HEAD_DIM, (hd + 1) * XA_HEAD_DIM)
        s = _dot_nt(q[:, hs], k_ref[:, hs]) * scale
        m = jnp.max(s, axis=-1, keepdims=True)
        p = jnp.exp(s - m)
        inv = 1.0 / jnp.sum(p, axis=-1, keepdims=True)
        att_ref[:, hs] = (_dot(p.astype(BF16), v_ref[:, hs]) * inv).astype(BF16)
    o_ref[...] = h_ref[...] + _dot(att_ref[...], wo_ref[...])


def _xattn(hn, h, wq, wo, mkv, seq, mem_len):
    t, d = h.shape
    tiles_per_seq = seq // ROW_TM
    row = lambda i: (i, 0)
    return pl.pallas_call(
        _xattn_kernel,
        out_shape=jax.ShapeDtypeStruct((t, d), F32),
        grid=(t // ROW_TM,),
        in_specs=[
            pl.BlockSpec((ROW_TM, d), row),
            pl.BlockSpec((ROW_TM, d), row),
            _resident((d, d)),
            _resident((d, d)),
            pl.BlockSpec((mem_len, d), lambda i: (i // tiles_per_seq, 0)),
            pl.BlockSpec((mem_len, d), lambda i: (i // tiles_per_seq, 1)),
        ],
        out_specs=pl.BlockSpec((ROW_TM, d), row),
        scratch_shapes=[pltpu.VMEM((ROW_TM, d), BF16)],
        compiler_params=_params(1),
        name="xattn",
    )(hn, h, wq, wo, mkv, mkv)


def kernel(x, mem, ffn1_norm, ffn1_w_gate, ffn1_w_up, ffn1_w_down, mix_norm, w_in, ret_gn_gain, swa_sinks,
           w_out, xa_norm, mem_norm, xa_wq, xa_wkv, xa_wo, ffn2_norm, ffn2_w_gate, ffn2_w_up, ffn2_w_down,
           final_norm):
    batch, seq, d = x.shape
    mem_len = mem.shape[1]
    depth = ffn1_norm.shape[0]
    h = x.reshape(batch * seq, d)
    mem2 = mem.reshape(batch * mem_len, d)
    bf = lambda w: w.astype(BF16)
    row = lambda g: g.reshape(1, -1).astype(F32)
    final_gain = row(final_norm)

    for l in range(depth):
        last = l == depth - 1
        h1, w1_gate, w1_up, w1_down = _ffn_head(h, row(ffn1_norm[l]), ffn1_w_gate[l], ffn1_w_up[l], ffn1_w_down[l])
        h = _ffn(h, row(ffn1_norm[l]), w1_gate, w1_up, w1_down, final_gain, final_norm=False, first_tile=1, into=h1)
        rq, rk, rv, rg, sq, sk4, sv4_t = _inproj(h, row(mix_norm[l]), bf(w_in[l]), seq)
        ret, (w2_gate,) = _retention(rq, rk, rv, rg, row(ret_gn_gain[l]), batch, seq, [ffn2_w_gate[l]])
        swa, (w2_up, w2_down, wo_mix, wq, wo) = _swa(sq, sk4, sv4_t, swa_sinks[l].astype(F32), batch, seq,
                                                     [ffn2_w_up[l], ffn2_w_down[l], w_out[l], xa_wq[l], xa_wo[l]])
        h, hn = _outproj(h, ret, swa, wo_mix, row(xa_norm[l]))
        mkv = _memkv(mem2, row(mem_norm[l]), xa_wkv[l])
        h = _xattn(hn, h, wq, wo, mkv, seq, mem_len)
        h = _ffn(h, row(ffn2_norm[l]), w2_gate, w2_up, w2_down, final_gain, final_norm=last)
    if depth == 0:
        raise ValueError("depth must be at least 1")
    return h.reshape(batch, seq, d)
```

```python
import functools

import numpy as np
import jax
import jax.numpy as jnp
from jax import lax
from jax.experimental import pallas as pl
from jax.experimental.pallas import tpu as pltpu

F32 = jnp.float32
BF16 = jnp.bfloat16

D_MODEL = 2048
D_FF = 5632
RET_HEADS = 8
RET_DK = 128
RET_DV = 128
RET_WIDTH = RET_HEADS * RET_DV
RET_CHUNK = 128
SWA_HEADS = 16
SWA_KV_HEADS = 2
SWA_HEAD_DIM = 64
SWA_WIDTH = SWA_HEADS * SWA_HEAD_DIM
SWA_KV_WIDTH = SWA_KV_HEADS * SWA_HEAD_DIM
WINDOW = 128
XA_HEADS = 4
XA_HEAD_DIM = D_MODEL // XA_HEADS
ROPE_THETA = 10000.0
EPS = 1e-6
IN_SIZES = (RET_WIDTH, RET_WIDTH, RET_WIDTH, RET_WIDTH, SWA_WIDTH, SWA_KV_WIDTH, SWA_KV_WIDTH)
IN_COLS = sum(IN_SIZES)

LANES = 128
BF16_SUBLANES = 16
V7X_VMEM_BYTES = 64 * 1024 * 1024
VMEM_LIMIT_BYTES = 60000 * 1024

FFN_TM = 1024
FFN_TF = 512
FFN_HEAD_TF = 256
ROW_TM = 512
NORM_ROWS = 256
NEG_INF = float(np.finfo(np.float32).min)


def _params(n_axes):
    return pltpu.CompilerParams(
        dimension_semantics=("arbitrary",) * n_axes,
        vmem_limit_bytes=VMEM_LIMIT_BYTES,
    )


def _resident(shape):
    zeros = (0,) * len(shape)
    return pl.BlockSpec(shape, lambda *_: zeros, pipeline_mode=pl.Buffered(1))


def _rms_rows(x, gain):
    ms = jnp.mean(x * x, axis=-1, keepdims=True)
    return x * lax.rsqrt(ms + EPS) * gain


def _dot(a, b):
    return lax.dot_general(a, b, (((1,), (0,)), ((), ())), preferred_element_type=F32)


def _dot_nt(a, b):
    return lax.dot_general(a, b, (((1,), (1,)), ((), ())), preferred_element_type=F32)


def _dot_tn(a, b):
    return lax.dot_general(a, b, (((0,), (0,)), ((), ())), preferred_element_type=F32)


def _cast_block(shape, n_steps):
    rows, cols = shape
    for col_splits in (1, 2, 4, 8):
        row_blocks, rem = divmod(n_steps, col_splits)
        if rem or rows % row_blocks or cols % col_splits:
            continue
        br, bc = rows // row_blocks, cols // col_splits
        if br % BF16_SUBLANES == 0 and bc % LANES == 0:
            return br, bc, col_splits
    raise ValueError(f"no aligned {n_steps}-way split of {shape}")


def _with_casts(kernel_fn, n_in, n_out, n_cast):
    def wrapped(*refs):
        ins, rest = refs[:n_in], refs[n_in:]
        cast_in, rest = rest[:n_cast], rest[n_cast:]
        outs, rest = rest[:n_out], rest[n_out:]
        cast_out, scratch = rest[:n_cast], rest[n_cast:]
        kernel_fn(*ins, *outs, *scratch)
        for src, dst in zip(cast_in, cast_out):
            dst[...] = src[...].astype(BF16)
    return wrapped


def _cast_specs(weights, grid):
    n_steps = int(np.prod(grid))
    specs, shapes = [], []
    for w in weights:
        br, bc, col_splits = _cast_block(w.shape, n_steps)

        def index_map(*idx, col_splits=col_splits):
            step = idx[0]
            for size, i in zip(grid[1:], idx[1:]):
                step = step * size + i
            return step // col_splits, step % col_splits
        specs.append(pl.BlockSpec((br, bc), index_map))
        shapes.append(jax.ShapeDtypeStruct(w.shape, BF16))
    return specs, shapes


def _cast_specs_tiled(weights, grid):
    n_outer, n_inner = grid
    specs, shapes = [], []
    for w in weights:
        rows, cols = w.shape
        bc = cols // n_outer
        row_blocks = max(r for r in range(1, n_inner + 1) if rows % r == 0 and (rows // r) % BF16_SUBLANES == 0)
        if cols % n_outer or bc % LANES:
            raise ValueError(f"no aligned split of {w.shape} over {grid}")

        def index_map(i, j, row_blocks=row_blocks):
            return jnp.minimum(j, row_blocks - 1), i
        specs.append(pl.BlockSpec((rows // row_blocks, bc), index_map))
        shapes.append(jax.ShapeDtypeStruct(w.shape, BF16))
    return specs, shapes


def _rows_loop(n_rows, fn):
    def body(r, carry):
        fn(pl.ds(pl.multiple_of(r * NORM_ROWS, NORM_ROWS), NORM_ROWS))
        return carry
    lax.fori_loop(0, n_rows // NORM_ROWS, body, 0)


def _ffn_prologue(h_ref, g_ref, o_ref, xn_ref):
    def norm(rows):
        h = h_ref[rows, :]
        xn_ref[rows, :] = _rms_rows(h, g_ref[...]).astype(BF16)
        o_ref[rows, :] = h
    _rows_loop(h_ref.shape[0], norm)


def _ffn_step(xn_ref, wg, wu, wd, o_ref):
    xn = xn_ref[...]
    g = _dot(xn, wg)
    u = _dot(xn, wu)
    a = (g * (0.5 / (1.0 + jnp.exp(-g))) * u).astype(BF16)
    for c in range(0, o_ref.shape[1], FFN_TF):
        o_ref[:, c:c + FFN_TF] += _dot(a, wd[:, c:c + FFN_TF])


def _ffn_kernel(*refs, n_ff, final_norm, aliased):
    if aliased:
        refs = refs[1:]
    h_ref, g_ref, wg_ref, wu_ref, wd_ref, fg_ref, o_ref, xn_ref = refs
    j = pl.program_id(1)

    @pl.when(j == 0)
    def _():
        _ffn_prologue(h_ref, g_ref, o_ref, xn_ref)

    _ffn_step(xn_ref, wg_ref[...], wu_ref[...], wd_ref[...], o_ref)

    if final_norm:
        @pl.when(j == n_ff - 1)
        def _():
            def norm(rows):
                o_ref[rows, :] = _rms_rows(o_ref[rows, :], fg_ref[...])
            _rows_loop(o_ref.shape[0], norm)


def _ffn(h, gain, wg, wu, wd, final_gain, *, final_norm, first_tile=0, into=None, cast_weights=()):
    t, d = h.shape
    f = wg.shape[1]
    n_ff = f // FFN_TF
    grid = (t // FFN_TM - first_tile, n_ff)
    aliased = into is not None
    tile = lambda i, j: (i + first_tile, 0)
    in_specs = [
        pl.BlockSpec((FFN_TM, d), tile),
        pl.BlockSpec((1, d), lambda i, j: (0, 0)),
        pl.BlockSpec((d, FFN_TF), lambda i, j: (0, j)),
        pl.BlockSpec((d, FFN_TF), lambda i, j: (0, j)),
        pl.BlockSpec((FFN_TF, d), lambda i, j: (j, 0)),
        pl.BlockSpec((1, d), lambda i, j: (0, 0)),
    ]
    args = (h, gain, wg, wu, wd, final_gain)
    if aliased:
        in_specs = [pl.BlockSpec(memory_space=pl.ANY)] + in_specs
        args = (into,) + args
    cast_specs, cast_shapes = _cast_specs_tiled(cast_weights, grid)
    host = functools.partial(_ffn_kernel, n_ff=n_ff, final_norm=final_norm, aliased=aliased)
    out = pl.pallas_call(
        _with_casts(host, len(in_specs), 1, len(cast_weights)),
        out_shape=[jax.ShapeDtypeStruct((t, d), F32)] + cast_shapes,
        grid=grid,
        in_specs=in_specs + cast_specs,
        out_specs=[pl.BlockSpec((FFN_TM, d), tile)] + cast_specs,
        scratch_shapes=[pltpu.VMEM((FFN_TM, d), BF16)],
        input_output_aliases={0: 0} if aliased else {},
        compiler_params=_params(2),
        name="ffn",
    )(*args, *cast_weights)
    return out[0], out[1:]


def _ffn_head_kernel(h_ref, g_ref, wg_ref, wu_ref, wd_ref, o_ref, wg_out, wu_out, wd_out, xn_ref):
    @pl.when(pl.program_id(0) == 0)
    def _():
        _ffn_prologue(h_ref, g_ref, o_ref, xn_ref)

    wg, wu, wd = (w[...].astype(BF16) for w in (wg_ref, wu_ref, wd_ref))
    wg_out[...] = wg
    wu_out[...] = wu
    wd_out[...] = wd
    _ffn_step(xn_ref, wg, wu, wd, o_ref)


def _ffn_head(h, gain, wg, wu, wd):
    t, d = h.shape
    f = wg.shape[1]
    tf = FFN_HEAD_TF
    col = pl.BlockSpec((d, tf), lambda j: (0, j))
    row = pl.BlockSpec((tf, d), lambda j: (j, 0))
    tile0 = pl.BlockSpec((FFN_TM, d), lambda j: (0, 0))
    return pl.pallas_call(
        _ffn_head_kernel,
        out_shape=[jax.ShapeDtypeStruct((t, d), F32), jax.ShapeDtypeStruct(wg.shape, BF16),
                   jax.ShapeDtypeStruct(wu.shape, BF16), jax.ShapeDtypeStruct(wd.shape, BF16)],
        grid=(f // tf,),
        in_specs=[tile0, pl.BlockSpec((1, d), lambda j: (0, 0)), col, col, row],
        out_specs=[tile0, col, col, row],
        scratch_shapes=[pltpu.VMEM((FFN_TM, d), BF16)],
        compiler_params=_params(1),
        name="ffn_head",
    )(h, gain, wg, wu, wd)


def _rope_tables(seq):
    pos = np.arange(seq, dtype=np.float32)

    def angles(d):
        inv = np.float32(ROPE_THETA) ** (-np.arange(0, d, 2, dtype=np.float32) / np.float32(d))
        return (pos[:, None] * inv[None, :].astype(np.float32)).astype(np.float32).astype(np.float64)

    a128 = angles(RET_DK)
    cos_r = np.concatenate([np.cos(a128), np.cos(a128)], -1)
    sin_r = np.concatenate([-np.sin(a128), np.sin(a128)], -1)
    a64 = angles(SWA_HEAD_DIM)
    c, s, z = np.cos(a64), np.sin(a64), np.zeros_like(a64)
    cos_s = np.concatenate([c, c, c, c], -1)
    sin_lo = np.concatenate([-s, z, -s, z], -1)
    sin_hi = np.concatenate([z, s, z, s], -1)
    return [jnp.asarray(v, dtype=F32) for v in (cos_r, sin_r, cos_s, sin_lo, sin_hi)]


def _inproj_kernel(h_ref, g_ref, w_ref, cr_ref, sr_ref, cs_ref, sl_ref, sh_ref,
                   rq_ref, rk_ref, rv_ref, rg_ref, sq_ref, sk_ref, sv_ref, xn_ref):
    tm = h_ref.shape[0]

    def body(r, carry):
        rows = pl.ds(pl.multiple_of(r * NORM_ROWS, NORM_ROWS), NORM_ROWS)
        xn_ref[rows, :] = _rms_rows(h_ref[rows, :], g_ref[...]).astype(BF16)
        return carry
    lax.fori_loop(0, tm // NORM_ROWS, body, 0)

    xn = xn_ref[...]
    cr, sr = cr_ref[...], sr_ref[...]
    cs, sl, sh = cs_ref[...], sl_ref[...], sh_ref[...]
    half = LANES // 2

    def rope_ret(x):
        return x * cr + pltpu.roll(x, half, 1) * sr

    def rope_swa(x):
        return x * cs + pltpu.roll(x, LANES - half // 2, 1) * sl + pltpu.roll(x, half // 2, 1) * sh

    def project(col0, width):
        return _dot(xn, w_ref[:, col0:col0 + width])

    col = 0
    for out_ref, fn in ((rq_ref, rope_ret), (rk_ref, rope_ret), (rv_ref, None), (rg_ref, None)):
        y = project(col, RET_WIDTH)
        for s in range(RET_WIDTH // LANES):
            slab = y[:, s * LANES:(s + 1) * LANES]
            if fn is not None:
                slab = fn(slab)
            out_ref[:, s * LANES:(s + 1) * LANES] = slab.astype(BF16)
        col += RET_WIDTH

    y = project(col, SWA_WIDTH)
    scale = SWA_HEAD_DIM ** -0.5
    for s in range(SWA_WIDTH // LANES):
        slab = rope_swa(y[:, s * LANES:(s + 1) * LANES]) * scale
        sq_ref[:, s * LANES:(s + 1) * LANES] = slab.astype(BF16)
    col += SWA_WIDTH

    y = project(col, 2 * SWA_KV_WIDTH)
    lo = lax.broadcasted_iota(jnp.int32, (tm, LANES), 1) < half
    for transposed, out_ref, x in ((False, sk_ref, rope_swa(y[:, :LANES])), (True, sv_ref, y[:, LANES:])):
        xr = pltpu.roll(x, half, 1)
        variants = (jnp.where(lo, x, 0.0), jnp.where(lo, 0.0, xr),
                    jnp.where(lo, xr, 0.0), jnp.where(lo, 0.0, x))
        for s, v in enumerate(variants):
            if transposed:
                out_ref[s * LANES:(s + 1) * LANES, :] = v.T.astype(BF16)
            else:
                out_ref[:, s * LANES:(s + 1) * LANES] = v.astype(BF16)


def _inproj(h, gain, w_in, seq):
    t, d = h.shape
    tables = _rope_tables(seq)
    tiles_per_seq = seq // ROW_TM
    tab_spec = pl.BlockSpec((ROW_TM, LANES), lambda i: (i % tiles_per_seq, 0))

    def row_spec(width):
        return pl.BlockSpec((ROW_TM, width), lambda i: (i, 0))

    widths = (RET_WIDTH, RET_WIDTH, RET_WIDTH, RET_WIDTH, SWA_WIDTH, 4 * LANES)
    return pl.pallas_call(
        _inproj_kernel,
        out_shape=[jax.ShapeDtypeStruct((t, w), BF16) for w in widths]
        + [jax.ShapeDtypeStruct((4 * LANES, t), BF16)],
        grid=(t // ROW_TM,),
        in_specs=[row_spec(d), _resident((1, d)), _resident((d, IN_COLS))] + [tab_spec] * 5,
        out_specs=[row_spec(w) for w in widths] + [pl.BlockSpec((4 * LANES, ROW_TM), lambda i: (0, i))],
        scratch_shapes=[pltpu.VMEM((ROW_TM, d), BF16)],
        compiler_params=_params(1),
        name="inproj",
    )(h, gain, w_in, *tables)


def _retention_tables():
    c = RET_CHUNK
    heads = np.arange(RET_HEADS, dtype=np.float64)
    log_gamma = np.log1p(-np.exp2(-5.0 - heads))
    idx = np.arange(c, dtype=np.float64)
    diff = idx[:, None] - idx[None, :]
    scale = RET_DK ** -0.5
    dmat = np.where(diff[None] >= 0, np.exp(np.maximum(diff, 0.0)[None] * log_gamma[:, None, None]), 0.0)
    zeta = np.exp((c - 1.0 - idx)[None, :] * log_gamma[:, None])
    xi = np.exp((idx + 1.0)[None, :] * log_gamma[:, None])
    chunk_decay = tuple(float(v) for v in np.exp(c * log_gamma))
    zeta_b = np.broadcast_to((zeta * scale)[:, :, None], (RET_HEADS, c, RET_DK))
    xi_b = np.broadcast_to(xi[:, :, None], (RET_HEADS, c, RET_DV))
    tabs = [jnp.asarray(v, dtype=F32) for v in (dmat * scale, zeta_b, xi_b)]
    return tabs, chunk_decay


def _ret_kernel(q_ref, k_ref, v_ref, g_ref, dmat_ref, zeta_ref, xi_ref, gain_ref, o_ref, state_ref,
                *, chunk_decay):
    n = pl.program_id(1)

    @pl.when(n == 0)
    def _():
        state_ref[...] = jnp.zeros_like(state_ref)

    for h in range(RET_HEADS):
        hs = slice(h * RET_DK, (h + 1) * RET_DK)
        q, k, v = q_ref[:, hs], k_ref[:, hs], v_ref[:, hs]
        state = state_ref[h]
        s = _dot_nt(q, k) * dmat_ref[h]
        intra = _dot(s.astype(BF16), v)
        cross = _dot(q, state.astype(BF16)) * xi_ref[h]
        kz = (k.astype(F32) * zeta_ref[h]).astype(BF16)
        state_ref[h] = state * chunk_decay[h] + _dot_tn(kz, v)
        ret = intra + cross
        mu = jnp.mean(ret, axis=-1, keepdims=True)
        cen = ret - mu
        var = jnp.mean(cen * cen, axis=-1, keepdims=True)
        y = cen * lax.rsqrt(var + EPS) * gain_ref[:, hs]
        gate = g_ref[:, hs].astype(F32)
        o_ref[:, hs] = (gate * (1.0 / (1.0 + jnp.exp(-gate))) * y).astype(BF16)


def _retention(rq, rk, rv, rg, gn_gain, batch, seq, cast_weights):
    t = rq.shape[0]
    n_chunks = seq // RET_CHUNK
    tabs, chunk_decay = _retention_tables()
    blk = pl.BlockSpec((RET_CHUNK, RET_WIDTH), lambda b, n: (b * n_chunks + n, 0))
    tab_spec = _resident((RET_HEADS, RET_CHUNK, RET_DK))
    cast_specs, cast_shapes = _cast_specs(cast_weights, (batch, n_chunks))
    host = functools.partial(_ret_kernel, chunk_decay=chunk_decay)
    out = pl.pallas_call(
        _with_casts(host, 8, 1, len(cast_weights)),
        out_shape=[jax.ShapeDtypeStruct((t, RET_WIDTH), BF16)] + cast_shapes,
        grid=(batch, n_chunks),
        in_specs=[blk, blk, blk, blk, tab_spec, tab_spec, tab_spec, _resident((1, RET_WIDTH))] + cast_specs,
        out_specs=[blk] + cast_specs,
        scratch_shapes=[pltpu.VMEM((RET_HEADS, RET_DK, RET_DV), F32)],
        compiler_params=_params(2),
        name="retention",
    )(rq, rk, rv, rg, *tabs, gn_gain, *cast_weights)
    return out[0], out[1:]


def _swa_kernel(sink_ref, q_ref, kp_ref, kc_ref, vp_ref, vc_ref, o_ref):
    n = pl.program_id(1)
    w = WINDOW
    pairs = SWA_HEADS // SWA_KV_HEADS // 2
    key = lax.broadcasted_iota(jnp.int32, (2 * w, pairs * w), 0)
    qry = lax.broadcasted_iota(jnp.int32, (2 * w, pairs * w), 1) % w
    first_key = jnp.where(n == 0, w, 0)
    valid = (key > qry) & (key <= qry + w) & (key >= first_key)

    for g in range(SWA_KV_HEADS):
        slabs = [g * pairs + p for p in range(pairs)]
        q = jnp.concatenate([q_ref[:, sl * LANES:(sl + 1) * LANES] for sl in slabs], 0)
        acc = None
        for e in range(2):
            c = 2 * g + e
            k = jnp.concatenate([kp_ref[:, c * LANES:(c + 1) * LANES], kc_ref[:, c * LANES:(c + 1) * LANES]], 0)
            v_t = jnp.concatenate([vp_ref[c * LANES:(c + 1) * LANES, :], vc_ref[c * LANES:(c + 1) * LANES, :]], 1)
            sink = jnp.concatenate([jnp.full((1, w), sink_ref[2 * sl + e], F32) for sl in slabs], 1)
            s = jnp.where(valid, _dot_nt(k, q), NEG_INF)
            m = jnp.maximum(jnp.max(s, axis=0, keepdims=True), sink)
            p = jnp.exp(s - m)
            inv = 1.0 / (jnp.sum(p, axis=0, keepdims=True) + jnp.exp(sink - m))
            pv = _dot(v_t, (p * inv).astype(BF16))
            acc = pv if acc is None else acc + pv
        for i, sl in enumerate(slabs):
            o_ref[sl * LANES:(sl + 1) * LANES, :] = acc[:, i * w:(i + 1) * w].astype(BF16)


def _swa(sq, sk4, sv4_t, sinks, batch, seq, cast_weights):
    t = sq.shape[0]
    n_blocks = seq // WINDOW
    cur = lambda b, n: (b * n_blocks + n, 0)
    prev = lambda b, n: (b * n_blocks + jnp.maximum(n - 1, 0), 0)
    cur_t = lambda b, n: (0, b * n_blocks + n)
    prev_t = lambda b, n: (0, b * n_blocks + jnp.maximum(n - 1, 0))
    cast_specs, cast_shapes = _cast_specs(cast_weights, (batch, n_blocks))
    out = pl.pallas_call(
        _with_casts(_swa_kernel, 6, 1, len(cast_weights)),
        out_shape=[jax.ShapeDtypeStruct((SWA_WIDTH, t), BF16)] + cast_shapes,
        grid=(batch, n_blocks),
        in_specs=[
            pl.BlockSpec(memory_space=pltpu.SMEM),
            pl.BlockSpec((WINDOW, SWA_WIDTH), cur),
            pl.BlockSpec((WINDOW, 4 * LANES), prev),
            pl.BlockSpec((WINDOW, 4 * LANES), cur),
            pl.BlockSpec((4 * LANES, WINDOW), prev_t),
            pl.BlockSpec((4 * LANES, WINDOW), cur_t),
        ] + cast_specs,
        out_specs=[pl.BlockSpec((SWA_WIDTH, WINDOW), cur_t)] + cast_specs,
        compiler_params=_params(2),
        name="swa",
    )(sinks, sq, sk4, sk4, sv4_t, sv4_t, *cast_weights)
    return out[0], out[1:]


def _outproj_kernel(h_ref, ret_ref, swa_ref, w_ref, g_ref, h2_ref, hn_ref):
    tm = h_ref.shape[0]
    y = _dot(ret_ref[...], w_ref[:RET_WIDTH, :]) + _dot_tn(swa_ref[...], w_ref[RET_WIDTH:, :])
    h2_ref[...] = h_ref[...] + y

    def body(r, carry):
        rows = pl.ds(pl.multiple_of(r * NORM_ROWS, NORM_ROWS), NORM_ROWS)
        hn_ref[rows, :] = _rms_rows(h2_ref[rows, :], g_ref[...]).astype(BF16)
        return carry
    lax.fori_loop(0, tm // NORM_ROWS, body, 0)


def _outproj(h, ret, swa, w_out, gain):
    t, d = h.shape
    row = lambda width: pl.BlockSpec((ROW_TM, width), lambda i: (i, 0))
    return pl.pallas_call(
        _outproj_kernel,
        out_shape=[jax.ShapeDtypeStruct((t, d), F32), jax.ShapeDtypeStruct((t, d), BF16)],
        grid=(t // ROW_TM,),
        in_specs=[row(d), row(RET_WIDTH), pl.BlockSpec((SWA_WIDTH, ROW_TM), lambda i: (0, i)),
                  _resident(w_out.shape), _resident((1, d))],
        out_specs=[row(d), row(d)],
        compiler_params=_params(1),
        name="outproj",
    )(h, ret, swa, w_out, gain)


def _memkv_kernel(m_ref, g_ref, w_ref, o_ref):
    xn = _rms_rows(m_ref[...], g_ref[...]).astype(BF16)
    o_ref[...] = _dot(xn, w_ref[...].astype(BF16)).astype(BF16)


def _memkv(mem, gain, wkv):
    rows, d = mem.shape
    n_out = wkv.shape[1]
    tn = 1024
    return pl.pallas_call(
        _memkv_kernel,
        out_shape=jax.ShapeDtypeStruct((rows, n_out), BF16),
        grid=(n_out // tn,),
        in_specs=[_resident((rows, d)), _resident((1, d)), pl.BlockSpec((d, tn), lambda j: (0, j))],
        out_specs=pl.BlockSpec((rows, tn), lambda j: (0, j)),
        compiler_params=_params(1),
        name="memkv",
    )(mem, gain, wkv)


def _xattn_kernel(hn_ref, h_ref, wq_ref, wo_ref, k_ref, v_ref, o_ref, att_ref):
    q = _dot(hn_ref[...], wq_ref[...]).astype(BF16)
    scale = XA_HEAD_DIM ** -0.5
    for hd in range(XA_HEADS):
        hs = slice(hd * XA_HEAD_DIM, (hd + 1) * XA_HEAD_DIM)
        s = _dot_nt(q[:, hs], k_ref[:, hs]) * scale
        m = jnp.max(s, axis=-1, keepdims=True)
        p = jnp.exp(s - m)
        inv = 1.0 / jnp.sum(p, axis=-1, keepdims=True)
        att_ref[:, hs] = (_dot(p.astype(BF16), v_ref[:, hs]) * inv).astype(BF16)
    o_ref[...] = h_ref[...] + _dot(att_ref[...], wo_ref[...])


def _xattn(hn, h, wq, wo, mkv, seq, mem_len, cast_weights):
    t, d = h.shape
    tiles_per_seq = seq // ROW_TM
    grid = (t // ROW_TM,)
    row = lambda i: (i, 0)
    cast_specs, cast_shapes = _cast_specs(cast_weights, grid)
    out = pl.pallas_call(
        _with_casts(_xattn_kernel, 6, 1, len(cast_weights)),
        out_shape=[jax.ShapeDtypeStruct((t, d), F32)] + cast_shapes,
        grid=grid,
        in_specs=[
            pl.BlockSpec((ROW_TM, d), row),
            pl.BlockSpec((ROW_TM, d), row),
            _resident((d, d)),
            _resident((d, d)),
            pl.BlockSpec((mem_len, d), lambda i: (i // tiles_per_seq, 0)),
            pl.BlockSpec((mem_len, d), lambda i: (i // tiles_per_seq, 1)),
        ] + cast_specs,
        out_specs=[pl.BlockSpec((ROW_TM, d), row)] + cast_specs,
        scratch_shapes=[pltpu.VMEM((ROW_TM, d), BF16)],
        compiler_params=_params(1),
        name="xattn",
    )(hn, h, wq, wo, mkv, mkv, *cast_weights)
    return out[0], out[1:]


def kernel(x, mem, ffn1_norm, ffn1_w_gate, ffn1_w_up, ffn1_w_down, mix_norm, w_in, ret_gn_gain, swa_sinks,
           w_out, xa_norm, mem_norm, xa_wq, xa_wkv, xa_wo, ffn2_norm, ffn2_w_gate, ffn2_w_up, ffn2_w_down,
           final_norm):
    batch, seq, d = x.shape
    mem_len = mem.shape[1]
    depth = ffn1_norm.shape[0]
    h = x.reshape(batch * seq, d)
    mem2 = mem.reshape(batch * mem_len, d)
    row = lambda g: g.reshape(1, -1).astype(F32)
    final_gain = row(final_norm)

    for l in range(depth):
        last = l == depth - 1
        h1, w1_gate, w1_up, w1_down = _ffn_head(h, row(ffn1_norm[l]), ffn1_w_gate[l], ffn1_w_up[l], ffn1_w_down[l])
        h, (w_mix,) = _ffn(h, row(ffn1_norm[l]), w1_gate, w1_up, w1_down, final_gain, final_norm=False,
                           first_tile=1, into=h1, cast_weights=[w_in[l]])
        rq, rk, rv, rg, sq, sk4, sv4_t = _inproj(h, row(mix_norm[l]), w_mix, seq)
        ret, (w2_gate,) = _retention(rq, rk, rv, rg, row(ret_gn_gain[l]), batch, seq, [ffn2_w_gate[l]])
        swa, (w2_up, wo_mix, wq, wo) = _swa(sq, sk4, sv4_t, swa_sinks[l].astype(F32), batch, seq,
                                            [ffn2_w_up[l], w_out[l], xa_wq[l], xa_wo[l]])
        h, hn = _outproj(h, ret, swa, wo_mix, row(xa_norm[l]))
        mkv = _memkv(mem2, row(mem_norm[l]), xa_wkv[l])
        h, (w2_down,) = _xattn(hn, h, wq, wo, mkv, seq, mem_len, [ffn2_w_down[l]])
        h, _ = _ffn(h, row(ffn2_norm[l]), w2_gate, w2_up, w2_down, final_gain, final_norm=last)
    if depth == 0:
        raise ValueError("depth must be at least 1")
    return h.reshape(batch, seq, d)
```

```python
import functools

import numpy as np
import jax
import jax.numpy as jnp
from jax import lax
from jax.experimental import pallas as pl
from jax.experimental.pallas import tpu as pltpu

F32 = jnp.float32
BF16 = jnp.bfloat16

D_MODEL = 2048
D_FF = 5632
RET_HEADS = 8
RET_DK = 128
RET_DV = 128
RET_WIDTH = RET_HEADS * RET_DV
RET_CHUNK = 128
SWA_HEADS = 16
SWA_KV_HEADS = 2
SWA_HEAD_DIM = 64
SWA_WIDTH = SWA_HEADS * SWA_HEAD_DIM
SWA_KV_WIDTH = SWA_KV_HEADS * SWA_HEAD_DIM
WINDOW = 128
XA_HEADS = 4
XA_HEAD_DIM = D_MODEL // XA_HEADS
ROPE_THETA = 10000.0
EPS = 1e-6
IN_SIZES = (RET_WIDTH, RET_WIDTH, RET_WIDTH, RET_WIDTH, SWA_WIDTH, SWA_KV_WIDTH, SWA_KV_WIDTH)
IN_COLS = sum(IN_SIZES)

LANES = 128
BF16_SUBLANES = 16
V7X_VMEM_BYTES = 64 * 1024 * 1024
VMEM_LIMIT_BYTES = 60000 * 1024

FFN_TM = 1024
FFN_TF = 512
FFN_HEAD_TF = 256
ROW_TM = 512
NORM_ROWS = 256
NEG_INF = float(np.finfo(np.float32).min)


def _params(n_axes):
    return pltpu.CompilerParams(
        dimension_semantics=("arbitrary",) * n_axes,
        vmem_limit_bytes=VMEM_LIMIT_BYTES,
    )


def _resident(shape):
    zeros = (0,) * len(shape)
    return pl.BlockSpec(shape, lambda *_: zeros, pipeline_mode=pl.Buffered(1))


def _rms_rows(x, gain):
    ms = jnp.mean(x * x, axis=-1, keepdims=True)
    return x * lax.rsqrt(ms + EPS) * gain


def _dot(a, b):
    return lax.dot_general(a, b, (((1,), (0,)), ((), ())), preferred_element_type=F32)


def _dot_nt(a, b):
    return lax.dot_general(a, b, (((1,), (1,)), ((), ())), preferred_element_type=F32)


def _dot_tn(a, b):
    return lax.dot_general(a, b, (((0,), (0,)), ((), ())), preferred_element_type=F32)


def _cast_block(shape, n_steps):
    rows, cols = shape
    for col_splits in (1, 2, 4, 8):
        row_blocks, rem = divmod(n_steps, col_splits)
        if rem or rows % row_blocks or cols % col_splits:
            continue
        br, bc = rows // row_blocks, cols // col_splits
        if br % BF16_SUBLANES == 0 and bc % LANES == 0:
            return br, bc, col_splits
    raise ValueError(f"no aligned {n_steps}-way split of {shape}")


def _with_casts(kernel_fn, n_in, n_out, n_cast):
    def wrapped(*refs):
        ins, rest = refs[:n_in], refs[n_in:]
        cast_in, rest = rest[:n_cast], rest[n_cast:]
        outs, rest = rest[:n_out], rest[n_out:]
        cast_out, scratch = rest[:n_cast], rest[n_cast:]
        kernel_fn(*ins, *outs, *scratch)
        for src, dst in zip(cast_in, cast_out):
            dst[...] = src[...].astype(BF16)
    return wrapped


def _cast_specs(weights, grid):
    n_steps = int(np.prod(grid))
    specs, shapes = [], []
    for w in weights:
        br, bc, col_splits = _cast_block(w.shape, n_steps)

        def index_map(*idx, col_splits=col_splits):
            step = idx[0]
            for size, i in zip(grid[1:], idx[1:]):
                step = step * size + i
            return step // col_splits, step % col_splits
        specs.append(pl.BlockSpec((br, bc), index_map))
        shapes.append(jax.ShapeDtypeStruct(w.shape, BF16))
    return specs, shapes


def _cast_specs_tiled(weights, grid):
    n_outer, n_inner = grid
    specs, shapes = [], []
    for w in weights:
        rows, cols = w.shape
        bc = cols // n_outer
        row_blocks = max(r for r in range(1, n_inner + 1) if rows % r == 0 and (rows // r) % BF16_SUBLANES == 0)
        if cols % n_outer or bc % LANES:
            raise ValueError(f"no aligned split of {w.shape} over {grid}")

        def index_map(i, j, row_blocks=row_blocks):
            return jnp.minimum(j, row_blocks - 1), i
        specs.append(pl.BlockSpec((rows // row_blocks, bc), index_map))
        shapes.append(jax.ShapeDtypeStruct(w.shape, BF16))
    return specs, shapes


def _rows_loop(n_rows, fn):
    def body(r, carry):
        fn(pl.ds(pl.multiple_of(r * NORM_ROWS, NORM_ROWS), NORM_ROWS))
        return carry
    lax.fori_loop(0, n_rows // NORM_ROWS, body, 0)


def _ffn_prologue(h_ref, g_ref, o_ref, xn_ref):
    def norm(rows):
        h = h_ref[rows, :]
        xn_ref[rows, :] = _rms_rows(h, g_ref[...]).astype(BF16)
        o_ref[rows, :] = h
    _rows_loop(h_ref.shape[0], norm)


def _ffn_step(xn_ref, wg, wu, wd, o_ref):
    xn = xn_ref[...]
    g = _dot(xn, wg)
    u = _dot(xn, wu)
    a = (g * (0.5 / (1.0 + jnp.exp(-g))) * u).astype(BF16)
    for c in range(0, o_ref.shape[1], FFN_TF):
        o_ref[:, c:c + FFN_TF] += _dot(a, wd[:, c:c + FFN_TF])


def _ffn_kernel(*refs, n_ff, final_norm, aliased):
    if aliased:
        refs = refs[1:]
    h_ref, g_ref, wg_ref, wu_ref, wd_ref, fg_ref, o_ref, xn_ref = refs
    j = pl.program_id(1)

    @pl.when(j == 0)
    def _():
        _ffn_prologue(h_ref, g_ref, o_ref, xn_ref)

    _ffn_step(xn_ref, wg_ref[...], wu_ref[...], wd_ref[...], o_ref)

    if final_norm:
        @pl.when(j == n_ff - 1)
        def _():
            def norm(rows):
                o_ref[rows, :] = _rms_rows(o_ref[rows, :], fg_ref[...])
            _rows_loop(o_ref.shape[0], norm)


def _ffn(h, gain, wg, wu, wd, final_gain, *, final_norm, first_tile=0, into=None, cast_weights=()):
    t, d = h.shape
    f = wg.shape[1]
    n_ff = f // FFN_TF
    grid = (t // FFN_TM - first_tile, n_ff)
    aliased = into is not None
    tile = lambda i, j: (i + first_tile, 0)
    in_specs = [
        pl.BlockSpec((FFN_TM, d), tile),
        pl.BlockSpec((1, d), lambda i, j: (0, 0)),
        pl.BlockSpec((d, FFN_TF), lambda i, j: (0, j)),
        pl.BlockSpec((d, FFN_TF), lambda i, j: (0, j)),
        pl.BlockSpec((FFN_TF, d), lambda i, j: (j, 0)),
        pl.BlockSpec((1, d), lambda i, j: (0, 0)),
    ]
    args = (h, gain, wg, wu, wd, final_gain)
    if aliased:
        in_specs = [pl.BlockSpec(memory_space=pl.ANY)] + in_specs
        args = (into,) + args
    cast_specs, cast_shapes = _cast_specs_tiled(cast_weights, grid)
    host = functools.partial(_ffn_kernel, n_ff=n_ff, final_norm=final_norm, aliased=aliased)
    out = pl.pallas_call(
        _with_casts(host, len(in_specs), 1, len(cast_weights)),
        out_shape=[jax.ShapeDtypeStruct((t, d), F32)] + cast_shapes,
        grid=grid,
        in_specs=in_specs + cast_specs,
        out_specs=[pl.BlockSpec((FFN_TM, d), tile)] + cast_specs,
        scratch_shapes=[pltpu.VMEM((FFN_TM, d), BF16)],
        input_output_aliases={0: 0} if aliased else {},
        compiler_params=_params(2),
        name="ffn",
    )(*args, *cast_weights)
    return out[0], out[1:]


def _ffn_head_kernel(h_ref, g_ref, wg_ref, wu_ref, wd_ref, o_ref, wg_out, wu_out, wd_out, xn_ref):
    @pl.when(pl.program_id(0) == 0)
    def _():
        _ffn_prologue(h_ref, g_ref, o_ref, xn_ref)

    wg, wu, wd = (w[...].astype(BF16) for w in (wg_ref, wu_ref, wd_ref))
    wg_out[...] = wg
    wu_out[...] = wu
    wd_out[...] = wd
    _ffn_step(xn_ref, wg, wu, wd, o_ref)


def _ffn_head(h, gain, wg, wu, wd):
    t, d = h.shape
    f = wg.shape[1]
    tf = FFN_HEAD_TF
    col = pl.BlockSpec((d, tf), lambda j: (0, j))
    row = pl.BlockSpec((tf, d), lambda j: (j, 0))
    tile0 = pl.BlockSpec((FFN_TM, d), lambda j: (0, 0))
    return pl.pallas_call(
        _ffn_head_kernel,
        out_shape=[jax.ShapeDtypeStruct((t, d), F32), jax.ShapeDtypeStruct(wg.shape, BF16),
                   jax.ShapeDtypeStruct(wu.shape, BF16), jax.ShapeDtypeStruct(wd.shape, BF16)],
        grid=(f // tf,),
        in_specs=[tile0, pl.BlockSpec((1, d), lambda j: (0, 0)), col, col, row],
        out_specs=[tile0, col, col, row],
        scratch_shapes=[pltpu.VMEM((FFN_TM, d), BF16)],
        compiler_params=_params(1),
        name="ffn_head",
    )(h, gain, wg, wu, wd)


def _rope_tables(seq):
    pos = np.arange(seq, dtype=np.float32)

    def angles(d):
        inv = np.float32(ROPE_THETA) ** (-np.arange(0, d, 2, dtype=np.float32) / np.float32(d))
        return (pos[:, None] * inv[None, :].astype(np.float32)).astype(np.float32).astype(np.float64)

    a128 = angles(RET_DK)
    cos_r = np.concatenate([np.cos(a128), np.cos(a128)], -1)
    sin_r = np.concatenate([-np.sin(a128), np.sin(a128)], -1)
    a64 = angles(SWA_HEAD_DIM)
    c, s, z = np.cos(a64), np.sin(a64), np.zeros_like(a64)
    cos_s = np.concatenate([c, c, c, c], -1)
    sin_lo = np.concatenate([-s, z, -s, z], -1)
    sin_hi = np.concatenate([z, s, z, s], -1)
    return [jnp.asarray(v, dtype=F32) for v in (cos_r, sin_r, cos_s, sin_lo, sin_hi)]


def _inproj_kernel(h_ref, g_ref, w_ref, cr_ref, sr_ref, cs_ref, sl_ref, sh_ref,
                   rq_ref, rk_ref, rv_ref, rg_ref, sq_ref, sk_ref, sv_ref, xn_ref):
    tm = h_ref.shape[0]

    def body(r, carry):
        rows = pl.ds(pl.multiple_of(r * NORM_ROWS, NORM_ROWS), NORM_ROWS)
        xn_ref[rows, :] = _rms_rows(h_ref[rows, :], g_ref[...]).astype(BF16)
        return carry
    lax.fori_loop(0, tm // NORM_ROWS, body, 0)

    xn = xn_ref[...]
    cr, sr = cr_ref[...], sr_ref[...]
    cs, sl, sh = cs_ref[...], sl_ref[...], sh_ref[...]
    half = LANES // 2

    def rope_ret(x):
        return x * cr + pltpu.roll(x, half, 1) * sr

    def rope_swa(x):
        return x * cs + pltpu.roll(x, LANES - half // 2, 1) * sl + pltpu.roll(x, half // 2, 1) * sh

    def project(col0, width):
        return _dot(xn, w_ref[:, col0:col0 + width])

    col = 0
    for out_ref, fn in ((rq_ref, rope_ret), (rk_ref, rope_ret), (rv_ref, None), (rg_ref, None)):
        y = project(col, RET_WIDTH)
        for s in range(RET_WIDTH // LANES):
            slab = y[:, s * LANES:(s + 1) * LANES]
            if fn is not None:
                slab = fn(slab)
            out_ref[:, s * LANES:(s + 1) * LANES] = slab.astype(BF16)
        col += RET_WIDTH

    y = project(col, SWA_WIDTH)
    scale = SWA_HEAD_DIM ** -0.5
    for s in range(SWA_WIDTH // LANES):
        slab = rope_swa(y[:, s * LANES:(s + 1) * LANES]) * scale
        sq_ref[:, s * LANES:(s + 1) * LANES] = slab.astype(BF16)
    col += SWA_WIDTH

    y = project(col, 2 * SWA_KV_WIDTH)
    lo = lax.broadcasted_iota(jnp.int32, (tm, LANES), 1) < half
    for transposed, out_ref, x in ((False, sk_ref, rope_swa(y[:, :LANES])), (True, sv_ref, y[:, LANES:])):
        xr = pltpu.roll(x, half, 1)
        variants = (jnp.where(lo, x, 0.0), jnp.where(lo, 0.0, xr),
                    jnp.where(lo, xr, 0.0), jnp.where(lo, 0.0, x))
        for s, v in enumerate(variants):
            if transposed:
                out_ref[s * LANES:(s + 1) * LANES, :] = v.T.astype(BF16)
            else:
                out_ref[:, s * LANES:(s + 1) * LANES] = v.astype(BF16)


def _inproj(h, gain, w_in, seq):
    t, d = h.shape
    tables = _rope_tables(seq)
    tiles_per_seq = seq // ROW_TM
    tab_spec = pl.BlockSpec((ROW_TM, LANES), lambda i: (i % tiles_per_seq, 0))

    def row_spec(width):
        return pl.BlockSpec((ROW_TM, width), lambda i: (i, 0))

    widths = (RET_WIDTH, RET_WIDTH, RET_WIDTH, RET_WIDTH, SWA_WIDTH, 4 * LANES)
    return pl.pallas_call(
        _inproj_kernel,
        out_shape=[jax.ShapeDtypeStruct((t, w), BF16) for w in widths]
        + [jax.ShapeDtypeStruct((4 * LANES, t), BF16)],
        grid=(t // ROW_TM,),
        in_specs=[row_spec(d), _resident((1, d)), _resident((d, IN_COLS))] + [tab_spec] * 5,
        out_specs=[row_spec(w) for w in widths] + [pl.BlockSpec((4 * LANES, ROW_TM), lambda i: (0, i))],
        scratch_shapes=[pltpu.VMEM((ROW_TM, d), BF16)],
        compiler_params=_params(1),
        name="inproj",
    )(h, gain, w_in, *tables)


def _retention_tables():
    c = RET_CHUNK
    heads = np.arange(RET_HEADS, dtype=np.float64)
    log_gamma = np.log1p(-np.exp2(-5.0 - heads))
    idx = np.arange(c, dtype=np.float64)
    diff = idx[:, None] - idx[None, :]
    scale = RET_DK ** -0.5
    dmat = np.where(diff[None] >= 0, np.exp(np.maximum(diff, 0.0)[None] * log_gamma[:, None, None]), 0.0)
    zeta = np.exp((c - 1.0 - idx)[None, :] * log_gamma[:, None])
    xi = np.exp((idx + 1.0)[None, :] * log_gamma[:, None])
    chunk_decay = tuple(float(v) for v in np.exp(c * log_gamma))
    zeta_b = np.broadcast_to((zeta * scale)[:, :, None], (RET_HEADS, c, RET_DK))
    xi_b = np.broadcast_to(xi[:, :, None], (RET_HEADS, c, RET_DV))
    tabs = [jnp.asarray(v, dtype=F32) for v in (dmat * scale, zeta_b, xi_b)]
    return tabs, chunk_decay


def _ret_kernel(q_ref, k_ref, v_ref, g_ref, dmat_ref, zeta_ref, xi_ref, gain_ref, o_ref, state_ref,
                *, chunk_decay):
    n = pl.program_id(1)

    @pl.when(n == 0)
    def _():
        state_ref[...] = jnp.zeros_like(state_ref)

    for h in range(RET_HEADS):
        hs = slice(h * RET_DK, (h + 1) * RET_DK)
        q, k, v = q_ref[:, hs], k_ref[:, hs], v_ref[:, hs]
        state = state_ref[h]
        s = _dot_nt(q, k) * dmat_ref[h]
        intra = _dot(s.astype(BF16), v)
        cross = _dot(q, state.astype(BF16)) * xi_ref[h]
        kz = (k.astype(F32) * zeta_ref[h]).astype(BF16)
        state_ref[h] = state * chunk_decay[h] + _dot_tn(kz, v)
        ret = intra + cross
        mu = jnp.mean(ret, axis=-1, keepdims=True)
        cen = ret - mu
        var = jnp.mean(cen * cen, axis=-1, keepdims=True)
        y = cen * lax.rsqrt(var + EPS) * gain_ref[:, hs]
        gate = g_ref[:, hs].astype(F32)
        o_ref[:, hs] = (gate * (1.0 / (1.0 + jnp.exp(-gate))) * y).astype(BF16)


def _retention(rq, rk, rv, rg, gn_gain, batch, seq, cast_weights):
    t = rq.shape[0]
    n_chunks = seq // RET_CHUNK
    tabs, chunk_decay = _retention_tables()
    blk = pl.BlockSpec((RET_CHUNK, RET_WIDTH), lambda b, n: (b * n_chunks + n, 0))
    tab_spec = _resident((RET_HEADS, RET_CHUNK, RET_DK))
    cast_specs, cast_shapes = _cast_specs(cast_weights, (batch, n_chunks))
    host = functools.partial(_ret_kernel, chunk_decay=chunk_decay)
    out = pl.pallas_call(
        _with_casts(host, 8, 1, len(cast_weights)),
        out_shape=[jax.ShapeDtypeStruct((t, RET_WIDTH), BF16)] + cast_shapes,
        grid=(batch, n_chunks),
        in_specs=[blk, blk, blk, blk, tab_spec, tab_spec, tab_spec, _resident((1, RET_WIDTH))] + cast_specs,
        out_specs=[blk] + cast_specs,
        scratch_shapes=[pltpu.VMEM((RET_HEADS, RET_DK, RET_DV), F32)],
        compiler_params=_params(2),
        name="retention",
    )(rq, rk, rv, rg, *tabs, gn_gain, *cast_weights)
    return out[0], out[1:]


def _swa_kernel(sink_ref, q_ref, kp_ref, kc_ref, vp_ref, vc_ref, o_ref):
    n = pl.program_id(1)
    w = WINDOW
    pairs = SWA_HEADS // SWA_KV_HEADS // 2
    key = lax.broadcasted_iota(jnp.int32, (2 * w, pairs * w), 0)
    qry = lax.broadcasted_iota(jnp.int32, (2 * w, pairs * w), 1) % w
    first_key = jnp.where(n == 0, w, 0)
    valid = (key > qry) & (key <= qry + w) & (key >= first_key)

    for g in range(SWA_KV_HEADS):
        slabs = [g * pairs + p for p in range(pairs)]
        q = jnp.concatenate([q_ref[:, sl * LANES:(sl + 1) * LANES] for sl in slabs], 0)
        acc = None
        for e in range(2):
            c = 2 * g + e
            k = jnp.concatenate([kp_ref[:, c * LANES:(c + 1) * LANES], kc_ref[:, c * LANES:(c + 1) * LANES]], 0)
            v_t = jnp.concatenate([vp_ref[c * LANES:(c + 1) * LANES, :], vc_ref[c * LANES:(c + 1) * LANES, :]], 1)
            sink = jnp.concatenate([jnp.full((1, w), sink_ref[2 * sl + e], F32) for sl in slabs], 1)
            s = jnp.where(valid, _dot_nt(k, q), NEG_INF)
            m = jnp.maximum(jnp.max(s, axis=0, keepdims=True), sink)
            p = jnp.exp(s - m)
            inv = 1.0 / (jnp.sum(p, axis=0, keepdims=True) + jnp.exp(sink - m))
            pv = _dot(v_t, (p * inv).astype(BF16))
            acc = pv if acc is None else acc + pv
        for i, sl in enumerate(slabs):
            o_ref[sl * LANES:(sl + 1) * LANES, :] = acc[:, i * w:(i + 1) * w].astype(BF16)


def _swa(sq, sk4, sv4_t, sinks, batch, seq, cast_weights):
    t = sq.shape[0]
    n_blocks = seq // WINDOW
    cur = lambda b, n: (b * n_blocks + n, 0)
    prev = lambda b, n: (b * n_blocks + jnp.maximum(n - 1, 0), 0)
    cur_t = lambda b, n: (0, b * n_blocks + n)
    prev_t = lambda b, n: (0, b * n_blocks + jnp.maximum(n - 1, 0))
    cast_specs, cast_shapes = _cast_specs(cast_weights, (batch, n_blocks))
    out = pl.pallas_call(
        _with_casts(_swa_kernel, 6, 1, len(cast_weights)),
        out_shape=[jax.ShapeDtypeStruct((SWA_WIDTH, t), BF16)] + cast_shapes,
        grid=(batch, n_blocks),
        in_specs=[
            pl.BlockSpec(memory_space=pltpu.SMEM),
            pl.BlockSpec((WINDOW, SWA_WIDTH), cur),
            pl.BlockSpec((WINDOW, 4 * LANES), prev),
            pl.BlockSpec((WINDOW, 4 * LANES), cur),
            pl.BlockSpec((4 * LANES, WINDOW), prev_t),
            pl.BlockSpec((4 * LANES, WINDOW), cur_t),
        ] + cast_specs,
        out_specs=[pl.BlockSpec((SWA_WIDTH, WINDOW), cur_t)] + cast_specs,
        compiler_params=_params(2),
        name="swa",
    )(sinks, sq, sk4, sk4, sv4_t, sv4_t, *cast_weights)
    return out[0], out[1:]


def _mixers_kernel(*refs, chunk_decay):
    ret_in, swa_in = refs[:8], refs[8:14]
    ret_out, swa_out, state_ref = refs[14:]
    _ret_kernel(*ret_in, ret_out, state_ref, chunk_decay=chunk_decay)
    _swa_kernel(*swa_in, swa_out)


def _mixers(rq, rk, rv, rg, gn_gain, sq, sk4, sv4_t, sinks, batch, seq, cast_weights):
    assert RET_CHUNK == WINDOW
    t = rq.shape[0]
    n_blocks = seq // WINDOW
    grid = (batch, n_blocks)
    tabs, chunk_decay = _retention_tables()
    cur = lambda b, n: (b * n_blocks + n, 0)
    prev = lambda b, n: (b * n_blocks + jnp.maximum(n - 1, 0), 0)
    cur_t = lambda b, n: (0, b * n_blocks + n)
    prev_t = lambda b, n: (0, b * n_blocks + jnp.maximum(n - 1, 0))
    blk = pl.BlockSpec((RET_CHUNK, RET_WIDTH), cur)
    tab_spec = _resident((RET_HEADS, RET_CHUNK, RET_DK))
    in_specs = [
        blk, blk, blk, blk, tab_spec, tab_spec, tab_spec, _resident((1, RET_WIDTH)),
        pl.BlockSpec(memory_space=pltpu.SMEM),
        pl.BlockSpec((WINDOW, SWA_WIDTH), cur),
        pl.BlockSpec((WINDOW, 4 * LANES), prev),
        pl.BlockSpec((WINDOW, 4 * LANES), cur),
        pl.BlockSpec((4 * LANES, WINDOW), prev_t),
        pl.BlockSpec((4 * LANES, WINDOW), cur_t),
    ]
    cast_specs, cast_shapes = _cast_specs(cast_weights, grid)
    host = functools.partial(_mixers_kernel, chunk_decay=chunk_decay)
    out = pl.pallas_call(
        _with_casts(host, len(in_specs), 2, len(cast_weights)),
        out_shape=[jax.ShapeDtypeStruct((t, RET_WIDTH), BF16), jax.ShapeDtypeStruct((SWA_WIDTH, t), BF16)]
        + cast_shapes,
        grid=grid,
        in_specs=in_specs + cast_specs,
        out_specs=[blk, pl.BlockSpec((SWA_WIDTH, WINDOW), cur_t)] + cast_specs,
        scratch_shapes=[pltpu.VMEM((RET_HEADS, RET_DK, RET_DV), F32)],
        compiler_params=_params(2),
        name="mixers",
    )(rq, rk, rv, rg, *tabs, gn_gain, sinks, sq, sk4, sk4, sv4_t, sv4_t, *cast_weights)
    return out[0], out[1], out[2:]


def _outproj_kernel(h_ref, ret_ref, swa_ref, w_ref, g_ref, h2_ref, hn_ref):
    tm = h_ref.shape[0]
    y = _dot(ret_ref[...], w_ref[:RET_WIDTH, :]) + _dot_tn(swa_ref[...], w_ref[RET_WIDTH:, :])
    h2_ref[...] = h_ref[...] + y

    def body(r, carry):
        rows = pl.ds(pl.multiple_of(r * NORM_ROWS, NORM_ROWS), NORM_ROWS)
        hn_ref[rows, :] = _rms_rows(h2_ref[rows, :], g_ref[...]).astype(BF16)
        return carry
    lax.fori_loop(0, tm // NORM_ROWS, body, 0)


def _outproj(h, ret, swa, w_out, gain):
    t, d = h.shape
    row = lambda width: pl.BlockSpec((ROW_TM, width), lambda i: (i, 0))
    return pl.pallas_call(
        _outproj_kernel,
        out_shape=[jax.ShapeDtypeStruct((t, d), F32), jax.ShapeDtypeStruct((t, d), BF16)],
        grid=(t // ROW_TM,),
        in_specs=[row(d), row(RET_WIDTH), pl.BlockSpec((SWA_WIDTH, ROW_TM), lambda i: (0, i)),
                  _resident(w_out.shape), _resident((1, d))],
        out_specs=[row(d), row(d)],
        compiler_params=_params(1),
        name="outproj",
    )(h, ret, swa, w_out, gain)


def _memkv_kernel(m_ref, g_ref, w_ref, o_ref):
    xn = _rms_rows(m_ref[...], g_ref[...]).astype(BF16)
    o_ref[...] = _dot(xn, w_ref[...].astype(BF16)).astype(BF16)


def _memkv(mem, gain, wkv):
    rows, d = mem.shape
    n_out = wkv.shape[1]
    tn = 1024
    return pl.pallas_call(
        _memkv_kernel,
        out_shape=jax.ShapeDtypeStruct((rows, n_out), BF16),
        grid=(n_out // tn,),
        in_specs=[_resident((rows, d)), _resident((1, d)), pl.BlockSpec((d, tn), lambda j: (0, j))],
        out_specs=pl.BlockSpec((rows, tn), lambda j: (0, j)),
        compiler_params=_params(1),
        name="memkv",
    )(mem, gain, wkv)


def _xattn_kernel(hn_ref, h_ref, wq_ref, wo_ref, k_ref, v_ref, o_ref, att_ref):
    q = _dot(hn_ref[...], wq_ref[...]).astype(BF16)
    scale = XA_HEAD_DIM ** -0.5
    for hd in range(XA_HEADS):
        hs = slice(hd * XA_HEAD_DIM, (hd + 1) * XA_HEAD_DIM)
        s = _dot_nt(q[:, hs], k_ref[:, hs]) * scale
        m = jnp.max(s, axis=-1, keepdims=True)
        p = jnp.exp(s - m)
        inv = 1.0 / jnp.sum(p, axis=-1, keepdims=True)
        att_ref[:, hs] = (_dot(p.astype(BF16), v_ref[:, hs]) * inv).astype(BF16)
    o_ref[...] = h_ref[...] + _dot(att_ref[...], wo_ref[...])


def _xattn(hn, h, wq, wo, mkv, seq, mem_len, cast_weights):
    t, d = h.shape
    tiles_per_seq = seq // ROW_TM
    grid = (t // ROW_TM,)
    row = lambda i: (i, 0)
    cast_specs, cast_shapes = _cast_specs(cast_weights, grid)
    out = pl.pallas_call(
        _with_casts(_xattn_kernel, 6, 1, len(cast_weights)),
        out_shape=[jax.ShapeDtypeStruct((t, d), F32)] + cast_shapes,
        grid=grid,
        in_specs=[
            pl.BlockSpec((ROW_TM, d), row),
            pl.BlockSpec((ROW_TM, d), row),
            _resident((d, d)),
            _resident((d, d)),
            pl.BlockSpec((mem_len, d), lambda i: (i // tiles_per_seq, 0)),
            pl.BlockSpec((mem_len, d), lambda i: (i // tiles_per_seq, 1)),
        ] + cast_specs,
        out_specs=[pl.BlockSpec((ROW_TM, d), row)] + cast_specs,
        scratch_shapes=[pltpu.VMEM((ROW_TM, d), BF16)],
        compiler_params=_params(1),
        name="xattn",
    )(hn, h, wq, wo, mkv, mkv, *cast_weights)
    return out[0], out[1:]


def kernel(x, mem, ffn1_norm, ffn1_w_gate, ffn1_w_up, ffn1_w_down, mix_norm, w_in, ret_gn_gain, swa_sinks,
           w_out, xa_norm, mem_norm, xa_wq, xa_wkv, xa_wo, ffn2_norm, ffn2_w_gate, ffn2_w_up, ffn2_w_down,
           final_norm):
    batch, seq, d = x.shape
    mem_len = mem.shape[1]
    depth = ffn1_norm.shape[0]
    h = x.reshape(batch * seq, d)
    mem2 = mem.reshape(batch * mem_len, d)
    row = lambda g: g.reshape(1, -1).astype(F32)
    final_gain = row(final_norm)

    for l in range(depth):
        last = l == depth - 1
        h1, w1_gate, w1_up, w1_down = _ffn_head(h, row(ffn1_norm[l]), ffn1_w_gate[l], ffn1_w_up[l], ffn1_w_down[l])
        h, (w_mix,) = _ffn(h, row(ffn1_norm[l]), w1_gate, w1_up, w1_down, final_gain, final_norm=False,
                           first_tile=1, into=h1, cast_weights=[w_in[l]])
        rq, rk, rv, rg, sq, sk4, sv4_t = _inproj(h, row(mix_norm[l]), w_mix, seq)
        ret, swa, (w2_gate, w2_up, wo_mix, wq, wo) = _mixers(
            rq, rk, rv, rg, row(ret_gn_gain[l]), sq, sk4, sv4_t, swa_sinks[l].astype(F32), batch, seq,
            [ffn2_w_gate[l], ffn2_w_up[l], w_out[l], xa_wq[l], xa_wo[l]])
        h, hn = _outproj(h, ret, swa, wo_mix, row(xa_norm[l]))
        mkv = _memkv(mem2, row(mem_norm[l]), xa_wkv[l])
        h, (w2_down,) = _xattn(hn, h, wq, wo, mkv, seq, mem_len, [ffn2_w_down[l]])
        h, _ = _ffn(h, row(ffn2_norm[l]), w2_gate, w2_up, w2_down, final_gain, final_norm=last)
    if depth == 0:
        raise ValueError("depth must be at least 1")
    return h.reshape(batch, seq, d)
```

```python
import functools

import numpy as np
import jax
import jax.numpy as jnp
from jax import lax
from jax.experimental import pallas as pl
from jax.experimental.pallas import tpu as pltpu

F32 = jnp.float32
BF16 = jnp.bfloat16

D_MODEL = 2048
D_FF = 5632
RET_HEADS = 8
RET_DK = 128
RET_DV = 128
RET_WIDTH = RET_HEADS * RET_DV
RET_CHUNK = 128
SWA_HEADS = 16
SWA_KV_HEADS = 2
SWA_HEAD_DIM = 64
SWA_WIDTH = SWA_HEADS * SWA_HEAD_DIM
SWA_KV_WIDTH = SWA_KV_HEADS * SWA_HEAD_DIM
WINDOW = 128
XA_HEADS = 4
XA_HEAD_DIM = D_MODEL // XA_HEADS
ROPE_THETA = 10000.0
EPS = 1e-6
IN_SIZES = (RET_WIDTH, RET_WIDTH, RET_WIDTH, RET_WIDTH, SWA_WIDTH, SWA_KV_WIDTH, SWA_KV_WIDTH)
IN_COLS = sum(IN_SIZES)

LANES = 128
BF16_SUBLANES = 16
V7X_VMEM_BYTES = 64 * 1024 * 1024
VMEM_LIMIT_BYTES = 60000 * 1024

FFN_TM = 1024
FFN_TF = 512
FFN_HEAD_TF = 256
ROW_TM = 512
NORM_ROWS = 256
NEG_INF = float(np.finfo(np.float32).min)
LOG2E = float(np.log2(np.e))


def _params(n_axes):
    return pltpu.CompilerParams(
        dimension_semantics=("arbitrary",) * n_axes,
        vmem_limit_bytes=VMEM_LIMIT_BYTES,
    )


def _resident(shape):
    zeros = (0,) * len(shape)
    return pl.BlockSpec(shape, lambda *_: zeros, pipeline_mode=pl.Buffered(1))


def _rms_rows(x, gain):
    ms = jnp.mean(x * x, axis=-1, keepdims=True)
    return x * lax.rsqrt(ms + EPS) * gain


def _dot(a, b):
    return lax.dot_general(a, b, (((1,), (0,)), ((), ())), preferred_element_type=F32)


def _dot_nt(a, b):
    return lax.dot_general(a, b, (((1,), (1,)), ((), ())), preferred_element_type=F32)


def _dot_tn(a, b):
    return lax.dot_general(a, b, (((0,), (0,)), ((), ())), preferred_element_type=F32)


def _cast_block(shape, n_steps):
    rows, cols = shape
    for col_splits in (1, 2, 4, 8):
        row_blocks, rem = divmod(n_steps, col_splits)
        if rem or rows % row_blocks or cols % col_splits:
            continue
        br, bc = rows // row_blocks, cols // col_splits
        if br % BF16_SUBLANES == 0 and bc % LANES == 0:
            return br, bc, col_splits
    raise ValueError(f"no aligned {n_steps}-way split of {shape}")


def _with_casts(kernel_fn, n_in, n_out, n_cast):
    def wrapped(*refs):
        ins, rest = refs[:n_in], refs[n_in:]
        cast_in, rest = rest[:n_cast], rest[n_cast:]
        outs, rest = rest[:n_out], rest[n_out:]
        cast_out, scratch = rest[:n_cast], rest[n_cast:]
        kernel_fn(*ins, *outs, *scratch)
        for src, dst in zip(cast_in, cast_out):
            dst[...] = src[...].astype(BF16)
    return wrapped


def _cast_specs(weights, grid):
    n_steps = int(np.prod(grid))
    specs, shapes = [], []
    for w in weights:
        br, bc, col_splits = _cast_block(w.shape, n_steps)

        def index_map(*idx, col_splits=col_splits):
            step = idx[0]
            for size, i in zip(grid[1:], idx[1:]):
                step = step * size + i
            return step // col_splits, step % col_splits
        specs.append(pl.BlockSpec((br, bc), index_map))
        shapes.append(jax.ShapeDtypeStruct(w.shape, BF16))
    return specs, shapes


def _cast_specs_tiled(weights, grid):
    n_outer, n_inner = grid
    specs, shapes = [], []
    for w in weights:
        rows, cols = w.shape
        bc = cols // n_outer
        row_blocks = max(r for r in range(1, n_inner + 1) if rows % r == 0 and (rows // r) % BF16_SUBLANES == 0)
        if cols % n_outer or bc % LANES:
            raise ValueError(f"no aligned split of {w.shape} over {grid}")

        def index_map(i, j, row_blocks=row_blocks):
            return jnp.minimum(j, row_blocks - 1), i
        specs.append(pl.BlockSpec((rows // row_blocks, bc), index_map))
        shapes.append(jax.ShapeDtypeStruct(w.shape, BF16))
    return specs, shapes


def _rows_loop(n_rows, fn):
    def body(r, carry):
        fn(pl.ds(pl.multiple_of(r * NORM_ROWS, NORM_ROWS), NORM_ROWS))
        return carry
    lax.fori_loop(0, n_rows // NORM_ROWS, body, 0)


def _ffn_prologue(h_ref, g_ref, o_ref, xn_ref):
    def norm(rows):
        h = h_ref[rows, :]
        xn_ref[rows, :] = _rms_rows(h, g_ref[...]).astype(BF16)
        o_ref[rows, :] = h
    _rows_loop(h_ref.shape[0], norm)


def _ffn_step(xn_ref, wg, wu, wd, o_ref):
    xn = xn_ref[...]
    g = _dot(xn, wg)
    u = _dot(xn, wu)
    a = (g * (0.5 / (1.0 + jnp.exp(-g))) * u).astype(BF16)
    for c in range(0, o_ref.shape[1], FFN_TF):
        o_ref[:, c:c + FFN_TF] += _dot(a, wd[:, c:c + FFN_TF])


def _ffn_kernel(*refs, n_ff, final_norm, aliased):
    if aliased:
        refs = refs[1:]
    h_ref, g_ref, wg_ref, wu_ref, wd_ref, fg_ref, o_ref, xn_ref = refs
    j = pl.program_id(1)

    @pl.when(j == 0)
    def _():
        _ffn_prologue(h_ref, g_ref, o_ref, xn_ref)

    _ffn_step(xn_ref, wg_ref[...], wu_ref[...], wd_ref[...], o_ref)

    if final_norm:
        @pl.when(j == n_ff - 1)
        def _():
            def norm(rows):
                o_ref[rows, :] = _rms_rows(o_ref[rows, :], fg_ref[...])
            _rows_loop(o_ref.shape[0], norm)


def _ffn(h, gain, wg, wu, wd, final_gain, *, final_norm, first_tile=0, into=None, cast_weights=()):
    t, d = h.shape
    f = wg.shape[1]
    n_ff = f // FFN_TF
    grid = (t // FFN_TM - first_tile, n_ff)
    aliased = into is not None
    tile = lambda i, j: (i + first_tile, 0)
    in_specs = [
        pl.BlockSpec((FFN_TM, d), tile),
        pl.BlockSpec((1, d), lambda i, j: (0, 0)),
        pl.BlockSpec((d, FFN_TF), lambda i, j: (0, j)),
        pl.BlockSpec((d, FFN_TF), lambda i, j: (0, j)),
        pl.BlockSpec((FFN_TF, d), lambda i, j: (j, 0)),
        pl.BlockSpec((1, d), lambda i, j: (0, 0)),
    ]
    args = (h, gain, wg, wu, wd, final_gain)
    if aliased:
        in_specs = [pl.BlockSpec(memory_space=pl.ANY)] + in_specs
        args = (into,) + args
    cast_specs, cast_shapes = _cast_specs_tiled(cast_weights, grid)
    host = functools.partial(_ffn_kernel, n_ff=n_ff, final_norm=final_norm, aliased=aliased)
    out = pl.pallas_call(
        _with_casts(host, len(in_specs), 1, len(cast_weights)),
        out_shape=[jax.ShapeDtypeStruct((t, d), F32)] + cast_shapes,
        grid=grid,
        in_specs=in_specs + cast_specs,
        out_specs=[pl.BlockSpec((FFN_TM, d), tile)] + cast_specs,
        scratch_shapes=[pltpu.VMEM((FFN_TM, d), BF16)],
        input_output_aliases={0: 0} if aliased else {},
        compiler_params=_params(2),
        name="ffn",
    )(*args, *cast_weights)
    return out[0], out[1:]


def _ffn_head_kernel(h_ref, g_ref, wg_ref, wu_ref, wd_ref, o_ref, wg_out, wu_out, wd_out, xn_ref):
    @pl.when(pl.program_id(0) == 0)
    def _():
        _ffn_prologue(h_ref, g_ref, o_ref, xn_ref)

    wg, wu, wd = (w[...].astype(BF16) for w in (wg_ref, wu_ref, wd_ref))
    wg_out[...] = wg
    wu_out[...] = wu
    wd_out[...] = wd
    _ffn_step(xn_ref, wg, wu, wd, o_ref)


def _ffn_head(h, gain, wg, wu, wd):
    t, d = h.shape
    f = wg.shape[1]
    tf = FFN_HEAD_TF
    col = pl.BlockSpec((d, tf), lambda j: (0, j))
    row = pl.BlockSpec((tf, d), lambda j: (j, 0))
    tile0 = pl.BlockSpec((FFN_TM, d), lambda j: (0, 0))
    return pl.pallas_call(
        _ffn_head_kernel,
        out_shape=[jax.ShapeDtypeStruct((t, d), F32), jax.ShapeDtypeStruct(wg.shape, BF16),
                   jax.ShapeDtypeStruct(wu.shape, BF16), jax.ShapeDtypeStruct(wd.shape, BF16)],
        grid=(f // tf,),
        in_specs=[tile0, pl.BlockSpec((1, d), lambda j: (0, 0)), col, col, row],
        out_specs=[tile0, col, col, row],
        scratch_shapes=[pltpu.VMEM((FFN_TM, d), BF16)],
        compiler_params=_params(1),
        name="ffn_head",
    )(h, gain, wg, wu, wd)


def _rope_tables(seq):
    pos = np.arange(seq, dtype=np.float32)

    def angles(d):
        inv = np.float32(ROPE_THETA) ** (-np.arange(0, d, 2, dtype=np.float32) / np.float32(d))
        return (pos[:, None] * inv[None, :].astype(np.float32)).astype(np.float32).astype(np.float64)

    a128 = angles(RET_DK)
    cos_r = np.concatenate([np.cos(a128), np.cos(a128)], -1)
    sin_r = np.concatenate([-np.sin(a128), np.sin(a128)], -1)
    a64 = angles(SWA_HEAD_DIM)
    c, s, z = np.cos(a64), np.sin(a64), np.zeros_like(a64)
    cos_s = np.concatenate([c, c, c, c], -1)
    sin_lo = np.concatenate([-s, z, -s, z], -1)
    sin_hi = np.concatenate([z, s, z, s], -1)
    return [jnp.asarray(v, dtype=F32) for v in (cos_r, sin_r, cos_s, sin_lo, sin_hi)]


def _retention_tables():
    c = RET_CHUNK
    heads = np.arange(RET_HEADS, dtype=np.float64)
    log_gamma = np.log1p(-np.exp2(-5.0 - heads))
    idx = np.arange(c, dtype=np.float64)
    diff = idx[:, None] - idx[None, :]
    scale = RET_DK ** -0.5
    dmat = np.where(diff[None] >= 0, np.exp(np.maximum(diff, 0.0)[None] * log_gamma[:, None, None]), 0.0)
    zeta = np.exp((c - 1.0 - idx)[None, :] * log_gamma[:, None])
    xi = np.exp((idx + 1.0)[None, :] * log_gamma[:, None])
    chunk_decay = tuple(float(v) for v in np.exp(c * log_gamma))

    def token_tile(tab):
        return np.tile(np.repeat(tab.T, RET_DK, axis=1), (ROW_TM // c, 1))
    tabs = [jnp.asarray(v, dtype=F32) for v in (dmat * scale, token_tile(xi), token_tile(zeta * scale))]
    return tabs, chunk_decay


def _inproj_kernel(h_ref, g_ref, w_ref, cr_ref, sr_ref, cs_ref, sl_ref, sh_ref, xi_ref, zeta_ref, gn_ref,
                   rq_ref, rqx_ref, rk_ref, rkz_ref, rv_ref, rg_ref, sq_ref, sk_ref, sv_ref, xn_ref):
    tm = h_ref.shape[0]

    def norm(rows):
        xn_ref[rows, :] = _rms_rows(h_ref[rows, :], g_ref[...]).astype(BF16)
    _rows_loop(tm, norm)

    xn = xn_ref[...]
    cr, sr = cr_ref[...], sr_ref[...]
    cs, sl, sh = cs_ref[...], sl_ref[...], sh_ref[...]
    half = LANES // 2
    slabs = [slice(s * LANES, (s + 1) * LANES) for s in range(RET_WIDTH // LANES)]

    def rope_ret(x):
        return x * cr + pltpu.roll(x, half, 1) * sr

    def rope_swa(x):
        return x * cs + pltpu.roll(x, LANES - half // 2, 1) * sl + pltpu.roll(x, half // 2, 1) * sh

    def project(col0, width):
        return _dot(xn, w_ref[:, col0:col0 + width])

    col = 0
    for out_ref, scaled_ref, tab_ref in ((rq_ref, rqx_ref, xi_ref), (rk_ref, rkz_ref, zeta_ref)):
        y = project(col, RET_WIDTH)
        for sl_ in slabs:
            x = rope_ret(y[:, sl_])
            out_ref[:, sl_] = x.astype(BF16)
            scaled_ref[:, sl_] = (x * tab_ref[:, sl_]).astype(BF16)
        col += RET_WIDTH
    y = project(col, RET_WIDTH)
    for sl_ in slabs:
        rv_ref[:, sl_] = y[:, sl_].astype(BF16)
    col += RET_WIDTH
    y = project(col, RET_WIDTH)
    for sl_ in slabs:
        g = y[:, sl_]
        rg_ref[:, sl_] = (g * (1.0 / (1.0 + jnp.exp(-g))) * gn_ref[:, sl_]).astype(BF16)
    col += RET_WIDTH

    y = project(col, SWA_WIDTH)
    scale = SWA_HEAD_DIM ** -0.5 * LOG2E
    for sl_ in slabs:
        sq_ref[:, sl_] = (rope_swa(y[:, sl_]) * scale).astype(BF16)
    col += SWA_WIDTH

    y = project(col, 2 * SWA_KV_WIDTH)
    lo = lax.broadcasted_iota(jnp.int32, (tm, LANES), 1) < half
    for transposed, out_ref, x in ((False, sk_ref, rope_swa(y[:, :LANES])), (True, sv_ref, y[:, LANES:])):
        xr = pltpu.roll(x, half, 1)
        variants = (jnp.where(lo, x, 0.0), jnp.where(lo, 0.0, xr),
                    jnp.where(lo, xr, 0.0), jnp.where(lo, 0.0, x))
        for s, v in enumerate(variants):
            if transposed:
                out_ref[s * LANES:(s + 1) * LANES, :] = v.T.astype(BF16)
            else:
                out_ref[:, s * LANES:(s + 1) * LANES] = v.astype(BF16)


def _inproj(h, gain, w_in, gn_gain, xi_tile, zeta_tile, seq):
    t, d = h.shape
    tables = _rope_tables(seq)
    tiles_per_seq = seq // ROW_TM
    tab_spec = pl.BlockSpec((ROW_TM, LANES), lambda i: (i % tiles_per_seq, 0))

    def row_spec(width):
        return pl.BlockSpec((ROW_TM, width), lambda i: (i, 0))

    widths = (RET_WIDTH,) * 6 + (SWA_WIDTH, 4 * LANES)
    return pl.pallas_call(
        _inproj_kernel,
        out_shape=[jax.ShapeDtypeStruct((t, w), BF16) for w in widths]
        + [jax.ShapeDtypeStruct((4 * LANES, t), BF16)],
        grid=(t // ROW_TM,),
        in_specs=[row_spec(d), _resident((1, d)), _resident((d, IN_COLS))] + [tab_spec] * 5
        + [_resident((ROW_TM, RET_WIDTH))] * 2 + [_resident((1, RET_WIDTH))],
        out_specs=[row_spec(w) for w in widths] + [pl.BlockSpec((4 * LANES, ROW_TM), lambda i: (0, i))],
        scratch_shapes=[pltpu.VMEM((ROW_TM, d), BF16)],
        compiler_params=_params(1),
        name="inproj",
    )(h, gain, w_in, *tables, xi_tile, zeta_tile, gn_gain)


def _mixers_kernel(q_ref, qx_ref, k_ref, kz_ref, v_ref, g_ref, dmat_ref,
                   sink_ref, sq_ref, kp_ref, kc_ref, vp_ref, vc_ref, ret_ref, swa_ref, state_ref, *, chunk_decay):
    n = pl.program_id(1)
    w = WINDOW

    @pl.when(n == 0)
    def _():
        state_ref[...] = jnp.zeros_like(state_ref)

    pairs = SWA_HEADS // SWA_KV_HEADS // 2
    key = lax.broadcasted_iota(jnp.int32, (2 * w, pairs * w), 0)
    qry = lax.broadcasted_iota(jnp.int32, (2 * w, pairs * w), 1) % w
    first_key = jnp.where(n == 0, w, 0)
    valid = (key > qry) & (key <= qry + w) & (key >= first_key)

    heads = range(RET_HEADS)
    hs = [slice(h * RET_DK, (h + 1) * RET_DK) for h in heads]
    lane_blk = lambda c: slice(c * LANES, (c + 1) * LANES)
    phases = [(g, e) for g in range(SWA_KV_HEADS) for e in range(2)]
    slabs = {g: [g * pairs + p for p in range(pairs)] for g in range(SWA_KV_HEADS)}

    s_ret = [_dot_nt(q_ref[:, hs[h]], k_ref[:, hs[h]]) for h in heads]
    q_swa = {g: jnp.concatenate([sq_ref[:, lane_blk(sl)] for sl in slabs[g]], 0) for g in slabs}
    s_swa = []
    for g, e in phases:
        c = 2 * g + e
        k = jnp.concatenate([kp_ref[:, lane_blk(c)], kc_ref[:, lane_blk(c)]], 0)
        s_swa.append(jnp.where(valid, _dot_nt(k, q_swa[g]), NEG_INF))

    states = [state_ref[h] for h in heads]
    ret = []
    for h in heads:
        lhs = jnp.concatenate([(s_ret[h] * dmat_ref[h]).astype(BF16), qx_ref[:, hs[h]]], 1)
        rhs = jnp.concatenate([v_ref[:, hs[h]], states[h].astype(BF16)], 0)
        ret.append(_dot(lhs, rhs))
    for h in heads:
        state_ref[h] = states[h] * chunk_decay[h] + _dot_tn(kz_ref[:, hs[h]], v_ref[:, hs[h]])

    pv = []
    for (g, e), s in zip(phases, s_swa):
        c = 2 * g + e
        sink = jnp.concatenate([jnp.full((1, w), sink_ref[2 * sl + e] * LOG2E, F32) for sl in slabs[g]], 1)
        m = jnp.maximum(jnp.max(s, axis=0, keepdims=True), sink)
        p = jnp.exp2(s - m)
        inv = 1.0 / (jnp.sum(p, axis=0, keepdims=True) + jnp.exp2(sink - m))
        v_t = jnp.concatenate([vp_ref[lane_blk(c), :], vc_ref[lane_blk(c), :]], 1)
        pv.append(_dot(v_t, p.astype(BF16)) * inv)

    for h in heads:
        mu = jnp.mean(ret[h], axis=-1, keepdims=True)
        cen = ret[h] - mu
        var = jnp.mean(cen * cen, axis=-1, keepdims=True)
        ret_ref[:, hs[h]] = (cen * lax.rsqrt(var + EPS) * g_ref[:, hs[h]].astype(F32)).astype(BF16)

    for g in slabs:
        acc = pv[2 * g] + pv[2 * g + 1]
        for i, sl in enumerate(slabs[g]):
            swa_ref[lane_blk(sl), :] = acc[:, i * w:(i + 1) * w].astype(BF16)


def _mixers(rq, rqx, rk, rkz, rv, rg, dmat, chunk_decay, sq, sk4, sv4_t, sinks, batch, seq, cast_weights):
    assert RET_CHUNK == WINDOW
    t = rq.shape[0]
    n_blocks = seq // WINDOW
    grid = (batch, n_blocks)
    cur = lambda b, n: (b * n_blocks + n, 0)
    prev = lambda b, n: (b * n_blocks + jnp.maximum(n - 1, 0), 0)
    cur_t = lambda b, n: (0, b * n_blocks + n)
    prev_t = lambda b, n: (0, b * n_blocks + jnp.maximum(n - 1, 0))
    blk = pl.BlockSpec((RET_CHUNK, RET_WIDTH), cur)
    in_specs = [
        blk, blk, blk, blk, blk, blk, _resident((RET_HEADS, RET_CHUNK, RET_CHUNK)),
        pl.BlockSpec(memory_space=pltpu.SMEM),
        pl.BlockSpec((WINDOW, SWA_WIDTH), cur),
        pl.BlockSpec((WINDOW, 4 * LANES), prev),
        pl.BlockSpec((WINDOW, 4 * LANES), cur),
        pl.BlockSpec((4 * LANES, WINDOW), prev_t),
        pl.BlockSpec((4 * LANES, WINDOW), cur_t),
    ]
    cast_specs, cast_shapes = _cast_specs(cast_weights, grid)
    host = functools.partial(_mixers_kernel, chunk_decay=chunk_decay)
    out = pl.pallas_call(
        _with_casts(host, len(in_specs), 2, len(cast_weights)),
        out_shape=[jax.ShapeDtypeStruct((t, RET_WIDTH), BF16), jax.ShapeDtypeStruct((SWA_WIDTH, t), BF16)]
        + cast_shapes,
        grid=grid,
        in_specs=in_specs + cast_specs,
        out_specs=[blk, pl.BlockSpec((SWA_WIDTH, WINDOW), cur_t)] + cast_specs,
        scratch_shapes=[pltpu.VMEM((RET_HEADS, RET_DK, RET_DV), F32)],
        compiler_params=_params(2),
        name="mixers",
    )(rq, rqx, rk, rkz, rv, rg, dmat, sinks, sq, sk4, sk4, sv4_t, sv4_t, *cast_weights)
    return out[0], out[1], out[2:]


def _outproj_kernel(h_ref, ret_ref, swa_ref, w_ref, g_ref, h2_ref, hn_ref):
    tm = h_ref.shape[0]
    y = _dot(ret_ref[...], w_ref[:RET_WIDTH, :]) + _dot_tn(swa_ref[...], w_ref[RET_WIDTH:, :])
    h2_ref[...] = h_ref[...] + y

    def body(r, carry):
        rows = pl.ds(pl.multiple_of(r * NORM_ROWS, NORM_ROWS), NORM_ROWS)
        hn_ref[rows, :] = _rms_rows(h2_ref[rows, :], g_ref[...]).astype(BF16)
        return carry
    lax.fori_loop(0, tm // NORM_ROWS, body, 0)


def _outproj(h, ret, swa, w_out, gain):
    t, d = h.shape
    row = lambda width: pl.BlockSpec((ROW_TM, width), lambda i: (i, 0))
    return pl.pallas_call(
        _outproj_kernel,
        out_shape=[jax.ShapeDtypeStruct((t, d), F32), jax.ShapeDtypeStruct((t, d), BF16)],
        grid=(t // ROW_TM,),
        in_specs=[row(d), row(RET_WIDTH), pl.BlockSpec((SWA_WIDTH, ROW_TM), lambda i: (0, i)),
                  _resident(w_out.shape), _resident((1, d))],
        out_specs=[row(d), row(d)],
        compiler_params=_params(1),
        name="outproj",
    )(h, ret, swa, w_out, gain)


def _memkv_kernel(m_ref, g_ref, w_ref, o_ref):
    xn = _rms_rows(m_ref[...], g_ref[...]).astype(BF16)
    o_ref[...] = _dot(xn, w_ref[...].astype(BF16)).astype(BF16)


def _memkv(mem, gain, wkv):
    rows, d = mem.shape
    n_out = wkv.shape[1]
    tn = 1024
    return pl.pallas_call(
        _memkv_kernel,
        out_shape=jax.ShapeDtypeStruct((rows, n_out), BF16),
        grid=(n_out // tn,),
        in_specs=[_resident((rows, d)), _resident((1, d)), pl.BlockSpec((d, tn), lambda j: (0, j))],
        out_specs=pl.BlockSpec((rows, tn), lambda j: (0, j)),
        compiler_params=_params(1),
        name="memkv",
    )(mem, gain, wkv)


def _xattn_kernel(hn_ref, h_ref, wq_ref, wo_ref, k_ref, v_ref, o_ref, att_ref):
    q = _dot(hn_ref[...], wq_ref[...]).astype(BF16)
    scale = XA_HEAD_DIM ** -0.5
    for hd in range(XA_HEADS):
        hs = slice(hd * XA_HEAD_DIM, (hd + 1) * XA_HEAD_DIM)
        s = _dot_nt(q[:, hs], k_ref[:, hs]) * scale
        m = jnp.max(s, axis=-1, keepdims=True)
        p = jnp.exp(s - m)
        inv = 1.0 / jnp.sum(p, axis=-1, keepdims=True)
        att_ref[:, hs] = (_dot(p.astype(BF16), v_ref[:, hs]) * inv).astype(BF16)
    o_ref[...] = h_ref[...] + _dot(att_ref[...], wo_ref[...])


def _xattn(hn, h, wq, wo, mkv, seq, mem_len, cast_weights):
    t, d = h.shape
    tiles_per_seq = seq // ROW_TM
    grid = (t // ROW_TM,)
    row = lambda i: (i, 0)
    cast_specs, cast_shapes = _cast_specs(cast_weights, grid)
    out = pl.pallas_call(
        _with_casts(_xattn_kernel, 6, 1, len(cast_weights)),
        out_shape=[jax.ShapeDtypeStruct((t, d), F32)] + cast_shapes,
        grid=grid,
        in_specs=[
            pl.BlockSpec((ROW_TM, d), row),
            pl.BlockSpec((ROW_TM, d), row),
            _resident((d, d)),
            _resident((d, d)),
            pl.BlockSpec((mem_len, d), lambda i: (i // tiles_per_seq, 0)),
            pl.BlockSpec((mem_len, d), lambda i: (i // tiles_per_seq, 1)),
        ] + cast_specs,
        out_specs=[pl.BlockSpec((ROW_TM, d), row)] + cast_specs,
        scratch_shapes=[pltpu.VMEM((ROW_TM, d), BF16)],
        compiler_params=_params(1),
        name="xattn",
    )(hn, h, wq, wo, mkv, mkv, *cast_weights)
    return out[0], out[1:]


def kernel(x, mem, ffn1_norm, ffn1_w_gate, ffn1_w_up, ffn1_w_down, mix_norm, w_in, ret_gn_gain, swa_sinks,
           w_out, xa_norm, mem_norm, xa_wq, xa_wkv, xa_wo, ffn2_norm, ffn2_w_gate, ffn2_w_up, ffn2_w_down,
           final_norm):
    batch, seq, d = x.shape
    mem_len = mem.shape[1]
    depth = ffn1_norm.shape[0]
    h = x.reshape(batch * seq, d)
    mem2 = mem.reshape(batch * mem_len, d)
    row = lambda g: g.reshape(1, -1).astype(F32)
    final_gain = row(final_norm)

    for l in range(depth):
        last = l == depth - 1
        h1, w1_gate, w1_up, w1_down = _ffn_head(h, row(ffn1_norm[l]), ffn1_w_gate[l], ffn1_w_up[l], ffn1_w_down[l])
        h, (w_mix,) = _ffn(h, row(ffn1_norm[l]), w1_gate, w1_up, w1_down, final_gain, final_norm=False,
                           first_tile=1, into=h1, cast_weights=[w_in[l]])
        (dmat, xi_tile, zeta_tile), chunk_decay = _retention_tables()
        rq, rqx, rk, rkz, rv, rg, sq, sk4, sv4_t = _inproj(h, row(mix_norm[l]), w_mix, row(ret_gn_gain[l]),
                                                           xi_tile, zeta_tile, seq)
        ret, swa, (w2_gate, w2_up, wo_mix, wq, wo) = _mixers(
            rq, rqx, rk, rkz, rv, rg, dmat, chunk_decay, sq, sk4, sv4_t, swa_sinks[l].astype(F32), batch, seq,
            [ffn2_w_gate[l], ffn2_w_up[l], w_out[l], xa_wq[l], xa_wo[l]])
        h, hn = _outproj(h, ret, swa, wo_mix, row(xa_norm[l]))
        mkv = _memkv(mem2, row(mem_norm[l]), xa_wkv[l])
        h, (w2_down,) = _xattn(hn, h, wq, wo, mkv, seq, mem_len, [ffn2_w_down[l]])
        h, _ = _ffn(h, row(ffn2_norm[l]), w2_gate, w2_up, w2_down, final_gain, final_norm=last)
    if depth == 0:
        raise ValueError("depth must be at least 1")
    return h.reshape(batch, seq, d)
```

```python
import functools

import numpy as np
import jax
import jax.numpy as jnp
from jax import lax
from jax.experimental import pallas as pl
from jax.experimental.pallas import tpu as pltpu

F32 = jnp.float32
BF16 = jnp.bfloat16

D_MODEL = 2048
D_FF = 5632
RET_HEADS = 8
RET_DK = 128
RET_DV = 128
RET_WIDTH = RET_HEADS * RET_DV
RET_CHUNK = 128
SWA_HEADS = 16
SWA_KV_HEADS = 2
SWA_HEAD_DIM = 64
SWA_WIDTH = SWA_HEADS * SWA_HEAD_DIM
SWA_KV_WIDTH = SWA_KV_HEADS * SWA_HEAD_DIM
WINDOW = 128
XA_HEADS = 4
XA_HEAD_DIM = D_MODEL // XA_HEADS
ROPE_THETA = 10000.0
EPS = 1e-6
IN_SIZES = (RET_WIDTH, RET_WIDTH, RET_WIDTH, RET_WIDTH, SWA_WIDTH, SWA_KV_WIDTH, SWA_KV_WIDTH)
IN_COLS = sum(IN_SIZES)

LANES = 128
BF16_SUBLANES = 16
V7X_VMEM_BYTES = 64 * 1024 * 1024
VMEM_LIMIT_BYTES = 60000 * 1024

FFN_TM = 1024
FFN_TF = 512
FFN_HEAD_TF = 256
ROW_TM = 512
NORM_ROWS = 256
NEG_INF = float(np.finfo(np.float32).min)
LOG2E = float(np.log2(np.e))


def _params(n_axes):
    return pltpu.CompilerParams(
        dimension_semantics=("arbitrary",) * n_axes,
        vmem_limit_bytes=VMEM_LIMIT_BYTES,
    )


def _resident(shape):
    zeros = (0,) * len(shape)
    return pl.BlockSpec(shape, lambda *_: zeros, pipeline_mode=pl.Buffered(1))


def _rms_rows(x, gain):
    ms = jnp.mean(x * x, axis=-1, keepdims=True)
    return x * lax.rsqrt(ms + EPS) * gain


def _dot(a, b):
    return lax.dot_general(a, b, (((1,), (0,)), ((), ())), preferred_element_type=F32)


def _dot_nt(a, b):
    return lax.dot_general(a, b, (((1,), (1,)), ((), ())), preferred_element_type=F32)


def _dot_tn(a, b):
    return lax.dot_general(a, b, (((0,), (0,)), ((), ())), preferred_element_type=F32)


def _cast_block(shape, n_steps):
    rows, cols = shape
    for col_splits in (1, 2, 4, 8):
        row_blocks, rem = divmod(n_steps, col_splits)
        if rem or rows % row_blocks or cols % col_splits:
            continue
        br, bc = rows // row_blocks, cols // col_splits
        if br % BF16_SUBLANES == 0 and bc % LANES == 0:
            return br, bc, col_splits
    raise ValueError(f"no aligned {n_steps}-way split of {shape}")


def _with_casts(kernel_fn, n_in, n_out, n_cast):
    def wrapped(*refs):
        ins, rest = refs[:n_in], refs[n_in:]
        cast_in, rest = rest[:n_cast], rest[n_cast:]
        outs, rest = rest[:n_out], rest[n_out:]
        cast_out, scratch = rest[:n_cast], rest[n_cast:]
        kernel_fn(*ins, *outs, *scratch)
        for src, dst in zip(cast_in, cast_out):
            dst[...] = src[...].astype(BF16)
    return wrapped


def _cast_specs(weights, grid):
    n_steps = int(np.prod(grid))
    specs, shapes = [], []
    for w in weights:
        br, bc, col_splits = _cast_block(w.shape, n_steps)

        def index_map(*idx, col_splits=col_splits):
            step = idx[0]
            for size, i in zip(grid[1:], idx[1:]):
                step = step * size + i
            return step // col_splits, step % col_splits
        specs.append(pl.BlockSpec((br, bc), index_map))
        shapes.append(jax.ShapeDtypeStruct(w.shape, BF16))
    return specs, shapes


def _cast_specs_tiled(weights, grid):
    n_outer, n_inner = grid
    specs, shapes = [], []
    for w in weights:
        rows, cols = w.shape
        bc = cols // n_outer
        row_blocks = max(r for r in range(1, n_inner + 1) if rows % r == 0 and (rows // r) % BF16_SUBLANES == 0)
        if cols % n_outer or bc % LANES:
            raise ValueError(f"no aligned split of {w.shape} over {grid}")

        def index_map(i, j, row_blocks=row_blocks):
            return jnp.minimum(j, row_blocks - 1), i
        specs.append(pl.BlockSpec((rows // row_blocks, bc), index_map))
        shapes.append(jax.ShapeDtypeStruct(w.shape, BF16))
    return specs, shapes


def _rows_loop(n_rows, fn):
    def body(r, carry):
        fn(pl.ds(pl.multiple_of(r * NORM_ROWS, NORM_ROWS), NORM_ROWS))
        return carry
    lax.fori_loop(0, n_rows // NORM_ROWS, body, 0)


def _ffn_prologue(h_ref, g_ref, o_ref, xn_ref):
    def norm(rows):
        h = h_ref[rows, :]
        xn_ref[rows, :] = _rms_rows(h, g_ref[...]).astype(BF16)
        o_ref[rows, :] = h
    _rows_loop(h_ref.shape[0], norm)


def _ffn_step(xn_ref, wg, wu, wd, o_ref):
    xn = xn_ref[...]
    g = _dot(xn, wg)
    u = _dot(xn, wu)
    a = (g * (0.5 / (1.0 + jnp.exp(-g))) * u).astype(BF16)
    for c in range(0, o_ref.shape[1], FFN_TF):
        o_ref[:, c:c + FFN_TF] += _dot(a, wd[:, c:c + FFN_TF])


def _ffn_kernel(*refs, n_ff, final_norm, aliased):
    if aliased:
        refs = refs[1:]
    h_ref, g_ref, wg_ref, wu_ref, wd_ref, fg_ref, o_ref, xn_ref = refs
    j = pl.program_id(1)

    @pl.when(j == 0)
    def _():
        _ffn_prologue(h_ref, g_ref, o_ref, xn_ref)

    _ffn_step(xn_ref, wg_ref[...], wu_ref[...], wd_ref[...], o_ref)

    if final_norm:
        @pl.when(j == n_ff - 1)
        def _():
            def norm(rows):
                o_ref[rows, :] = _rms_rows(o_ref[rows, :], fg_ref[...])
            _rows_loop(o_ref.shape[0], norm)


def _ffn(h, gain, wg, wu, wd, final_gain, *, final_norm, first_tile=0, into=None, cast_weights=()):
    t, d = h.shape
    f = wg.shape[1]
    n_ff = f // FFN_TF
    grid = (t // FFN_TM - first_tile, n_ff)
    aliased = into is not None
    tile = lambda i, j: (i + first_tile, 0)
    in_specs = [
        pl.BlockSpec((FFN_TM, d), tile),
        pl.BlockSpec((1, d), lambda i, j: (0, 0)),
        pl.BlockSpec((d, FFN_TF), lambda i, j: (0, j)),
        pl.BlockSpec((d, FFN_TF), lambda i, j: (0, j)),
        pl.BlockSpec((FFN_TF, d), lambda i, j: (j, 0)),
        pl.BlockSpec((1, d), lambda i, j: (0, 0)),
    ]
    args = (h, gain, wg, wu, wd, final_gain)
    if aliased:
        in_specs = [pl.BlockSpec(memory_space=pl.ANY)] + in_specs
        args = (into,) + args
    cast_specs, cast_shapes = _cast_specs_tiled(cast_weights, grid)
    host = functools.partial(_ffn_kernel, n_ff=n_ff, final_norm=final_norm, aliased=aliased)
    out = pl.pallas_call(
        _with_casts(host, len(in_specs), 1, len(cast_weights)),
        out_shape=[jax.ShapeDtypeStruct((t, d), F32)] + cast_shapes,
        grid=grid,
        in_specs=in_specs + cast_specs,
        out_specs=[pl.BlockSpec((FFN_TM, d), tile)] + cast_specs,
        scratch_shapes=[pltpu.VMEM((FFN_TM, d), BF16)],
        input_output_aliases={0: 0} if aliased else {},
        compiler_params=_params(2),
        name="ffn",
    )(*args, *cast_weights)
    return out[0], out[1:]


def _ffn_head_kernel(h_ref, g_ref, wg_ref, wu_ref, wd_ref, o_ref, wg_out, wu_out, wd_out, xn_ref):
    @pl.when(pl.program_id(0) == 0)
    def _():
        _ffn_prologue(h_ref, g_ref, o_ref, xn_ref)

    wg, wu, wd = (w[...].astype(BF16) for w in (wg_ref, wu_ref, wd_ref))
    wg_out[...] = wg
    wu_out[...] = wu
    wd_out[...] = wd
    _ffn_step(xn_ref, wg, wu, wd, o_ref)


def _ffn_head(h, gain, wg, wu, wd):
    t, d = h.shape
    f = wg.shape[1]
    tf = FFN_HEAD_TF
    col = pl.BlockSpec((d, tf), lambda j: (0, j))
    row = pl.BlockSpec((tf, d), lambda j: (j, 0))
    tile0 = pl.BlockSpec((FFN_TM, d), lambda j: (0, 0))
    return pl.pallas_call(
        _ffn_head_kernel,
        out_shape=[jax.ShapeDtypeStruct((t, d), F32), jax.ShapeDtypeStruct(wg.shape, BF16),
                   jax.ShapeDtypeStruct(wu.shape, BF16), jax.ShapeDtypeStruct(wd.shape, BF16)],
        grid=(f // tf,),
        in_specs=[tile0, pl.BlockSpec((1, d), lambda j: (0, 0)), col, col, row],
        out_specs=[tile0, col, col, row],
        scratch_shapes=[pltpu.VMEM((FFN_TM, d), BF16)],
        compiler_params=_params(1),
        name="ffn_head",
    )(h, gain, wg, wu, wd)


def _rope_tables(seq):
    pos = np.arange(seq, dtype=np.float32)

    def angles(d):
        inv = np.float32(ROPE_THETA) ** (-np.arange(0, d, 2, dtype=np.float32) / np.float32(d))
        return (pos[:, None] * inv[None, :].astype(np.float32)).astype(np.float32).astype(np.float64)

    a128 = angles(RET_DK)
    cos_r = np.concatenate([np.cos(a128), np.cos(a128)], -1)
    sin_r = np.concatenate([-np.sin(a128), np.sin(a128)], -1)
    a64 = angles(SWA_HEAD_DIM)
    c, s, z = np.cos(a64), np.sin(a64), np.zeros_like(a64)
    cos_s = np.concatenate([c, c, c, c], -1)
    sin_lo = np.concatenate([-s, z, -s, z], -1)
    sin_hi = np.concatenate([z, s, z, s], -1)
    return [jnp.asarray(v, dtype=F32) for v in (cos_r, sin_r, cos_s, sin_lo, sin_hi)]


def _retention_tables():
    c = RET_CHUNK
    heads = np.arange(RET_HEADS, dtype=np.float64)
    log_gamma = np.log1p(-np.exp2(-5.0 - heads))
    idx = np.arange(c, dtype=np.float64)
    diff = idx[:, None] - idx[None, :]
    scale = RET_DK ** -0.5
    dmat = np.where(diff[None] >= 0, np.exp(np.maximum(diff, 0.0)[None] * log_gamma[:, None, None]), 0.0)
    zeta = np.exp((c - 1.0 - idx)[None, :] * log_gamma[:, None])
    xi = np.exp((idx + 1.0)[None, :] * log_gamma[:, None])
    chunk_decay = tuple(float(v) for v in np.exp(c * log_gamma))

    def token_tile(tab):
        return np.tile(np.repeat(tab.T, RET_DK, axis=1), (ROW_TM // c, 1))
    tabs = [jnp.asarray(v, dtype=F32) for v in (dmat * scale, token_tile(xi), token_tile(zeta * scale))]
    return tabs, chunk_decay


def _inproj_kernel(h_ref, g_ref, w_ref, cr_ref, sr_ref, cs_ref, sl_ref, sh_ref, xi_ref, zeta_ref, gn_ref,
                   rq_ref, rqx_ref, rk_ref, rkz_ref, rv_ref, rg_ref, sq_ref, sk_ref, sv_ref, xn_ref):
    tm = h_ref.shape[0]

    def norm(rows):
        xn_ref[rows, :] = _rms_rows(h_ref[rows, :], g_ref[...]).astype(BF16)
    _rows_loop(tm, norm)

    xn = xn_ref[...]
    cr, sr = cr_ref[...], sr_ref[...]
    cs, sl, sh = cs_ref[...], sl_ref[...], sh_ref[...]
    half = LANES // 2
    slabs = [slice(s * LANES, (s + 1) * LANES) for s in range(RET_WIDTH // LANES)]

    def rope_ret(x):
        return x * cr + pltpu.roll(x, half, 1) * sr

    def rope_swa(x):
        return x * cs + pltpu.roll(x, LANES - half // 2, 1) * sl + pltpu.roll(x, half // 2, 1) * sh

    def project(col0, width):
        return _dot(xn, w_ref[:, col0:col0 + width])

    def ret_rotary(out_ref, scaled_ref, tab_ref):
        def epilogue(y):
            for sl_ in slabs:
                x = rope_ret(y[:, sl_])
                out_ref[:, sl_] = x.astype(BF16)
                scaled_ref[:, sl_] = (x * tab_ref[:, sl_]).astype(BF16)
        return epilogue

    def ret_values(y):
        for sl_ in slabs:
            rv_ref[:, sl_] = y[:, sl_].astype(BF16)

    def ret_gate(y):
        for sl_ in slabs:
            g = y[:, sl_]
            rg_ref[:, sl_] = (g * (1.0 / (1.0 + jnp.exp(-g))) * gn_ref[:, sl_]).astype(BF16)

    def swa_queries(y):
        scale = SWA_HEAD_DIM ** -0.5 * LOG2E
        for sl_ in slabs:
            sq_ref[:, sl_] = (rope_swa(y[:, sl_]) * scale).astype(BF16)

    def swa_keys_values(y):
        lo = lax.broadcasted_iota(jnp.int32, (tm, LANES), 1) < half
        for transposed, out_ref, x in ((False, sk_ref, rope_swa(y[:, :LANES])), (True, sv_ref, y[:, LANES:])):
            xr = pltpu.roll(x, half, 1)
            variants = (jnp.where(lo, x, 0.0), jnp.where(lo, 0.0, xr),
                        jnp.where(lo, xr, 0.0), jnp.where(lo, 0.0, x))
            for s, v in enumerate(variants):
                if transposed:
                    out_ref[s * LANES:(s + 1) * LANES, :] = v.T.astype(BF16)
                else:
                    out_ref[:, s * LANES:(s + 1) * LANES] = v.astype(BF16)

    epilogues = (ret_rotary(rq_ref, rqx_ref, xi_ref), ret_rotary(rk_ref, rkz_ref, zeta_ref), ret_values, ret_gate,
                 swa_queries, swa_keys_values)
    col0 = np.cumsum((0,) + IN_SIZES[:-2])
    widths = IN_SIZES[:-2] + (2 * SWA_KV_WIDTH,)
    order = (5, 0, 1, 3, 4, 2)
    for r in order:
        epilogues[r](project(int(col0[r]), widths[r]))


def _inproj(h, gain, w_in, gn_gain, xi_tile, zeta_tile, seq):
    t, d = h.shape
    tables = _rope_tables(seq)
    tiles_per_seq = seq // ROW_TM
    tab_spec = pl.BlockSpec((ROW_TM, LANES), lambda i: (i % tiles_per_seq, 0))

    def row_spec(width):
        return pl.BlockSpec((ROW_TM, width), lambda i: (i, 0))

    widths = (RET_WIDTH,) * 6 + (SWA_WIDTH, 4 * LANES)
    return pl.pallas_call(
        _inproj_kernel,
        out_shape=[jax.ShapeDtypeStruct((t, w), BF16) for w in widths]
        + [jax.ShapeDtypeStruct((4 * LANES, t), BF16)],
        grid=(t // ROW_TM,),
        in_specs=[row_spec(d), _resident((1, d)), _resident((d, IN_COLS))] + [tab_spec] * 5
        + [_resident((ROW_TM, RET_WIDTH))] * 2 + [_resident((1, RET_WIDTH))],
        out_specs=[row_spec(w) for w in widths] + [pl.BlockSpec((4 * LANES, ROW_TM), lambda i: (0, i))],
        scratch_shapes=[pltpu.VMEM((ROW_TM, d), BF16)],
        compiler_params=_params(1),
        name="inproj",
    )(h, gain, w_in, *tables, xi_tile, zeta_tile, gn_gain)


def _mixers_kernel(q_ref, qx_ref, k_ref, kz_ref, v_ref, g_ref, dmat_ref,
                   sink_ref, sq_ref, kp_ref, kc_ref, vp_ref, vc_ref, ret_ref, swa_ref, state_ref, *, chunk_decay):
    n = pl.program_id(1)
    w = WINDOW

    @pl.when(n == 0)
    def _():
        state_ref[...] = jnp.zeros_like(state_ref)

    pairs = SWA_HEADS // SWA_KV_HEADS // 2
    key = lax.broadcasted_iota(jnp.int32, (2 * w, pairs * w), 0)
    qry = lax.broadcasted_iota(jnp.int32, (2 * w, pairs * w), 1) % w
    first_key = jnp.where(n == 0, w, 0)
    valid = (key > qry) & (key <= qry + w) & (key >= first_key)

    heads = range(RET_HEADS)
    hs = [slice(h * RET_DK, (h + 1) * RET_DK) for h in heads]
    lane_blk = lambda c: slice(c * LANES, (c + 1) * LANES)
    phases = [(g, e) for g in range(SWA_KV_HEADS) for e in range(2)]
    slabs = {g: [g * pairs + p for p in range(pairs)] for g in range(SWA_KV_HEADS)}

    s_ret = [_dot_nt(q_ref[:, hs[h]], k_ref[:, hs[h]]) for h in heads]
    q_swa = {g: jnp.concatenate([sq_ref[:, lane_blk(sl)] for sl in slabs[g]], 0) for g in slabs}
    s_swa = []
    for g, e in phases:
        c = 2 * g + e
        k = jnp.concatenate([kp_ref[:, lane_blk(c)], kc_ref[:, lane_blk(c)]], 0)
        s_swa.append(jnp.where(valid, _dot_nt(k, q_swa[g]), NEG_INF))

    states = [state_ref[h] for h in heads]
    ret = []
    for h in heads:
        lhs = jnp.concatenate([(s_ret[h] * dmat_ref[h]).astype(BF16), qx_ref[:, hs[h]]], 1)
        rhs = jnp.concatenate([v_ref[:, hs[h]], states[h].astype(BF16)], 0)
        ret.append(_dot(lhs, rhs))
    for h in heads:
        state_ref[h] = states[h] * chunk_decay[h] + _dot_tn(kz_ref[:, hs[h]], v_ref[:, hs[h]])

    pv = []
    for (g, e), s in zip(phases, s_swa):
        c = 2 * g + e
        sink = jnp.concatenate([jnp.full((1, w), sink_ref[2 * sl + e] * LOG2E, F32) for sl in slabs[g]], 1)
        m = jnp.maximum(jnp.max(s, axis=0, keepdims=True), sink)
        p = jnp.exp2(s - m)
        inv = 1.0 / (jnp.sum(p, axis=0, keepdims=True) + jnp.exp2(sink - m))
        v_t = jnp.concatenate([vp_ref[lane_blk(c), :], vc_ref[lane_blk(c), :]], 1)
        pv.append(_dot(v_t, p.astype(BF16)) * inv)

    for h in heads:
        mu = jnp.mean(ret[h], axis=-1, keepdims=True)
        cen = ret[h] - mu
        var = jnp.mean(cen * cen, axis=-1, keepdims=True)
        ret_ref[:, hs[h]] = (cen * lax.rsqrt(var + EPS) * g_ref[:, hs[h]].astype(F32)).astype(BF16)

    for g in slabs:
        acc = pv[2 * g] + pv[2 * g + 1]
        for i, sl in enumerate(slabs[g]):
            swa_ref[lane_blk(sl), :] = acc[:, i * w:(i + 1) * w].astype(BF16)


def _mixers(rq, rqx, rk, rkz, rv, rg, dmat, chunk_decay, sq, sk4, sv4_t, sinks, batch, seq, cast_weights):
    assert RET_CHUNK == WINDOW
    t = rq.shape[0]
    n_blocks = seq // WINDOW
    grid = (batch, n_blocks)
    cur = lambda b, n: (b * n_blocks + n, 0)
    prev = lambda b, n: (b * n_blocks + jnp.maximum(n - 1, 0), 0)
    cur_t = lambda b, n: (0, b * n_blocks + n)
    prev_t = lambda b, n: (0, b * n_blocks + jnp.maximum(n - 1, 0))
    blk = pl.BlockSpec((RET_CHUNK, RET_WIDTH), cur)
    in_specs = [
        blk, blk, blk, blk, blk, blk, _resident((RET_HEADS, RET_CHUNK, RET_CHUNK)),
        pl.BlockSpec(memory_space=pltpu.SMEM),
        pl.BlockSpec((WINDOW, SWA_WIDTH), cur),
        pl.BlockSpec((WINDOW, 4 * LANES), prev),
        pl.BlockSpec((WINDOW, 4 * LANES), cur),
        pl.BlockSpec((4 * LANES, WINDOW), prev_t),
        pl.BlockSpec((4 * LANES, WINDOW), cur_t),
    ]
    cast_specs, cast_shapes = _cast_specs(cast_weights, grid)
    host = functools.partial(_mixers_kernel, chunk_decay=chunk_decay)
    out = pl.pallas_call(
        _with_casts(host, len(in_specs), 2, len(cast_weights)),
        out_shape=[jax.ShapeDtypeStruct((t, RET_WIDTH), BF16), jax.ShapeDtypeStruct((SWA_WIDTH, t), BF16)]
        + cast_shapes,
        grid=grid,
        in_specs=in_specs + cast_specs,
        out_specs=[blk, pl.BlockSpec((SWA_WIDTH, WINDOW), cur_t)] + cast_specs,
        scratch_shapes=[pltpu.VMEM((RET_HEADS, RET_DK, RET_DV), F32)],
        compiler_params=_params(2),
        name="mixers",
    )(rq, rqx, rk, rkz, rv, rg, dmat, sinks, sq, sk4, sk4, sv4_t, sv4_t, *cast_weights)
    return out[0], out[1], out[2:]


def _outproj_kernel(h_ref, ret_ref, swa_ref, w_ref, g_ref, h2_ref, hn_ref):
    tm = h_ref.shape[0]
    y = _dot(ret_ref[...], w_ref[:RET_WIDTH, :]) + _dot_tn(swa_ref[...], w_ref[RET_WIDTH:, :])
    h2_ref[...] = h_ref[...] + y

    def body(r, carry):
        rows = pl.ds(pl.multiple_of(r * NORM_ROWS, NORM_ROWS), NORM_ROWS)
        hn_ref[rows, :] = _rms_rows(h2_ref[rows, :], g_ref[...]).astype(BF16)
        return carry
    lax.fori_loop(0, tm // NORM_ROWS, body, 0)


def _outproj(h, ret, swa, w_out, gain):
    t, d = h.shape
    row = lambda width: pl.BlockSpec((ROW_TM, width), lambda i: (i, 0))
    return pl.pallas_call(
        _outproj_kernel,
        out_shape=[jax.ShapeDtypeStruct((t, d), F32), jax.ShapeDtypeStruct((t, d), BF16)],
        grid=(t // ROW_TM,),
        in_specs=[row(d), row(RET_WIDTH), pl.BlockSpec((SWA_WIDTH, ROW_TM), lambda i: (0, i)),
                  _resident(w_out.shape), _resident((1, d))],
        out_specs=[row(d), row(d)],
        compiler_params=_params(1),
        name="outproj",
    )(h, ret, swa, w_out, gain)


def _memkv_kernel(m_ref, g_ref, w_ref, o_ref):
    xn = _rms_rows(m_ref[...], g_ref[...]).astype(BF16)
    o_ref[...] = _dot(xn, w_ref[...].astype(BF16)).astype(BF16)


def _memkv(mem, gain, wkv):
    rows, d = mem.shape
    n_out = wkv.shape[1]
    tn = 1024
    return pl.pallas_call(
        _memkv_kernel,
        out_shape=jax.ShapeDtypeStruct((rows, n_out), BF16),
        grid=(n_out // tn,),
        in_specs=[_resident((rows, d)), _resident((1, d)), pl.BlockSpec((d, tn), lambda j: (0, j))],
        out_specs=pl.BlockSpec((rows, tn), lambda j: (0, j)),
        compiler_params=_params(1),
        name="memkv",
    )(mem, gain, wkv)


def _xattn_kernel(hn_ref, h_ref, wq_ref, wo_ref, k_ref, v_ref, o_ref, att_ref):
    q = _dot(hn_ref[...], wq_ref[...]).astype(BF16)
    scale = XA_HEAD_DIM ** -0.5
    heads = [slice(hd * XA_HEAD_DIM, (hd + 1) * XA_HEAD_DIM) for hd in range(XA_HEADS)]
    scores = [_dot_nt(q[:, hs], k_ref[:, hs]) * scale for hs in heads]
    probs, invs = [], []
    for s in scores:
        p = jnp.exp(s - jnp.max(s, axis=-1, keepdims=True))
        probs.append(p.astype(BF16))
        invs.append(1.0 / jnp.sum(p, axis=-1, keepdims=True))
    for hs, p, inv in zip(heads, probs, invs):
        att_ref[:, hs] = (_dot(p, v_ref[:, hs]) * inv).astype(BF16)
    o_ref[...] = h_ref[...] + _dot(att_ref[...], wo_ref[...])


def _xattn(hn, h, wq, wo, mkv, seq, mem_len, cast_weights):
    t, d = h.shape
    tiles_per_seq = seq // ROW_TM
    grid = (t // ROW_TM,)
    row = lambda i: (i, 0)
    cast_specs, cast_shapes = _cast_specs(cast_weights, grid)
    out = pl.pallas_call(
        _with_casts(_xattn_kernel, 6, 1, len(cast_weights)),
        out_shape=[jax.ShapeDtypeStruct((t, d), F32)] + cast_shapes,
        grid=grid,
        in_specs=[
            pl.BlockSpec((ROW_TM, d), row),
            pl.BlockSpec((ROW_TM, d), row),
            _resident((d, d)),
            _resident((d, d)),
            pl.BlockSpec((mem_len, d), lambda i: (i // tiles_per_seq, 0)),
            pl.BlockSpec((mem_len, d), lambda i: (i // tiles_per_seq, 1)),
        ] + cast_specs,
        out_specs=[pl.BlockSpec((ROW_TM, d), row)] + cast_specs,
        scratch_shapes=[pltpu.VMEM((ROW_TM, d), BF16)],
        compiler_params=_params(1),
        name="xattn",
    )(hn, h, wq, wo, mkv, mkv, *cast_weights)
    return out[0], out[1:]


def kernel(x, mem, ffn1_norm, ffn1_w_gate, ffn1_w_up, ffn1_w_down, mix_norm, w_in, ret_gn_gain, swa_sinks,
           w_out, xa_norm, mem_norm, xa_wq, xa_wkv, xa_wo, ffn2_norm, ffn2_w_gate, ffn2_w_up, ffn2_w_down,
           final_norm):
    batch, seq, d = x.shape
    mem_len = mem.shape[1]
    depth = ffn1_norm.shape[0]
    h = x.reshape(batch * seq, d)
    mem2 = mem.reshape(batch * mem_len, d)
    row = lambda g: g.reshape(1, -1).astype(F32)
    final_gain = row(final_norm)

    for l in range(depth):
        last = l == depth - 1
        h1, w1_gate, w1_up, w1_down = _ffn_head(h, row(ffn1_norm[l]), ffn1_w_gate[l], ffn1_w_up[l], ffn1_w_down[l])
        h, (w_mix,) = _ffn(h, row(ffn1_norm[l]), w1_gate, w1_up, w1_down, final_gain, final_norm=False,
                           first_tile=1, into=h1, cast_weights=[w_in[l]])
        (dmat, xi_tile, zeta_tile), chunk_decay = _retention_tables()
        rq, rqx, rk, rkz, rv, rg, sq, sk4, sv4_t = _inproj(h, row(mix_norm[l]), w_mix, row(ret_gn_gain[l]),
                                                           xi_tile, zeta_tile, seq)
        ret, swa, (w2_gate, w2_up, wo_mix, wq, wo) = _mixers(
            rq, rqx, rk, rkz, rv, rg, dmat, chunk_decay, sq, sk4, sv4_t, swa_sinks[l].astype(F32), batch, seq,
            [ffn2_w_gate[l], ffn2_w_up[l], w_out[l], xa_wq[l], xa_wo[l]])
        h, hn = _outproj(h, ret, swa, wo_mix, row(xa_norm[l]))
        mkv = _memkv(mem2, row(mem_norm[l]), xa_wkv[l])
        h, (w2_down,) = _xattn(hn, h, wq, wo, mkv, seq, mem_len, [ffn2_w_down[l]])
        h, _ = _ffn(h, row(ffn2_norm[l]), w2_gate, w2_up, w2_down, final_gain, final_norm=last)
    if depth == 0:
        raise ValueError("depth must be at least 1")
    return h.reshape(batch, seq, d)
```

```python
import functools

import numpy as np
import jax
import jax.numpy as jnp
from jax import lax
from jax.experimental import pallas as pl
from jax.experimental.pallas import tpu as pltpu

F32 = jnp.float32
BF16 = jnp.bfloat16

D_MODEL = 2048
D_FF = 5632
RET_HEADS = 8
RET_DK = 128
RET_DV = 128
RET_WIDTH = RET_HEADS * RET_DV
RET_CHUNK = 128
SWA_HEADS = 16
SWA_KV_HEADS = 2
SWA_HEAD_DIM = 64
SWA_WIDTH = SWA_HEADS * SWA_HEAD_DIM
SWA_KV_WIDTH = SWA_KV_HEADS * SWA_HEAD_DIM
WINDOW = 128
XA_HEADS = 4
XA_HEAD_DIM = D_MODEL // XA_HEADS
ROPE_THETA = 10000.0
EPS = 1e-6
IN_SIZES = (RET_WIDTH, RET_WIDTH, RET_WIDTH, RET_WIDTH, SWA_WIDTH, SWA_KV_WIDTH, SWA_KV_WIDTH)
IN_COLS = sum(IN_SIZES)

LANES = 128
BF16_SUBLANES = 16
V7X_VMEM_BYTES = 64 * 1024 * 1024
VMEM_LIMIT_BYTES = 60000 * 1024

FFN_TM = 1024
FFN_TF = 512
FFN_HEAD_TF = 256
ROW_TM = 512
NORM_ROWS = 256
NEG_INF = float(np.finfo(np.float32).min)
LOG2E = float(np.log2(np.e))


def _params(n_axes):
    return pltpu.CompilerParams(
        dimension_semantics=("arbitrary",) * n_axes,
        vmem_limit_bytes=VMEM_LIMIT_BYTES,
    )


def _resident(shape):
    zeros = (0,) * len(shape)
    return pl.BlockSpec(shape, lambda *_: zeros, pipeline_mode=pl.Buffered(1))


def _rms_rows(x, gain):
    ms = jnp.mean(x * x, axis=-1, keepdims=True)
    return x * lax.rsqrt(ms + EPS) * gain


def _dot(a, b):
    return lax.dot_general(a, b, (((1,), (0,)), ((), ())), preferred_element_type=F32)


def _dot_nt(a, b):
    return lax.dot_general(a, b, (((1,), (1,)), ((), ())), preferred_element_type=F32)


def _dot_tn(a, b):
    return lax.dot_general(a, b, (((0,), (0,)), ((), ())), preferred_element_type=F32)


def _cast_block(shape, n_steps):
    rows, cols = shape
    for col_splits in (1, 2, 4, 8):
        row_blocks, rem = divmod(n_steps, col_splits)
        if rem or rows % row_blocks or cols % col_splits:
            continue
        br, bc = rows // row_blocks, cols // col_splits
        if br % BF16_SUBLANES == 0 and bc % LANES == 0:
            return br, bc, col_splits
    raise ValueError(f"no aligned {n_steps}-way split of {shape}")


def _with_casts(kernel_fn, n_in, n_out, n_cast):
    def wrapped(*refs):
        ins, rest = refs[:n_in], refs[n_in:]
        cast_in, rest = rest[:n_cast], rest[n_cast:]
        outs, rest = rest[:n_out], rest[n_out:]
        cast_out, scratch = rest[:n_cast], rest[n_cast:]
        kernel_fn(*ins, *outs, *scratch)
        for src, dst in zip(cast_in, cast_out):
            dst[...] = src[...].astype(BF16)
    return wrapped


def _cast_specs(weights, grid):
    n_steps = int(np.prod(grid))
    specs, shapes = [], []
    for w in weights:
        br, bc, col_splits = _cast_block(w.shape, n_steps)

        def index_map(*idx, col_splits=col_splits):
            step = idx[0]
            for size, i in zip(grid[1:], idx[1:]):
                step = step * size + i
            return step // col_splits, step % col_splits
        specs.append(pl.BlockSpec((br, bc), index_map))
        shapes.append(jax.ShapeDtypeStruct(w.shape, BF16))
    return specs, shapes


def _cast_specs_tiled(weights, grid):
    n_outer, n_inner = grid
    specs, shapes = [], []
    for w in weights:
        rows, cols = w.shape
        bc = cols // n_outer
        row_blocks = max(r for r in range(1, n_inner + 1) if rows % r == 0 and (rows // r) % BF16_SUBLANES == 0)
        if cols % n_outer or bc % LANES:
            raise ValueError(f"no aligned split of {w.shape} over {grid}")

        def index_map(i, j, row_blocks=row_blocks):
            return jnp.minimum(j, row_blocks - 1), i
        specs.append(pl.BlockSpec((rows // row_blocks, bc), index_map))
        shapes.append(jax.ShapeDtypeStruct(w.shape, BF16))
    return specs, shapes


def _rows_loop(n_rows, fn):
    def body(r, carry):
        fn(pl.ds(pl.multiple_of(r * NORM_ROWS, NORM_ROWS), NORM_ROWS))
        return carry
    lax.fori_loop(0, n_rows // NORM_ROWS, body, 0)


def _ffn_prologue(h_ref, g_ref, o_ref, xn_ref):
    def norm(rows):
        h = h_ref[rows, :]
        xn_ref[rows, :] = _rms_rows(h, g_ref[...]).astype(BF16)
        o_ref[rows, :] = h
    _rows_loop(h_ref.shape[0], norm)


def _ffn_step(xn_ref, wg, wu, wd, o_ref):
    xn = xn_ref[...]
    g = _dot(xn, wg)
    u = _dot(xn, wu)
    a = (g * (0.5 / (1.0 + jnp.exp(-g))) * u).astype(BF16)
    for c in range(0, o_ref.shape[1], FFN_TF):
        o_ref[:, c:c + FFN_TF] += _dot(a, wd[:, c:c + FFN_TF])


def _ffn_kernel(*refs, n_ff, final_norm, aliased):
    if aliased:
        refs = refs[1:]
    h_ref, g_ref, wg_ref, wu_ref, wd_ref, fg_ref, o_ref, xn_ref = refs
    j = pl.program_id(1)

    @pl.when(j == 0)
    def _():
        _ffn_prologue(h_ref, g_ref, o_ref, xn_ref)

    _ffn_step(xn_ref, wg_ref[...], wu_ref[...], wd_ref[...], o_ref)

    if final_norm:
        @pl.when(j == n_ff - 1)
        def _():
            def norm(rows):
                o_ref[rows, :] = _rms_rows(o_ref[rows, :], fg_ref[...])
            _rows_loop(o_ref.shape[0], norm)


def _ffn(h, gain, wg, wu, wd, final_gain, *, final_norm, first_tile=0, into=None, cast_weights=()):
    t, d = h.shape
    f = wg.shape[1]
    n_ff = f // FFN_TF
    grid = (t // FFN_TM - first_tile, n_ff)
    aliased = into is not None
    tile = lambda i, j: (i + first_tile, 0)
    in_specs = [
        pl.BlockSpec((FFN_TM, d), tile),
        pl.BlockSpec((1, d), lambda i, j: (0, 0)),
        pl.BlockSpec((d, FFN_TF), lambda i, j: (0, j)),
        pl.BlockSpec((d, FFN_TF), lambda i, j: (0, j)),
        pl.BlockSpec((FFN_TF, d), lambda i, j: (j, 0)),
        pl.BlockSpec((1, d), lambda i, j: (0, 0)),
    ]
    args = (h, gain, wg, wu, wd, final_gain)
    if aliased:
        in_specs = [pl.BlockSpec(memory_space=pl.ANY)] + in_specs
        args = (into,) + args
    cast_specs, cast_shapes = _cast_specs_tiled(cast_weights, grid)
    host = functools.partial(_ffn_kernel, n_ff=n_ff, final_norm=final_norm, aliased=aliased)
    out = pl.pallas_call(
        _with_casts(host, len(in_specs), 1, len(cast_weights)),
        out_shape=[jax.ShapeDtypeStruct((t, d), F32)] + cast_shapes,
        grid=grid,
        in_specs=in_specs + cast_specs,
        out_specs=[pl.BlockSpec((FFN_TM, d), tile)] + cast_specs,
        scratch_shapes=[pltpu.VMEM((FFN_TM, d), BF16)],
        input_output_aliases={0: 0} if aliased else {},
        compiler_params=_params(2),
        name="ffn",
    )(*args, *cast_weights)
    return out[0], out[1:]


def _ffn_head_kernel(h_ref, g_ref, wg_ref, wu_ref, wd_ref, fg_ref, o_ref, wg_out, wu_out, wd_out, xn_ref,
                     *, n_ff, final_norm):
    j = pl.program_id(0)

    @pl.when(j == 0)
    def _():
        _ffn_prologue(h_ref, g_ref, o_ref, xn_ref)

    wg, wu, wd = (w[...].astype(BF16) for w in (wg_ref, wu_ref, wd_ref))
    wg_out[...] = wg
    wu_out[...] = wu
    wd_out[...] = wd
    _ffn_step(xn_ref, wg, wu, wd, o_ref)

    if final_norm:
        @pl.when(j == n_ff - 1)
        def _():
            def norm(rows):
                o_ref[rows, :] = _rms_rows(o_ref[rows, :], fg_ref[...])
            _rows_loop(o_ref.shape[0], norm)


def _ffn_head(h, gain, wg, wu, wd, final_gain, *, final_norm):
    t, d = h.shape
    f = wg.shape[1]
    tf = FFN_HEAD_TF
    col = pl.BlockSpec((d, tf), lambda j: (0, j))
    row = pl.BlockSpec((tf, d), lambda j: (j, 0))
    vec = pl.BlockSpec((1, d), lambda j: (0, 0))
    tile0 = pl.BlockSpec((FFN_TM, d), lambda j: (0, 0))
    return pl.pallas_call(
        functools.partial(_ffn_head_kernel, n_ff=f // tf, final_norm=final_norm),
        out_shape=[jax.ShapeDtypeStruct((t, d), F32), jax.ShapeDtypeStruct(wg.shape, BF16),
                   jax.ShapeDtypeStruct(wu.shape, BF16), jax.ShapeDtypeStruct(wd.shape, BF16)],
        grid=(f // tf,),
        in_specs=[tile0, vec, col, col, row, vec],
        out_specs=[tile0, col, col, row],
        scratch_shapes=[pltpu.VMEM((FFN_TM, d), BF16)],
        compiler_params=_params(1),
        name="ffn_head",
    )(h, gain, wg, wu, wd, final_gain)


def _ffn_f32(h, gain, wg, wu, wd, final_gain, *, final_norm, cast_weights=()):
    head, wg16, wu16, wd16 = _ffn_head(h, gain, wg, wu, wd, final_gain, final_norm=final_norm)
    return _ffn(h, gain, wg16, wu16, wd16, final_gain, final_norm=final_norm, first_tile=1, into=head,
                cast_weights=cast_weights)


def _rope_tables(seq):
    pos = np.arange(seq, dtype=np.float32)

    def angles(d):
        inv = np.float32(ROPE_THETA) ** (-np.arange(0, d, 2, dtype=np.float32) / np.float32(d))
        return (pos[:, None] * inv[None, :].astype(np.float32)).astype(np.float32).astype(np.float64)

    a128 = angles(RET_DK)
    cos_r = np.concatenate([np.cos(a128), np.cos(a128)], -1)
    sin_r = np.concatenate([-np.sin(a128), np.sin(a128)], -1)
    a64 = angles(SWA_HEAD_DIM)
    c, s, z = np.cos(a64), np.sin(a64), np.zeros_like(a64)
    cos_s = np.concatenate([c, c, c, c], -1)
    sin_lo = np.concatenate([-s, z, -s, z], -1)
    sin_hi = np.concatenate([z, s, z, s], -1)
    return [jnp.asarray(v, dtype=F32) for v in (cos_r, sin_r, cos_s, sin_lo, sin_hi)]


def _retention_tables():
    c = RET_CHUNK
    heads = np.arange(RET_HEADS, dtype=np.float64)
    log_gamma = np.log1p(-np.exp2(-5.0 - heads))
    idx = np.arange(c, dtype=np.float64)
    diff = idx[:, None] - idx[None, :]
    scale = RET_DK ** -0.5
    dmat = np.where(diff[None] >= 0, np.exp(np.maximum(diff, 0.0)[None] * log_gamma[:, None, None]), 0.0)
    zeta = np.exp((c - 1.0 - idx)[None, :] * log_gamma[:, None])
    xi = np.exp((idx + 1.0)[None, :] * log_gamma[:, None])
    chunk_decay = tuple(float(v) for v in np.exp(c * log_gamma))

    def token_tile(tab):
        return np.tile(np.repeat(tab.T, RET_DK, axis=1), (ROW_TM // c, 1))
    tabs = [jnp.asarray(v, dtype=F32) for v in (dmat * scale, token_tile(xi), token_tile(zeta * scale))]
    return tabs, chunk_decay


def _inproj_kernel(h_ref, g_ref, w_ref, cr_ref, sr_ref, cs_ref, sl_ref, sh_ref, xi_ref, zeta_ref, gn_ref,
                   rq_ref, rqx_ref, rk_ref, rkz_ref, rv_ref, rg_ref, sq_ref, sk_ref, sv_ref, xn_ref):
    tm = h_ref.shape[0]

    def norm(rows):
        xn_ref[rows, :] = _rms_rows(h_ref[rows, :], g_ref[...]).astype(BF16)
    _rows_loop(tm, norm)

    xn = xn_ref[...]
    cr, sr = cr_ref[...], sr_ref[...]
    cs, sl, sh = cs_ref[...], sl_ref[...], sh_ref[...]
    half = LANES // 2
    slabs = [slice(s * LANES, (s + 1) * LANES) for s in range(RET_WIDTH // LANES)]

    def rope_ret(x):
        return x * cr + pltpu.roll(x, half, 1) * sr

    def rope_swa(x):
        return x * cs + pltpu.roll(x, LANES - half // 2, 1) * sl + pltpu.roll(x, half // 2, 1) * sh

    def project(col0, width):
        return _dot(xn, w_ref[:, col0:col0 + width])

    def ret_rotary(out_ref, scaled_ref, tab_ref):
        def epilogue(y):
            for sl_ in slabs:
                x = rope_ret(y[:, sl_])
                out_ref[:, sl_] = x.astype(BF16)
                scaled_ref[:, sl_] = (x * tab_ref[:, sl_]).astype(BF16)
        return epilogue

    def ret_values(y):
        for sl_ in slabs:
            rv_ref[:, sl_] = y[:, sl_].astype(BF16)

    def ret_gate(y):
        for sl_ in slabs:
            g = y[:, sl_]
            rg_ref[:, sl_] = (g * (1.0 / (1.0 + jnp.exp(-g))) * gn_ref[:, sl_]).astype(BF16)

    def swa_queries(y):
        scale = SWA_HEAD_DIM ** -0.5 * LOG2E
        for sl_ in slabs:
            sq_ref[:, sl_] = (rope_swa(y[:, sl_]) * scale).astype(BF16)

    def swa_keys_values(y):
        lo = lax.broadcasted_iota(jnp.int32, (tm, LANES), 1) < half
        for transposed, out_ref, x in ((False, sk_ref, rope_swa(y[:, :LANES])), (True, sv_ref, y[:, LANES:])):
            xr = pltpu.roll(x, half, 1)
            variants = (jnp.where(lo, x, 0.0), jnp.where(lo, 0.0, xr),
                        jnp.where(lo, xr, 0.0), jnp.where(lo, 0.0, x))
            for s, v in enumerate(variants):
                if transposed:
                    out_ref[s * LANES:(s + 1) * LANES, :] = v.T.astype(BF16)
                else:
                    out_ref[:, s * LANES:(s + 1) * LANES] = v.astype(BF16)

    epilogues = (ret_rotary(rq_ref, rqx_ref, xi_ref), ret_rotary(rk_ref, rkz_ref, zeta_ref), ret_values, ret_gate,
                 swa_queries, swa_keys_values)
    col0 = np.cumsum((0,) + IN_SIZES[:-2])
    widths = IN_SIZES[:-2] + (2 * SWA_KV_WIDTH,)
    order = (5, 0, 1, 3, 4, 2)
    for r in order:
        epilogues[r](project(int(col0[r]), widths[r]))


def _inproj(h, gain, w_in, gn_gain, xi_tile, zeta_tile, seq):
    t, d = h.shape
    tables = _rope_tables(seq)
    tiles_per_seq = seq // ROW_TM
    tab_spec = pl.BlockSpec((ROW_TM, LANES), lambda i: (i % tiles_per_seq, 0))

    def row_spec(width):
        return pl.BlockSpec((ROW_TM, width), lambda i: (i, 0))

    widths = (RET_WIDTH,) * 6 + (SWA_WIDTH, 4 * LANES)
    return pl.pallas_call(
        _inproj_kernel,
        out_shape=[jax.ShapeDtypeStruct((t, w), BF16) for w in widths]
        + [jax.ShapeDtypeStruct((4 * LANES, t), BF16)],
        grid=(t // ROW_TM,),
        in_specs=[row_spec(d), _resident((1, d)), _resident((d, IN_COLS))] + [tab_spec] * 5
        + [_resident((ROW_TM, RET_WIDTH))] * 2 + [_resident((1, RET_WIDTH))],
        out_specs=[row_spec(w) for w in widths] + [pl.BlockSpec((4 * LANES, ROW_TM), lambda i: (0, i))],
        scratch_shapes=[pltpu.VMEM((ROW_TM, d), BF16)],
        compiler_params=_params(1),
        name="inproj",
    )(h, gain, w_in, *tables, xi_tile, zeta_tile, gn_gain)


def _mixers_kernel(q_ref, qx_ref, k_ref, kz_ref, v_ref, g_ref, dmat_ref,
                   sink_ref, sq_ref, kp_ref, kc_ref, vp_ref, vc_ref, ret_ref, swa_ref, state_ref, *, chunk_decay):
    n = pl.program_id(1)
    w = WINDOW

    @pl.when(n == 0)
    def _():
        state_ref[...] = jnp.zeros_like(state_ref)

    pairs = SWA_HEADS // SWA_KV_HEADS // 2
    key = lax.broadcasted_iota(jnp.int32, (2 * w, pairs * w), 0)
    qry = lax.broadcasted_iota(jnp.int32, (2 * w, pairs * w), 1) % w
    first_key = jnp.where(n == 0, w, 0)
    valid = (key > qry) & (key <= qry + w) & (key >= first_key)

    heads = range(RET_HEADS)
    hs = [slice(h * RET_DK, (h + 1) * RET_DK) for h in heads]
    lane_blk = lambda c: slice(c * LANES, (c + 1) * LANES)
    phases = [(g, e) for g in range(SWA_KV_HEADS) for e in range(2)]
    slabs = {g: [g * pairs + p for p in range(pairs)] for g in range(SWA_KV_HEADS)}

    s_ret = [_dot_nt(q_ref[:, hs[h]], k_ref[:, hs[h]]) for h in heads]
    q_swa = {g: jnp.concatenate([sq_ref[:, lane_blk(sl)] for sl in slabs[g]], 0) for g in slabs}
    s_swa = []
    for g, e in phases:
        c = 2 * g + e
        k = jnp.concatenate([kp_ref[:, lane_blk(c)], kc_ref[:, lane_blk(c)]], 0)
        s_swa.append(jnp.where(valid, _dot_nt(k, q_swa[g]), NEG_INF))

    states = [state_ref[h] for h in heads]
    ret = []
    for h in heads:
        lhs = jnp.concatenate([(s_ret[h] * dmat_ref[h]).astype(BF16), qx_ref[:, hs[h]]], 1)
        rhs = jnp.concatenate([v_ref[:, hs[h]], states[h].astype(BF16)], 0)
        ret.append(_dot(lhs, rhs))
    for h in heads:
        state_ref[h] = states[h] * chunk_decay[h] + _dot_tn(kz_ref[:, hs[h]], v_ref[:, hs[h]])

    pv = []
    for (g, e), s in zip(phases, s_swa):
        c = 2 * g + e
        sink = jnp.concatenate([jnp.full((1, w), sink_ref[2 * sl + e] * LOG2E, F32) for sl in slabs[g]], 1)
        m = jnp.maximum(jnp.max(s, axis=0, keepdims=True), sink)
        p = jnp.exp2(s - m)
        inv = 1.0 / (jnp.sum(p, axis=0, keepdims=True) + jnp.exp2(sink - m))
        v_t = jnp.concatenate([vp_ref[lane_blk(c), :], vc_ref[lane_blk(c), :]], 1)
        pv.append(_dot(v_t, p.astype(BF16)) * inv)

    for h in heads:
        mu = jnp.mean(ret[h], axis=-1, keepdims=True)
        cen = ret[h] - mu
        var = jnp.mean(cen * cen, axis=-1, keepdims=True)
        ret_ref[:, hs[h]] = (cen * lax.rsqrt(var + EPS) * g_ref[:, hs[h]].astype(F32)).astype(BF16)

    for g in slabs:
        acc = pv[2 * g] + pv[2 * g + 1]
        for i, sl in enumerate(slabs[g]):
            swa_ref[lane_blk(sl), :] = acc[:, i * w:(i + 1) * w].astype(BF16)


def _mixers(rq, rqx, rk, rkz, rv, rg, dmat, chunk_decay, sq, sk4, sv4_t, sinks, batch, seq, cast_weights):
    assert RET_CHUNK == WINDOW
    t = rq.shape[0]
    n_blocks = seq // WINDOW
    grid = (batch, n_blocks)
    cur = lambda b, n: (b * n_blocks + n, 0)
    prev = lambda b, n: (b * n_blocks + jnp.maximum(n - 1, 0), 0)
    cur_t = lambda b, n: (0, b * n_blocks + n)
    prev_t = lambda b, n: (0, b * n_blocks + jnp.maximum(n - 1, 0))
    blk = pl.BlockSpec((RET_CHUNK, RET_WIDTH), cur)
    in_specs = [
        blk, blk, blk, blk, blk, blk, _resident((RET_HEADS, RET_CHUNK, RET_CHUNK)),
        pl.BlockSpec(memory_space=pltpu.SMEM),
        pl.BlockSpec((WINDOW, SWA_WIDTH), cur),
        pl.BlockSpec((WINDOW, 4 * LANES), prev),
        pl.BlockSpec((WINDOW, 4 * LANES), cur),
        pl.BlockSpec((4 * LANES, WINDOW), prev_t),
        pl.BlockSpec((4 * LANES, WINDOW), cur_t),
    ]
    cast_specs, cast_shapes = _cast_specs(cast_weights, grid)
    host = functools.partial(_mixers_kernel, chunk_decay=chunk_decay)
    out = pl.pallas_call(
        _with_casts(host, len(in_specs), 2, len(cast_weights)),
        out_shape=[jax.ShapeDtypeStruct((t, RET_WIDTH), BF16), jax.ShapeDtypeStruct((SWA_WIDTH, t), BF16)]
        + cast_shapes,
        grid=grid,
        in_specs=in_specs + cast_specs,
        out_specs=[blk, pl.BlockSpec((SWA_WIDTH, WINDOW), cur_t)] + cast_specs,
        scratch_shapes=[pltpu.VMEM((RET_HEADS, RET_DK, RET_DV), F32)],
        compiler_params=_params(2),
        name="mixers",
    )(rq, rqx, rk, rkz, rv, rg, dmat, sinks, sq, sk4, sk4, sv4_t, sv4_t, *cast_weights)
    return out[0], out[1], out[2:]


def _outproj_kernel(h_ref, ret_ref, swa_ref, w_ref, g_ref, h2_ref, hn_ref):
    tm = h_ref.shape[0]
    y = _dot(ret_ref[...], w_ref[:RET_WIDTH, :]) + _dot_tn(swa_ref[...], w_ref[RET_WIDTH:, :])
    h2_ref[...] = h_ref[...] + y

    def body(r, carry):
        rows = pl.ds(pl.multiple_of(r * NORM_ROWS, NORM_ROWS), NORM_ROWS)
        hn_ref[rows, :] = _rms_rows(h2_ref[rows, :], g_ref[...]).astype(BF16)
        return carry
    lax.fori_loop(0, tm // NORM_ROWS, body, 0)


def _outproj(h, ret, swa, w_out, gain):
    t, d = h.shape
    row = lambda width: pl.BlockSpec((ROW_TM, width), lambda i: (i, 0))
    return pl.pallas_call(
        _outproj_kernel,
        out_shape=[jax.ShapeDtypeStruct((t, d), F32), jax.ShapeDtypeStruct((t, d), BF16)],
        grid=(t // ROW_TM,),
        in_specs=[row(d), row(RET_WIDTH), pl.BlockSpec((SWA_WIDTH, ROW_TM), lambda i: (0, i)),
                  _resident(w_out.shape), _resident((1, d))],
        out_specs=[row(d), row(d)],
        compiler_params=_params(1),
        name="outproj",
    )(h, ret, swa, w_out, gain)


def _memkv_kernel(m_ref, g_ref, w_ref, o_ref):
    xn = _rms_rows(m_ref[...], g_ref[...]).astype(BF16)
    o_ref[...] = _dot(xn, w_ref[...].astype(BF16)).astype(BF16)


def _memkv(mem, gain, wkv):
    rows, d = mem.shape
    n_out = wkv.shape[1]
    tn = 1024
    return pl.pallas_call(
        _memkv_kernel,
        out_shape=jax.ShapeDtypeStruct((rows, n_out), BF16),
        grid=(n_out // tn,),
        in_specs=[_resident((rows, d)), _resident((1, d)), pl.BlockSpec((d, tn), lambda j: (0, j))],
        out_specs=pl.BlockSpec((rows, tn), lambda j: (0, j)),
        compiler_params=_params(1),
        name="memkv",
    )(mem, gain, wkv)


def _xattn_kernel(hn_ref, h_ref, wq_ref, wo_ref, k_ref, v_ref, o_ref, att_ref):
    q = _dot(hn_ref[...], wq_ref[...]).astype(BF16)
    scale = XA_HEAD_DIM ** -0.5
    heads = [slice(hd * XA_HEAD_DIM, (hd + 1) * XA_HEAD_DIM) for hd in range(XA_HEADS)]
    scores = [_dot_nt(q[:, hs], k_ref[:, hs]) * scale for hs in heads]
    probs, invs = [], []
    for s in scores:
        p = jnp.exp(s - jnp.max(s, axis=-1, keepdims=True))
        probs.append(p.astype(BF16))
        invs.append(1.0 / jnp.sum(p, axis=-1, keepdims=True))
    for hs, p, inv in zip(heads, probs, invs):
        att_ref[:, hs] = (_dot(p, v_ref[:, hs]) * inv).astype(BF16)
    o_ref[...] = h_ref[...] + _dot(att_ref[...], wo_ref[...])


def _xattn(hn, h, wq, wo, mkv, seq, mem_len, cast_weights):
    t, d = h.shape
    tiles_per_seq = seq // ROW_TM
    grid = (t // ROW_TM,)
    row = lambda i: (i, 0)
    cast_specs, cast_shapes = _cast_specs(cast_weights, grid)
    out = pl.pallas_call(
        _with_casts(_xattn_kernel, 6, 1, len(cast_weights)),
        out_shape=[jax.ShapeDtypeStruct((t, d), F32)] + cast_shapes,
        grid=grid,
        in_specs=[
            pl.BlockSpec((ROW_TM, d), row),
            pl.BlockSpec((ROW_TM, d), row),
            _resident((d, d)),
            _resident((d, d)),
            pl.BlockSpec((mem_len, d), lambda i: (i // tiles_per_seq, 0)),
            pl.BlockSpec((mem_len, d), lambda i: (i // tiles_per_seq, 1)),
        ] + cast_specs,
        out_specs=[pl.BlockSpec((ROW_TM, d), row)] + cast_specs,
        scratch_shapes=[pltpu.VMEM((ROW_TM, d), BF16)],
        compiler_params=_params(1),
        name="xattn",
    )(hn, h, wq, wo, mkv, mkv, *cast_weights)
    return out[0], out[1:]


def kernel(x, mem, ffn1_norm, ffn1_w_gate, ffn1_w_up, ffn1_w_down, mix_norm, w_in, ret_gn_gain, swa_sinks,
           w_out, xa_norm, mem_norm, xa_wq, xa_wkv, xa_wo, ffn2_norm, ffn2_w_gate, ffn2_w_up, ffn2_w_down,
           final_norm):
    batch, seq, d = x.shape
    mem_len = mem.shape[1]
    depth = ffn1_norm.shape[0]
    h = x.reshape(batch * seq, d)
    mem2 = mem.reshape(batch * mem_len, d)
    row = lambda g: g.reshape(1, -1).astype(F32)
    final_gain = row(final_norm)

    for l in range(depth):
        last = l == depth - 1
        h, (w_mix,) = _ffn_f32(h, row(ffn1_norm[l]), ffn1_w_gate[l], ffn1_w_up[l], ffn1_w_down[l], final_gain,
                               final_norm=False, cast_weights=[w_in[l]])
        (dmat, xi_tile, zeta_tile), chunk_decay = _retention_tables()
        rq, rqx, rk, rkz, rv, rg, sq, sk4, sv4_t = _inproj(h, row(mix_norm[l]), w_mix, row(ret_gn_gain[l]),
                                                           xi_tile, zeta_tile, seq)
        ret, swa, (wo_mix, wq, wo) = _mixers(
            rq, rqx, rk, rkz, rv, rg, dmat, chunk_decay, sq, sk4, sv4_t, swa_sinks[l].astype(F32), batch, seq,
            [w_out[l], xa_wq[l], xa_wo[l]])
        h, hn = _outproj(h, ret, swa, wo_mix, row(xa_norm[l]))
        mkv = _memkv(mem2, row(mem_norm[l]), xa_wkv[l])
        h, _ = _xattn(hn, h, wq, wo, mkv, seq, mem_len, [])
        h, _ = _ffn_f32(h, row(ffn2_norm[l]), ffn2_w_gate[l], ffn2_w_up[l], ffn2_w_down[l], final_gain,
                        final_norm=last)
    if depth == 0:
        raise ValueError("depth must be at least 1")
    return h.reshape(batch, seq, d)
```

```python
import functools

import numpy as np
import jax
import jax.numpy as jnp
from jax import lax
from jax.experimental import pallas as pl
from jax.experimental.pallas import tpu as pltpu

F32 = jnp.float32
BF16 = jnp.bfloat16

D_MODEL = 2048
D_FF = 5632
RET_HEADS = 8
RET_DK = 128
RET_DV = 128
RET_WIDTH = RET_HEADS * RET_DV
RET_CHUNK = 128
SWA_HEADS = 16
SWA_KV_HEADS = 2
SWA_HEAD_DIM = 64
SWA_WIDTH = SWA_HEADS * SWA_HEAD_DIM
SWA_KV_WIDTH = SWA_KV_HEADS * SWA_HEAD_DIM
WINDOW = 128
XA_HEADS = 4
XA_HEAD_DIM = D_MODEL // XA_HEADS
ROPE_THETA = 10000.0
EPS = 1e-6
IN_SIZES = (RET_WIDTH, RET_WIDTH, RET_WIDTH, RET_WIDTH, SWA_WIDTH, SWA_KV_WIDTH, SWA_KV_WIDTH)
IN_COLS = sum(IN_SIZES)

LANES = 128
BF16_SUBLANES = 16
V7X_VMEM_BYTES = 64 * 1024 * 1024
VMEM_LIMIT_BYTES = 60000 * 1024

FFN_TM = 1024
FFN_TF = 512
FFN_HEAD_TF = 256
ROW_TM = 512
NORM_ROWS = 256
MIXER_PARTS = 2
MIXER_BLOCKS = 2
NEG_INF = float(np.finfo(np.float32).min)
LOG2E = float(np.log2(np.e))


def _params(n_axes):
    return pltpu.CompilerParams(
        dimension_semantics=("arbitrary",) * n_axes,
        vmem_limit_bytes=VMEM_LIMIT_BYTES,
    )


def _resident(shape):
    zeros = (0,) * len(shape)
    return pl.BlockSpec(shape, lambda *_: zeros, pipeline_mode=pl.Buffered(1))


def _rms_rows(x, gain):
    ms = jnp.mean(x * x, axis=-1, keepdims=True)
    return x * lax.rsqrt(ms + EPS) * gain


def _dot(a, b):
    return lax.dot_general(a, b, (((1,), (0,)), ((), ())), preferred_element_type=F32)


def _dot_nt(a, b):
    return lax.dot_general(a, b, (((1,), (1,)), ((), ())), preferred_element_type=F32)


def _dot_tn(a, b):
    return lax.dot_general(a, b, (((0,), (0,)), ((), ())), preferred_element_type=F32)


def _cast_block(shape, n_steps):
    rows, cols = shape
    for col_splits in (1, 2, 4, 8):
        row_blocks, rem = divmod(n_steps, col_splits)
        if rem or rows % row_blocks or cols % col_splits:
            continue
        br, bc = rows // row_blocks, cols // col_splits
        if br % BF16_SUBLANES == 0 and bc % LANES == 0:
            return br, bc, col_splits
    raise ValueError(f"no aligned {n_steps}-way split of {shape}")


def _with_casts(kernel_fn, n_in, n_out, n_cast):
    def wrapped(*refs):
        ins, rest = refs[:n_in], refs[n_in:]
        cast_in, rest = rest[:n_cast], rest[n_cast:]
        outs, rest = rest[:n_out], rest[n_out:]
        cast_out, scratch = rest[:n_cast], rest[n_cast:]
        kernel_fn(*ins, *outs, *scratch)
        for src, dst in zip(cast_in, cast_out):
            dst[...] = src[...].astype(BF16)
    return wrapped


def _cast_specs(weights, grid):
    n_steps = int(np.prod(grid))
    specs, shapes = [], []
    for w in weights:
        br, bc, col_splits = _cast_block(w.shape, n_steps)

        def index_map(*idx, col_splits=col_splits):
            step = idx[0]
            for size, i in zip(grid[1:], idx[1:]):
                step = step * size + i
            return step // col_splits, step % col_splits
        specs.append(pl.BlockSpec((br, bc), index_map))
        shapes.append(jax.ShapeDtypeStruct(w.shape, BF16))
    return specs, shapes


def _cast_specs_tiled(weights, grid):
    n_outer, n_inner = grid
    specs, shapes = [], []
    for w in weights:
        rows, cols = w.shape
        bc = cols // n_outer
        row_blocks = max(r for r in range(1, n_inner + 1) if rows % r == 0 and (rows // r) % BF16_SUBLANES == 0)
        if cols % n_outer or bc % LANES:
            raise ValueError(f"no aligned split of {w.shape} over {grid}")

        def index_map(i, j, row_blocks=row_blocks):
            return jnp.minimum(j, row_blocks - 1), i
        specs.append(pl.BlockSpec((rows // row_blocks, bc), index_map))
        shapes.append(jax.ShapeDtypeStruct(w.shape, BF16))
    return specs, shapes


def _rows_loop(n_rows, fn):
    def body(r, carry):
        fn(pl.ds(pl.multiple_of(r * NORM_ROWS, NORM_ROWS), NORM_ROWS))
        return carry
    lax.fori_loop(0, n_rows // NORM_ROWS, body, 0)


def _ffn_prologue(h_ref, g_ref, o_ref, xn_ref):
    def norm(rows):
        h = h_ref[rows, :]
        xn_ref[rows, :] = _rms_rows(h, g_ref[...]).astype(BF16)
        o_ref[rows, :] = h
    _rows_loop(h_ref.shape[0], norm)


def _ffn_step(xn_ref, wg, wu, wd, o_ref):
    xn = xn_ref[...]
    g = _dot(xn, wg)
    u = _dot(xn, wu)
    a = (g * (0.5 / (1.0 + jnp.exp(-g))) * u).astype(BF16)
    for c in range(0, o_ref.shape[1], FFN_TF):
        o_ref[:, c:c + FFN_TF] += _dot(a, wd[:, c:c + FFN_TF])


def _ffn_kernel(*refs, n_ff, final_norm, aliased):
    if aliased:
        refs = refs[1:]
    h_ref, g_ref, wg_ref, wu_ref, wd_ref, fg_ref, o_ref, xn_ref = refs
    j = pl.program_id(1)

    @pl.when(j == 0)
    def _():
        _ffn_prologue(h_ref, g_ref, o_ref, xn_ref)

    _ffn_step(xn_ref, wg_ref[...], wu_ref[...], wd_ref[...], o_ref)

    if final_norm:
        @pl.when(j == n_ff - 1)
        def _():
            def norm(rows):
                o_ref[rows, :] = _rms_rows(o_ref[rows, :], fg_ref[...])
            _rows_loop(o_ref.shape[0], norm)


def _ffn(h, gain, wg, wu, wd, final_gain, *, final_norm, first_tile=0, into=None, cast_weights=()):
    t, d = h.shape
    f = wg.shape[1]
    n_ff = f // FFN_TF
    grid = (t // FFN_TM - first_tile, n_ff)
    aliased = into is not None
    tile = lambda i, j: (i + first_tile, 0)
    in_specs = [
        pl.BlockSpec((FFN_TM, d), tile),
        pl.BlockSpec((1, d), lambda i, j: (0, 0)),
        pl.BlockSpec((d, FFN_TF), lambda i, j: (0, j)),
        pl.BlockSpec((d, FFN_TF), lambda i, j: (0, j)),
        pl.BlockSpec((FFN_TF, d), lambda i, j: (j, 0)),
        pl.BlockSpec((1, d), lambda i, j: (0, 0)),
    ]
    args = (h, gain, wg, wu, wd, final_gain)
    if aliased:
        in_specs = [pl.BlockSpec(memory_space=pl.ANY)] + in_specs
        args = (into,) + args
    cast_specs, cast_shapes = _cast_specs_tiled(cast_weights, grid)
    host = functools.partial(_ffn_kernel, n_ff=n_ff, final_norm=final_norm, aliased=aliased)
    out = pl.pallas_call(
        _with_casts(host, len(in_specs), 1, len(cast_weights)),
        out_shape=[jax.ShapeDtypeStruct((t, d), F32)] + cast_shapes,
        grid=grid,
        in_specs=in_specs + cast_specs,
        out_specs=[pl.BlockSpec((FFN_TM, d), tile)] + cast_specs,
        scratch_shapes=[pltpu.VMEM((FFN_TM, d), BF16)],
        input_output_aliases={0: 0} if aliased else {},
        compiler_params=_params(2),
        name="ffn",
    )(*args, *cast_weights)
    return out[0], out[1:]


def _ffn_head_kernel(h_ref, g_ref, wg_ref, wu_ref, wd_ref, fg_ref, o_ref, wg_out, wu_out, wd_out, xn_ref,
                     *, n_ff, final_norm):
    j = pl.program_id(0)

    @pl.when(j == 0)
    def _():
        _ffn_prologue(h_ref, g_ref, o_ref, xn_ref)

    wg, wu, wd = (w[...].astype(BF16) for w in (wg_ref, wu_ref, wd_ref))
    wg_out[...] = wg
    wu_out[...] = wu
    wd_out[...] = wd
    _ffn_step(xn_ref, wg, wu, wd, o_ref)

    if final_norm:
        @pl.when(j == n_ff - 1)
        def _():
            def norm(rows):
                o_ref[rows, :] = _rms_rows(o_ref[rows, :], fg_ref[...])
            _rows_loop(o_ref.shape[0], norm)


def _ffn_head(h, gain, wg, wu, wd, final_gain, *, final_norm):
    t, d = h.shape
    f = wg.shape[1]
    tf = FFN_HEAD_TF
    col = pl.BlockSpec((d, tf), lambda j: (0, j))
    row = pl.BlockSpec((tf, d), lambda j: (j, 0))
    vec = pl.BlockSpec((1, d), lambda j: (0, 0))
    tile0 = pl.BlockSpec((FFN_TM, d), lambda j: (0, 0))
    return pl.pallas_call(
        functools.partial(_ffn_head_kernel, n_ff=f // tf, final_norm=final_norm),
        out_shape=[jax.ShapeDtypeStruct((t, d), F32), jax.ShapeDtypeStruct(wg.shape, BF16),
                   jax.ShapeDtypeStruct(wu.shape, BF16), jax.ShapeDtypeStruct(wd.shape, BF16)],
        grid=(f // tf,),
        in_specs=[tile0, vec, col, col, row, vec],
        out_specs=[tile0, col, col, row],
        scratch_shapes=[pltpu.VMEM((FFN_TM, d), BF16)],
        compiler_params=_params(1),
        name="ffn_head",
    )(h, gain, wg, wu, wd, final_gain)


def _ffn_f32(h, gain, wg, wu, wd, final_gain, *, final_norm, cast_weights=()):
    head, wg16, wu16, wd16 = _ffn_head(h, gain, wg, wu, wd, final_gain, final_norm=final_norm)
    return _ffn(h, gain, wg16, wu16, wd16, final_gain, final_norm=final_norm, first_tile=1, into=head,
                cast_weights=cast_weights)


def _rope_tables(seq):
    pos = np.arange(seq, dtype=np.float32)

    def angles(d):
        inv = np.float32(ROPE_THETA) ** (-np.arange(0, d, 2, dtype=np.float32) / np.float32(d))
        return (pos[:, None] * inv[None, :].astype(np.float32)).astype(np.float32).astype(np.float64)

    a128 = angles(RET_DK)
    cos_r = np.concatenate([np.cos(a128), np.cos(a128)], -1)
    sin_r = np.concatenate([-np.sin(a128), np.sin(a128)], -1)
    a64 = angles(SWA_HEAD_DIM)
    c, s, z = np.cos(a64), np.sin(a64), np.zeros_like(a64)
    cos_s = np.concatenate([c, c, c, c], -1)
    sin_lo = np.concatenate([-s, z, -s, z], -1)
    sin_hi = np.concatenate([z, s, z, s], -1)
    return [jnp.asarray(v, dtype=F32) for v in (cos_r, sin_r, cos_s, sin_lo, sin_hi)]


def _retention_tables():
    c = RET_CHUNK
    heads = np.arange(RET_HEADS, dtype=np.float64)
    log_gamma = np.log1p(-np.exp2(-5.0 - heads))
    idx = np.arange(c, dtype=np.float64)
    diff = idx[:, None] - idx[None, :]
    scale = RET_DK ** -0.5
    dmat = np.where(diff[None] >= 0, np.exp(np.maximum(diff, 0.0)[None] * log_gamma[:, None, None]), 0.0)
    zeta = np.exp((c - 1.0 - idx)[None, :] * log_gamma[:, None])
    xi = np.exp((idx + 1.0)[None, :] * log_gamma[:, None])
    chunk_decay = tuple(float(v) for v in np.exp(c * log_gamma))

    def token_tile(tab):
        return np.tile(np.repeat(tab.T, RET_DK, axis=1), (ROW_TM // c, 1))
    tabs = [jnp.asarray(v, dtype=F32) for v in (dmat * scale, token_tile(xi), token_tile(zeta * scale))]
    return tabs, chunk_decay


def _inproj_kernel(h_ref, g_ref, w_ref, cr_ref, sr_ref, cs_ref, sl_ref, sh_ref, xi_ref, zeta_ref, gn_ref,
                   rq_ref, rqx_ref, rk_ref, rkz_ref, rv_ref, rg_ref, sq_ref, sk_ref, sv_ref, xn_ref):
    tm = h_ref.shape[0]

    def norm(rows):
        xn_ref[rows, :] = _rms_rows(h_ref[rows, :], g_ref[...]).astype(BF16)
    _rows_loop(tm, norm)

    xn = xn_ref[...]
    cr, sr = cr_ref[...], sr_ref[...]
    cs, sl, sh = cs_ref[...], sl_ref[...], sh_ref[...]
    half = LANES // 2
    slabs = [slice(s * LANES, (s + 1) * LANES) for s in range(RET_WIDTH // LANES)]

    def rope_ret(x):
        return x * cr + pltpu.roll(x, half, 1) * sr

    def rope_swa(x):
        return x * cs + pltpu.roll(x, LANES - half // 2, 1) * sl + pltpu.roll(x, half // 2, 1) * sh

    def project(col0, width):
        return _dot(xn, w_ref[:, col0:col0 + width])

    def ret_rotary(out_ref, scaled_ref, tab_ref):
        def epilogue(y):
            for sl_ in slabs:
                x = rope_ret(y[:, sl_])
                out_ref[:, sl_] = x.astype(BF16)
                scaled_ref[:, sl_] = (x * tab_ref[:, sl_]).astype(BF16)
        return epilogue

    def ret_values(y):
        for sl_ in slabs:
            rv_ref[:, sl_] = y[:, sl_].astype(BF16)

    def ret_gate(y):
        for sl_ in slabs:
            g = y[:, sl_]
            rg_ref[:, sl_] = (g * (1.0 / (1.0 + jnp.exp(-g))) * gn_ref[:, sl_]).astype(BF16)

    def swa_queries(y):
        scale = SWA_HEAD_DIM ** -0.5 * LOG2E
        for sl_ in slabs:
            sq_ref[:, sl_] = (rope_swa(y[:, sl_]) * scale).astype(BF16)

    def swa_keys_values(y):
        lo = lax.broadcasted_iota(jnp.int32, (tm, LANES), 1) < half
        for transposed, out_ref, x in ((False, sk_ref, rope_swa(y[:, :LANES])), (True, sv_ref, y[:, LANES:])):
            xr = pltpu.roll(x, half, 1)
            variants = (jnp.where(lo, x, 0.0), jnp.where(lo, 0.0, xr),
                        jnp.where(lo, xr, 0.0), jnp.where(lo, 0.0, x))
            for s, v in enumerate(variants):
                if transposed:
                    out_ref[s * LANES:(s + 1) * LANES, :] = v.T.astype(BF16)
                else:
                    out_ref[:, s * LANES:(s + 1) * LANES] = v.astype(BF16)

    epilogues = (ret_rotary(rq_ref, rqx_ref, xi_ref), ret_rotary(rk_ref, rkz_ref, zeta_ref), ret_values, ret_gate,
                 swa_queries, swa_keys_values)
    col0 = np.cumsum((0,) + IN_SIZES[:-2])
    widths = IN_SIZES[:-2] + (2 * SWA_KV_WIDTH,)
    order = (5, 0, 1, 3, 4, 2)
    for r in order:
        epilogues[r](project(int(col0[r]), widths[r]))


def _inproj(h, gain, w_in, gn_gain, xi_tile, zeta_tile, seq):
    t, d = h.shape
    tables = _rope_tables(seq)
    tiles_per_seq = seq // ROW_TM
    tab_spec = pl.BlockSpec((ROW_TM, LANES), lambda i: (i % tiles_per_seq, 0))

    def row_spec(width):
        return pl.BlockSpec((ROW_TM, width), lambda i: (i, 0))

    widths = (RET_WIDTH,) * 6 + (SWA_WIDTH, 4 * LANES)
    return pl.pallas_call(
        _inproj_kernel,
        out_shape=[jax.ShapeDtypeStruct((t, w), BF16) for w in widths]
        + [jax.ShapeDtypeStruct((4 * LANES, t), BF16)],
        grid=(t // ROW_TM,),
        in_specs=[row_spec(d), _resident((1, d)), _resident((d, IN_COLS))] + [tab_spec] * 5
        + [_resident((ROW_TM, RET_WIDTH))] * 2 + [_resident((1, RET_WIDTH))],
        out_specs=[row_spec(w) for w in widths] + [pl.BlockSpec((4 * LANES, ROW_TM), lambda i: (0, i))],
        scratch_shapes=[pltpu.VMEM((ROW_TM, d), BF16)],
        compiler_params=_params(1),
        name="inproj",
    )(h, gain, w_in, *tables, xi_tile, zeta_tile, gn_gain)


def _mixers_kernel(q_ref, qx_ref, k_ref, kz_ref, v_ref, g_ref, dmat_ref,
                   sink_ref, sq_ref, kp_ref, kc_ref, vp_ref, vc_ref, ret_ref, swa_ref, state_ref, *, chunk_decay):
    n = pl.program_id(1)
    w = WINDOW

    @pl.when(n == 0)
    def _():
        state_ref[...] = jnp.zeros_like(state_ref)

    pairs = SWA_HEADS // SWA_KV_HEADS // 2
    key = lax.broadcasted_iota(jnp.int32, (2 * w, pairs * w), 0)
    qry = lax.broadcasted_iota(jnp.int32, (2 * w, pairs * w), 1) % w
    band = (key > qry) & (key <= qry + w)
    band_first = band & (key >= jnp.where(n == 0, w, 0))

    hs = [slice(h * RET_DK, (h + 1) * RET_DK) for h in range(RET_HEADS)]
    lane_blk = lambda c: slice(c * LANES, (c + 1) * LANES)
    slabs = {g: [g * pairs + p for p in range(pairs)] for g in range(SWA_KV_HEADS)}

    for blk in range(MIXER_BLOCKS):
        rows = slice(blk * w, (blk + 1) * w)
        valid = band if blk else band_first

        def prev_keys(c):
            return kc_ref[(blk - 1) * w:blk * w, lane_blk(c)] if blk else kp_ref[:, lane_blk(c)]

        def prev_values_t(c):
            return vc_ref[lane_blk(c), (blk - 1) * w:blk * w] if blk else vp_ref[lane_blk(c), :]

        for part in range(MIXER_PARTS):
            heads = range(part * RET_HEADS // MIXER_PARTS, (part + 1) * RET_HEADS // MIXER_PARTS)
            groups = range(part * SWA_KV_HEADS // MIXER_PARTS, (part + 1) * SWA_KV_HEADS // MIXER_PARTS)
            phases = [(g, e) for g in groups for e in range(2)]

            s_ret = {h: _dot_nt(q_ref[rows, hs[h]], k_ref[rows, hs[h]]) for h in heads}
            q_swa = {g: jnp.concatenate([sq_ref[rows, lane_blk(sl)] for sl in slabs[g]], 0) for g in groups}
            s_swa = []
            for g, e in phases:
                c = 2 * g + e
                k = jnp.concatenate([prev_keys(c), kc_ref[rows, lane_blk(c)]], 0)
                s_swa.append(jnp.where(valid, _dot_nt(k, q_swa[g]), NEG_INF))

            states = {h: state_ref[h] for h in heads}
            ret = {}
            for h in heads:
                lhs = jnp.concatenate([(s_ret[h] * dmat_ref[h]).astype(BF16), qx_ref[rows, hs[h]]], 1)
                rhs = jnp.concatenate([v_ref[rows, hs[h]], states[h].astype(BF16)], 0)
                ret[h] = _dot(lhs, rhs)
            for h in heads:
                state_ref[h] = states[h] * chunk_decay[h] + _dot_tn(kz_ref[rows, hs[h]], v_ref[rows, hs[h]])

            pv = {}
            for (g, e), s in zip(phases, s_swa):
                c = 2 * g + e
                sink = jnp.concatenate([jnp.full((1, w), sink_ref[2 * sl + e] * LOG2E, F32) for sl in slabs[g]], 1)
                m = jnp.maximum(jnp.max(s, axis=0, keepdims=True), sink)
                p = jnp.exp2(s - m)
                inv = 1.0 / (jnp.sum(p, axis=0, keepdims=True) + jnp.exp2(sink - m))
                v_t = jnp.concatenate([prev_values_t(c), vc_ref[lane_blk(c), rows]], 1)
                pv[g, e] = _dot(v_t, p.astype(BF16)) * inv

            for h in heads:
                mu = jnp.mean(ret[h], axis=-1, keepdims=True)
                cen = ret[h] - mu
                var = jnp.mean(cen * cen, axis=-1, keepdims=True)
                ret_ref[rows, hs[h]] = (cen * lax.rsqrt(var + EPS) * g_ref[rows, hs[h]].astype(F32)).astype(BF16)

            for g in groups:
                acc = pv[g, 0] + pv[g, 1]
                for i, sl in enumerate(slabs[g]):
                    swa_ref[lane_blk(sl), rows] = acc[:, i * w:(i + 1) * w].astype(BF16)


def _mixers(rq, rqx, rk, rkz, rv, rg, dmat, chunk_decay, sq, sk4, sv4_t, sinks, batch, seq, cast_weights):
    assert RET_CHUNK == WINDOW
    t = rq.shape[0]
    rows = MIXER_BLOCKS * WINDOW
    n_steps = seq // rows
    grid = (batch, n_steps)
    cur = lambda b, n: (b * n_steps + n, 0)
    cur_t = lambda b, n: (0, b * n_steps + n)
    prev_idx = lambda b, n: b * n_steps * MIXER_BLOCKS + jnp.maximum(n * MIXER_BLOCKS - 1, 0)
    blk = pl.BlockSpec((rows, RET_WIDTH), cur)
    in_specs = [
        blk, blk, blk, blk, blk, blk, _resident((RET_HEADS, RET_CHUNK, RET_CHUNK)),
        pl.BlockSpec(memory_space=pltpu.SMEM),
        pl.BlockSpec((rows, SWA_WIDTH), cur),
        pl.BlockSpec((WINDOW, 4 * LANES), lambda b, n: (prev_idx(b, n), 0)),
        pl.BlockSpec((rows, 4 * LANES), cur),
        pl.BlockSpec((4 * LANES, WINDOW), lambda b, n: (0, prev_idx(b, n))),
        pl.BlockSpec((4 * LANES, rows), cur_t),
    ]
    cast_specs, cast_shapes = _cast_specs(cast_weights, grid)
    host = functools.partial(_mixers_kernel, chunk_decay=chunk_decay)
    out = pl.pallas_call(
        _with_casts(host, len(in_specs), 2, len(cast_weights)),
        out_shape=[jax.ShapeDtypeStruct((t, RET_WIDTH), BF16), jax.ShapeDtypeStruct((SWA_WIDTH, t), BF16)]
        + cast_shapes,
        grid=grid,
        in_specs=in_specs + cast_specs,
        out_specs=[blk, pl.BlockSpec((SWA_WIDTH, rows), cur_t)] + cast_specs,
        scratch_shapes=[pltpu.VMEM((RET_HEADS, RET_DK, RET_DV), F32)],
        compiler_params=_params(2),
        name="mixers",
    )(rq, rqx, rk, rkz, rv, rg, dmat, sinks, sq, sk4, sk4, sv4_t, sv4_t, *cast_weights)
    return out[0], out[1], out[2:]


def _outproj_kernel(h_ref, ret_ref, swa_ref, w_ref, g_ref, h2_ref, hn_ref):
    tm = h_ref.shape[0]
    y = _dot(ret_ref[...], w_ref[:RET_WIDTH, :]) + _dot_tn(swa_ref[...], w_ref[RET_WIDTH:, :])
    h2_ref[...] = h_ref[...] + y

    def body(r, carry):
        rows = pl.ds(pl.multiple_of(r * NORM_ROWS, NORM_ROWS), NORM_ROWS)
        hn_ref[rows, :] = _rms_rows(h2_ref[rows, :], g_ref[...]).astype(BF16)
        return carry
    lax.fori_loop(0, tm // NORM_ROWS, body, 0)


def _outproj(h, ret, swa, w_out, gain):
    t, d = h.shape
    row = lambda width: pl.BlockSpec((ROW_TM, width), lambda i: (i, 0))
    return pl.pallas_call(
        _outproj_kernel,
        out_shape=[jax.ShapeDtypeStruct((t, d), F32), jax.ShapeDtypeStruct((t, d), BF16)],
        grid=(t // ROW_TM,),
        in_specs=[row(d), row(RET_WIDTH), pl.BlockSpec((SWA_WIDTH, ROW_TM), lambda i: (0, i)),
                  _resident(w_out.shape), _resident((1, d))],
        out_specs=[row(d), row(d)],
        compiler_params=_params(1),
        name="outproj",
    )(h, ret, swa, w_out, gain)


def _memkv_kernel(m_ref, g_ref, w_ref, o_ref):
    xn = _rms_rows(m_ref[...], g_ref[...]).astype(BF16)
    o_ref[...] = _dot(xn, w_ref[...].astype(BF16)).astype(BF16)


def _memkv(mem, gain, wkv):
    rows, d = mem.shape
    n_out = wkv.shape[1]
    tn = 1024
    return pl.pallas_call(
        _memkv_kernel,
        out_shape=jax.ShapeDtypeStruct((rows, n_out), BF16),
        grid=(n_out // tn,),
        in_specs=[_resident((rows, d)), _resident((1, d)), pl.BlockSpec((d, tn), lambda j: (0, j))],
        out_specs=pl.BlockSpec((rows, tn), lambda j: (0, j)),
        compiler_params=_params(1),
        name="memkv",
    )(mem, gain, wkv)


def _xattn_kernel(hn_ref, h_ref, wq_ref, wo_ref, k_ref, v_ref, o_ref, att_ref):
    q = _dot(hn_ref[...], wq_ref[...]).astype(BF16)
    scale = XA_HEAD_DIM ** -0.5
    heads = [slice(hd * XA_HEAD_DIM, (hd + 1) * XA_HEAD_DIM) for hd in range(XA_HEADS)]
    scores = [_dot_nt(q[:, hs], k_ref[:, hs]) * scale for hs in heads]
    probs, invs = [], []
    for s in scores:
        p = jnp.exp(s - jnp.max(s, axis=-1, keepdims=True))
        probs.append(p.astype(BF16))
        invs.append(1.0 / jnp.sum(p, axis=-1, keepdims=True))
    for hs, p, inv in zip(heads, probs, invs):
        att_ref[:, hs] = (_dot(p, v_ref[:, hs]) * inv).astype(BF16)
    o_ref[...] = h_ref[...] + _dot(att_ref[...], wo_ref[...])


def _xattn(hn, h, wq, wo, mkv, seq, mem_len, cast_weights):
    t, d = h.shape
    tiles_per_seq = seq // ROW_TM
    grid = (t // ROW_TM,)
    row = lambda i: (i, 0)
    cast_specs, cast_shapes = _cast_specs(cast_weights, grid)
    out = pl.pallas_call(
        _with_casts(_xattn_kernel, 6, 1, len(cast_weights)),
        out_shape=[jax.ShapeDtypeStruct((t, d), F32)] + cast_shapes,
        grid=grid,
        in_specs=[
            pl.BlockSpec((ROW_TM, d), row),
            pl.BlockSpec((ROW_TM, d), row),
            _resident((d, d)),
            _resident((d, d)),
            pl.BlockSpec((mem_len, d), lambda i: (i // tiles_per_seq, 0)),
            pl.BlockSpec((mem_len, d), lambda i: (i // tiles_per_seq, 1)),
        ] + cast_specs,
        out_specs=[pl.BlockSpec((ROW_TM, d), row)] + cast_specs,
        scratch_shapes=[pltpu.VMEM((ROW_TM, d), BF16)],
        compiler_params=_params(1),
        name="xattn",
    )(hn, h, wq, wo, mkv, mkv, *cast_weights)
    return out[0], out[1:]


def kernel(x, mem, ffn1_norm, ffn1_w_gate, ffn1_w_up, ffn1_w_down, mix_norm, w_in, ret_gn_gain, swa_sinks,
           w_out, xa_norm, mem_norm, xa_wq, xa_wkv, xa_wo, ffn2_norm, ffn2_w_gate, ffn2_w_up, ffn2_w_down,
           final_norm):
    batch, seq, d = x.shape
    mem_len = mem.shape[1]
    depth = ffn1_norm.shape[0]
    h = x.reshape(batch * seq, d)
    mem2 = mem.reshape(batch * mem_len, d)
    row = lambda g: g.reshape(1, -1).astype(F32)
    final_gain = row(final_norm)

    for l in range(depth):
        last = l == depth - 1
        h, (w_mix,) = _ffn_f32(h, row(ffn1_norm[l]), ffn1_w_gate[l], ffn1_w_up[l], ffn1_w_down[l], final_gain,
                               final_norm=False, cast_weights=[w_in[l]])
        (dmat, xi_tile, zeta_tile), chunk_decay = _retention_tables()
        rq, rqx, rk, rkz, rv, rg, sq, sk4, sv4_t = _inproj(h, row(mix_norm[l]), w_mix, row(ret_gn_gain[l]),
                                                           xi_tile, zeta_tile, seq)
        ret, swa, (wo_mix, wq, wo) = _mixers(
            rq, rqx, rk, rkz, rv, rg, dmat, chunk_decay, sq, sk4, sv4_t, swa_sinks[l].astype(F32), batch, seq,
            [w_out[l], xa_wq[l], xa_wo[l]])
        h, hn = _outproj(h, ret, swa, wo_mix, row(xa_norm[l]))
        mkv = _memkv(mem2, row(mem_norm[l]), xa_wkv[l])
        h, _ = _xattn(hn, h, wq, wo, mkv, seq, mem_len, [])
        h, _ = _ffn_f32(h, row(ffn2_norm[l]), ffn2_w_gate[l], ffn2_w_up[l], ffn2_w_down[l], final_gain,
                        final_norm=last)
    if depth == 0:
        raise ValueError("depth must be at least 1")
    return h.reshape(batch, seq, d)
```

```python
import functools

import numpy as np
import jax
import jax.numpy as jnp
from jax import lax
from jax.experimental import pallas as pl
from jax.experimental.pallas import tpu as pltpu

F32 = jnp.float32
BF16 = jnp.bfloat16

D_MODEL = 2048
D_FF = 5632
RET_HEADS = 8
RET_DK = 128
RET_DV = 128
RET_WIDTH = RET_HEADS * RET_DV
RET_CHUNK = 128
SWA_HEADS = 16
SWA_KV_HEADS = 2
SWA_HEAD_DIM = 64
SWA_WIDTH = SWA_HEADS * SWA_HEAD_DIM
SWA_KV_WIDTH = SWA_KV_HEADS * SWA_HEAD_DIM
WINDOW = 128
XA_HEADS = 4
XA_HEAD_DIM = D_MODEL // XA_HEADS
ROPE_THETA = 10000.0
EPS = 1e-6
IN_SIZES = (RET_WIDTH, RET_WIDTH, RET_WIDTH, RET_WIDTH, SWA_WIDTH, SWA_KV_WIDTH, SWA_KV_WIDTH)
IN_COLS = sum(IN_SIZES)

LANES = 128
BF16_SUBLANES = 16
V7X_VMEM_BYTES = 64 * 1024 * 1024
VMEM_LIMIT_BYTES = 60000 * 1024

FFN_TM = 1024
FFN_TF = 512
FFN_HEAD_TF = 256
ROW_TM = 512
NORM_ROWS = 256
MIXER_PARTS = 2
MIXER_BLOCKS = 4
NEG_INF = float(np.finfo(np.float32).min)
LOG2E = float(np.log2(np.e))


def _params(n_axes):
    return pltpu.CompilerParams(
        dimension_semantics=("arbitrary",) * n_axes,
        vmem_limit_bytes=VMEM_LIMIT_BYTES,
    )


def _resident(shape):
    zeros = (0,) * len(shape)
    return pl.BlockSpec(shape, lambda *_: zeros, pipeline_mode=pl.Buffered(1))


def _rms_rows(x, gain):
    ms = jnp.mean(x * x, axis=-1, keepdims=True)
    return x * lax.rsqrt(ms + EPS) * gain


def _dot(a, b):
    return lax.dot_general(a, b, (((1,), (0,)), ((), ())), preferred_element_type=F32)


def _dot_nt(a, b):
    return lax.dot_general(a, b, (((1,), (1,)), ((), ())), preferred_element_type=F32)


def _dot_tn(a, b):
    return lax.dot_general(a, b, (((0,), (0,)), ((), ())), preferred_element_type=F32)


def _cast_block(shape, n_steps):
    rows, cols = shape
    for col_splits in (1, 2, 4, 8):
        row_blocks, rem = divmod(n_steps, col_splits)
        if rem or rows % row_blocks or cols % col_splits:
            continue
        br, bc = rows // row_blocks, cols // col_splits
        if br % BF16_SUBLANES == 0 and bc % LANES == 0:
            return br, bc, col_splits
    raise ValueError(f"no aligned {n_steps}-way split of {shape}")


def _with_casts(kernel_fn, n_in, n_out, n_cast):
    def wrapped(*refs):
        ins, rest = refs[:n_in], refs[n_in:]
        cast_in, rest = rest[:n_cast], rest[n_cast:]
        outs, rest = rest[:n_out], rest[n_out:]
        cast_out, scratch = rest[:n_cast], rest[n_cast:]
        kernel_fn(*ins, *outs, *scratch)
        for src, dst in zip(cast_in, cast_out):
            dst[...] = src[...].astype(BF16)
    return wrapped


def _cast_specs(weights, grid):
    n_steps = int(np.prod(grid))
    specs, shapes = [], []
    for w in weights:
        br, bc, col_splits = _cast_block(w.shape, n_steps)

        def index_map(*idx, col_splits=col_splits):
            step = idx[0]
            for size, i in zip(grid[1:], idx[1:]):
                step = step * size + i
            return step // col_splits, step % col_splits
        specs.append(pl.BlockSpec((br, bc), index_map))
        shapes.append(jax.ShapeDtypeStruct(w.shape, BF16))
    return specs, shapes


def _cast_specs_tiled(weights, grid):
    n_outer, n_inner = grid
    specs, shapes = [], []
    for w in weights:
        rows, cols = w.shape
        bc = cols // n_outer
        row_blocks = max(r for r in range(1, n_inner + 1) if rows % r == 0 and (rows // r) % BF16_SUBLANES == 0)
        if cols % n_outer or bc % LANES:
            raise ValueError(f"no aligned split of {w.shape} over {grid}")

        def index_map(i, j, row_blocks=row_blocks):
            return jnp.minimum(j, row_blocks - 1), i
        specs.append(pl.BlockSpec((rows // row_blocks, bc), index_map))
        shapes.append(jax.ShapeDtypeStruct(w.shape, BF16))
    return specs, shapes


def _rows_loop(n_rows, fn):
    def body(r, carry):
        fn(pl.ds(pl.multiple_of(r * NORM_ROWS, NORM_ROWS), NORM_ROWS))
        return carry
    lax.fori_loop(0, n_rows // NORM_ROWS, body, 0)


def _ffn_prologue(h_ref, g_ref, o_ref, xn_ref):
    def norm(rows):
        h = h_ref[rows, :]
        xn_ref[rows, :] = _rms_rows(h, g_ref[...]).astype(BF16)
        o_ref[rows, :] = h
    _rows_loop(h_ref.shape[0], norm)


def _ffn_step(xn_ref, wg, wu, wd, o_ref):
    xn = xn_ref[...]
    g = _dot(xn, wg)
    u = _dot(xn, wu)
    a = (g * (0.5 / (1.0 + jnp.exp(-g))) * u).astype(BF16)
    for c in range(0, o_ref.shape[1], FFN_TF):
        o_ref[:, c:c + FFN_TF] += _dot(a, wd[:, c:c + FFN_TF])


def _ffn_kernel(*refs, n_ff, final_norm, aliased):
    if aliased:
        refs = refs[1:]
    h_ref, g_ref, wg_ref, wu_ref, wd_ref, fg_ref, o_ref, xn_ref = refs
    j = pl.program_id(1)

    @pl.when(j == 0)
    def _():
        _ffn_prologue(h_ref, g_ref, o_ref, xn_ref)

    _ffn_step(xn_ref, wg_ref[...], wu_ref[...], wd_ref[...], o_ref)

    if final_norm:
        @pl.when(j == n_ff - 1)
        def _():
            def norm(rows):
                o_ref[rows, :] = _rms_rows(o_ref[rows, :], fg_ref[...])
            _rows_loop(o_ref.shape[0], norm)


def _ffn(h, gain, wg, wu, wd, final_gain, *, final_norm, first_tile=0, into=None, cast_weights=()):
    t, d = h.shape
    f = wg.shape[1]
    n_ff = f // FFN_TF
    grid = (t // FFN_TM - first_tile, n_ff)
    aliased = into is not None
    tile = lambda i, j: (i + first_tile, 0)
    in_specs = [
        pl.BlockSpec((FFN_TM, d), tile),
        pl.BlockSpec((1, d), lambda i, j: (0, 0)),
        pl.BlockSpec((d, FFN_TF), lambda i, j: (0, j)),
        pl.BlockSpec((d, FFN_TF), lambda i, j: (0, j)),
        pl.BlockSpec((FFN_TF, d), lambda i, j: (j, 0)),
        pl.BlockSpec((1, d), lambda i, j: (0, 0)),
    ]
    args = (h, gain, wg, wu, wd, final_gain)
    if aliased:
        in_specs = [pl.BlockSpec(memory_space=pl.ANY)] + in_specs
        args = (into,) + args
    cast_specs, cast_shapes = _cast_specs_tiled(cast_weights, grid)
    host = functools.partial(_ffn_kernel, n_ff=n_ff, final_norm=final_norm, aliased=aliased)
    out = pl.pallas_call(
        _with_casts(host, len(in_specs), 1, len(cast_weights)),
        out_shape=[jax.ShapeDtypeStruct((t, d), F32)] + cast_shapes,
        grid=grid,
        in_specs=in_specs + cast_specs,
        out_specs=[pl.BlockSpec((FFN_TM, d), tile)] + cast_specs,
        scratch_shapes=[pltpu.VMEM((FFN_TM, d), BF16)],
        input_output_aliases={0: 0} if aliased else {},
        compiler_params=_params(2),
        name="ffn",
    )(*args, *cast_weights)
    return out[0], out[1:]


def _ffn_head_kernel(h_ref, g_ref, wg_ref, wu_ref, wd_ref, fg_ref, o_ref, wg_out, wu_out, wd_out, xn_ref,
                     *, n_ff, final_norm):
    j = pl.program_id(0)

    @pl.when(j == 0)
    def _():
        _ffn_prologue(h_ref, g_ref, o_ref, xn_ref)

    wg, wu, wd = (w[...].astype(BF16) for w in (wg_ref, wu_ref, wd_ref))
    wg_out[...] = wg
    wu_out[...] = wu
    wd_out[...] = wd
    _ffn_step(xn_ref, wg, wu, wd, o_ref)

    if final_norm:
        @pl.when(j == n_ff - 1)
        def _():
            def norm(rows):
                o_ref[rows, :] = _rms_rows(o_ref[rows, :], fg_ref[...])
            _rows_loop(o_ref.shape[0], norm)


def _ffn_head(h, gain, wg, wu, wd, final_gain, *, final_norm):
    t, d = h.shape
    f = wg.shape[1]
    tf = FFN_HEAD_TF
    col = pl.BlockSpec((d, tf), lambda j: (0, j))
    row = pl.BlockSpec((tf, d), lambda j: (j, 0))
    vec = pl.BlockSpec((1, d), lambda j: (0, 0))
    tile0 = pl.BlockSpec((FFN_TM, d), lambda j: (0, 0))
    return pl.pallas_call(
        functools.partial(_ffn_head_kernel, n_ff=f // tf, final_norm=final_norm),
        out_shape=[jax.ShapeDtypeStruct((t, d), F32), jax.ShapeDtypeStruct(wg.shape, BF16),
                   jax.ShapeDtypeStruct(wu.shape, BF16), jax.ShapeDtypeStruct(wd.shape, BF16)],
        grid=(f // tf,),
        in_specs=[tile0, vec, col, col, row, vec],
        out_specs=[tile0, col, col, row],
        scratch_shapes=[pltpu.VMEM((FFN_TM, d), BF16)],
        compiler_params=_params(1),
        name="ffn_head",
    )(h, gain, wg, wu, wd, final_gain)


def _ffn_f32(h, gain, wg, wu, wd, final_gain, *, final_norm, cast_weights=()):
    head, wg16, wu16, wd16 = _ffn_head(h, gain, wg, wu, wd, final_gain, final_norm=final_norm)
    return _ffn(h, gain, wg16, wu16, wd16, final_gain, final_norm=final_norm, first_tile=1, into=head,
                cast_weights=cast_weights)


def _rope_tables(seq):
    pos = np.arange(seq, dtype=np.float32)

    def angles(d):
        inv = np.float32(ROPE_THETA) ** (-np.arange(0, d, 2, dtype=np.float32) / np.float32(d))
        return (pos[:, None] * inv[None, :].astype(np.float32)).astype(np.float32).astype(np.float64)

    a128 = angles(RET_DK)
    cos_r = np.concatenate([np.cos(a128), np.cos(a128)], -1)
    sin_r = np.concatenate([-np.sin(a128), np.sin(a128)], -1)
    a64 = angles(SWA_HEAD_DIM)
    c, s, z = np.cos(a64), np.sin(a64), np.zeros_like(a64)
    cos_s = np.concatenate([c, c, c, c], -1)
    sin_lo = np.concatenate([-s, z, -s, z], -1)
    sin_hi = np.concatenate([z, s, z, s], -1)
    return [jnp.asarray(v, dtype=F32) for v in (cos_r, sin_r, cos_s, sin_lo, sin_hi)]


def _retention_tables():
    c = RET_CHUNK
    heads = np.arange(RET_HEADS, dtype=np.float64)
    log_gamma = np.log1p(-np.exp2(-5.0 - heads))
    idx = np.arange(c, dtype=np.float64)
    diff = idx[:, None] - idx[None, :]
    scale = RET_DK ** -0.5
    dmat = np.where(diff[None] >= 0, np.exp(np.maximum(diff, 0.0)[None] * log_gamma[:, None, None]), 0.0)
    zeta = np.exp((c - 1.0 - idx)[None, :] * log_gamma[:, None])
    xi = np.exp((idx + 1.0)[None, :] * log_gamma[:, None])
    chunk_decay = tuple(float(v) for v in np.exp(c * log_gamma))

    def token_tile(tab):
        return np.tile(np.repeat(tab.T, RET_DK, axis=1), (ROW_TM // c, 1))
    tabs = [jnp.asarray(v, dtype=F32) for v in (dmat * scale, token_tile(xi), token_tile(zeta * scale))]
    return tabs, chunk_decay


def _inproj_kernel(h_ref, g_ref, w_ref, cr_ref, sr_ref, cs_ref, sl_ref, sh_ref, xi_ref, zeta_ref, gn_ref,
                   rq_ref, rqx_ref, rk_ref, rkz_ref, rv_ref, rg_ref, sq_ref, sk_ref, sv_ref, xn_ref):
    tm = h_ref.shape[0]

    def norm(rows):
        xn_ref[rows, :] = _rms_rows(h_ref[rows, :], g_ref[...]).astype(BF16)
    _rows_loop(tm, norm)

    xn = xn_ref[...]
    cr, sr = cr_ref[...], sr_ref[...]
    cs, sl, sh = cs_ref[...], sl_ref[...], sh_ref[...]
    half = LANES // 2
    slabs = [slice(s * LANES, (s + 1) * LANES) for s in range(RET_WIDTH // LANES)]

    def rope_ret(x):
        return x * cr + pltpu.roll(x, half, 1) * sr

    def rope_swa(x):
        return x * cs + pltpu.roll(x, LANES - half // 2, 1) * sl + pltpu.roll(x, half // 2, 1) * sh

    def project(col0, width):
        return _dot(xn, w_ref[:, col0:col0 + width])

    def ret_rotary(out_ref, scaled_ref, tab_ref):
        def epilogue(y):
            for sl_ in slabs:
                x = rope_ret(y[:, sl_])
                out_ref[:, sl_] = x.astype(BF16)
                scaled_ref[:, sl_] = (x * tab_ref[:, sl_]).astype(BF16)
        return epilogue

    def ret_values(y):
        for sl_ in slabs:
            rv_ref[:, sl_] = y[:, sl_].astype(BF16)

    def ret_gate(y):
        for sl_ in slabs:
            g = y[:, sl_]
            rg_ref[:, sl_] = (g * (1.0 / (1.0 + jnp.exp(-g))) * gn_ref[:, sl_]).astype(BF16)

    def swa_queries(y):
        scale = SWA_HEAD_DIM ** -0.5 * LOG2E
        for sl_ in slabs:
            sq_ref[:, sl_] = (rope_swa(y[:, sl_]) * scale).astype(BF16)

    def swa_keys_values(y):
        lo = lax.broadcasted_iota(jnp.int32, (tm, LANES), 1) < half
        for transposed, out_ref, x in ((False, sk_ref, rope_swa(y[:, :LANES])), (True, sv_ref, y[:, LANES:])):
            xr = pltpu.roll(x, half, 1)
            variants = (jnp.where(lo, x, 0.0), jnp.where(lo, 0.0, xr),
                        jnp.where(lo, xr, 0.0), jnp.where(lo, 0.0, x))
            for s, v in enumerate(variants):
                if transposed:
                    out_ref[s * LANES:(s + 1) * LANES, :] = v.T.astype(BF16)
                else:
                    out_ref[:, s * LANES:(s + 1) * LANES] = v.astype(BF16)

    epilogues = (ret_rotary(rq_ref, rqx_ref, xi_ref), ret_rotary(rk_ref, rkz_ref, zeta_ref), ret_values, ret_gate,
                 swa_queries, swa_keys_values)
    col0 = np.cumsum((0,) + IN_SIZES[:-2])
    widths = IN_SIZES[:-2] + (2 * SWA_KV_WIDTH,)
    order = (5, 0, 1, 3, 4, 2)
    for r in order:
        epilogues[r](project(int(col0[r]), widths[r]))


def _inproj(h, gain, w_in, gn_gain, xi_tile, zeta_tile, seq):
    t, d = h.shape
    tables = _rope_tables(seq)
    tiles_per_seq = seq // ROW_TM
    tab_spec = pl.BlockSpec((ROW_TM, LANES), lambda i: (i % tiles_per_seq, 0))

    def row_spec(width):
        return pl.BlockSpec((ROW_TM, width), lambda i: (i, 0))

    widths = (RET_WIDTH,) * 6 + (SWA_WIDTH, 4 * LANES)
    return pl.pallas_call(
        _inproj_kernel,
        out_shape=[jax.ShapeDtypeStruct((t, w), BF16) for w in widths]
        + [jax.ShapeDtypeStruct((4 * LANES, t), BF16)],
        grid=(t // ROW_TM,),
        in_specs=[row_spec(d), _resident((1, d)), _resident((d, IN_COLS))] + [tab_spec] * 5
        + [_resident((ROW_TM, RET_WIDTH))] * 2 + [_resident((1, RET_WIDTH))],
        out_specs=[row_spec(w) for w in widths] + [pl.BlockSpec((4 * LANES, ROW_TM), lambda i: (0, i))],
        scratch_shapes=[pltpu.VMEM((ROW_TM, d), BF16)],
        compiler_params=_params(1),
        name="inproj",
    )(h, gain, w_in, *tables, xi_tile, zeta_tile, gn_gain)


def _mixers_kernel(q_ref, qx_ref, k_ref, kz_ref, v_ref, g_ref, dmat_ref,
                   sink_ref, sq_ref, kp_ref, kc_ref, vp_ref, vc_ref, ret_ref, swa_ref, state_ref, *, chunk_decay):
    n = pl.program_id(1)
    w = WINDOW

    @pl.when(n == 0)
    def _():
        state_ref[...] = jnp.zeros_like(state_ref)

    pairs = SWA_HEADS // SWA_KV_HEADS // 2
    key = lax.broadcasted_iota(jnp.int32, (2 * w, pairs * w), 0)
    qry = lax.broadcasted_iota(jnp.int32, (2 * w, pairs * w), 1) % w
    band = (key > qry) & (key <= qry + w)
    band_first = band & (key >= jnp.where(n == 0, w, 0))

    hs = [slice(h * RET_DK, (h + 1) * RET_DK) for h in range(RET_HEADS)]
    lane_blk = lambda c: slice(c * LANES, (c + 1) * LANES)
    slabs = {g: [g * pairs + p for p in range(pairs)] for g in range(SWA_KV_HEADS)}

    for blk in range(MIXER_BLOCKS):
        rows = slice(blk * w, (blk + 1) * w)
        valid = band if blk else band_first

        def prev_keys(c):
            return kc_ref[(blk - 1) * w:blk * w, lane_blk(c)] if blk else kp_ref[:, lane_blk(c)]

        def prev_values_t(c):
            return vc_ref[lane_blk(c), (blk - 1) * w:blk * w] if blk else vp_ref[lane_blk(c), :]

        for part in range(MIXER_PARTS):
            heads = range(part * RET_HEADS // MIXER_PARTS, (part + 1) * RET_HEADS // MIXER_PARTS)
            groups = range(part * SWA_KV_HEADS // MIXER_PARTS, (part + 1) * SWA_KV_HEADS // MIXER_PARTS)
            phases = [(g, e) for g in groups for e in range(2)]

            s_ret = {h: _dot_nt(q_ref[rows, hs[h]], k_ref[rows, hs[h]]) for h in heads}
            q_swa = {g: jnp.concatenate([sq_ref[rows, lane_blk(sl)] for sl in slabs[g]], 0) for g in groups}
            s_swa = []
            for g, e in phases:
                c = 2 * g + e
                k = jnp.concatenate([prev_keys(c), kc_ref[rows, lane_blk(c)]], 0)
                s_swa.append(jnp.where(valid, _dot_nt(k, q_swa[g]), NEG_INF))

            states = {h: state_ref[h] for h in heads}
            ret = {}
            for h in heads:
                lhs = jnp.concatenate([(s_ret[h] * dmat_ref[h]).astype(BF16), qx_ref[rows, hs[h]]], 1)
                rhs = jnp.concatenate([v_ref[rows, hs[h]], states[h].astype(BF16)], 0)
                ret[h] = _dot(lhs, rhs)
            for h in heads:
                state_ref[h] = states[h] * chunk_decay[h] + _dot_tn(kz_ref[rows, hs[h]], v_ref[rows, hs[h]])

            pv = {}
            for (g, e), s in zip(phases, s_swa):
                c = 2 * g + e
                sink = jnp.concatenate([jnp.full((1, w), sink_ref[2 * sl + e] * LOG2E, F32) for sl in slabs[g]], 1)
                m = jnp.maximum(jnp.max(s, axis=0, keepdims=True), sink)
                p = jnp.exp2(s - m)
                inv = 1.0 / (jnp.sum(p, axis=0, keepdims=True) + jnp.exp2(sink - m))
                v_t = jnp.concatenate([prev_values_t(c), vc_ref[lane_blk(c), rows]], 1)
                pv[g, e] = _dot(v_t, p.astype(BF16)) * inv

            for h in heads:
                mu = jnp.mean(ret[h], axis=-1, keepdims=True)
                cen = ret[h] - mu
                var = jnp.mean(cen * cen, axis=-1, keepdims=True)
                ret_ref[rows, hs[h]] = (cen * lax.rsqrt(var + EPS) * g_ref[rows, hs[h]].astype(F32)).astype(BF16)

            for g in groups:
                acc = pv[g, 0] + pv[g, 1]
                for i, sl in enumerate(slabs[g]):
                    swa_ref[lane_blk(sl), rows] = acc[:, i * w:(i + 1) * w].astype(BF16)


def _mixers(rq, rqx, rk, rkz, rv, rg, dmat, chunk_decay, sq, sk4, sv4_t, sinks, batch, seq, cast_weights):
    assert RET_CHUNK == WINDOW
    t = rq.shape[0]
    rows = MIXER_BLOCKS * WINDOW
    n_steps = seq // rows
    grid = (batch, n_steps)
    cur = lambda b, n: (b * n_steps + n, 0)
    cur_t = lambda b, n: (0, b * n_steps + n)
    prev_idx = lambda b, n: b * n_steps * MIXER_BLOCKS + jnp.maximum(n * MIXER_BLOCKS - 1, 0)
    blk = pl.BlockSpec((rows, RET_WIDTH), cur)
    in_specs = [
        blk, blk, blk, blk, blk, blk, _resident((RET_HEADS, RET_CHUNK, RET_CHUNK)),
        pl.BlockSpec(memory_space=pltpu.SMEM),
        pl.BlockSpec((rows, SWA_WIDTH), cur),
        pl.BlockSpec((WINDOW, 4 * LANES), lambda b, n: (prev_idx(b, n), 0)),
        pl.BlockSpec((rows, 4 * LANES), cur),
        pl.BlockSpec((4 * LANES, WINDOW), lambda b, n: (0, prev_idx(b, n))),
        pl.BlockSpec((4 * LANES, rows), cur_t),
    ]
    cast_specs, cast_shapes = _cast_specs(cast_weights, grid)
    host = functools.partial(_mixers_kernel, chunk_decay=chunk_decay)
    out = pl.pallas_call(
        _with_casts(host, len(in_specs), 2, len(cast_weights)),
        out_shape=[jax.ShapeDtypeStruct((t, RET_WIDTH), BF16), jax.ShapeDtypeStruct((SWA_WIDTH, t), BF16)]
        + cast_shapes,
        grid=grid,
        in_specs=in_specs + cast_specs,
        out_specs=[blk, pl.BlockSpec((SWA_WIDTH, rows), cur_t)] + cast_specs,
        scratch_shapes=[pltpu.VMEM((RET_HEADS, RET_DK, RET_DV), F32)],
        compiler_params=_params(2),
        name="mixers",
    )(rq, rqx, rk, rkz, rv, rg, dmat, sinks, sq, sk4, sk4, sv4_t, sv4_t, *cast_weights)
    return out[0], out[1], out[2:]


def _outproj_kernel(h_ref, ret_ref, swa_ref, w_ref, g_ref, h2_ref, hn_ref):
    tm = h_ref.shape[0]
    y = _dot(ret_ref[...], w_ref[:RET_WIDTH, :]) + _dot_tn(swa_ref[...], w_ref[RET_WIDTH:, :])
    h2_ref[...] = h_ref[...] + y

    def body(r, carry):
        rows = pl.ds(pl.multiple_of(r * NORM_ROWS, NORM_ROWS), NORM_ROWS)
        hn_ref[rows, :] = _rms_rows(h2_ref[rows, :], g_ref[...]).astype(BF16)
        return carry
    lax.fori_loop(0, tm // NORM_ROWS, body, 0)


def _outproj(h, ret, swa, w_out, gain):
    t, d = h.shape
    row = lambda width: pl.BlockSpec((ROW_TM, width), lambda i: (i, 0))
    return pl.pallas_call(
        _outproj_kernel,
        out_shape=[jax.ShapeDtypeStruct((t, d), F32), jax.ShapeDtypeStruct((t, d), BF16)],
        grid=(t // ROW_TM,),
        in_specs=[row(d), row(RET_WIDTH), pl.BlockSpec((SWA_WIDTH, ROW_TM), lambda i: (0, i)),
                  _resident(w_out.shape), _resident((1, d))],
        out_specs=[row(d), row(d)],
        compiler_params=_params(1),
        name="outproj",
    )(h, ret, swa, w_out, gain)


def _memkv_kernel(m_ref, g_ref, w_ref, o_ref):
    xn = _rms_rows(m_ref[...], g_ref[...]).astype(BF16)
    o_ref[...] = _dot(xn, w_ref[...].astype(BF16)).astype(BF16)


def _memkv(mem, gain, wkv):
    rows, d = mem.shape
    n_out = wkv.shape[1]
    tn = 1024
    return pl.pallas_call(
        _memkv_kernel,
        out_shape=jax.ShapeDtypeStruct((rows, n_out), BF16),
        grid=(n_out // tn,),
        in_specs=[_resident((rows, d)), _resident((1, d)), pl.BlockSpec((d, tn), lambda j: (0, j))],
        out_specs=pl.BlockSpec((rows, tn), lambda j: (0, j)),
        compiler_params=_params(1),
        name="memkv",
    )(mem, gain, wkv)


def _xattn_kernel(hn_ref, h_ref, wq_ref, wo_ref, k_ref, v_ref, o_ref, att_ref):
    q = _dot(hn_ref[...], wq_ref[...]).astype(BF16)
    scale = XA_HEAD_DIM ** -0.5
    heads = [slice(hd * XA_HEAD_DIM, (hd + 1) * XA_HEAD_DIM) for hd in range(XA_HEADS)]
    scores = [_dot_nt(q[:, hs], k_ref[:, hs]) * scale for hs in heads]
    probs, invs = [], []
    for s in scores:
        p = jnp.exp(s - jnp.max(s, axis=-1, keepdims=True))
        probs.append(p.astype(BF16))
        invs.append(1.0 / jnp.sum(p, axis=-1, keepdims=True))
    for hs, p, inv in zip(heads, probs, invs):
        att_ref[:, hs] = (_dot(p, v_ref[:, hs]) * inv).astype(BF16)
    o_ref[...] = h_ref[...] + _dot(att_ref[...], wo_ref[...])


def _xattn(hn, h, wq, wo, mkv, seq, mem_len, cast_weights):
    t, d = h.shape
    tiles_per_seq = seq // ROW_TM
    grid = (t // ROW_TM,)
    row = lambda i: (i, 0)
    cast_specs, cast_shapes = _cast_specs(cast_weights, grid)
    out = pl.pallas_call(
        _with_casts(_xattn_kernel, 6, 1, len(cast_weights)),
        out_shape=[jax.ShapeDtypeStruct((t, d), F32)] + cast_shapes,
        grid=grid,
        in_specs=[
            pl.BlockSpec((ROW_TM, d), row),
            pl.BlockSpec((ROW_TM, d), row),
            _resident((d, d)),
            _resident((d, d)),
            pl.BlockSpec((mem_len, d), lambda i: (i // tiles_per_seq, 0)),
            pl.BlockSpec((mem_len, d), lambda i: (i // tiles_per_seq, 1)),
        ] + cast_specs,
        out_specs=[pl.BlockSpec((ROW_TM, d), row)] + cast_specs,
        scratch_shapes=[pltpu.VMEM((ROW_TM, d), BF16)],
        compiler_params=_params(1),
        name="xattn",
    )(hn, h, wq, wo, mkv, mkv, *cast_weights)
    return out[0], out[1:]


def kernel(x, mem, ffn1_norm, ffn1_w_gate, ffn1_w_up, ffn1_w_down, mix_norm, w_in, ret_gn_gain, swa_sinks,
           w_out, xa_norm, mem_norm, xa_wq, xa_wkv, xa_wo, ffn2_norm, ffn2_w_gate, ffn2_w_up, ffn2_w_down,
           final_norm):
    batch, seq, d = x.shape
    mem_len = mem.shape[1]
    depth = ffn1_norm.shape[0]
    h = x.reshape(batch * seq, d)
    mem2 = mem.reshape(batch * mem_len, d)
    row = lambda g: g.reshape(1, -1).astype(F32)
    final_gain = row(final_norm)

    for l in range(depth):
        last = l == depth - 1
        h, (w_mix,) = _ffn_f32(h, row(ffn1_norm[l]), ffn1_w_gate[l], ffn1_w_up[l], ffn1_w_down[l], final_gain,
                               final_norm=False, cast_weights=[w_in[l]])
        (dmat, xi_tile, zeta_tile), chunk_decay = _retention_tables()
        rq, rqx, rk, rkz, rv, rg, sq, sk4, sv4_t = _inproj(h, row(mix_norm[l]), w_mix, row(ret_gn_gain[l]),
                                                           xi_tile, zeta_tile, seq)
        ret, swa, (wo_mix, wq, wo) = _mixers(
            rq, rqx, rk, rkz, rv, rg, dmat, chunk_decay, sq, sk4, sv4_t, swa_sinks[l].astype(F32), batch, seq,
            [w_out[l], xa_wq[l], xa_wo[l]])
        h, hn = _outproj(h, ret, swa, wo_mix, row(xa_norm[l]))
        mkv = _memkv(mem2, row(mem_norm[l]), xa_wkv[l])
        h, _ = _xattn(hn, h, wq, wo, mkv, seq, mem_len, [])
        h, _ = _ffn_f32(h, row(ffn2_norm[l]), ffn2_w_gate[l], ffn2_w_up[l], ffn2_w_down[l], final_gain,
                        final_norm=last)
    if depth == 0:
        raise ValueError("depth must be at least 1")
    return h.reshape(batch, seq, d)
```

```python
import functools

import numpy as np
import jax
import jax.numpy as jnp
from jax import lax
from jax.experimental import pallas as pl
from jax.experimental.pallas import tpu as pltpu

F32 = jnp.float32
BF16 = jnp.bfloat16

D_MODEL = 2048
RET_HEADS = 8
RET_DK = 128
RET_DV = 128
RET_WIDTH = RET_HEADS * RET_DV
RET_CHUNK = 128
SWA_HEADS = 16
SWA_KV_HEADS = 2
SWA_HEAD_DIM = 64
SWA_WIDTH = SWA_HEADS * SWA_HEAD_DIM
SWA_KV_WIDTH = SWA_KV_HEADS * SWA_HEAD_DIM
WINDOW = 128
XA_HEADS = 4
XA_HEAD_DIM = D_MODEL // XA_HEADS
ROPE_THETA = 10000.0
EPS = 1e-6
IN_SIZES = (RET_WIDTH, RET_WIDTH, RET_WIDTH, RET_WIDTH, SWA_WIDTH, SWA_KV_WIDTH, SWA_KV_WIDTH)
IN_COLS = sum(IN_SIZES)

LANES = 128
BF16_SUBLANES = 16
VMEM_LIMIT_BYTES = 60000 * 1024

FFN_TM = 1024
FFN_TF = 512
FFN_HEAD_TF = 256
ROW_TM = 512
NORM_ROWS = 256
MIXER_PARTS = 2
MIXER_BLOCKS = 4
NEG_INF = float(np.finfo(np.float32).min)
LOG2E = float(np.log2(np.e))


def _params(n_axes):
    return pltpu.CompilerParams(
        dimension_semantics=("arbitrary",) * n_axes,
        vmem_limit_bytes=VMEM_LIMIT_BYTES,
    )


def _resident(shape):
    zeros = (0,) * len(shape)
    return pl.BlockSpec(shape, lambda *_: zeros, pipeline_mode=pl.Buffered(1))


def _rms_rows(x, gain):
    ms = jnp.mean(x * x, axis=-1, keepdims=True)
    return x * lax.rsqrt(ms + EPS) * gain


def _dot(a, b):
    return lax.dot_general(a, b, (((1,), (0,)), ((), ())), preferred_element_type=F32)


def _dot_nt(a, b):
    return lax.dot_general(a, b, (((1,), (1,)), ((), ())), preferred_element_type=F32)


def _dot_tn(a, b):
    return lax.dot_general(a, b, (((0,), (0,)), ((), ())), preferred_element_type=F32)


def _cast_block(shape, n_steps):
    rows, cols = shape
    for col_splits in (1, 2, 4, 8):
        row_blocks, rem = divmod(n_steps, col_splits)
        if rem or rows % row_blocks or cols % col_splits:
            continue
        br, bc = rows // row_blocks, cols // col_splits
        if br % BF16_SUBLANES == 0 and bc % LANES == 0:
            return br, bc, col_splits
    raise ValueError(f"no aligned {n_steps}-way split of {shape}")


def _with_casts(kernel_fn, n_in, n_out, n_cast):
    def wrapped(*refs):
        ins, rest = refs[:n_in], refs[n_in:]
        cast_in, rest = rest[:n_cast], rest[n_cast:]
        outs, rest = rest[:n_out], rest[n_out:]
        cast_out, scratch = rest[:n_cast], rest[n_cast:]
        kernel_fn(*ins, *outs, *scratch)
        for src, dst in zip(cast_in, cast_out):
            dst[...] = src[...].astype(BF16)
    return wrapped


def _cast_specs(weights, grid):
    n_steps = int(np.prod(grid))
    specs, shapes = [], []
    for w in weights:
        br, bc, col_splits = _cast_block(w.shape, n_steps)

        def index_map(*idx, col_splits=col_splits):
            step = idx[0]
            for size, i in zip(grid[1:], idx[1:]):
                step = step * size + i
            return step // col_splits, step % col_splits
        specs.append(pl.BlockSpec((br, bc), index_map))
        shapes.append(jax.ShapeDtypeStruct(w.shape, BF16))
    return specs, shapes


def _cast_specs_tiled(weights, grid):
    n_outer, n_inner = grid
    specs, shapes = [], []
    for w in weights:
        rows, cols = w.shape
        bc = cols // n_outer
        row_blocks = max(r for r in range(1, n_inner + 1) if rows % r == 0 and (rows // r) % BF16_SUBLANES == 0)
        if cols % n_outer or bc % LANES:
            raise ValueError(f"no aligned split of {w.shape} over {grid}")

        def index_map(i, j, row_blocks=row_blocks):
            return jnp.minimum(j, row_blocks - 1), i
        specs.append(pl.BlockSpec((rows // row_blocks, bc), index_map))
        shapes.append(jax.ShapeDtypeStruct(w.shape, BF16))
    return specs, shapes


def _rows_loop(n_rows, fn):
    def body(r, carry):
        fn(pl.ds(pl.multiple_of(r * NORM_ROWS, NORM_ROWS), NORM_ROWS))
        return carry
    lax.fori_loop(0, n_rows // NORM_ROWS, body, 0)


def _ffn_prologue(h_ref, g_ref, o_ref, xn_ref):
    def norm(rows):
        h = h_ref[rows, :]
        xn_ref[rows, :] = _rms_rows(h, g_ref[...]).astype(BF16)
        o_ref[rows, :] = h
    _rows_loop(h_ref.shape[0], norm)


def _ffn_step(xn_ref, wg, wu, wd, o_ref):
    xn = xn_ref[...]
    g = _dot(xn, wg)
    u = _dot(xn, wu)
    a = (g * (0.5 / (1.0 + jnp.exp(-g))) * u).astype(BF16)
    for c in range(0, o_ref.shape[1], FFN_TF):
        o_ref[:, c:c + FFN_TF] += _dot(a, wd[:, c:c + FFN_TF])


def _ffn_kernel(*refs, n_ff, final_norm, aliased):
    if aliased:
        refs = refs[1:]
    h_ref, g_ref, wg_ref, wu_ref, wd_ref, fg_ref, o_ref, xn_ref = refs
    j = pl.program_id(1)

    @pl.when(j == 0)
    def _():
        _ffn_prologue(h_ref, g_ref, o_ref, xn_ref)

    _ffn_step(xn_ref, wg_ref[...], wu_ref[...], wd_ref[...], o_ref)

    if final_norm:
        @pl.when(j == n_ff - 1)
        def _():
            def norm(rows):
                o_ref[rows, :] = _rms_rows(o_ref[rows, :], fg_ref[...])
            _rows_loop(o_ref.shape[0], norm)


def _ffn(h, gain, wg, wu, wd, final_gain, *, final_norm, first_tile=0, into=None, cast_weights=()):
    t, d = h.shape
    f = wg.shape[1]
    n_ff = f // FFN_TF
    grid = (t // FFN_TM - first_tile, n_ff)
    aliased = into is not None
    tile = lambda i, j: (i + first_tile, 0)
    in_specs = [
        pl.BlockSpec((FFN_TM, d), tile),
        pl.BlockSpec((1, d), lambda i, j: (0, 0)),
        pl.BlockSpec((d, FFN_TF), lambda i, j: (0, j)),
        pl.BlockSpec((d, FFN_TF), lambda i, j: (0, j)),
        pl.BlockSpec((FFN_TF, d), lambda i, j: (j, 0)),
        pl.BlockSpec((1, d), lambda i, j: (0, 0)),
    ]
    args = (h, gain, wg, wu, wd, final_gain)
    if aliased:
        in_specs = [pl.BlockSpec(memory_space=pl.ANY)] + in_specs
        args = (into,) + args
    cast_specs, cast_shapes = _cast_specs_tiled(cast_weights, grid)
    host = functools.partial(_ffn_kernel, n_ff=n_ff, final_norm=final_norm, aliased=aliased)
    out = pl.pallas_call(
        _with_casts(host, len(in_specs), 1, len(cast_weights)),
        out_shape=[jax.ShapeDtypeStruct((t, d), F32)] + cast_shapes,
        grid=grid,
        in_specs=in_specs + cast_specs,
        out_specs=[pl.BlockSpec((FFN_TM, d), tile)] + cast_specs,
        scratch_shapes=[pltpu.VMEM((FFN_TM, d), BF16)],
        input_output_aliases={0: 0} if aliased else {},
        compiler_params=_params(2),
        name="ffn",
    )(*args, *cast_weights)
    return out[0], out[1:]


def _ffn_head_kernel(h_ref, g_ref, wg_ref, wu_ref, wd_ref, fg_ref, o_ref, wg_out, wu_out, wd_out, xn_ref,
                     *, n_ff, final_norm):
    j = pl.program_id(0)

    @pl.when(j == 0)
    def _():
        _ffn_prologue(h_ref, g_ref, o_ref, xn_ref)

    wg, wu, wd = (w[...].astype(BF16) for w in (wg_ref, wu_ref, wd_ref))
    wg_out[...] = wg
    wu_out[...] = wu
    wd_out[...] = wd
    _ffn_step(xn_ref, wg, wu, wd, o_ref)

    if final_norm:
        @pl.when(j == n_ff - 1)
        def _():
            def norm(rows):
                o_ref[rows, :] = _rms_rows(o_ref[rows, :], fg_ref[...])
            _rows_loop(o_ref.shape[0], norm)


def _ffn_head(h, gain, wg, wu, wd, final_gain, *, final_norm):
    t, d = h.shape
    f = wg.shape[1]
    tf = FFN_HEAD_TF
    col = pl.BlockSpec((d, tf), lambda j: (0, j))
    row = pl.BlockSpec((tf, d), lambda j: (j, 0))
    vec = pl.BlockSpec((1, d), lambda j: (0, 0))
    tile0 = pl.BlockSpec((FFN_TM, d), lambda j: (0, 0))
    return pl.pallas_call(
        functools.partial(_ffn_head_kernel, n_ff=f // tf, final_norm=final_norm),
        out_shape=[jax.ShapeDtypeStruct((t, d), F32), jax.ShapeDtypeStruct(wg.shape, BF16),
                   jax.ShapeDtypeStruct(wu.shape, BF16), jax.ShapeDtypeStruct(wd.shape, BF16)],
        grid=(f // tf,),
        in_specs=[tile0, vec, col, col, row, vec],
        out_specs=[tile0, col, col, row],
        scratch_shapes=[pltpu.VMEM((FFN_TM, d), BF16)],
        compiler_params=_params(1),
        name="ffn_head",
    )(h, gain, wg, wu, wd, final_gain)


def _ffn_f32(h, gain, wg, wu, wd, final_gain, *, final_norm, cast_weights=()):
    head, wg16, wu16, wd16 = _ffn_head(h, gain, wg, wu, wd, final_gain, final_norm=final_norm)
    return _ffn(h, gain, wg16, wu16, wd16, final_gain, final_norm=final_norm, first_tile=1, into=head,
                cast_weights=cast_weights)


def _rope_tables(seq):
    pos = np.arange(seq, dtype=np.float32)

    def angles(d):
        inv = np.float32(ROPE_THETA) ** (-np.arange(0, d, 2, dtype=np.float32) / np.float32(d))
        return (pos[:, None] * inv[None, :].astype(np.float32)).astype(np.float32).astype(np.float64)

    a128 = angles(RET_DK)
    cos_r = np.concatenate([np.cos(a128), np.cos(a128)], -1)
    sin_r = np.concatenate([-np.sin(a128), np.sin(a128)], -1)
    a64 = angles(SWA_HEAD_DIM)
    c, s, z = np.cos(a64), np.sin(a64), np.zeros_like(a64)
    cos_s = np.concatenate([c, c, c, c], -1)
    sin_lo = np.concatenate([-s, z, -s, z], -1)
    sin_hi = np.concatenate([z, s, z, s], -1)
    return [jnp.asarray(v, dtype=F32) for v in (cos_r, sin_r, cos_s, sin_lo, sin_hi)]


def _retention_tables():
    c = RET_CHUNK
    heads = np.arange(RET_HEADS, dtype=np.float64)
    log_gamma = np.log1p(-np.exp2(-5.0 - heads))
    idx = np.arange(c, dtype=np.float64)
    diff = idx[:, None] - idx[None, :]
    scale = RET_DK ** -0.5
    dmat = np.where(diff[None] >= 0, np.exp(np.maximum(diff, 0.0)[None] * log_gamma[:, None, None]), 0.0)
    zeta = np.exp((c - 1.0 - idx)[None, :] * log_gamma[:, None])
    xi = np.exp((idx + 1.0)[None, :] * log_gamma[:, None])
    chunk_decay = tuple(float(v) for v in np.exp(c * log_gamma))

    def token_tile(tab):
        return np.tile(np.repeat(tab.T, RET_DK, axis=1), (ROW_TM // c, 1))
    tabs = [jnp.asarray(v, dtype=F32) for v in (dmat * scale, token_tile(xi), token_tile(zeta * scale))]
    return tabs, chunk_decay


def _inproj_kernel(h_ref, g_ref, w_ref, cr_ref, sr_ref, cs_ref, sl_ref, sh_ref, xi_ref, zeta_ref, gn_ref,
                   rq_ref, rqx_ref, rk_ref, rkz_ref, rv_ref, rg_ref, sq_ref, sk_ref, sv_ref, xn_ref):
    tm = h_ref.shape[0]

    def norm(rows):
        xn_ref[rows, :] = _rms_rows(h_ref[rows, :], g_ref[...]).astype(BF16)
    _rows_loop(tm, norm)

    xn = xn_ref[...]
    cr, sr = cr_ref[...], sr_ref[...]
    cs, sl, sh = cs_ref[...], sl_ref[...], sh_ref[...]
    half = LANES // 2
    slabs = [slice(s * LANES, (s + 1) * LANES) for s in range(RET_WIDTH // LANES)]

    def rope_ret(x):
        return x * cr + pltpu.roll(x, half, 1) * sr

    def rope_swa(x):
        return x * cs + pltpu.roll(x, LANES - half // 2, 1) * sl + pltpu.roll(x, half // 2, 1) * sh

    def project(col0, width):
        return _dot(xn, w_ref[:, col0:col0 + width])

    def ret_rotary(out_ref, scaled_ref, tab_ref):
        def epilogue(y):
            for sl_ in slabs:
                x = rope_ret(y[:, sl_])
                out_ref[:, sl_] = x.astype(BF16)
                scaled_ref[:, sl_] = (x * tab_ref[:, sl_]).astype(BF16)
        return epilogue

    def ret_values(y):
        for sl_ in slabs:
            rv_ref[:, sl_] = y[:, sl_].astype(BF16)

    def ret_gate(y):
        for sl_ in slabs:
            g = y[:, sl_]
            rg_ref[:, sl_] = (g * (1.0 / (1.0 + jnp.exp(-g))) * gn_ref[:, sl_]).astype(BF16)

    def swa_queries(y):
        scale = SWA_HEAD_DIM ** -0.5 * LOG2E
        for sl_ in slabs:
            sq_ref[:, sl_] = (rope_swa(y[:, sl_]) * scale).astype(BF16)

    def swa_keys_values(y):
        lo = lax.broadcasted_iota(jnp.int32, (tm, LANES), 1) < half
        for transposed, out_ref, x in ((False, sk_ref, rope_swa(y[:, :LANES])), (True, sv_ref, y[:, LANES:])):
            xr = pltpu.roll(x, half, 1)
            variants = (jnp.where(lo, x, 0.0), jnp.where(lo, 0.0, xr),
                        jnp.where(lo, xr, 0.0), jnp.where(lo, 0.0, x))
            for s, v in enumerate(variants):
                if transposed:
                    out_ref[s * LANES:(s + 1) * LANES, :] = v.T.astype(BF16)
                else:
                    out_ref[:, s * LANES:(s + 1) * LANES] = v.astype(BF16)

    epilogues = (ret_rotary(rq_ref, rqx_ref, xi_ref), ret_rotary(rk_ref, rkz_ref, zeta_ref), ret_values, ret_gate,
                 swa_queries, swa_keys_values)
    col0 = np.cumsum((0,) + IN_SIZES[:-2])
    widths = IN_SIZES[:-2] + (2 * SWA_KV_WIDTH,)
    order = (5, 0, 1, 3, 4, 2)
    for r in order:
        epilogues[r](project(int(col0[r]), widths[r]))


def _inproj(h, gain, w_in, gn_gain, xi_tile, zeta_tile, seq):
    t, d = h.shape
    tables = _rope_tables(seq)
    tiles_per_seq = seq // ROW_TM
    tab_spec = pl.BlockSpec((ROW_TM, LANES), lambda i: (i % tiles_per_seq, 0))

    def row_spec(width):
        return pl.BlockSpec((ROW_TM, width), lambda i: (i, 0))

    widths = (RET_WIDTH,) * 6 + (SWA_WIDTH, 4 * LANES)
    return pl.pallas_call(
        _inproj_kernel,
        out_shape=[jax.ShapeDtypeStruct((t, w), BF16) for w in widths]
        + [jax.ShapeDtypeStruct((4 * LANES, t), BF16)],
        grid=(t // ROW_TM,),
        in_specs=[row_spec(d), _resident((1, d)), _resident((d, IN_COLS))] + [tab_spec] * 5
        + [_resident((ROW_TM, RET_WIDTH))] * 2 + [_resident((1, RET_WIDTH))],
        out_specs=[row_spec(w) for w in widths] + [pl.BlockSpec((4 * LANES, ROW_TM), lambda i: (0, i))],
        scratch_shapes=[pltpu.VMEM((ROW_TM, d), BF16)],
        compiler_params=_params(1),
        name="inproj",
    )(h, gain, w_in, *tables, xi_tile, zeta_tile, gn_gain)


def _mixers_kernel(q_ref, qx_ref, k_ref, kz_ref, v_ref, g_ref, dmat_ref,
                   sink_ref, sq_ref, kp_ref, kc_ref, vp_ref, vc_ref, ret_ref, swa_ref, state_ref, *, chunk_decay):
    n = pl.program_id(1)
    w = WINDOW

    @pl.when(n == 0)
    def _():
        state_ref[...] = jnp.zeros_like(state_ref)

    pairs = SWA_HEADS // SWA_KV_HEADS // 2
    key = lax.broadcasted_iota(jnp.int32, (2 * w, pairs * w), 0)
    qry = lax.broadcasted_iota(jnp.int32, (2 * w, pairs * w), 1) % w
    band = (key > qry) & (key <= qry + w)
    bias = jnp.where(band, 0.0, NEG_INF)
    bias_first = jnp.where(band & (key >= jnp.where(n == 0, w, 0)), 0.0, NEG_INF)

    hs = [slice(h * RET_DK, (h + 1) * RET_DK) for h in range(RET_HEADS)]
    lane_blk = lambda c: slice(c * LANES, (c + 1) * LANES)
    slabs = {g: [g * pairs + p for p in range(pairs)] for g in range(SWA_KV_HEADS)}

    for blk in range(MIXER_BLOCKS):
        rows = slice(blk * w, (blk + 1) * w)
        mask_bias = bias if blk else bias_first

        def prev_keys(c):
            return kc_ref[(blk - 1) * w:blk * w, lane_blk(c)] if blk else kp_ref[:, lane_blk(c)]

        def prev_values_t(c):
            return vc_ref[lane_blk(c), (blk - 1) * w:blk * w] if blk else vp_ref[lane_blk(c), :]

        for part in range(MIXER_PARTS):
            heads = range(part * RET_HEADS // MIXER_PARTS, (part + 1) * RET_HEADS // MIXER_PARTS)
            groups = range(part * SWA_KV_HEADS // MIXER_PARTS, (part + 1) * SWA_KV_HEADS // MIXER_PARTS)
            phases = [(g, e) for g in groups for e in range(2)]

            s_ret = {h: _dot_nt(q_ref[rows, hs[h]], k_ref[rows, hs[h]]) for h in heads}
            q_swa = {g: jnp.concatenate([sq_ref[rows, lane_blk(sl)] for sl in slabs[g]], 0) for g in groups}
            s_swa = []
            for g, e in phases:
                c = 2 * g + e
                k = jnp.concatenate([prev_keys(c), kc_ref[rows, lane_blk(c)]], 0)
                s_swa.append(_dot_nt(k, q_swa[g]) + mask_bias)

            states = {h: state_ref[h] for h in heads}
            ret = {}
            for h in heads:
                lhs = jnp.concatenate([(s_ret[h] * dmat_ref[h]).astype(BF16), qx_ref[rows, hs[h]]], 1)
                rhs = jnp.concatenate([v_ref[rows, hs[h]], states[h].astype(BF16)], 0)
                ret[h] = _dot(lhs, rhs)
            for h in heads:
                state_ref[h] = states[h] * chunk_decay[h] + _dot_tn(kz_ref[rows, hs[h]], v_ref[rows, hs[h]])

            pv = {}
            for (g, e), s in zip(phases, s_swa):
                c = 2 * g + e
                sink = jnp.concatenate([jnp.full((1, w), sink_ref[2 * sl + e] * LOG2E, F32) for sl in slabs[g]], 1)
                m = jnp.maximum(jnp.max(s, axis=0, keepdims=True), sink)
                p = jnp.exp2(s - m)
                inv = 1.0 / (jnp.sum(p, axis=0, keepdims=True) + jnp.exp2(sink - m))
                v_t = jnp.concatenate([prev_values_t(c), vc_ref[lane_blk(c), rows]], 1)
                pv[g, e] = _dot(v_t, p.astype(BF16)) * inv

            for h in heads:
                mu = jnp.mean(ret[h], axis=-1, keepdims=True)
                cen = ret[h] - mu
                var = jnp.mean(cen * cen, axis=-1, keepdims=True)
                ret_ref[rows, hs[h]] = (cen * lax.rsqrt(var + EPS) * g_ref[rows, hs[h]].astype(F32)).astype(BF16)

            for g in groups:
                acc = pv[g, 0] + pv[g, 1]
                for i, sl in enumerate(slabs[g]):
                    swa_ref[lane_blk(sl), rows] = acc[:, i * w:(i + 1) * w].astype(BF16)


def _mixers(rq, rqx, rk, rkz, rv, rg, dmat, chunk_decay, sq, sk4, sv4_t, sinks, batch, seq, cast_weights):
    assert RET_CHUNK == WINDOW
    t = rq.shape[0]
    rows = MIXER_BLOCKS * WINDOW
    n_steps = seq // rows
    grid = (batch, n_steps)
    cur = lambda b, n: (b * n_steps + n, 0)
    cur_t = lambda b, n: (0, b * n_steps + n)
    prev_idx = lambda b, n: b * n_steps * MIXER_BLOCKS + jnp.maximum(n * MIXER_BLOCKS - 1, 0)
    blk = pl.BlockSpec((rows, RET_WIDTH), cur)
    in_specs = [
        blk, blk, blk, blk, blk, blk, _resident((RET_HEADS, RET_CHUNK, RET_CHUNK)),
        pl.BlockSpec(memory_space=pltpu.SMEM),
        pl.BlockSpec((rows, SWA_WIDTH), cur),
        pl.BlockSpec((WINDOW, 4 * LANES), lambda b, n: (prev_idx(b, n), 0)),
        pl.BlockSpec((rows, 4 * LANES), cur),
        pl.BlockSpec((4 * LANES, WINDOW), lambda b, n: (0, prev_idx(b, n))),
        pl.BlockSpec((4 * LANES, rows), cur_t),
    ]
    cast_specs, cast_shapes = _cast_specs(cast_weights, grid)
    host = functools.partial(_mixers_kernel, chunk_decay=chunk_decay)
    out = pl.pallas_call(
        _with_casts(host, len(in_specs), 2, len(cast_weights)),
        out_shape=[jax.ShapeDtypeStruct((t, RET_WIDTH), BF16), jax.ShapeDtypeStruct((SWA_WIDTH, t), BF16)]
        + cast_shapes,
        grid=grid,
        in_specs=in_specs + cast_specs,
        out_specs=[blk, pl.BlockSpec((SWA_WIDTH, rows), cur_t)] + cast_specs,
        scratch_shapes=[pltpu.VMEM((RET_HEADS, RET_DK, RET_DV), F32)],
        compiler_params=_params(2),
        name="mixers",
    )(rq, rqx, rk, rkz, rv, rg, dmat, sinks, sq, sk4, sk4, sv4_t, sv4_t, *cast_weights)
    return out[0], out[1], out[2:]


def _outproj_kernel(h_ref, ret_ref, swa_ref, w_ref, g_ref, h2_ref, hn_ref):
    tm = h_ref.shape[0]
    y = _dot(ret_ref[...], w_ref[:RET_WIDTH, :]) + _dot_tn(swa_ref[...], w_ref[RET_WIDTH:, :])
    h2_ref[...] = h_ref[...] + y

    def body(r, carry):
        rows = pl.ds(pl.multiple_of(r * NORM_ROWS, NORM_ROWS), NORM_ROWS)
        hn_ref[rows, :] = _rms_rows(h2_ref[rows, :], g_ref[...]).astype(BF16)
        return carry
    lax.fori_loop(0, tm // NORM_ROWS, body, 0)


def _outproj(h, ret, swa, w_out, gain):
    t, d = h.shape
    row = lambda width: pl.BlockSpec((ROW_TM, width), lambda i: (i, 0))
    return pl.pallas_call(
        _outproj_kernel,
        out_shape=[jax.ShapeDtypeStruct((t, d), F32), jax.ShapeDtypeStruct((t, d), BF16)],
        grid=(t // ROW_TM,),
        in_specs=[row(d), row(RET_WIDTH), pl.BlockSpec((SWA_WIDTH, ROW_TM), lambda i: (0, i)),
                  _resident(w_out.shape), _resident((1, d))],
        out_specs=[row(d), row(d)],
        compiler_params=_params(1),
        name="outproj",
    )(h, ret, swa, w_out, gain)


def _memkv_kernel(m_ref, g_ref, w_ref, o_ref):
    xn = _rms_rows(m_ref[...], g_ref[...]).astype(BF16)
    o_ref[...] = _dot(xn, w_ref[...].astype(BF16)).astype(BF16)


def _memkv(mem, gain, wkv):
    rows, d = mem.shape
    n_out = wkv.shape[1]
    tn = 512
    return pl.pallas_call(
        _memkv_kernel,
        out_shape=jax.ShapeDtypeStruct((rows, n_out), BF16),
        grid=(n_out // tn,),
        in_specs=[_resident((rows, d)), _resident((1, d)), pl.BlockSpec((d, tn), lambda j: (0, j))],
        out_specs=pl.BlockSpec((rows, tn), lambda j: (0, j)),
        compiler_params=_params(1),
        name="memkv",
    )(mem, gain, wkv)


def _xattn_kernel(hn_ref, h_ref, wq_ref, wo_ref, k_ref, v_ref, o_ref, att_ref):
    q = _dot(hn_ref[...], wq_ref[...]).astype(BF16)
    scale = XA_HEAD_DIM ** -0.5
    heads = [slice(hd * XA_HEAD_DIM, (hd + 1) * XA_HEAD_DIM) for hd in range(XA_HEADS)]
    scores = [_dot_nt(q[:, hs], k_ref[:, hs]) * scale for hs in heads]
    probs, invs = [], []
    for s in scores:
        p = jnp.exp(s - jnp.max(s, axis=-1, keepdims=True))
        probs.append(p.astype(BF16))
        invs.append(1.0 / jnp.sum(p, axis=-1, keepdims=True))
    for hs, p, inv in zip(heads, probs, invs):
        att_ref[:, hs] = (_dot(p, v_ref[:, hs]) * inv).astype(BF16)
    o_ref[...] = h_ref[...] + _dot(att_ref[...], wo_ref[...])


def _xattn(hn, h, wq, wo, mkv, seq, mem_len):
    t, d = h.shape
    tiles_per_seq = seq // ROW_TM
    row = lambda i: (i, 0)
    return pl.pallas_call(
        _xattn_kernel,
        out_shape=jax.ShapeDtypeStruct((t, d), F32),
        grid=(t // ROW_TM,),
        in_specs=[
            pl.BlockSpec((ROW_TM, d), row),
            pl.BlockSpec((ROW_TM, d), row),
            _resident((d, d)),
            _resident((d, d)),
            pl.BlockSpec((mem_len, d), lambda i: (i // tiles_per_seq, 0)),
            pl.BlockSpec((mem_len, d), lambda i: (i // tiles_per_seq, 1)),
        ],
        out_specs=pl.BlockSpec((ROW_TM, d), row),
        scratch_shapes=[pltpu.VMEM((ROW_TM, d), BF16)],
        compiler_params=_params(1),
        name="xattn",
    )(hn, h, wq, wo, mkv, mkv)


def kernel(x, mem, ffn1_norm, ffn1_w_gate, ffn1_w_up, ffn1_w_down, mix_norm, w_in, ret_gn_gain, swa_sinks,
           w_out, xa_norm, mem_norm, xa_wq, xa_wkv, xa_wo, ffn2_norm, ffn2_w_gate, ffn2_w_up, ffn2_w_down,
           final_norm):
    batch, seq, d = x.shape
    mem_len = mem.shape[1]
    depth = ffn1_norm.shape[0]
    h = x.reshape(batch * seq, d)
    mem2 = mem.reshape(batch * mem_len, d)
    row = lambda g: g.reshape(1, -1).astype(F32)
    final_gain = row(final_norm)

    for l in range(depth):
        last = l == depth - 1
        h, (w_mix,) = _ffn_f32(h, row(ffn1_norm[l]), ffn1_w_gate[l], ffn1_w_up[l], ffn1_w_down[l], final_gain,
                               final_norm=False, cast_weights=[w_in[l]])
        (dmat, xi_tile, zeta_tile), chunk_decay = _retention_tables()
        rq, rqx, rk, rkz, rv, rg, sq, sk4, sv4_t = _inproj(h, row(mix_norm[l]), w_mix, row(ret_gn_gain[l]),
                                                           xi_tile, zeta_tile, seq)
        ret, swa, (wo_mix, wq, wo) = _mixers(
            rq, rqx, rk, rkz, rv, rg, dmat, chunk_decay, sq, sk4, sv4_t, swa_sinks[l].astype(F32), batch, seq,
            [w_out[l], xa_wq[l], xa_wo[l]])
        h, hn = _outproj(h, ret, swa, wo_mix, row(xa_norm[l]))
        mkv = _memkv(mem2, row(mem_norm[l]), xa_wkv[l])
        h = _xattn(hn, h, wq, wo, mkv, seq, mem_len)
        h, _ = _ffn_f32(h, row(ffn2_norm[l]), ffn2_w_gate[l], ffn2_w_up[l], ffn2_w_down[l], final_gain,
                        final_norm=last)
    if depth == 0:
        raise ValueError("depth must be at least 1")
    return h.reshape(batch, seq, d)
```

```python
import functools

import numpy as np
import jax
import jax.numpy as jnp
from jax import lax
from jax.experimental import pallas as pl
from jax.experimental.pallas import tpu as pltpu

F32 = jnp.float32
BF16 = jnp.bfloat16

D_MODEL = 2048
RET_HEADS = 8
RET_DK = 128
RET_DV = 128
RET_WIDTH = RET_HEADS * RET_DV
RET_CHUNK = 128
SWA_HEADS = 16
SWA_KV_HEADS = 2
SWA_HEAD_DIM = 64
SWA_WIDTH = SWA_HEADS * SWA_HEAD_DIM
SWA_KV_WIDTH = SWA_KV_HEADS * SWA_HEAD_DIM
WINDOW = 128
XA_HEADS = 4
XA_HEAD_DIM = D_MODEL // XA_HEADS
ROPE_THETA = 10000.0
EPS = 1e-6
IN_SIZES = (RET_WIDTH, RET_WIDTH, RET_WIDTH, RET_WIDTH, SWA_WIDTH, SWA_KV_WIDTH, SWA_KV_WIDTH)
IN_COLS = sum(IN_SIZES)

LANES = 128
BF16_SUBLANES = 16
VMEM_LIMIT_BYTES = 60000 * 1024

FFN_TM = 1024
FFN_TF = 512
FFN_HEAD_TF = 256
ROW_TM = 512
NORM_ROWS = 512
MIXER_PARTS = 2
MIXER_BLOCKS = 4
NEG_INF = float(np.finfo(np.float32).min)
LOG2E = float(np.log2(np.e))


def _params(n_axes):
    return pltpu.CompilerParams(
        dimension_semantics=("arbitrary",) * n_axes,
        vmem_limit_bytes=VMEM_LIMIT_BYTES,
    )


def _resident(shape):
    zeros = (0,) * len(shape)
    return pl.BlockSpec(shape, lambda *_: zeros, pipeline_mode=pl.Buffered(1))


def _rms_rows(x, gain):
    ms = jnp.mean(x * x, axis=-1, keepdims=True)
    return x * lax.rsqrt(ms + EPS) * gain


def _dot(a, b):
    return lax.dot_general(a, b, (((1,), (0,)), ((), ())), preferred_element_type=F32)


def _dot_nt(a, b):
    return lax.dot_general(a, b, (((1,), (1,)), ((), ())), preferred_element_type=F32)


def _dot_tn(a, b):
    return lax.dot_general(a, b, (((0,), (0,)), ((), ())), preferred_element_type=F32)


def _cast_block(shape, n_steps):
    rows, cols = shape
    for col_splits in (1, 2, 4, 8):
        row_blocks, rem = divmod(n_steps, col_splits)
        if rem or rows % row_blocks or cols % col_splits:
            continue
        br, bc = rows // row_blocks, cols // col_splits
        if br % BF16_SUBLANES == 0 and bc % LANES == 0:
            return br, bc, col_splits
    raise ValueError(f"no aligned {n_steps}-way split of {shape}")


def _with_casts(kernel_fn, n_in, n_out, n_cast):
    def wrapped(*refs):
        ins, rest = refs[:n_in], refs[n_in:]
        cast_in, rest = rest[:n_cast], rest[n_cast:]
        outs, rest = rest[:n_out], rest[n_out:]
        cast_out, scratch = rest[:n_cast], rest[n_cast:]
        kernel_fn(*ins, *outs, *scratch)
        for src, dst in zip(cast_in, cast_out):
            dst[...] = src[...].astype(BF16)
    return wrapped


def _cast_specs(weights, grid):
    n_steps = int(np.prod(grid))
    specs, shapes = [], []
    for w in weights:
        br, bc, col_splits = _cast_block(w.shape, n_steps)

        def index_map(*idx, col_splits=col_splits):
            step = idx[0]
            for size, i in zip(grid[1:], idx[1:]):
                step = step * size + i
            return step // col_splits, step % col_splits
        specs.append(pl.BlockSpec((br, bc), index_map))
        shapes.append(jax.ShapeDtypeStruct(w.shape, BF16))
    return specs, shapes


def _cast_specs_tiled(weights, grid):
    n_outer, n_inner = grid
    specs, shapes = [], []
    for w in weights:
        rows, cols = w.shape
        bc = cols // n_outer
        row_blocks = max(r for r in range(1, n_inner + 1) if rows % r == 0 and (rows // r) % BF16_SUBLANES == 0)
        if cols % n_outer or bc % LANES:
            raise ValueError(f"no aligned split of {w.shape} over {grid}")

        def index_map(i, j, row_blocks=row_blocks):
            return jnp.minimum(j, row_blocks - 1), i
        specs.append(pl.BlockSpec((rows // row_blocks, bc), index_map))
        shapes.append(jax.ShapeDtypeStruct(w.shape, BF16))
    return specs, shapes


def _rows_loop(n_rows, fn):
    def body(r, carry):
        fn(pl.ds(pl.multiple_of(r * NORM_ROWS, NORM_ROWS), NORM_ROWS))
        return carry
    lax.fori_loop(0, n_rows // NORM_ROWS, body, 0)


def _ffn_prologue(h_ref, g_ref, o_ref, xn_ref):
    def norm(rows):
        h = h_ref[rows, :]
        xn_ref[rows, :] = _rms_rows(h, g_ref[...]).astype(BF16)
        o_ref[rows, :] = h
    _rows_loop(h_ref.shape[0], norm)


def _ffn_step(xn_ref, wg, wu, wd, o_ref):
    xn = xn_ref[...]
    g = _dot(xn, wg)
    u = _dot(xn, wu)
    a = (g * (0.5 / (1.0 + jnp.exp(-g))) * u).astype(BF16)
    for c in range(0, o_ref.shape[1], FFN_TF):
        o_ref[:, c:c + FFN_TF] += _dot(a, wd[:, c:c + FFN_TF])


def _ffn_kernel(*refs, n_ff, final_norm, aliased):
    if aliased:
        refs = refs[1:]
    h_ref, g_ref, wg_ref, wu_ref, wd_ref, fg_ref, o_ref, xn_ref = refs
    j = pl.program_id(1)

    @pl.when(j == 0)
    def _():
        _ffn_prologue(h_ref, g_ref, o_ref, xn_ref)

    _ffn_step(xn_ref, wg_ref[...], wu_ref[...], wd_ref[...], o_ref)

    if final_norm:
        @pl.when(j == n_ff - 1)
        def _():
            def norm(rows):
                o_ref[rows, :] = _rms_rows(o_ref[rows, :], fg_ref[...])
            _rows_loop(o_ref.shape[0], norm)


def _ffn(h, gain, wg, wu, wd, final_gain, *, final_norm, first_tile=0, into=None, cast_weights=()):
    t, d = h.shape
    f = wg.shape[1]
    n_ff = f // FFN_TF
    grid = (t // FFN_TM - first_tile, n_ff)
    aliased = into is not None
    tile = lambda i, j: (i + first_tile, 0)
    in_specs = [
        pl.BlockSpec((FFN_TM, d), tile),
        pl.BlockSpec((1, d), lambda i, j: (0, 0)),
        pl.BlockSpec((d, FFN_TF), lambda i, j: (0, j)),
        pl.BlockSpec((d, FFN_TF), lambda i, j: (0, j)),
        pl.BlockSpec((FFN_TF, d), lambda i, j: (j, 0)),
        pl.BlockSpec((1, d), lambda i, j: (0, 0)),
    ]
    args = (h, gain, wg, wu, wd, final_gain)
    if aliased:
        in_specs = [pl.BlockSpec(memory_space=pl.ANY)] + in_specs
        args = (into,) + args
    cast_specs, cast_shapes = _cast_specs_tiled(cast_weights, grid)
    host = functools.partial(_ffn_kernel, n_ff=n_ff, final_norm=final_norm, aliased=aliased)
    out = pl.pallas_call(
        _with_casts(host, len(in_specs), 1, len(cast_weights)),
        out_shape=[jax.ShapeDtypeStruct((t, d), F32)] + cast_shapes,
        grid=grid,
        in_specs=in_specs + cast_specs,
        out_specs=[pl.BlockSpec((FFN_TM, d), tile)] + cast_specs,
        scratch_shapes=[pltpu.VMEM((FFN_TM, d), BF16)],
        input_output_aliases={0: 0} if aliased else {},
        compiler_params=_params(2),
        name="ffn",
    )(*args, *cast_weights)
    return out[0], out[1:]


def _ffn_head_kernel(h_ref, g_ref, wg_ref, wu_ref, wd_ref, fg_ref, o_ref, wg_out, wu_out, wd_out, xn_ref,
                     *, n_ff, final_norm):
    j = pl.program_id(0)

    @pl.when(j == 0)
    def _():
        _ffn_prologue(h_ref, g_ref, o_ref, xn_ref)

    wg, wu, wd = (w[...].astype(BF16) for w in (wg_ref, wu_ref, wd_ref))
    wg_out[...] = wg
    wu_out[...] = wu
    wd_out[...] = wd
    _ffn_step(xn_ref, wg, wu, wd, o_ref)

    if final_norm:
        @pl.when(j == n_ff - 1)
        def _():
            def norm(rows):
                o_ref[rows, :] = _rms_rows(o_ref[rows, :], fg_ref[...])
            _rows_loop(o_ref.shape[0], norm)


def _ffn_head(h, gain, wg, wu, wd, final_gain, *, final_norm):
    t, d = h.shape
    f = wg.shape[1]
    tf = FFN_HEAD_TF
    col = pl.BlockSpec((d, tf), lambda j: (0, j))
    row = pl.BlockSpec((tf, d), lambda j: (j, 0))
    vec = pl.BlockSpec((1, d), lambda j: (0, 0))
    tile0 = pl.BlockSpec((FFN_TM, d), lambda j: (0, 0))
    return pl.pallas_call(
        functools.partial(_ffn_head_kernel, n_ff=f // tf, final_norm=final_norm),
        out_shape=[jax.ShapeDtypeStruct((t, d), F32), jax.ShapeDtypeStruct(wg.shape, BF16),
                   jax.ShapeDtypeStruct(wu.shape, BF16), jax.ShapeDtypeStruct(wd.shape, BF16)],
        grid=(f // tf,),
        in_specs=[tile0, vec, col, col, row, vec],
        out_specs=[tile0, col, col, row],
        scratch_shapes=[pltpu.VMEM((FFN_TM, d), BF16)],
        compiler_params=_params(1),
        name="ffn_head",
    )(h, gain, wg, wu, wd, final_gain)


def _ffn_f32(h, gain, wg, wu, wd, final_gain, *, final_norm, cast_weights=()):
    head, wg16, wu16, wd16 = _ffn_head(h, gain, wg, wu, wd, final_gain, final_norm=final_norm)
    return _ffn(h, gain, wg16, wu16, wd16, final_gain, final_norm=final_norm, first_tile=1, into=head,
                cast_weights=cast_weights)


def _rope_tables(seq):
    pos = np.arange(seq, dtype=np.float32)

    def angles(d):
        inv = np.float32(ROPE_THETA) ** (-np.arange(0, d, 2, dtype=np.float32) / np.float32(d))
        return (pos[:, None] * inv[None, :].astype(np.float32)).astype(np.float32).astype(np.float64)

    a128 = angles(RET_DK)
    cos_r = np.concatenate([np.cos(a128), np.cos(a128)], -1)
    sin_r = np.concatenate([-np.sin(a128), np.sin(a128)], -1)
    a64 = angles(SWA_HEAD_DIM)
    c, s, z = np.cos(a64), np.sin(a64), np.zeros_like(a64)
    cos_s = np.concatenate([c, c, c, c], -1)
    sin_lo = np.concatenate([-s, z, -s, z], -1)
    sin_hi = np.concatenate([z, s, z, s], -1)
    return [jnp.asarray(v, dtype=F32) for v in (cos_r, sin_r, cos_s, sin_lo, sin_hi)]


def _retention_tables():
    c = RET_CHUNK
    heads = np.arange(RET_HEADS, dtype=np.float64)
    log_gamma = np.log1p(-np.exp2(-5.0 - heads))
    idx = np.arange(c, dtype=np.float64)
    diff = idx[:, None] - idx[None, :]
    scale = RET_DK ** -0.5
    dmat = np.where(diff[None] >= 0, np.exp(np.maximum(diff, 0.0)[None] * log_gamma[:, None, None]), 0.0)
    zeta = np.exp((c - 1.0 - idx)[None, :] * log_gamma[:, None])
    xi = np.exp((idx + 1.0)[None, :] * log_gamma[:, None])
    chunk_decay = tuple(float(v) for v in np.exp(c * log_gamma))

    def token_tile(tab):
        return np.tile(np.repeat(tab.T, RET_DK, axis=1), (ROW_TM // c, 1))
    tabs = [jnp.asarray(v, dtype=F32) for v in (dmat * scale, token_tile(xi), token_tile(zeta * scale))]
    return tabs, chunk_decay


def _inproj_kernel(h_ref, g_ref, w_ref, cr_ref, sr_ref, cs_ref, sl_ref, sh_ref, xi_ref, zeta_ref, gn_ref,
                   rq_ref, rqx_ref, rk_ref, rkz_ref, rv_ref, rg_ref, sq_ref, sk_ref, sv_ref, xn_ref):
    tm = h_ref.shape[0]

    def norm(rows):
        xn_ref[rows, :] = _rms_rows(h_ref[rows, :], g_ref[...]).astype(BF16)
    _rows_loop(tm, norm)

    xn = xn_ref[...]
    cr, sr = cr_ref[...], sr_ref[...]
    cs, sl, sh = cs_ref[...], sl_ref[...], sh_ref[...]
    half = LANES // 2
    slabs = [slice(s * LANES, (s + 1) * LANES) for s in range(RET_WIDTH // LANES)]

    def rope_ret(x):
        return x * cr + pltpu.roll(x, half, 1) * sr

    def rope_swa(x):
        return x * cs + pltpu.roll(x, LANES - half // 2, 1) * sl + pltpu.roll(x, half // 2, 1) * sh

    def project(col0, width):
        return _dot(xn, w_ref[:, col0:col0 + width])

    def ret_rotary(out_ref, scaled_ref, tab_ref):
        def epilogue(y):
            for sl_ in slabs:
                x = rope_ret(y[:, sl_])
                out_ref[:, sl_] = x.astype(BF16)
                scaled_ref[:, sl_] = (x * tab_ref[:, sl_]).astype(BF16)
        return epilogue

    def ret_values(y):
        for sl_ in slabs:
            rv_ref[:, sl_] = y[:, sl_].astype(BF16)

    def ret_gate(y):
        for sl_ in slabs:
            g = y[:, sl_]
            rg_ref[:, sl_] = (g * (1.0 / (1.0 + jnp.exp(-g))) * gn_ref[:, sl_]).astype(BF16)

    def swa_queries(y):
        scale = SWA_HEAD_DIM ** -0.5 * LOG2E
        for sl_ in slabs:
            sq_ref[:, sl_] = (rope_swa(y[:, sl_]) * scale).astype(BF16)

    def swa_keys_values(y):
        lo = lax.broadcasted_iota(jnp.int32, (tm, LANES), 1) < half
        for transposed, out_ref, x in ((False, sk_ref, rope_swa(y[:, :LANES])), (True, sv_ref, y[:, LANES:])):
            xr = pltpu.roll(x, half, 1)
            variants = (jnp.where(lo, x, 0.0), jnp.where(lo, 0.0, xr),
                        jnp.where(lo, xr, 0.0), jnp.where(lo, 0.0, x))
            for s, v in enumerate(variants):
                if transposed:
                    out_ref[s * LANES:(s + 1) * LANES, :] = v.T.astype(BF16)
                else:
                    out_ref[:, s * LANES:(s + 1) * LANES] = v.astype(BF16)

    epilogues = (ret_rotary(rq_ref, rqx_ref, xi_ref), ret_rotary(rk_ref, rkz_ref, zeta_ref), ret_values, ret_gate,
                 swa_queries, swa_keys_values)
    col0 = np.cumsum((0,) + IN_SIZES[:-2])
    widths = IN_SIZES[:-2] + (2 * SWA_KV_WIDTH,)
    order = (5, 0, 1, 3, 4, 2)
    for r in order:
        epilogues[r](project(int(col0[r]), widths[r]))


def _inproj(h, gain, w_in, gn_gain, xi_tile, zeta_tile, seq):
    t, d = h.shape
    tables = _rope_tables(seq)
    tiles_per_seq = seq // ROW_TM
    tab_spec = pl.BlockSpec((ROW_TM, LANES), lambda i: (i % tiles_per_seq, 0))

    def row_spec(width):
        return pl.BlockSpec((ROW_TM, width), lambda i: (i, 0))

    widths = (RET_WIDTH,) * 6 + (SWA_WIDTH, 4 * LANES)
    return pl.pallas_call(
        _inproj_kernel,
        out_shape=[jax.ShapeDtypeStruct((t, w), BF16) for w in widths]
        + [jax.ShapeDtypeStruct((4 * LANES, t), BF16)],
        grid=(t // ROW_TM,),
        in_specs=[row_spec(d), _resident((1, d)), _resident((d, IN_COLS))] + [tab_spec] * 5
        + [_resident((ROW_TM, RET_WIDTH))] * 2 + [_resident((1, RET_WIDTH))],
        out_specs=[row_spec(w) for w in widths] + [pl.BlockSpec((4 * LANES, ROW_TM), lambda i: (0, i))],
        scratch_shapes=[pltpu.VMEM((ROW_TM, d), BF16)],
        compiler_params=_params(1),
        name="inproj",
    )(h, gain, w_in, *tables, xi_tile, zeta_tile, gn_gain)


def _mixers_kernel(q_ref, qx_ref, k_ref, kz_ref, v_ref, g_ref, dmat_ref,
                   sink_ref, sq_ref, kp_ref, kc_ref, vp_ref, vc_ref, ret_ref, swa_ref, state_ref, *, chunk_decay):
    n = pl.program_id(1)
    w = WINDOW

    @pl.when(n == 0)
    def _():
        state_ref[...] = jnp.zeros_like(state_ref)

    pairs = SWA_HEADS // SWA_KV_HEADS // 2
    key = lax.broadcasted_iota(jnp.int32, (2 * w, pairs * w), 0)
    qry = lax.broadcasted_iota(jnp.int32, (2 * w, pairs * w), 1) % w
    band = (key > qry) & (key <= qry + w)
    bias = jnp.where(band, 0.0, NEG_INF)
    bias_first = jnp.where(band & (key >= jnp.where(n == 0, w, 0)), 0.0, NEG_INF)

    hs = [slice(h * RET_DK, (h + 1) * RET_DK) for h in range(RET_HEADS)]
    lane_blk = lambda c: slice(c * LANES, (c + 1) * LANES)
    slabs = {g: [g * pairs + p for p in range(pairs)] for g in range(SWA_KV_HEADS)}

    for blk in range(MIXER_BLOCKS):
        rows = slice(blk * w, (blk + 1) * w)
        mask_bias = bias if blk else bias_first

        def prev_keys(c):
            return kc_ref[(blk - 1) * w:blk * w, lane_blk(c)] if blk else kp_ref[:, lane_blk(c)]

        def prev_values_t(c):
            return vc_ref[lane_blk(c), (blk - 1) * w:blk * w] if blk else vp_ref[lane_blk(c), :]

        for part in range(MIXER_PARTS):
            heads = range(part * RET_HEADS // MIXER_PARTS, (part + 1) * RET_HEADS // MIXER_PARTS)
            groups = range(part * SWA_KV_HEADS // MIXER_PARTS, (part + 1) * SWA_KV_HEADS // MIXER_PARTS)
            phases = [(g, e) for g in groups for e in range(2)]

            s_ret = {h: _dot_nt(q_ref[rows, hs[h]], k_ref[rows, hs[h]]) for h in heads}
            q_swa = {g: jnp.concatenate([sq_ref[rows, lane_blk(sl)] for sl in slabs[g]], 0) for g in groups}
            s_swa = []
            for g, e in phases:
                c = 2 * g + e
                k = jnp.concatenate([prev_keys(c), kc_ref[rows, lane_blk(c)]], 0)
                s_swa.append(_dot_nt(k, q_swa[g]) + mask_bias)

            states = {h: state_ref[h] for h in heads}
            ret = {}
            for h in heads:
                lhs = jnp.concatenate([(s_ret[h] * dmat_ref[h]).astype(BF16), qx_ref[rows, hs[h]]], 1)
                rhs = jnp.concatenate([v_ref[rows, hs[h]], states[h].astype(BF16)], 0)
                ret[h] = _dot(lhs, rhs)
            for h in heads:
                state_ref[h] = states[h] * chunk_decay[h] + _dot_tn(kz_ref[rows, hs[h]], v_ref[rows, hs[h]])

            pv = {}
            for (g, e), s in zip(phases, s_swa):
                c = 2 * g + e
                sink = jnp.concatenate([jnp.full((1, w), sink_ref[2 * sl + e] * LOG2E, F32) for sl in slabs[g]], 1)
                m = jnp.maximum(jnp.max(s, axis=0, keepdims=True), sink)
                p = jnp.exp2(s - m)
                inv = 1.0 / (jnp.sum(p, axis=0, keepdims=True) + jnp.exp2(sink - m))
                v_t = jnp.concatenate([prev_values_t(c), vc_ref[lane_blk(c), rows]], 1)
                pv[g, e] = _dot(v_t, p.astype(BF16)) * inv

            for h in heads:
                mu = jnp.mean(ret[h], axis=-1, keepdims=True)
                cen = ret[h] - mu
                var = jnp.mean(cen * cen, axis=-1, keepdims=True)
                ret_ref[rows, hs[h]] = (cen * lax.rsqrt(var + EPS) * g_ref[rows, hs[h]].astype(F32)).astype(BF16)

            for g in groups:
                acc = pv[g, 0] + pv[g, 1]
                for i, sl in enumerate(slabs[g]):
                    swa_ref[lane_blk(sl), rows] = acc[:, i * w:(i + 1) * w].astype(BF16)


def _mixers(rq, rqx, rk, rkz, rv, rg, dmat, chunk_decay, sq, sk4, sv4_t, sinks, batch, seq, cast_weights):
    assert RET_CHUNK == WINDOW
    t = rq.shape[0]
    rows = MIXER_BLOCKS * WINDOW
    n_steps = seq // rows
    grid = (batch, n_steps)
    cur = lambda b, n: (b * n_steps + n, 0)
    cur_t = lambda b, n: (0, b * n_steps + n)
    prev_idx = lambda b, n: b * n_steps * MIXER_BLOCKS + jnp.maximum(n * MIXER_BLOCKS - 1, 0)
    blk = pl.BlockSpec((rows, RET_WIDTH), cur)
    in_specs = [
        blk, blk, blk, blk, blk, blk, _resident((RET_HEADS, RET_CHUNK, RET_CHUNK)),
        pl.BlockSpec(memory_space=pltpu.SMEM),
        pl.BlockSpec((rows, SWA_WIDTH), cur),
        pl.BlockSpec((WINDOW, 4 * LANES), lambda b, n: (prev_idx(b, n), 0)),
        pl.BlockSpec((rows, 4 * LANES), cur),
        pl.BlockSpec((4 * LANES, WINDOW), lambda b, n: (0, prev_idx(b, n))),
        pl.BlockSpec((4 * LANES, rows), cur_t),
    ]
    cast_specs, cast_shapes = _cast_specs(cast_weights, grid)
    host = functools.partial(_mixers_kernel, chunk_decay=chunk_decay)
    out = pl.pallas_call(
        _with_casts(host, len(in_specs), 2, len(cast_weights)),
        out_shape=[jax.ShapeDtypeStruct((t, RET_WIDTH), BF16), jax.ShapeDtypeStruct((SWA_WIDTH, t), BF16)]
        + cast_shapes,
        grid=grid,
        in_specs=in_specs + cast_specs,
        out_specs=[blk, pl.BlockSpec((SWA_WIDTH, rows), cur_t)] + cast_specs,
        scratch_shapes=[pltpu.VMEM((RET_HEADS, RET_DK, RET_DV), F32)],
        compiler_params=_params(2),
        name="mixers",
    )(rq, rqx, rk, rkz, rv, rg, dmat, sinks, sq, sk4, sk4, sv4_t, sv4_t, *cast_weights)
    return out[0], out[1], out[2:]


def _outproj_kernel(h_ref, ret_ref, swa_ref, w_ref, g_ref, h2_ref, hn_ref):
    tm = h_ref.shape[0]
    y = _dot(ret_ref[...], w_ref[:RET_WIDTH, :]) + _dot_tn(swa_ref[...], w_ref[RET_WIDTH:, :])
    h2_ref[...] = h_ref[...] + y

    def body(r, carry):
        rows = pl.ds(pl.multiple_of(r * NORM_ROWS, NORM_ROWS), NORM_ROWS)
        hn_ref[rows, :] = _rms_rows(h2_ref[rows, :], g_ref[...]).astype(BF16)
        return carry
    lax.fori_loop(0, tm // NORM_ROWS, body, 0)


def _outproj(h, ret, swa, w_out, gain):
    t, d = h.shape
    row = lambda width: pl.BlockSpec((ROW_TM, width), lambda i: (i, 0))
    return pl.pallas_call(
        _outproj_kernel,
        out_shape=[jax.ShapeDtypeStruct((t, d), F32), jax.ShapeDtypeStruct((t, d), BF16)],
        grid=(t // ROW_TM,),
        in_specs=[row(d), row(RET_WIDTH), pl.BlockSpec((SWA_WIDTH, ROW_TM), lambda i: (0, i)),
                  _resident(w_out.shape), _resident((1, d))],
        out_specs=[row(d), row(d)],
        compiler_params=_params(1),
        name="outproj",
    )(h, ret, swa, w_out, gain)


def _memkv_kernel(m_ref, g_ref, w_ref, o_ref, xn_ref):
    @pl.when(pl.program_id(0) == 0)
    def _():
        xn_ref[...] = _rms_rows(m_ref[...], g_ref[...]).astype(BF16)

    o_ref[...] = _dot(xn_ref[...], w_ref[...].astype(BF16)).astype(BF16)


def _memkv(mem, gain, wkv):
    rows, d = mem.shape
    n_out = wkv.shape[1]
    tn = 1024
    return pl.pallas_call(
        _memkv_kernel,
        out_shape=jax.ShapeDtypeStruct((rows, n_out), BF16),
        grid=(n_out // tn,),
        in_specs=[_resident((rows, d)), _resident((1, d)), pl.BlockSpec((d, tn), lambda j: (0, j))],
        out_specs=pl.BlockSpec((rows, tn), lambda j: (0, j)),
        scratch_shapes=[pltpu.VMEM((rows, d), BF16)],
        compiler_params=_params(1),
        name="memkv",
    )(mem, gain, wkv)


def _xattn_kernel(hn_ref, h_ref, wq_ref, wo_ref, k_ref, v_ref, o_ref, att_ref):
    q = _dot(hn_ref[...], wq_ref[...]).astype(BF16)
    scale = XA_HEAD_DIM ** -0.5
    heads = [slice(hd * XA_HEAD_DIM, (hd + 1) * XA_HEAD_DIM) for hd in range(XA_HEADS)]
    scores = [_dot_nt(q[:, hs], k_ref[:, hs]) * scale for hs in heads]
    probs, invs = [], []
    for s in scores:
        p = jnp.exp(s - jnp.max(s, axis=-1, keepdims=True))
        probs.append(p.astype(BF16))
        invs.append(1.0 / jnp.sum(p, axis=-1, keepdims=True))
    for hs, p, inv in zip(heads, probs, invs):
        att_ref[:, hs] = (_dot(p, v_ref[:, hs]) * inv).astype(BF16)
    o_ref[...] = h_ref[...] + _dot(att_ref[...], wo_ref[...])


def _xattn(hn, h, wq, wo, mkv, seq, mem_len):
    t, d = h.shape
    tiles_per_seq = seq // ROW_TM
    row = lambda i: (i, 0)
    return pl.pallas_call(
        _xattn_kernel,
        out_shape=jax.ShapeDtypeStruct((t, d), F32),
        grid=(t // ROW_TM,),
        in_specs=[
            pl.BlockSpec((ROW_TM, d), row),
            pl.BlockSpec((ROW_TM, d), row),
            _resident((d, d)),
            _resident((d, d)),
            pl.BlockSpec((mem_len, d), lambda i: (i // tiles_per_seq, 0)),
            pl.BlockSpec((mem_len, d), lambda i: (i // tiles_per_seq, 1)),
        ],
        out_specs=pl.BlockSpec((ROW_TM, d), row),
        scratch_shapes=[pltpu.VMEM((ROW_TM, d), BF16)],
        compiler_params=_params(1),
        name="xattn",
    )(hn, h, wq, wo, mkv, mkv)


def kernel(x, mem, ffn1_norm, ffn1_w_gate, ffn1_w_up, ffn1_w_down, mix_norm, w_in, ret_gn_gain, swa_sinks,
           w_out, xa_norm, mem_norm, xa_wq, xa_wkv, xa_wo, ffn2_norm, ffn2_w_gate, ffn2_w_up, ffn2_w_down,
           final_norm):
    batch, seq, d = x.shape
    mem_len = mem.shape[1]
    depth = ffn1_norm.shape[0]
    h = x.reshape(batch * seq, d)
    mem2 = mem.reshape(batch * mem_len, d)
    row = lambda g: g.reshape(1, -1).astype(F32)
    final_gain = row(final_norm)

    for l in range(depth):
        last = l == depth - 1
        h, (w_mix,) = _ffn_f32(h, row(ffn1_norm[l]), ffn1_w_gate[l], ffn1_w_up[l], ffn1_w_down[l], final_gain,
                               final_norm=False, cast_weights=[w_in[l]])
        (dmat, xi_tile, zeta_tile), chunk_decay = _retention_tables()
        rq, rqx, rk, rkz, rv, rg, sq, sk4, sv4_t = _inproj(h, row(mix_norm[l]), w_mix, row(ret_gn_gain[l]),
                                                           xi_tile, zeta_tile, seq)
        ret, swa, (wo_mix, wq, wo) = _mixers(
            rq, rqx, rk, rkz, rv, rg, dmat, chunk_decay, sq, sk4, sv4_t, swa_sinks[l].astype(F32), batch, seq,
            [w_out[l], xa_wq[l], xa_wo[l]])
        h, hn = _outproj(h, ret, swa, wo_mix, row(xa_norm[l]))
        mkv = _memkv(mem2, row(mem_norm[l]), xa_wkv[l])
        h = _xattn(hn, h, wq, wo, mkv, seq, mem_len)
        h, _ = _ffn_f32(h, row(ffn2_norm[l]), ffn2_w_gate[l], ffn2_w_up[l], ffn2_w_down[l], final_gain,
                        final_norm=last)
    if depth == 0:
        raise ValueError("depth must be at least 1")
    return h.reshape(batch, seq, d)
```

```python
import functools

import numpy as np
import jax
import jax.numpy as jnp
from jax import lax
from jax.experimental import pallas as pl
from jax.experimental.pallas import tpu as pltpu

F32 = jnp.float32
BF16 = jnp.bfloat16

D_MODEL = 2048
RET_HEADS = 8
RET_DK = 128
RET_DV = 128
RET_WIDTH = RET_HEADS * RET_DV
RET_CHUNK = 128
SWA_HEADS = 16
SWA_KV_HEADS = 2
SWA_HEAD_DIM = 64
SWA_WIDTH = SWA_HEADS * SWA_HEAD_DIM
SWA_KV_WIDTH = SWA_KV_HEADS * SWA_HEAD_DIM
WINDOW = 128
XA_HEADS = 4
XA_HEAD_DIM = D_MODEL // XA_HEADS
ROPE_THETA = 10000.0
EPS = 1e-6
IN_SIZES = (RET_WIDTH, RET_WIDTH, RET_WIDTH, RET_WIDTH, SWA_WIDTH, SWA_KV_WIDTH, SWA_KV_WIDTH)
IN_COLS = sum(IN_SIZES)
PACKED_WIDTH = 6 * RET_WIDTH + SWA_WIDTH

LANES = 128
BF16_SUBLANES = 16
VMEM_LIMIT_BYTES = 60000 * 1024

FFN_TM = 1024
FFN_TF = 512
FFN_HEAD_TF = 256
ROW_TM = 512
NORM_ROWS = 512
MIXER_PARTS = 2
MIXER_BLOCKS = 4
NEG_INF = float(np.finfo(np.float32).min)
LOG2E = float(np.log2(np.e))


def _params(n_axes):
    return pltpu.CompilerParams(
        dimension_semantics=("arbitrary",) * n_axes,
        vmem_limit_bytes=VMEM_LIMIT_BYTES,
    )


def _resident(shape):
    zeros = (0,) * len(shape)
    return pl.BlockSpec(shape, lambda *_: zeros, pipeline_mode=pl.Buffered(1))


def _rms_rows(x, gain):
    ms = jnp.mean(x * x, axis=-1, keepdims=True)
    return x * lax.rsqrt(ms + EPS) * gain


def _dot(a, b):
    return lax.dot_general(a, b, (((1,), (0,)), ((), ())), preferred_element_type=F32)


def _dot_nt(a, b):
    return lax.dot_general(a, b, (((1,), (1,)), ((), ())), preferred_element_type=F32)


def _dot_tn(a, b):
    return lax.dot_general(a, b, (((0,), (0,)), ((), ())), preferred_element_type=F32)


def _cast_block(shape, n_steps):
    rows, cols = shape
    for col_splits in (1, 2, 4, 8):
        row_blocks, rem = divmod(n_steps, col_splits)
        if rem or rows % row_blocks or cols % col_splits:
            continue
        br, bc = rows // row_blocks, cols // col_splits
        if br % BF16_SUBLANES == 0 and bc % LANES == 0:
            return br, bc, col_splits
    raise ValueError(f"no aligned {n_steps}-way split of {shape}")


def _with_casts(kernel_fn, n_in, n_out, n_cast):
    def wrapped(*refs):
        ins, rest = refs[:n_in], refs[n_in:]
        cast_in, rest = rest[:n_cast], rest[n_cast:]
        outs, rest = rest[:n_out], rest[n_out:]
        cast_out, scratch = rest[:n_cast], rest[n_cast:]
        kernel_fn(*ins, *outs, *scratch)
        for src, dst in zip(cast_in, cast_out):
            dst[...] = src[...].astype(BF16)
    return wrapped


def _cast_specs(weights, grid):
    n_steps = int(np.prod(grid))
    specs, shapes = [], []
    for w in weights:
        br, bc, col_splits = _cast_block(w.shape, n_steps)

        def index_map(*idx, col_splits=col_splits):
            step = idx[0]
            for size, i in zip(grid[1:], idx[1:]):
                step = step * size + i
            return step // col_splits, step % col_splits
        specs.append(pl.BlockSpec((br, bc), index_map))
        shapes.append(jax.ShapeDtypeStruct(w.shape, BF16))
    return specs, shapes


def _cast_specs_tiled(weights, grid):
    n_outer, n_inner = grid
    specs, shapes = [], []
    for w in weights:
        rows, cols = w.shape
        bc = cols // n_outer
        row_blocks = max(r for r in range(1, n_inner + 1) if rows % r == 0 and (rows // r) % BF16_SUBLANES == 0)
        if cols % n_outer or bc % LANES:
            raise ValueError(f"no aligned split of {w.shape} over {grid}")

        def index_map(i, j, row_blocks=row_blocks):
            return jnp.minimum(j, row_blocks - 1), i
        specs.append(pl.BlockSpec((rows // row_blocks, bc), index_map))
        shapes.append(jax.ShapeDtypeStruct(w.shape, BF16))
    return specs, shapes


def _rows_loop(n_rows, fn):
    def body(r, carry):
        fn(pl.ds(pl.multiple_of(r * NORM_ROWS, NORM_ROWS), NORM_ROWS))
        return carry
    lax.fori_loop(0, n_rows // NORM_ROWS, body, 0)


def _ffn_prologue(h_ref, g_ref, o_ref, xn_ref):
    def norm(rows):
        h = h_ref[rows, :]
        xn_ref[rows, :] = _rms_rows(h, g_ref[...]).astype(BF16)
        o_ref[rows, :] = h
    _rows_loop(h_ref.shape[0], norm)


def _ffn_step(xn_ref, wgu, wd, o_ref):
    xn = xn_ref[...]
    acts = []
    for c in range(0, wgu.shape[1], 2 * FFN_HEAD_TF):
        g = _dot(xn, wgu[:, c:c + FFN_HEAD_TF])
        u = _dot(xn, wgu[:, c + FFN_HEAD_TF:c + 2 * FFN_HEAD_TF])
        acts.append((g * (0.5 / (1.0 + jnp.exp(-g))) * u).astype(BF16))
    a = acts[0] if len(acts) == 1 else jnp.concatenate(acts, 1)
    for c in range(0, o_ref.shape[1], FFN_TF):
        o_ref[:, c:c + FFN_TF] += _dot(a, wd[:, c:c + FFN_TF])


def _ffn_kernel(*refs, n_ff, final_norm, aliased):
    if aliased:
        refs = refs[1:]
    h_ref, g_ref, wgu_ref, wd_ref, fg_ref, o_ref, xn_ref = refs
    j = pl.program_id(1)

    @pl.when(j == 0)
    def _():
        _ffn_prologue(h_ref, g_ref, o_ref, xn_ref)

    _ffn_step(xn_ref, wgu_ref[...], wd_ref[...], o_ref)

    if final_norm:
        @pl.when(j == n_ff - 1)
        def _():
            def norm(rows):
                o_ref[rows, :] = _rms_rows(o_ref[rows, :], fg_ref[...])
            _rows_loop(o_ref.shape[0], norm)


def _ffn(h, gain, wgu, wd, final_gain, *, final_norm, first_tile=0, into=None, cast_weights=()):
    t, d = h.shape
    f = wd.shape[0]
    n_ff = f // FFN_TF
    grid = (t // FFN_TM - first_tile, n_ff)
    aliased = into is not None
    tile = lambda i, j: (i + first_tile, 0)
    in_specs = [
        pl.BlockSpec((FFN_TM, d), tile),
        pl.BlockSpec((1, d), lambda i, j: (0, 0)),
        pl.BlockSpec((d, 2 * FFN_TF), lambda i, j: (0, j)),
        pl.BlockSpec((FFN_TF, d), lambda i, j: (j, 0)),
        pl.BlockSpec((1, d), lambda i, j: (0, 0)),
    ]
    args = (h, gain, wgu, wd, final_gain)
    if aliased:
        in_specs = [pl.BlockSpec(memory_space=pl.ANY)] + in_specs
        args = (into,) + args
    cast_specs, cast_shapes = _cast_specs_tiled(cast_weights, grid)
    host = functools.partial(_ffn_kernel, n_ff=n_ff, final_norm=final_norm, aliased=aliased)
    out = pl.pallas_call(
        _with_casts(host, len(in_specs), 1, len(cast_weights)),
        out_shape=[jax.ShapeDtypeStruct((t, d), F32)] + cast_shapes,
        grid=grid,
        in_specs=in_specs + cast_specs,
        out_specs=[pl.BlockSpec((FFN_TM, d), tile)] + cast_specs,
        scratch_shapes=[pltpu.VMEM((FFN_TM, d), BF16)],
        input_output_aliases={0: 0} if aliased else {},
        compiler_params=_params(2),
        name="ffn",
    )(*args, *cast_weights)
    return out[0], out[1:]


def _ffn_head_kernel(h_ref, g_ref, wg_ref, wu_ref, wd_ref, fg_ref, o_ref, wgu_out, wd_out, xn_ref,
                     *, n_ff, final_norm):
    j = pl.program_id(0)

    @pl.when(j == 0)
    def _():
        _ffn_prologue(h_ref, g_ref, o_ref, xn_ref)

    wgu = jnp.concatenate([wg_ref[...].astype(BF16), wu_ref[...].astype(BF16)], 1)
    wd = wd_ref[...].astype(BF16)
    wgu_out[...] = wgu
    wd_out[...] = wd
    _ffn_step(xn_ref, wgu, wd, o_ref)

    if final_norm:
        @pl.when(j == n_ff - 1)
        def _():
            def norm(rows):
                o_ref[rows, :] = _rms_rows(o_ref[rows, :], fg_ref[...])
            _rows_loop(o_ref.shape[0], norm)


def _ffn_head(h, gain, wg, wu, wd, final_gain, *, final_norm):
    t, d = h.shape
    f = wg.shape[1]
    tf = FFN_HEAD_TF
    col = pl.BlockSpec((d, tf), lambda j: (0, j))
    row = pl.BlockSpec((tf, d), lambda j: (j, 0))
    vec = pl.BlockSpec((1, d), lambda j: (0, 0))
    tile0 = pl.BlockSpec((FFN_TM, d), lambda j: (0, 0))
    return pl.pallas_call(
        functools.partial(_ffn_head_kernel, n_ff=f // tf, final_norm=final_norm),
        out_shape=[jax.ShapeDtypeStruct((t, d), F32), jax.ShapeDtypeStruct((d, 2 * f), BF16),
                   jax.ShapeDtypeStruct(wd.shape, BF16)],
        grid=(f // tf,),
        in_specs=[tile0, vec, col, col, row, vec],
        out_specs=[tile0, pl.BlockSpec((d, 2 * tf), lambda j: (0, j)), row],
        scratch_shapes=[pltpu.VMEM((FFN_TM, d), BF16)],
        compiler_params=_params(1),
        name="ffn_head",
    )(h, gain, wg, wu, wd, final_gain)


def _ffn_f32(h, gain, wg, wu, wd, final_gain, *, final_norm, cast_weights=()):
    head, wgu16, wd16 = _ffn_head(h, gain, wg, wu, wd, final_gain, final_norm=final_norm)
    return _ffn(h, gain, wgu16, wd16, final_gain, final_norm=final_norm, first_tile=1, into=head,
                cast_weights=cast_weights)


def _rope_tables(seq):
    pos = np.arange(seq, dtype=np.float32)

    def angles(d):
        inv = np.float32(ROPE_THETA) ** (-np.arange(0, d, 2, dtype=np.float32) / np.float32(d))
        return (pos[:, None] * inv[None, :].astype(np.float32)).astype(np.float32).astype(np.float64)

    a128 = angles(RET_DK)
    cos_r = np.concatenate([np.cos(a128), np.cos(a128)], -1)
    sin_r = np.concatenate([-np.sin(a128), np.sin(a128)], -1)
    a64 = angles(SWA_HEAD_DIM)
    c, s, z = np.cos(a64), np.sin(a64), np.zeros_like(a64)
    cos_s = np.concatenate([c, c, c, c], -1)
    sin_lo = np.concatenate([-s, z, -s, z], -1)
    sin_hi = np.concatenate([z, s, z, s], -1)
    return [jnp.asarray(v, dtype=F32) for v in (cos_r, sin_r, cos_s, sin_lo, sin_hi)]


def _retention_tables():
    c = RET_CHUNK
    heads = np.arange(RET_HEADS, dtype=np.float64)
    log_gamma = np.log1p(-np.exp2(-5.0 - heads))
    idx = np.arange(c, dtype=np.float64)
    diff = idx[:, None] - idx[None, :]
    scale = RET_DK ** -0.5
    dmat = np.where(diff[None] >= 0, np.exp(np.maximum(diff, 0.0)[None] * log_gamma[:, None, None]), 0.0)
    zeta = np.exp((c - 1.0 - idx)[None, :] * log_gamma[:, None])
    xi = np.exp((idx + 1.0)[None, :] * log_gamma[:, None])
    chunk_decay = tuple(float(v) for v in np.exp(c * log_gamma))

    def token_tile(tab):
        return np.tile(np.repeat(tab.T, RET_DK, axis=1), (ROW_TM // c, 1))
    tabs = [jnp.asarray(v, dtype=F32) for v in (dmat * scale, token_tile(xi), token_tile(zeta * scale))]
    return tabs, chunk_decay


def _packed_views(pk_ref):
    return [pk_ref.at[:, i * RET_WIDTH:(i + 1) * RET_WIDTH] for i in range(PACKED_WIDTH // RET_WIDTH)]


def _inproj_kernel(h_ref, g_ref, w_ref, cr_ref, sr_ref, cs_ref, sl_ref, sh_ref, xi_ref, zeta_ref, gn_ref,
                   pk_ref, sk_ref, sv_ref, xn_ref):
    tm = h_ref.shape[0]
    rq_ref, rqx_ref, rk_ref, rkz_ref, rv_ref, rg_ref, sq_ref = _packed_views(pk_ref)

    def norm(rows):
        xn_ref[rows, :] = _rms_rows(h_ref[rows, :], g_ref[...]).astype(BF16)
    _rows_loop(tm, norm)

    xn = xn_ref[...]
    cr, sr = cr_ref[...], sr_ref[...]
    cs, sl, sh = cs_ref[...], sl_ref[...], sh_ref[...]
    half = LANES // 2
    slabs = [slice(s * LANES, (s + 1) * LANES) for s in range(RET_WIDTH // LANES)]

    def rope_ret(x):
        return x * cr + pltpu.roll(x, half, 1) * sr

    def rope_swa(x):
        return x * cs + pltpu.roll(x, LANES - half // 2, 1) * sl + pltpu.roll(x, half // 2, 1) * sh

    def project(col0, width):
        return _dot(xn, w_ref[:, col0:col0 + width])

    def ret_rotary(out_ref, scaled_ref, tab_ref):
        def epilogue(y):
            for sl_ in slabs:
                x = rope_ret(y[:, sl_])
                out_ref[:, sl_] = x.astype(BF16)
                scaled_ref[:, sl_] = (x * tab_ref[:, sl_]).astype(BF16)
        return epilogue

    def ret_values(y):
        for sl_ in slabs:
            rv_ref[:, sl_] = y[:, sl_].astype(BF16)

    def ret_gate(y):
        for sl_ in slabs:
            g = y[:, sl_]
            rg_ref[:, sl_] = (g * (1.0 / (1.0 + jnp.exp(-g))) * gn_ref[:, sl_]).astype(BF16)

    def swa_queries(y):
        scale = SWA_HEAD_DIM ** -0.5 * LOG2E
        for sl_ in slabs:
            sq_ref[:, sl_] = (rope_swa(y[:, sl_]) * scale).astype(BF16)

    def swa_keys_values(y):
        lo = lax.broadcasted_iota(jnp.int32, (tm, LANES), 1) < half
        for transposed, out_ref, x in ((False, sk_ref, rope_swa(y[:, :LANES])), (True, sv_ref, y[:, LANES:])):
            xr = pltpu.roll(x, half, 1)
            variants = (jnp.where(lo, x, 0.0), jnp.where(lo, 0.0, xr),
                        jnp.where(lo, xr, 0.0), jnp.where(lo, 0.0, x))
            for s, v in enumerate(variants):
                if transposed:
                    out_ref[s * LANES:(s + 1) * LANES, :] = v.T.astype(BF16)
                else:
                    out_ref[:, s * LANES:(s + 1) * LANES] = v.astype(BF16)

    epilogues = (ret_rotary(rq_ref, rqx_ref, xi_ref), ret_rotary(rk_ref, rkz_ref, zeta_ref), ret_values, ret_gate,
                 swa_queries, swa_keys_values)
    col0 = np.cumsum((0,) + IN_SIZES[:-2])
    widths = IN_SIZES[:-2] + (2 * SWA_KV_WIDTH,)
    order = (5, 0, 1, 3, 4, 2)
    for r in order:
        epilogues[r](project(int(col0[r]), widths[r]))


def _inproj(h, gain, w_in, gn_gain, xi_tile, zeta_tile, seq):
    t, d = h.shape
    tables = _rope_tables(seq)
    tiles_per_seq = seq // ROW_TM
    tab_spec = pl.BlockSpec((ROW_TM, LANES), lambda i: (i % tiles_per_seq, 0))

    def row_spec(width):
        return pl.BlockSpec((ROW_TM, width), lambda i: (i, 0))

    widths = (PACKED_WIDTH, 4 * LANES)
    return pl.pallas_call(
        _inproj_kernel,
        out_shape=[jax.ShapeDtypeStruct((t, w), BF16) for w in widths]
        + [jax.ShapeDtypeStruct((4 * LANES, t), BF16)],
        grid=(t // ROW_TM,),
        in_specs=[row_spec(d), _resident((1, d)), _resident((d, IN_COLS))] + [tab_spec] * 5
        + [_resident((ROW_TM, RET_WIDTH))] * 2 + [_resident((1, RET_WIDTH))],
        out_specs=[row_spec(w) for w in widths] + [pl.BlockSpec((4 * LANES, ROW_TM), lambda i: (0, i))],
        scratch_shapes=[pltpu.VMEM((ROW_TM, d), BF16)],
        compiler_params=_params(1),
        name="inproj",
    )(h, gain, w_in, *tables, xi_tile, zeta_tile, gn_gain)


def _mixers_kernel(pk_ref, dmat_ref, sink_ref, kp_ref, kc_ref, vp_ref, vc_ref, ret_ref, swa_ref, state_ref,
                   *, chunk_decay):
    n = pl.program_id(1)
    w = WINDOW
    q_ref, qx_ref, k_ref, kz_ref, v_ref, g_ref, sq_ref = _packed_views(pk_ref)

    @pl.when(n == 0)
    def _():
        state_ref[...] = jnp.zeros_like(state_ref)

    pairs = SWA_HEADS // SWA_KV_HEADS // 2
    key = lax.broadcasted_iota(jnp.int32, (2 * w, pairs * w), 0)
    qry = lax.broadcasted_iota(jnp.int32, (2 * w, pairs * w), 1) % w
    band = (key > qry) & (key <= qry + w)
    bias = jnp.where(band, 0.0, NEG_INF)
    bias_first = jnp.where(band & (key >= jnp.where(n == 0, w, 0)), 0.0, NEG_INF)

    hs = [slice(h * RET_DK, (h + 1) * RET_DK) for h in range(RET_HEADS)]
    lane_blk = lambda c: slice(c * LANES, (c + 1) * LANES)
    slabs = {g: [g * pairs + p for p in range(pairs)] for g in range(SWA_KV_HEADS)}

    for blk in range(MIXER_BLOCKS):
        rows = slice(blk * w, (blk + 1) * w)
        mask_bias = bias if blk else bias_first

        def prev_keys(c):
            return kc_ref[(blk - 1) * w:blk * w, lane_blk(c)] if blk else kp_ref[:, lane_blk(c)]

        def prev_values_t(c):
            return vc_ref[lane_blk(c), (blk - 1) * w:blk * w] if blk else vp_ref[lane_blk(c), :]

        for part in range(MIXER_PARTS):
            heads = range(part * RET_HEADS // MIXER_PARTS, (part + 1) * RET_HEADS // MIXER_PARTS)
            groups = range(part * SWA_KV_HEADS // MIXER_PARTS, (part + 1) * SWA_KV_HEADS // MIXER_PARTS)
            phases = [(g, e) for g in groups for e in range(2)]

            s_ret = {h: _dot_nt(q_ref[rows, hs[h]], k_ref[rows, hs[h]]) for h in heads}
            q_swa = {g: jnp.concatenate([sq_ref[rows, lane_blk(sl)] for sl in slabs[g]], 0) for g in groups}
            s_swa = []
            for g, e in phases:
                c = 2 * g + e
                k = jnp.concatenate([prev_keys(c), kc_ref[rows, lane_blk(c)]], 0)
                s_swa.append(_dot_nt(k, q_swa[g]) + mask_bias)

            states = {h: state_ref[h] for h in heads}
            ret = {}
            for h in heads:
                lhs = jnp.concatenate([(s_ret[h] * dmat_ref[h]).astype(BF16), qx_ref[rows, hs[h]]], 1)
                rhs = jnp.concatenate([v_ref[rows, hs[h]], states[h].astype(BF16)], 0)
                ret[h] = _dot(lhs, rhs)
            for h in heads:
                state_ref[h] = states[h] * chunk_decay[h] + _dot_tn(kz_ref[rows, hs[h]], v_ref[rows, hs[h]])

            pv = {}
            for (g, e), s in zip(phases, s_swa):
                c = 2 * g + e
                sink = jnp.concatenate([jnp.full((1, w), sink_ref[2 * sl + e] * LOG2E, F32) for sl in slabs[g]], 1)
                m = jnp.maximum(jnp.max(s, axis=0, keepdims=True), sink)
                p = jnp.exp2(s - m)
                inv = 1.0 / (jnp.sum(p, axis=0, keepdims=True) + jnp.exp2(sink - m))
                v_t = jnp.concatenate([prev_values_t(c), vc_ref[lane_blk(c), rows]], 1)
                pv[g, e] = _dot(v_t, p.astype(BF16)) * inv

            for h in heads:
                mu = jnp.mean(ret[h], axis=-1, keepdims=True)
                cen = ret[h] - mu
                var = jnp.mean(cen * cen, axis=-1, keepdims=True)
                ret_ref[rows, hs[h]] = (cen * lax.rsqrt(var + EPS) * g_ref[rows, hs[h]].astype(F32)).astype(BF16)

            for g in groups:
                acc = pv[g, 0] + pv[g, 1]
                for i, sl in enumerate(slabs[g]):
                    swa_ref[lane_blk(sl), rows] = acc[:, i * w:(i + 1) * w].astype(BF16)


def _mixers(packed, dmat, chunk_decay, sk4, sv4_t, sinks, batch, seq, cast_weights):
    assert RET_CHUNK == WINDOW
    t = packed.shape[0]
    rows = MIXER_BLOCKS * WINDOW
    n_steps = seq // rows
    grid = (batch, n_steps)
    cur = lambda b, n: (b * n_steps + n, 0)
    cur_t = lambda b, n: (0, b * n_steps + n)
    prev_idx = lambda b, n: b * n_steps * MIXER_BLOCKS + jnp.maximum(n * MIXER_BLOCKS - 1, 0)
    blk = pl.BlockSpec((rows, RET_WIDTH), cur)
    in_specs = [
        pl.BlockSpec((rows, PACKED_WIDTH), cur), _resident((RET_HEADS, RET_CHUNK, RET_CHUNK)),
        pl.BlockSpec(memory_space=pltpu.SMEM),
        pl.BlockSpec((WINDOW, 4 * LANES), lambda b, n: (prev_idx(b, n), 0)),
        pl.BlockSpec((rows, 4 * LANES), cur),
        pl.BlockSpec((4 * LANES, WINDOW), lambda b, n: (0, prev_idx(b, n))),
        pl.BlockSpec((4 * LANES, rows), cur_t),
    ]
    cast_specs, cast_shapes = _cast_specs(cast_weights, grid)
    host = functools.partial(_mixers_kernel, chunk_decay=chunk_decay)
    out = pl.pallas_call(
        _with_casts(host, len(in_specs), 2, len(cast_weights)),
        out_shape=[jax.ShapeDtypeStruct((t, RET_WIDTH), BF16), jax.ShapeDtypeStruct((SWA_WIDTH, t), BF16)]
        + cast_shapes,
        grid=grid,
        in_specs=in_specs + cast_specs,
        out_specs=[blk, pl.BlockSpec((SWA_WIDTH, rows), cur_t)] + cast_specs,
        scratch_shapes=[pltpu.VMEM((RET_HEADS, RET_DK, RET_DV), F32)],
        compiler_params=_params(2),
        name="mixers",
    )(packed, dmat, sinks, sk4, sk4, sv4_t, sv4_t, *cast_weights)
    return out[0], out[1], out[2:]


def _outproj_kernel(h_ref, ret_ref, swa_ref, w_ref, g_ref, h2_ref, hn_ref):
    tm = h_ref.shape[0]
    y = _dot(ret_ref[...], w_ref[:RET_WIDTH, :]) + _dot_tn(swa_ref[...], w_ref[RET_WIDTH:, :])
    h2_ref[...] = h_ref[...] + y

    def body(r, carry):
        rows = pl.ds(pl.multiple_of(r * NORM_ROWS, NORM_ROWS), NORM_ROWS)
        hn_ref[rows, :] = _rms_rows(h2_ref[rows, :], g_ref[...]).astype(BF16)
        return carry
    lax.fori_loop(0, tm // NORM_ROWS, body, 0)


def _outproj(h, ret, swa, w_out, gain):
    t, d = h.shape
    row = lambda width: pl.BlockSpec((ROW_TM, width), lambda i: (i, 0))
    return pl.pallas_call(
        _outproj_kernel,
        out_shape=[jax.ShapeDtypeStruct((t, d), F32), jax.ShapeDtypeStruct((t, d), BF16)],
        grid=(t // ROW_TM,),
        in_specs=[row(d), row(RET_WIDTH), pl.BlockSpec((SWA_WIDTH, ROW_TM), lambda i: (0, i)),
                  _resident(w_out.shape), _resident((1, d))],
        out_specs=[row(d), row(d)],
        compiler_params=_params(1),
        name="outproj",
    )(h, ret, swa, w_out, gain)


def _memkv_kernel(m_ref, g_ref, w_ref, o_ref, xn_ref):
    @pl.when(pl.program_id(0) == 0)
    def _():
        xn_ref[...] = _rms_rows(m_ref[...], g_ref[...]).astype(BF16)

    o_ref[...] = _dot(xn_ref[...], w_ref[...].astype(BF16)).astype(BF16)


def _memkv(mem, gain, wkv):
    rows, d = mem.shape
    n_out = wkv.shape[1]
    tn = 1024
    return pl.pallas_call(
        _memkv_kernel,
        out_shape=jax.ShapeDtypeStruct((rows, n_out), BF16),
        grid=(n_out // tn,),
        in_specs=[_resident((rows, d)), _resident((1, d)), pl.BlockSpec((d, tn), lambda j: (0, j))],
        out_specs=pl.BlockSpec((rows, tn), lambda j: (0, j)),
        scratch_shapes=[pltpu.VMEM((rows, d), BF16)],
        compiler_params=_params(1),
        name="memkv",
    )(mem, gain, wkv)


def _xattn_kernel(hn_ref, h_ref, wq_ref, wo_ref, k_ref, v_ref, o_ref, att_ref):
    q = _dot(hn_ref[...], wq_ref[...]).astype(BF16)
    scale = XA_HEAD_DIM ** -0.5
    heads = [slice(hd * XA_HEAD_DIM, (hd + 1) * XA_HEAD_DIM) for hd in range(XA_HEADS)]
    scores = [_dot_nt(q[:, hs], k_ref[:, hs]) * scale for hs in heads]
    probs, invs = [], []
    for s in scores:
        p = jnp.exp(s - jnp.max(s, axis=-1, keepdims=True))
        probs.append(p.astype(BF16))
        invs.append(1.0 / jnp.sum(p, axis=-1, keepdims=True))
    for hs, p, inv in zip(heads, probs, invs):
        att_ref[:, hs] = (_dot(p, v_ref[:, hs]) * inv).astype(BF16)
    o_ref[...] = h_ref[...] + _dot(att_ref[...], wo_ref[...])


def _xattn(hn, h, wq, wo, mkv, seq, mem_len):
    t, d = h.shape
    tiles_per_seq = seq // ROW_TM
    row = lambda i: (i, 0)
    return pl.pallas_call(
        _xattn_kernel,
        out_shape=jax.ShapeDtypeStruct((t, d), F32),
        grid=(t // ROW_TM,),
        in_specs=[
            pl.BlockSpec((ROW_TM, d), row),
            pl.BlockSpec((ROW_TM, d), row),
            _resident((d, d)),
            _resident((d, d)),
            pl.BlockSpec((mem_len, d), lambda i: (i // tiles_per_seq, 0)),
            pl.BlockSpec((mem_len, d), lambda i: (i // tiles_per_seq, 1)),
        ],
        out_specs=pl.BlockSpec((ROW_TM, d), row),
        scratch_shapes=[pltpu.VMEM((ROW_TM, d), BF16)],
        compiler_params=_params(1),
        name="xattn",
    )(hn, h, wq, wo, mkv, mkv)


def kernel(x, mem, ffn1_norm, ffn1_w_gate, ffn1_w_up, ffn1_w_down, mix_norm, w_in, ret_gn_gain, swa_sinks,
           w_out, xa_norm, mem_norm, xa_wq, xa_wkv, xa_wo, ffn2_norm, ffn2_w_gate, ffn2_w_up, ffn2_w_down,
           final_norm):
    batch, seq, d = x.shape
    mem_len = mem.shape[1]
    depth = ffn1_norm.shape[0]
    h = x.reshape(batch * seq, d)
    mem2 = mem.reshape(batch * mem_len, d)
    row = lambda g: g.reshape(1, -1).astype(F32)
    final_gain = row(final_norm)

    for l in range(depth):
        last = l == depth - 1
        h, (w_mix,) = _ffn_f32(h, row(ffn1_norm[l]), ffn1_w_gate[l], ffn1_w_up[l], ffn1_w_down[l], final_gain,
                               final_norm=False, cast_weights=[w_in[l]])
        (dmat, xi_tile, zeta_tile), chunk_decay = _retention_tables()
        packed, sk4, sv4_t = _inproj(h, row(mix_norm[l]), w_mix, row(ret_gn_gain[l]), xi_tile, zeta_tile, seq)
        ret, swa, (wo_mix, wq, wo) = _mixers(packed, dmat, chunk_decay, sk4, sv4_t, swa_sinks[l].astype(F32),
                                             batch, seq, [w_out[l], xa_wq[l], xa_wo[l]])
        h, hn = _outproj(h, ret, swa, wo_mix, row(xa_norm[l]))
        mkv = _memkv(mem2, row(mem_norm[l]), xa_wkv[l])
        h = _xattn(hn, h, wq, wo, mkv, seq, mem_len)
        h, _ = _ffn_f32(h, row(ffn2_norm[l]), ffn2_w_gate[l], ffn2_w_up[l], ffn2_w_down[l], final_gain,
                        final_norm=last)
    if depth == 0:
        raise ValueError("depth must be at least 1")
    return h.reshape(batch, seq, d)
```

```python
import functools

import numpy as np
import jax
import jax.numpy as jnp
from jax import lax
from jax.experimental import pallas as pl
from jax.experimental.pallas import tpu as pltpu

F32 = jnp.float32
BF16 = jnp.bfloat16

D_MODEL = 2048
RET_HEADS = 8
RET_DK = 128
RET_DV = 128
RET_WIDTH = RET_HEADS * RET_DV
RET_CHUNK = 128
SWA_HEADS = 16
SWA_KV_HEADS = 2
SWA_HEAD_DIM = 64
SWA_WIDTH = SWA_HEADS * SWA_HEAD_DIM
SWA_KV_WIDTH = SWA_KV_HEADS * SWA_HEAD_DIM
WINDOW = 128
XA_HEADS = 4
XA_HEAD_DIM = D_MODEL // XA_HEADS
ROPE_THETA = 10000.0
EPS = 1e-6
IN_SIZES = (RET_WIDTH, RET_WIDTH, RET_WIDTH, RET_WIDTH, SWA_WIDTH, SWA_KV_WIDTH, SWA_KV_WIDTH)
IN_COLS = sum(IN_SIZES)

LANES = 128
BF16_SUBLANES = 16
VMEM_LIMIT_BYTES = 60000 * 1024

FFN_TM = 1024
FFN_TF = 512
FFN_HEAD_TF = 256
ROW_TM = 512
NORM_ROWS = 512
MIXER_PARTS = 2
MIXER_BLOCKS = 4
NEG_INF = float(np.finfo(np.float32).min)
LOG2E = float(np.log2(np.e))


def _params(n_axes):
    return pltpu.CompilerParams(
        dimension_semantics=("arbitrary",) * n_axes,
        vmem_limit_bytes=VMEM_LIMIT_BYTES,
    )


def _resident(shape):
    zeros = (0,) * len(shape)
    return pl.BlockSpec(shape, lambda *_: zeros, pipeline_mode=pl.Buffered(1))


def _rms_rows(x, gain):
    ms = jnp.mean(x * x, axis=-1, keepdims=True)
    return x * lax.rsqrt(ms + EPS) * gain


def _dot(a, b):
    return lax.dot_general(a, b, (((1,), (0,)), ((), ())), preferred_element_type=F32)


def _dot_nt(a, b):
    return lax.dot_general(a, b, (((1,), (1,)), ((), ())), preferred_element_type=F32)


def _dot_tn(a, b):
    return lax.dot_general(a, b, (((0,), (0,)), ((), ())), preferred_element_type=F32)


def _cast_block(shape, n_steps):
    rows, cols = shape
    for col_splits in (1, 2, 4, 8):
        row_blocks, rem = divmod(n_steps, col_splits)
        if rem or rows % row_blocks or cols % col_splits:
            continue
        br, bc = rows // row_blocks, cols // col_splits
        if br % BF16_SUBLANES == 0 and bc % LANES == 0:
            return br, bc, col_splits
    raise ValueError(f"no aligned {n_steps}-way split of {shape}")


def _with_casts(kernel_fn, n_in, n_out, n_cast):
    def wrapped(*refs):
        ins, rest = refs[:n_in], refs[n_in:]
        cast_in, rest = rest[:n_cast], rest[n_cast:]
        outs, rest = rest[:n_out], rest[n_out:]
        cast_out, scratch = rest[:n_cast], rest[n_cast:]
        kernel_fn(*ins, *outs, *scratch)
        for src, dst in zip(cast_in, cast_out):
            dst[...] = src[...].astype(BF16)
    return wrapped


def _cast_specs(weights, grid):
    n_steps = int(np.prod(grid))
    specs, shapes = [], []
    for w in weights:
        br, bc, col_splits = _cast_block(w.shape, n_steps)

        def index_map(*idx, col_splits=col_splits):
            step = idx[0]
            for size, i in zip(grid[1:], idx[1:]):
                step = step * size + i
            return step // col_splits, step % col_splits
        specs.append(pl.BlockSpec((br, bc), index_map))
        shapes.append(jax.ShapeDtypeStruct(w.shape, BF16))
    return specs, shapes


def _cast_specs_tiled(weights, grid):
    n_outer, n_inner = grid
    specs, shapes = [], []
    for w in weights:
        rows, cols = w.shape
        bc = cols // n_outer
        row_blocks = max(r for r in range(1, n_inner + 1) if rows % r == 0 and (rows // r) % BF16_SUBLANES == 0)
        if cols % n_outer or bc % LANES:
            raise ValueError(f"no aligned split of {w.shape} over {grid}")

        def index_map(i, j, row_blocks=row_blocks):
            return jnp.minimum(j, row_blocks - 1), i
        specs.append(pl.BlockSpec((rows // row_blocks, bc), index_map))
        shapes.append(jax.ShapeDtypeStruct(w.shape, BF16))
    return specs, shapes


def _rows_loop(n_rows, fn):
    def body(r, carry):
        fn(pl.ds(pl.multiple_of(r * NORM_ROWS, NORM_ROWS), NORM_ROWS))
        return carry
    lax.fori_loop(0, n_rows // NORM_ROWS, body, 0)


def _ffn_prologue(h_ref, g_ref, o_ref, xn_ref):
    def norm(rows):
        h = h_ref[rows, :]
        xn_ref[rows, :] = _rms_rows(h, g_ref[...]).astype(BF16)
        o_ref[rows, :] = h
    _rows_loop(h_ref.shape[0], norm)


def _ffn_step(xn_ref, wg, wu, wd, o_ref):
    xn = xn_ref[...]
    g = _dot(xn, wg)
    u = _dot(xn, wu)
    a = (g * (0.5 / (1.0 + jnp.exp(-g))) * u).astype(BF16)
    for c in range(0, o_ref.shape[1], FFN_TF):
        o_ref[:, c:c + FFN_TF] += _dot(a, wd[:, c:c + FFN_TF])


def _ffn_kernel(*refs, n_ff, final_norm, aliased):
    if aliased:
        refs = refs[1:]
    h_ref, g_ref, wg_ref, wu_ref, wd_ref, fg_ref, o_ref, xn_ref = refs
    j = pl.program_id(1)

    @pl.when(j == 0)
    def _():
        _ffn_prologue(h_ref, g_ref, o_ref, xn_ref)

    _ffn_step(xn_ref, wg_ref[...], wu_ref[...], wd_ref[...], o_ref)

    if final_norm:
        @pl.when(j == n_ff - 1)
        def _():
            def norm(rows):
                o_ref[rows, :] = _rms_rows(o_ref[rows, :], fg_ref[...])
            _rows_loop(o_ref.shape[0], norm)


def _ffn(h, gain, wg, wu, wd, final_gain, *, final_norm, first_tile=0, into=None, cast_weights=()):
    t, d = h.shape
    f = wg.shape[1]
    n_ff = f // FFN_TF
    grid = (t // FFN_TM - first_tile, n_ff)
    aliased = into is not None
    tile = lambda i, j: (i + first_tile, 0)
    in_specs = [
        pl.BlockSpec((FFN_TM, d), tile),
        pl.BlockSpec((1, d), lambda i, j: (0, 0)),
        pl.BlockSpec((d, FFN_TF), lambda i, j: (0, j)),
        pl.BlockSpec((d, FFN_TF), lambda i, j: (0, j)),
        pl.BlockSpec((FFN_TF, d), lambda i, j: (j, 0)),
        pl.BlockSpec((1, d), lambda i, j: (0, 0)),
    ]
    args = (h, gain, wg, wu, wd, final_gain)
    if aliased:
        in_specs = [pl.BlockSpec(memory_space=pl.ANY)] + in_specs
        args = (into,) + args
    cast_specs, cast_shapes = _cast_specs_tiled(cast_weights, grid)
    host = functools.partial(_ffn_kernel, n_ff=n_ff, final_norm=final_norm, aliased=aliased)
    out = pl.pallas_call(
        _with_casts(host, len(in_specs), 1, len(cast_weights)),
        out_shape=[jax.ShapeDtypeStruct((t, d), F32)] + cast_shapes,
        grid=grid,
        in_specs=in_specs + cast_specs,
        out_specs=[pl.BlockSpec((FFN_TM, d), tile)] + cast_specs,
        scratch_shapes=[pltpu.VMEM((FFN_TM, d), BF16)],
        input_output_aliases={0: 0} if aliased else {},
        compiler_params=_params(2),
        name="ffn",
    )(*args, *cast_weights)
    return out[0], out[1:]


def _ffn_head_kernel(h_ref, g_ref, wg_ref, wu_ref, wd_ref, fg_ref, o_ref, wg_out, wu_out, wd_out, xn_ref,
                     *, n_ff, final_norm):
    j = pl.program_id(0)

    @pl.when(j == 0)
    def _():
        _ffn_prologue(h_ref, g_ref, o_ref, xn_ref)

    wg, wu, wd = (w[...].astype(BF16) for w in (wg_ref, wu_ref, wd_ref))
    wg_out[...] = wg
    wu_out[...] = wu
    wd_out[...] = wd
    _ffn_step(xn_ref, wg, wu, wd, o_ref)

    if final_norm:
        @pl.when(j == n_ff - 1)
        def _():
            def norm(rows):
                o_ref[rows, :] = _rms_rows(o_ref[rows, :], fg_ref[...])
            _rows_loop(o_ref.shape[0], norm)


def _ffn_head(h, gain, wg, wu, wd, final_gain, *, final_norm):
    t, d = h.shape
    f = wg.shape[1]
    tf = FFN_HEAD_TF
    col = pl.BlockSpec((d, tf), lambda j: (0, j))
    row = pl.BlockSpec((tf, d), lambda j: (j, 0))
    vec = pl.BlockSpec((1, d), lambda j: (0, 0))
    tile0 = pl.BlockSpec((FFN_TM, d), lambda j: (0, 0))
    return pl.pallas_call(
        functools.partial(_ffn_head_kernel, n_ff=f // tf, final_norm=final_norm),
        out_shape=[jax.ShapeDtypeStruct((t, d), F32), jax.ShapeDtypeStruct(wg.shape, BF16),
                   jax.ShapeDtypeStruct(wu.shape, BF16), jax.ShapeDtypeStruct(wd.shape, BF16)],
        grid=(f // tf,),
        in_specs=[tile0, vec, col, col, row, vec],
        out_specs=[tile0, col, col, row],
        scratch_shapes=[pltpu.VMEM((FFN_TM, d), BF16)],
        compiler_params=_params(1),
        name="ffn_head",
    )(h, gain, wg, wu, wd, final_gain)


def _ffn_f32(h, gain, wg, wu, wd, final_gain, *, final_norm, cast_weights=()):
    head, wg16, wu16, wd16 = _ffn_head(h, gain, wg, wu, wd, final_gain, final_norm=final_norm)
    return _ffn(h, gain, wg16, wu16, wd16, final_gain, final_norm=final_norm, first_tile=1, into=head,
                cast_weights=cast_weights)


def _rope_tables(seq):
    pos = np.arange(seq, dtype=np.float32)

    def angles(d):
        inv = np.float32(ROPE_THETA) ** (-np.arange(0, d, 2, dtype=np.float32) / np.float32(d))
        return (pos[:, None] * inv[None, :].astype(np.float32)).astype(np.float32).astype(np.float64)

    a128 = angles(RET_DK)
    cos_r = np.concatenate([np.cos(a128), np.cos(a128)], -1)
    sin_r = np.concatenate([-np.sin(a128), np.sin(a128)], -1)
    a64 = angles(SWA_HEAD_DIM)
    c, s, z = np.cos(a64), np.sin(a64), np.zeros_like(a64)
    cos_s = np.concatenate([c, c, c, c], -1)
    sin_lo = np.concatenate([-s, z, -s, z], -1)
    sin_hi = np.concatenate([z, s, z, s], -1)
    return [jnp.asarray(v, dtype=F32) for v in (cos_r, sin_r, cos_s, sin_lo, sin_hi)]


def _retention_tables():
    c = RET_CHUNK
    heads = np.arange(RET_HEADS, dtype=np.float64)
    log_gamma = np.log1p(-np.exp2(-5.0 - heads))
    idx = np.arange(c, dtype=np.float64)
    diff = idx[:, None] - idx[None, :]
    scale = RET_DK ** -0.5
    dmat = np.where(diff[None] >= 0, np.exp(np.maximum(diff, 0.0)[None] * log_gamma[:, None, None]), 0.0)
    zeta = np.exp((c - 1.0 - idx)[None, :] * log_gamma[:, None])
    xi = np.exp((idx + 1.0)[None, :] * log_gamma[:, None])
    chunk_decay = tuple(float(v) for v in np.exp(c * log_gamma))

    def token_tile(tab):
        return np.tile(np.repeat(tab.T, RET_DK, axis=1), (ROW_TM // c, 1))
    tabs = [jnp.asarray(v, dtype=F32) for v in (dmat * scale, token_tile(xi), token_tile(zeta * scale))]
    return tabs, chunk_decay


def _inproj_kernel(h_ref, g_ref, w_ref, cr_ref, sr_ref, cs_ref, sl_ref, sh_ref, xi_ref, zeta_ref, gn_ref,
                   rq_ref, rqx_ref, rk_ref, rkz_ref, rv_ref, rg_ref, sq_ref, sk_ref, sv_ref, xn_ref):
    tm = h_ref.shape[0]

    def norm(rows):
        xn_ref[rows, :] = _rms_rows(h_ref[rows, :], g_ref[...]).astype(BF16)
    _rows_loop(tm, norm)

    xn = xn_ref[...]
    cr, sr = cr_ref[...], sr_ref[...]
    cs, sl, sh = cs_ref[...], sl_ref[...], sh_ref[...]
    half = LANES // 2
    slabs = [slice(s * LANES, (s + 1) * LANES) for s in range(RET_WIDTH // LANES)]

    def rope_ret(x):
        return x * cr + pltpu.roll(x, half, 1) * sr

    def rope_swa(x):
        return x * cs + pltpu.roll(x, LANES - half // 2, 1) * sl + pltpu.roll(x, half // 2, 1) * sh

    def project(col0, width):
        return _dot(xn, w_ref[:, col0:col0 + width])

    def ret_rotary(out_ref, scaled_ref, tab_ref):
        def epilogue(y):
            for sl_ in slabs:
                x = rope_ret(y[:, sl_])
                out_ref[:, sl_] = x.astype(BF16)
                scaled_ref[:, sl_] = (x * tab_ref[:, sl_]).astype(BF16)
        return epilogue

    def ret_values(y):
        for sl_ in slabs:
            rv_ref[:, sl_] = y[:, sl_].astype(BF16)

    def ret_gate(y):
        for sl_ in slabs:
            g = y[:, sl_]
            rg_ref[:, sl_] = (g * (1.0 / (1.0 + jnp.exp(-g))) * gn_ref[:, sl_]).astype(BF16)

    def swa_queries(y):
        scale = SWA_HEAD_DIM ** -0.5 * LOG2E
        for sl_ in slabs:
            sq_ref[:, sl_] = (rope_swa(y[:, sl_]) * scale).astype(BF16)

    def swa_keys_values(y):
        lo = lax.broadcasted_iota(jnp.int32, (tm, LANES), 1) < half
        for transposed, out_ref, x in ((False, sk_ref, rope_swa(y[:, :LANES])), (True, sv_ref, y[:, LANES:])):
            xr = pltpu.roll(x, half, 1)
            variants = (jnp.where(lo, x, 0.0), jnp.where(lo, 0.0, xr),
                        jnp.where(lo, xr, 0.0), jnp.where(lo, 0.0, x))
            for s, v in enumerate(variants):
                if transposed:
                    out_ref[s * LANES:(s + 1) * LANES, :] = v.T.astype(BF16)
                else:
                    out_ref[:, s * LANES:(s + 1) * LANES] = v.astype(BF16)

    epilogues = (ret_rotary(rq_ref, rqx_ref, xi_ref), ret_rotary(rk_ref, rkz_ref, zeta_ref), ret_values, ret_gate,
                 swa_queries, swa_keys_values)
    col0 = np.cumsum((0,) + IN_SIZES[:-2])
    widths = IN_SIZES[:-2] + (2 * SWA_KV_WIDTH,)
    order = (5, 0, 1, 3, 4, 2)
    for r in order:
        epilogues[r](project(int(col0[r]), widths[r]))


def _inproj(h, gain, w_in, gn_gain, xi_tile, zeta_tile, seq):
    t, d = h.shape
    tables = _rope_tables(seq)
    tiles_per_seq = seq // ROW_TM
    tab_spec = pl.BlockSpec((ROW_TM, LANES), lambda i: (i % tiles_per_seq, 0))

    def row_spec(width):
        return pl.BlockSpec((ROW_TM, width), lambda i: (i, 0))

    widths = (RET_WIDTH,) * 6 + (SWA_WIDTH, 4 * LANES)
    return pl.pallas_call(
        _inproj_kernel,
        out_shape=[jax.ShapeDtypeStruct((t, w), BF16) for w in widths]
        + [jax.ShapeDtypeStruct((4 * LANES, t), BF16)],
        grid=(t // ROW_TM,),
        in_specs=[row_spec(d), _resident((1, d)), _resident((d, IN_COLS))] + [tab_spec] * 5
        + [_resident((ROW_TM, RET_WIDTH))] * 2 + [_resident((1, RET_WIDTH))],
        out_specs=[row_spec(w) for w in widths] + [pl.BlockSpec((4 * LANES, ROW_TM), lambda i: (0, i))],
        scratch_shapes=[pltpu.VMEM((ROW_TM, d), BF16)],
        compiler_params=_params(1),
        name="inproj",
    )(h, gain, w_in, *tables, xi_tile, zeta_tile, gn_gain)


def _mixers_kernel(q_ref, qx_ref, k_ref, kz_ref, v_ref, g_ref, dmat_ref,
                   sink_ref, sq_ref, kp_ref, kc_ref, vp_ref, vc_ref, ret_ref, swa_ref, state_ref, *, chunk_decay):
    n = pl.program_id(1)
    w = WINDOW

    @pl.when(n == 0)
    def _():
        state_ref[...] = jnp.zeros_like(state_ref)

    pairs = SWA_HEADS // SWA_KV_HEADS // 2
    key = lax.broadcasted_iota(jnp.int32, (2 * w, pairs * w), 0)
    qry = lax.broadcasted_iota(jnp.int32, (2 * w, pairs * w), 1) % w
    band = (key > qry) & (key <= qry + w)
    bias = jnp.where(band, 0.0, NEG_INF)
    bias_first = jnp.where(band & (key >= jnp.where(n == 0, w, 0)), 0.0, NEG_INF)

    hs = [slice(h * RET_DK, (h + 1) * RET_DK) for h in range(RET_HEADS)]
    lane_blk = lambda c: slice(c * LANES, (c + 1) * LANES)
    slabs = {g: [g * pairs + p for p in range(pairs)] for g in range(SWA_KV_HEADS)}

    for blk in range(MIXER_BLOCKS):
        rows = slice(blk * w, (blk + 1) * w)
        mask_bias = bias if blk else bias_first

        def prev_keys(c):
            return kc_ref[(blk - 1) * w:blk * w, lane_blk(c)] if blk else kp_ref[:, lane_blk(c)]

        def prev_values_t(c):
            return vc_ref[lane_blk(c), (blk - 1) * w:blk * w] if blk else vp_ref[lane_blk(c), :]

        for part in range(MIXER_PARTS):
            heads = range(part * RET_HEADS // MIXER_PARTS, (part + 1) * RET_HEADS // MIXER_PARTS)
            groups = range(part * SWA_KV_HEADS // MIXER_PARTS, (part + 1) * SWA_KV_HEADS // MIXER_PARTS)
            phases = [(g, e) for g in groups for e in range(2)]

            s_ret = {h: _dot_nt(q_ref[rows, hs[h]], k_ref[rows, hs[h]]) for h in heads}
            q_swa = {g: jnp.concatenate([sq_ref[rows, lane_blk(sl)] for sl in slabs[g]], 0) for g in groups}
            s_swa = []
            for g, e in phases:
                c = 2 * g + e
                k = jnp.concatenate([prev_keys(c), kc_ref[rows, lane_blk(c)]], 0)
                s_swa.append(_dot_nt(k, q_swa[g]) + mask_bias)

            states = {h: state_ref[h] for h in heads}
            ret = {}
            for h in heads:
                lhs = jnp.concatenate([(s_ret[h] * dmat_ref[h]).astype(BF16), qx_ref[rows, hs[h]]], 1)
                rhs = jnp.concatenate([v_ref[rows, hs[h]], states[h].astype(BF16)], 0)
                ret[h] = _dot(lhs, rhs)
            for h in heads:
                state_ref[h] = states[h] * chunk_decay[h] + _dot_tn(kz_ref[rows, hs[h]], v_ref[rows, hs[h]])

            pv = {}
            for (g, e), s in zip(phases, s_swa):
                c = 2 * g + e
                sink = jnp.concatenate([jnp.full((1, w), sink_ref[2 * sl + e] * LOG2E, F32) for sl in slabs[g]], 1)
                m = jnp.maximum(jnp.max(s, axis=0, keepdims=True), sink)
                p = jnp.exp2(s - m)
                inv = 1.0 / (jnp.sum(p, axis=0, keepdims=True) + jnp.exp2(sink - m))
                v_t = jnp.concatenate([prev_values_t(c), vc_ref[lane_blk(c), rows]], 1)
                pv[g, e] = _dot(v_t, p.astype(BF16)) * inv

            for h in heads:
                mu = jnp.mean(ret[h], axis=-1, keepdims=True)
                cen = ret[h] - mu
                var = jnp.mean(cen * cen, axis=-1, keepdims=True)
                ret_ref[rows, hs[h]] = (cen * lax.rsqrt(var + EPS) * g_ref[rows, hs[h]].astype(F32)).astype(BF16)

            for g in groups:
                acc = pv[g, 0] + pv[g, 1]
                for i, sl in enumerate(slabs[g]):
                    swa_ref[lane_blk(sl), rows] = acc[:, i * w:(i + 1) * w].astype(BF16)


def _mixers(rq, rqx, rk, rkz, rv, rg, dmat, chunk_decay, sq, sk4, sv4_t, sinks, batch, seq, cast_weights):
    assert RET_CHUNK == WINDOW
    t = rq.shape[0]
    rows = MIXER_BLOCKS * WINDOW
    n_steps = seq // rows
    grid = (batch, n_steps)
    cur = lambda b, n: (b * n_steps + n, 0)
    cur_t = lambda b, n: (0, b * n_steps + n)
    prev_idx = lambda b, n: b * n_steps * MIXER_BLOCKS + jnp.maximum(n * MIXER_BLOCKS - 1, 0)
    blk = pl.BlockSpec((rows, RET_WIDTH), cur)
    in_specs = [
        blk, blk, blk, blk, blk, blk, _resident((RET_HEADS, RET_CHUNK, RET_CHUNK)),
        pl.BlockSpec(memory_space=pltpu.SMEM),
        pl.BlockSpec((rows, SWA_WIDTH), cur),
        pl.BlockSpec((WINDOW, 4 * LANES), lambda b, n: (prev_idx(b, n), 0)),
        pl.BlockSpec((rows, 4 * LANES), cur),
        pl.BlockSpec((4 * LANES, WINDOW), lambda b, n: (0, prev_idx(b, n))),
        pl.BlockSpec((4 * LANES, rows), cur_t),
    ]
    cast_specs, cast_shapes = _cast_specs(cast_weights, grid)
    host = functools.partial(_mixers_kernel, chunk_decay=chunk_decay)
    out = pl.pallas_call(
        _with_casts(host, len(in_specs), 2, len(cast_weights)),
        out_shape=[jax.ShapeDtypeStruct((t, RET_WIDTH), BF16), jax.ShapeDtypeStruct((SWA_WIDTH, t), BF16)]
        + cast_shapes,
        grid=grid,
        in_specs=in_specs + cast_specs,
        out_specs=[blk, pl.BlockSpec((SWA_WIDTH, rows), cur_t)] + cast_specs,
        scratch_shapes=[pltpu.VMEM((RET_HEADS, RET_DK, RET_DV), F32)],
        compiler_params=_params(2),
        name="mixers",
    )(rq, rqx, rk, rkz, rv, rg, dmat, sinks, sq, sk4, sk4, sv4_t, sv4_t, *cast_weights)
    return out[0], out[1], out[2:]


def _outproj_kernel(h_ref, ret_ref, swa_ref, w_ref, g_ref, mem_ref, mem_gain_ref, wkv_ref,
                    h2_ref, hn_ref, mkv_ref, memn_ref):
    tm = h_ref.shape[0]

    @pl.when(pl.program_id(0) == 0)
    def _():
        memn_ref[...] = _rms_rows(mem_ref[...], mem_gain_ref[...]).astype(BF16)

    y = _dot(ret_ref[...], w_ref[:RET_WIDTH, :]) + _dot_tn(swa_ref[...], w_ref[RET_WIDTH:, :])
    h2_ref[...] = h_ref[...] + y
    mkv_ref[...] = _dot(memn_ref[...], wkv_ref[...].astype(BF16)).astype(BF16)

    def body(r, carry):
        rows = pl.ds(pl.multiple_of(r * NORM_ROWS, NORM_ROWS), NORM_ROWS)
        hn_ref[rows, :] = _rms_rows(h2_ref[rows, :], g_ref[...]).astype(BF16)
        return carry
    lax.fori_loop(0, tm // NORM_ROWS, body, 0)


def _outproj(h, ret, swa, w_out, gain, mem, mem_gain, wkv):
    t, d = h.shape
    n_steps = t // ROW_TM
    kv_cols = wkv.shape[1] // n_steps
    row = lambda width: pl.BlockSpec((ROW_TM, width), lambda i: (i, 0))
    return pl.pallas_call(
        _outproj_kernel,
        out_shape=[jax.ShapeDtypeStruct((t, d), F32), jax.ShapeDtypeStruct((t, d), BF16),
                   jax.ShapeDtypeStruct((mem.shape[0], wkv.shape[1]), BF16)],
        grid=(n_steps,),
        in_specs=[row(d), row(RET_WIDTH), pl.BlockSpec((SWA_WIDTH, ROW_TM), lambda i: (0, i)),
                  _resident(w_out.shape), _resident((1, d)),
                  _resident(mem.shape), _resident(mem_gain.shape),
                  pl.BlockSpec((wkv.shape[0], kv_cols), lambda i: (0, i))],
        out_specs=[row(d), row(d), pl.BlockSpec((mem.shape[0], kv_cols), lambda i: (0, i))],
        scratch_shapes=[pltpu.VMEM(mem.shape, BF16)],
        compiler_params=_params(1),
        name="outproj",
    )(h, ret, swa, w_out, gain, mem, mem_gain, wkv)


def _xattn_kernel(hn_ref, h_ref, wq_ref, wo_ref, k_ref, v_ref, o_ref):
    q = _dot(hn_ref[...], wq_ref[...]).astype(BF16)
    scale = XA_HEAD_DIM ** -0.5
    heads = [slice(hd * XA_HEAD_DIM, (hd + 1) * XA_HEAD_DIM) for hd in range(XA_HEADS)]
    scores = [_dot_nt(q[:, hs], k_ref[:, hs]) * scale for hs in heads]
    probs, invs = [], []
    for s in scores:
        p = jnp.exp(s - jnp.max(s, axis=-1, keepdims=True))
        probs.append(p.astype(BF16))
        invs.append(1.0 / jnp.sum(p, axis=-1, keepdims=True))
    out = h_ref[...]
    for hs, p, inv in zip(heads, probs, invs):
        att = (_dot(p, v_ref[:, hs]) * inv).astype(BF16)
        out = out + _dot(att, wo_ref[hs, :])
    o_ref[...] = out


def _xattn(hn, h, wq, wo, mkv, seq, mem_len):
    t, d = h.shape
    tiles_per_seq = seq // ROW_TM
    row = lambda i: (i, 0)
    return pl.pallas_call(
        _xattn_kernel,
        out_shape=jax.ShapeDtypeStruct((t, d), F32),
        grid=(t // ROW_TM,),
        in_specs=[
            pl.BlockSpec((ROW_TM, d), row),
            pl.BlockSpec((ROW_TM, d), row),
            _resident((d, d)),
            _resident((d, d)),
            pl.BlockSpec((mem_len, d), lambda i: (i // tiles_per_seq, 0)),
            pl.BlockSpec((mem_len, d), lambda i: (i // tiles_per_seq, 1)),
        ],
        out_specs=pl.BlockSpec((ROW_TM, d), row),
        compiler_params=_params(1),
        name="xattn",
    )(hn, h, wq, wo, mkv, mkv)


def kernel(x, mem, ffn1_norm, ffn1_w_gate, ffn1_w_up, ffn1_w_down, mix_norm, w_in, ret_gn_gain, swa_sinks,
           w_out, xa_norm, mem_norm, xa_wq, xa_wkv, xa_wo, ffn2_norm, ffn2_w_gate, ffn2_w_up, ffn2_w_down,
           final_norm):
    batch, seq, d = x.shape
    mem_len = mem.shape[1]
    depth = ffn1_norm.shape[0]
    h = x.reshape(batch * seq, d)
    mem2 = mem.reshape(batch * mem_len, d)
    row = lambda g: g.reshape(1, -1).astype(F32)
    final_gain = row(final_norm)

    for l in range(depth):
        last = l == depth - 1
        h, (w_mix,) = _ffn_f32(h, row(ffn1_norm[l]), ffn1_w_gate[l], ffn1_w_up[l], ffn1_w_down[l], final_gain,
                               final_norm=False, cast_weights=[w_in[l]])
        (dmat, xi_tile, zeta_tile), chunk_decay = _retention_tables()
        rq, rqx, rk, rkz, rv, rg, sq, sk4, sv4_t = _inproj(h, row(mix_norm[l]), w_mix, row(ret_gn_gain[l]),
                                                           xi_tile, zeta_tile, seq)
        ret, swa, (wo_mix, wq, wo) = _mixers(
            rq, rqx, rk, rkz, rv, rg, dmat, chunk_decay, sq, sk4, sv4_t, swa_sinks[l].astype(F32), batch, seq,
            [w_out[l], xa_wq[l], xa_wo[l]])
        h, hn, mkv = _outproj(h, ret, swa, wo_mix, row(xa_norm[l]), mem2, row(mem_norm[l]), xa_wkv[l])
        h = _xattn(hn, h, wq, wo, mkv, seq, mem_len)
        h, _ = _ffn_f32(h, row(ffn2_norm[l]), ffn2_w_gate[l], ffn2_w_up[l], ffn2_w_down[l], final_gain,
                        final_norm=last)
    if depth == 0:
        raise ValueError("depth must be at least 1")
    return h.reshape(batch, seq, d)
```

```python
import functools

import numpy as np
import jax
import jax.numpy as jnp
from jax import lax
from jax.experimental import pallas as pl
from jax.experimental.pallas import tpu as pltpu

F32 = jnp.float32
BF16 = jnp.bfloat16

D_MODEL = 2048
RET_HEADS = 8
RET_DK = 128
RET_DV = 128
RET_WIDTH = RET_HEADS * RET_DV
RET_CHUNK = 128
SWA_HEADS = 16
SWA_KV_HEADS = 2
SWA_HEAD_DIM = 64
SWA_WIDTH = SWA_HEADS * SWA_HEAD_DIM
SWA_KV_WIDTH = SWA_KV_HEADS * SWA_HEAD_DIM
WINDOW = 128
XA_HEADS = 4
XA_HEAD_DIM = D_MODEL // XA_HEADS
ROPE_THETA = 10000.0
EPS = 1e-6
IN_SIZES = (RET_WIDTH, RET_WIDTH, RET_WIDTH, RET_WIDTH, SWA_WIDTH, SWA_KV_WIDTH, SWA_KV_WIDTH)
IN_COLS = sum(IN_SIZES)

LANES = 128
BF16_SUBLANES = 16
VMEM_LIMIT_BYTES = 60000 * 1024

FFN_TM = 1024
FFN_TF = 512
FFN_HEAD_TF = 256
ROW_TM = 512
NORM_ROWS = 512
MIXER_PARTS = 2
MIXER_BLOCKS = 4
NEG_INF = float(np.finfo(np.float32).min)
LOG2E = float(np.log2(np.e))


def _params(n_axes):
    return pltpu.CompilerParams(
        dimension_semantics=("arbitrary",) * n_axes,
        vmem_limit_bytes=VMEM_LIMIT_BYTES,
    )


def _resident(shape):
    zeros = (0,) * len(shape)
    return pl.BlockSpec(shape, lambda *_: zeros, pipeline_mode=pl.Buffered(1))


def _rms_rows(x, gain):
    ms = jnp.mean(x * x, axis=-1, keepdims=True)
    return x * lax.rsqrt(ms + EPS) * gain


def _dot(a, b):
    return lax.dot_general(a, b, (((1,), (0,)), ((), ())), preferred_element_type=F32)


def _dot_nt(a, b):
    return lax.dot_general(a, b, (((1,), (1,)), ((), ())), preferred_element_type=F32)


def _dot_tn(a, b):
    return lax.dot_general(a, b, (((0,), (0,)), ((), ())), preferred_element_type=F32)


def _cast_block(shape, n_steps):
    rows, cols = shape
    for col_splits in (1, 2, 4, 8):
        row_blocks, rem = divmod(n_steps, col_splits)
        if rem or rows % row_blocks or cols % col_splits:
            continue
        br, bc = rows // row_blocks, cols // col_splits
        if br % BF16_SUBLANES == 0 and bc % LANES == 0:
            return br, bc, col_splits
    raise ValueError(f"no aligned {n_steps}-way split of {shape}")


def _with_casts(kernel_fn, n_in, n_out, n_cast):
    def wrapped(*refs):
        ins, rest = refs[:n_in], refs[n_in:]
        cast_in, rest = rest[:n_cast], rest[n_cast:]
        outs, rest = rest[:n_out], rest[n_out:]
        cast_out, scratch = rest[:n_cast], rest[n_cast:]
        kernel_fn(*ins, *outs, *scratch)
        for src, dst in zip(cast_in, cast_out):
            dst[...] = src[...].astype(BF16)
    return wrapped


def _cast_specs(weights, grid):
    n_steps = int(np.prod(grid))
    specs, shapes = [], []
    for w in weights:
        br, bc, col_splits = _cast_block(w.shape, n_steps)

        def index_map(*idx, col_splits=col_splits):
            step = idx[0]
            for size, i in zip(grid[1:], idx[1:]):
                step = step * size + i
            return step // col_splits, step % col_splits
        specs.append(pl.BlockSpec((br, bc), index_map))
        shapes.append(jax.ShapeDtypeStruct(w.shape, BF16))
    return specs, shapes


def _cast_specs_tiled(weights, grid):
    n_outer, n_inner = grid
    specs, shapes = [], []
    for w in weights:
        rows, cols = w.shape
        bc = cols // n_outer
        row_blocks = max(r for r in range(1, n_inner + 1) if rows % r == 0 and (rows // r) % BF16_SUBLANES == 0)
        if cols % n_outer or bc % LANES:
            raise ValueError(f"no aligned split of {w.shape} over {grid}")

        def index_map(i, j, row_blocks=row_blocks):
            return jnp.minimum(j, row_blocks - 1), i
        specs.append(pl.BlockSpec((rows // row_blocks, bc), index_map))
        shapes.append(jax.ShapeDtypeStruct(w.shape, BF16))
    return specs, shapes


def _rows_loop(n_rows, fn):
    def body(r, carry):
        fn(pl.ds(pl.multiple_of(r * NORM_ROWS, NORM_ROWS), NORM_ROWS))
        return carry
    lax.fori_loop(0, n_rows // NORM_ROWS, body, 0)


def _ffn_prologue(h_ref, g_ref, o_ref, xn_ref):
    def norm(rows):
        h = h_ref[rows, :]
        xn_ref[rows, :] = _rms_rows(h, g_ref[...]).astype(BF16)
        o_ref[rows, :] = h
    _rows_loop(h_ref.shape[0], norm)


def _ffn_step(xn_ref, wg, wu, wd, o_ref):
    xn = xn_ref[...]
    g = _dot(xn, wg)
    u = _dot(xn, wu)
    a = (g * (0.5 / (1.0 + jnp.exp(-g))) * u).astype(BF16)
    for c in range(0, o_ref.shape[1], FFN_TF):
        o_ref[:, c:c + FFN_TF] += _dot(a, wd[:, c:c + FFN_TF])


def _ffn_kernel(*refs, n_ff, final_norm, aliased):
    if aliased:
        refs = refs[1:]
    h_ref, g_ref, wg_ref, wu_ref, wd_ref, fg_ref, o_ref, xn_ref = refs
    j = pl.program_id(1)

    @pl.when(j == 0)
    def _():
        _ffn_prologue(h_ref, g_ref, o_ref, xn_ref)

    _ffn_step(xn_ref, wg_ref[...], wu_ref[...], wd_ref[...], o_ref)

    if final_norm:
        @pl.when(j == n_ff - 1)
        def _():
            def norm(rows):
                o_ref[rows, :] = _rms_rows(o_ref[rows, :], fg_ref[...])
            _rows_loop(o_ref.shape[0], norm)


def _ffn(h, gain, wg, wu, wd, final_gain, *, final_norm, first_tile=0, into=None, cast_weights=()):
    t, d = h.shape
    f = wg.shape[1]
    n_ff = f // FFN_TF
    grid = (t // FFN_TM - first_tile, n_ff)
    aliased = into is not None
    tile = lambda i, j: (i + first_tile, 0)
    in_specs = [
        pl.BlockSpec((FFN_TM, d), tile),
        pl.BlockSpec((1, d), lambda i, j: (0, 0)),
        pl.BlockSpec((d, FFN_TF), lambda i, j: (0, j)),
        pl.BlockSpec((d, FFN_TF), lambda i, j: (0, j)),
        pl.BlockSpec((FFN_TF, d), lambda i, j: (j, 0)),
        pl.BlockSpec((1, d), lambda i, j: (0, 0)),
    ]
    args = (h, gain, wg, wu, wd, final_gain)
    if aliased:
        in_specs = [pl.BlockSpec(memory_space=pl.ANY)] + in_specs
        args = (into,) + args
    cast_specs, cast_shapes = _cast_specs_tiled(cast_weights, grid)
    host = functools.partial(_ffn_kernel, n_ff=n_ff, final_norm=final_norm, aliased=aliased)
    out = pl.pallas_call(
        _with_casts(host, len(in_specs), 1, len(cast_weights)),
        out_shape=[jax.ShapeDtypeStruct((t, d), F32)] + cast_shapes,
        grid=grid,
        in_specs=in_specs + cast_specs,
        out_specs=[pl.BlockSpec((FFN_TM, d), tile)] + cast_specs,
        scratch_shapes=[pltpu.VMEM((FFN_TM, d), BF16)],
        input_output_aliases={0: 0} if aliased else {},
        compiler_params=_params(2),
        name="ffn",
    )(*args, *cast_weights)
    return out[0], out[1:]


def _ffn_head_kernel(*refs, n_ff, final_norm, cast_down):
    if cast_down:
        h_ref, g_ref, wg_ref, wu_ref, wd_ref, fg_ref, o_ref, wg_out, wu_out, wd_out, xn_ref = refs
    else:
        h_ref, g_ref, wg_ref, wu_ref, wd_ref, fg_ref, o_ref, wg_out, wu_out, xn_ref = refs
    j = pl.program_id(0)

    @pl.when(j == 0)
    def _():
        _ffn_prologue(h_ref, g_ref, o_ref, xn_ref)

    wg, wu = wg_ref[...].astype(BF16), wu_ref[...].astype(BF16)
    wg_out[...] = wg
    wu_out[...] = wu
    if cast_down:
        wd = wd_ref[...].astype(BF16)
        wd_out[...] = wd
    else:
        wd = wd_ref[...]
    _ffn_step(xn_ref, wg, wu, wd, o_ref)

    if final_norm:
        @pl.when(j == n_ff - 1)
        def _():
            def norm(rows):
                o_ref[rows, :] = _rms_rows(o_ref[rows, :], fg_ref[...])
            _rows_loop(o_ref.shape[0], norm)


def _ffn_head(h, gain, wg, wu, wd, final_gain, *, final_norm):
    t, d = h.shape
    f = wg.shape[1]
    tf = FFN_HEAD_TF
    cast_down = wd.dtype != BF16
    col = pl.BlockSpec((d, tf), lambda j: (0, j))
    row = pl.BlockSpec((tf, d), lambda j: (j, 0))
    vec = pl.BlockSpec((1, d), lambda j: (0, 0))
    tile0 = pl.BlockSpec((FFN_TM, d), lambda j: (0, 0))
    out = pl.pallas_call(
        functools.partial(_ffn_head_kernel, n_ff=f // tf, final_norm=final_norm, cast_down=cast_down),
        out_shape=[jax.ShapeDtypeStruct((t, d), F32), jax.ShapeDtypeStruct(wg.shape, BF16),
                   jax.ShapeDtypeStruct(wu.shape, BF16)] + [jax.ShapeDtypeStruct(wd.shape, BF16)] * cast_down,
        grid=(f // tf,),
        in_specs=[tile0, vec, col, col, row, vec],
        out_specs=[tile0, col, col] + [row] * cast_down,
        scratch_shapes=[pltpu.VMEM((FFN_TM, d), BF16)],
        compiler_params=_params(1),
        name="ffn_head",
    )(h, gain, wg, wu, wd, final_gain)
    return (*out, wd)[:4]


def _ffn_f32(h, gain, wg, wu, wd, final_gain, *, final_norm, cast_weights=()):
    head, wg16, wu16, wd16 = _ffn_head(h, gain, wg, wu, wd, final_gain, final_norm=final_norm)
    return _ffn(h, gain, wg16, wu16, wd16, final_gain, final_norm=final_norm, first_tile=1, into=head,
                cast_weights=cast_weights)


def _rope_tables(seq):
    pos = np.arange(seq, dtype=np.float32)

    def angles(d):
        inv = np.float32(ROPE_THETA) ** (-np.arange(0, d, 2, dtype=np.float32) / np.float32(d))
        return (pos[:, None] * inv[None, :].astype(np.float32)).astype(np.float32).astype(np.float64)

    a128 = angles(RET_DK)
    cos_r = np.concatenate([np.cos(a128), np.cos(a128)], -1)
    sin_r = np.concatenate([-np.sin(a128), np.sin(a128)], -1)
    a64 = angles(SWA_HEAD_DIM)
    c, s, z = np.cos(a64), np.sin(a64), np.zeros_like(a64)
    cos_s = np.concatenate([c, c, c, c], -1)
    sin_lo = np.concatenate([-s, z, -s, z], -1)
    sin_hi = np.concatenate([z, s, z, s], -1)
    return [jnp.asarray(v, dtype=F32) for v in (cos_r, sin_r, cos_s, sin_lo, sin_hi)]


def _retention_tables():
    c = RET_CHUNK
    heads = np.arange(RET_HEADS, dtype=np.float64)
    log_gamma = np.log1p(-np.exp2(-5.0 - heads))
    idx = np.arange(c, dtype=np.float64)
    diff = idx[:, None] - idx[None, :]
    scale = RET_DK ** -0.5
    dmat = np.where(diff[None] >= 0, np.exp(np.maximum(diff, 0.0)[None] * log_gamma[:, None, None]), 0.0)
    zeta = np.exp((c - 1.0 - idx)[None, :] * log_gamma[:, None])
    xi = np.exp((idx + 1.0)[None, :] * log_gamma[:, None])
    chunk_decay = tuple(float(v) for v in np.exp(c * log_gamma))

    def token_tile(tab):
        return np.tile(np.repeat(tab.T, RET_DK, axis=1), (ROW_TM // c, 1))
    tabs = [jnp.asarray(v, dtype=F32) for v in (dmat * scale, token_tile(xi), token_tile(zeta * scale))]
    return tabs, chunk_decay


def _inproj_kernel(h_ref, g_ref, w_ref, cr_ref, sr_ref, cs_ref, sl_ref, sh_ref, xi_ref, zeta_ref, gn_ref,
                   rq_ref, rqx_ref, rk_ref, rkz_ref, rv_ref, rg_ref, sq_ref, sk_ref, sv_ref, xn_ref):
    tm = h_ref.shape[0]

    def norm(rows):
        xn_ref[rows, :] = _rms_rows(h_ref[rows, :], g_ref[...]).astype(BF16)
    _rows_loop(tm, norm)

    xn = xn_ref[...]
    cr, sr = cr_ref[...], sr_ref[...]
    cs, sl, sh = cs_ref[...], sl_ref[...], sh_ref[...]
    half = LANES // 2
    slabs = [slice(s * LANES, (s + 1) * LANES) for s in range(RET_WIDTH // LANES)]

    def rope_ret(x):
        return x * cr + pltpu.roll(x, half, 1) * sr

    def rope_swa(x):
        return x * cs + pltpu.roll(x, LANES - half // 2, 1) * sl + pltpu.roll(x, half // 2, 1) * sh

    def project(col0, width):
        return _dot(xn, w_ref[:, col0:col0 + width])

    def ret_rotary(out_ref, scaled_ref, tab_ref):
        def epilogue(y):
            for sl_ in slabs:
                x = rope_ret(y[:, sl_])
                out_ref[:, sl_] = x.astype(BF16)
                scaled_ref[:, sl_] = (x * tab_ref[:, sl_]).astype(BF16)
        return epilogue

    def ret_values(y):
        for sl_ in slabs:
            rv_ref[:, sl_] = y[:, sl_].astype(BF16)

    def ret_gate(y):
        for sl_ in slabs:
            g = y[:, sl_]
            rg_ref[:, sl_] = (g * (1.0 / (1.0 + jnp.exp(-g))) * gn_ref[:, sl_]).astype(BF16)

    def swa_queries(y):
        scale = SWA_HEAD_DIM ** -0.5 * LOG2E
        for sl_ in slabs:
            sq_ref[:, sl_] = (rope_swa(y[:, sl_]) * scale).astype(BF16)

    def swa_keys_values(y):
        lo = lax.broadcasted_iota(jnp.int32, (tm, LANES), 1) < half
        for transposed, out_ref, x in ((False, sk_ref, rope_swa(y[:, :LANES])), (True, sv_ref, y[:, LANES:])):
            xr = pltpu.roll(x, half, 1)
            variants = (jnp.where(lo, x, 0.0), jnp.where(lo, 0.0, xr),
                        jnp.where(lo, xr, 0.0), jnp.where(lo, 0.0, x))
            for s, v in enumerate(variants):
                if transposed:
                    out_ref[s * LANES:(s + 1) * LANES, :] = v.T.astype(BF16)
                else:
                    out_ref[:, s * LANES:(s + 1) * LANES] = v.astype(BF16)

    epilogues = (ret_rotary(rq_ref, rqx_ref, xi_ref), ret_rotary(rk_ref, rkz_ref, zeta_ref), ret_values, ret_gate,
                 swa_queries, swa_keys_values)
    col0 = np.cumsum((0,) + IN_SIZES[:-2])
    widths = IN_SIZES[:-2] + (2 * SWA_KV_WIDTH,)
    order = (5, 0, 1, 3, 4, 2)
    for r in order:
        epilogues[r](project(int(col0[r]), widths[r]))


def _inproj(h, gain, w_in, gn_gain, xi_tile, zeta_tile, seq):
    t, d = h.shape
    tables = _rope_tables(seq)
    tiles_per_seq = seq // ROW_TM
    tab_spec = pl.BlockSpec((ROW_TM, LANES), lambda i: (i % tiles_per_seq, 0))

    def row_spec(width):
        return pl.BlockSpec((ROW_TM, width), lambda i: (i, 0))

    widths = (RET_WIDTH,) * 6 + (SWA_WIDTH, 4 * LANES)
    return pl.pallas_call(
        _inproj_kernel,
        out_shape=[jax.ShapeDtypeStruct((t, w), BF16) for w in widths]
        + [jax.ShapeDtypeStruct((4 * LANES, t), BF16)],
        grid=(t // ROW_TM,),
        in_specs=[row_spec(d), _resident((1, d)), _resident((d, IN_COLS))] + [tab_spec] * 5
        + [_resident((ROW_TM, RET_WIDTH))] * 2 + [_resident((1, RET_WIDTH))],
        out_specs=[row_spec(w) for w in widths] + [pl.BlockSpec((4 * LANES, ROW_TM), lambda i: (0, i))],
        scratch_shapes=[pltpu.VMEM((ROW_TM, d), BF16)],
        compiler_params=_params(1),
        name="inproj",
    )(h, gain, w_in, *tables, xi_tile, zeta_tile, gn_gain)


def _mixers_kernel(q_ref, qx_ref, k_ref, kz_ref, v_ref, g_ref, dmat_ref,
                   sink_ref, sq_ref, kp_ref, kc_ref, vp_ref, vc_ref, ret_ref, swa_ref, state_ref, *, chunk_decay):
    n = pl.program_id(1)
    w = WINDOW

    @pl.when(n == 0)
    def _():
        state_ref[...] = jnp.zeros_like(state_ref)

    pairs = SWA_HEADS // SWA_KV_HEADS // 2
    key = lax.broadcasted_iota(jnp.int32, (2 * w, pairs * w), 0)
    qry = lax.broadcasted_iota(jnp.int32, (2 * w, pairs * w), 1) % w
    band = (key > qry) & (key <= qry + w)
    bias = jnp.where(band, 0.0, NEG_INF)
    bias_first = jnp.where(band & (key >= jnp.where(n == 0, w, 0)), 0.0, NEG_INF)

    hs = [slice(h * RET_DK, (h + 1) * RET_DK) for h in range(RET_HEADS)]
    lane_blk = lambda c: slice(c * LANES, (c + 1) * LANES)
    slabs = {g: [g * pairs + p for p in range(pairs)] for g in range(SWA_KV_HEADS)}

    for blk in range(MIXER_BLOCKS):
        rows = slice(blk * w, (blk + 1) * w)
        mask_bias = bias if blk else bias_first

        def prev_keys(c):
            return kc_ref[(blk - 1) * w:blk * w, lane_blk(c)] if blk else kp_ref[:, lane_blk(c)]

        def prev_values_t(c):
            return vc_ref[lane_blk(c), (blk - 1) * w:blk * w] if blk else vp_ref[lane_blk(c), :]

        for part in range(MIXER_PARTS):
            heads = range(part * RET_HEADS // MIXER_PARTS, (part + 1) * RET_HEADS // MIXER_PARTS)
            groups = range(part * SWA_KV_HEADS // MIXER_PARTS, (part + 1) * SWA_KV_HEADS // MIXER_PARTS)
            phases = [(g, e) for g in groups for e in range(2)]

            s_ret = {h: _dot_nt(q_ref[rows, hs[h]], k_ref[rows, hs[h]]) for h in heads}
            q_swa = {g: jnp.concatenate([sq_ref[rows, lane_blk(sl)] for sl in slabs[g]], 0) for g in groups}
            s_swa = []
            for g, e in phases:
                c = 2 * g + e
                k = jnp.concatenate([prev_keys(c), kc_ref[rows, lane_blk(c)]], 0)
                s_swa.append(_dot_nt(k, q_swa[g]) + mask_bias)

            states = {h: state_ref[h] for h in heads}
            ret = {}
            for h in heads:
                lhs = jnp.concatenate([(s_ret[h] * dmat_ref[h]).astype(BF16), qx_ref[rows, hs[h]]], 1)
                rhs = jnp.concatenate([v_ref[rows, hs[h]], states[h].astype(BF16)], 0)
                ret[h] = _dot(lhs, rhs)
            for h in heads:
                state_ref[h] = states[h] * chunk_decay[h] + _dot_tn(kz_ref[rows, hs[h]], v_ref[rows, hs[h]])

            pv = {}
            for (g, e), s in zip(phases, s_swa):
                c = 2 * g + e
                sink = jnp.concatenate([jnp.full((1, w), sink_ref[2 * sl + e] * LOG2E, F32) for sl in slabs[g]], 1)
                m = jnp.maximum(jnp.max(s, axis=0, keepdims=True), sink)
                p = jnp.exp2(s - m)
                inv = 1.0 / (jnp.sum(p, axis=0, keepdims=True) + jnp.exp2(sink - m))
                v_t = jnp.concatenate([prev_values_t(c), vc_ref[lane_blk(c), rows]], 1)
                pv[g, e] = _dot(v_t, p.astype(BF16)) * inv

            for h in heads:
                mu = jnp.mean(ret[h], axis=-1, keepdims=True)
                cen = ret[h] - mu
                var = jnp.mean(cen * cen, axis=-1, keepdims=True)
                ret_ref[rows, hs[h]] = (cen * lax.rsqrt(var + EPS) * g_ref[rows, hs[h]].astype(F32)).astype(BF16)

            for g in groups:
                acc = pv[g, 0] + pv[g, 1]
                for i, sl in enumerate(slabs[g]):
                    swa_ref[lane_blk(sl), rows] = acc[:, i * w:(i + 1) * w].astype(BF16)


def _mixers(rq, rqx, rk, rkz, rv, rg, dmat, chunk_decay, sq, sk4, sv4_t, sinks, batch, seq, cast_weights):
    assert RET_CHUNK == WINDOW
    t = rq.shape[0]
    rows = MIXER_BLOCKS * WINDOW
    n_steps = seq // rows
    grid = (batch, n_steps)
    cur = lambda b, n: (b * n_steps + n, 0)
    cur_t = lambda b, n: (0, b * n_steps + n)
    prev_idx = lambda b, n: b * n_steps * MIXER_BLOCKS + jnp.maximum(n * MIXER_BLOCKS - 1, 0)
    blk = pl.BlockSpec((rows, RET_WIDTH), cur)
    in_specs = [
        blk, blk, blk, blk, blk, blk, _resident((RET_HEADS, RET_CHUNK, RET_CHUNK)),
        pl.BlockSpec(memory_space=pltpu.SMEM),
        pl.BlockSpec((rows, SWA_WIDTH), cur),
        pl.BlockSpec((WINDOW, 4 * LANES), lambda b, n: (prev_idx(b, n), 0)),
        pl.BlockSpec((rows, 4 * LANES), cur),
        pl.BlockSpec((4 * LANES, WINDOW), lambda b, n: (0, prev_idx(b, n))),
        pl.BlockSpec((4 * LANES, rows), cur_t),
    ]
    cast_specs, cast_shapes = _cast_specs(cast_weights, grid)
    host = functools.partial(_mixers_kernel, chunk_decay=chunk_decay)
    out = pl.pallas_call(
        _with_casts(host, len(in_specs), 2, len(cast_weights)),
        out_shape=[jax.ShapeDtypeStruct((t, RET_WIDTH), BF16), jax.ShapeDtypeStruct((SWA_WIDTH, t), BF16)]
        + cast_shapes,
        grid=grid,
        in_specs=in_specs + cast_specs,
        out_specs=[blk, pl.BlockSpec((SWA_WIDTH, rows), cur_t)] + cast_specs,
        scratch_shapes=[pltpu.VMEM((RET_HEADS, RET_DK, RET_DV), F32)],
        compiler_params=_params(2),
        name="mixers",
    )(rq, rqx, rk, rkz, rv, rg, dmat, sinks, sq, sk4, sk4, sv4_t, sv4_t, *cast_weights)
    return out[0], out[1], out[2:]


def _outproj_kernel(h_ref, ret_ref, swa_ref, w_ref, g_ref, mem_ref, mem_gain_ref, wkv_ref,
                    h2_ref, hn_ref, mkv_ref, memn_ref):
    tm = h_ref.shape[0]

    @pl.when(pl.program_id(0) == 0)
    def _():
        memn_ref[...] = _rms_rows(mem_ref[...], mem_gain_ref[...]).astype(BF16)

    y = _dot(ret_ref[...], w_ref[:RET_WIDTH, :]) + _dot_tn(swa_ref[...], w_ref[RET_WIDTH:, :])
    h2_ref[...] = h_ref[...] + y
    mkv_ref[...] = _dot(memn_ref[...], wkv_ref[...].astype(BF16)).astype(BF16)

    def body(r, carry):
        rows = pl.ds(pl.multiple_of(r * NORM_ROWS, NORM_ROWS), NORM_ROWS)
        hn_ref[rows, :] = _rms_rows(h2_ref[rows, :], g_ref[...]).astype(BF16)
        return carry
    lax.fori_loop(0, tm // NORM_ROWS, body, 0)


def _outproj(h, ret, swa, w_out, gain, mem, mem_gain, wkv):
    t, d = h.shape
    n_steps = t // ROW_TM
    kv_cols = wkv.shape[1] // n_steps
    row = lambda width: pl.BlockSpec((ROW_TM, width), lambda i: (i, 0))
    return pl.pallas_call(
        _outproj_kernel,
        out_shape=[jax.ShapeDtypeStruct((t, d), F32), jax.ShapeDtypeStruct((t, d), BF16),
                   jax.ShapeDtypeStruct((mem.shape[0], wkv.shape[1]), BF16)],
        grid=(n_steps,),
        in_specs=[row(d), row(RET_WIDTH), pl.BlockSpec((SWA_WIDTH, ROW_TM), lambda i: (0, i)),
                  _resident(w_out.shape), _resident((1, d)),
                  _resident(mem.shape), _resident(mem_gain.shape),
                  pl.BlockSpec((wkv.shape[0], kv_cols), lambda i: (0, i))],
        out_specs=[row(d), row(d), pl.BlockSpec((mem.shape[0], kv_cols), lambda i: (0, i))],
        scratch_shapes=[pltpu.VMEM(mem.shape, BF16)],
        compiler_params=_params(1),
        name="outproj",
    )(h, ret, swa, w_out, gain, mem, mem_gain, wkv)


def _xattn_kernel(hn_ref, h_ref, wq_ref, wo_ref, k_ref, v_ref, o_ref):
    q = _dot(hn_ref[...], wq_ref[...]).astype(BF16)
    scale = XA_HEAD_DIM ** -0.5
    heads = [slice(hd * XA_HEAD_DIM, (hd + 1) * XA_HEAD_DIM) for hd in range(XA_HEADS)]
    scores = [_dot_nt(q[:, hs], k_ref[:, hs]) * scale for hs in heads]
    probs, invs = [], []
    for s in scores:
        p = jnp.exp(s - jnp.max(s, axis=-1, keepdims=True))
        probs.append(p.astype(BF16))
        invs.append(1.0 / jnp.sum(p, axis=-1, keepdims=True))
    out = h_ref[...]
    for hs, p, inv in zip(heads, probs, invs):
        att = (_dot(p, v_ref[:, hs]) * inv).astype(BF16)
        out = out + _dot(att, wo_ref[hs, :])
    o_ref[...] = out


def _xattn(hn, h, wq, wo, mkv, seq, mem_len, cast_weights):
    t, d = h.shape
    tiles_per_seq = seq // ROW_TM
    grid = (t // ROW_TM,)
    row = lambda i: (i, 0)
    cast_specs, cast_shapes = _cast_specs(cast_weights, grid)
    out = pl.pallas_call(
        _with_casts(_xattn_kernel, 6, 1, len(cast_weights)),
        out_shape=[jax.ShapeDtypeStruct((t, d), F32)] + cast_shapes,
        grid=grid,
        in_specs=[
            pl.BlockSpec((ROW_TM, d), row),
            pl.BlockSpec((ROW_TM, d), row),
            _resident((d, d)),
            _resident((d, d)),
            pl.BlockSpec((mem_len, d), lambda i: (i // tiles_per_seq, 0)),
            pl.BlockSpec((mem_len, d), lambda i: (i // tiles_per_seq, 1)),
        ] + cast_specs,
        out_specs=[pl.BlockSpec((ROW_TM, d), row)] + cast_specs,
        compiler_params=_params(1),
        name="xattn",
    )(hn, h, wq, wo, mkv, mkv, *cast_weights)
    return out[0], out[1:]


def kernel(x, mem, ffn1_norm, ffn1_w_gate, ffn1_w_up, ffn1_w_down, mix_norm, w_in, ret_gn_gain, swa_sinks,
           w_out, xa_norm, mem_norm, xa_wq, xa_wkv, xa_wo, ffn2_norm, ffn2_w_gate, ffn2_w_up, ffn2_w_down,
           final_norm):
    batch, seq, d = x.shape
    mem_len = mem.shape[1]
    depth = ffn1_norm.shape[0]
    h = x.reshape(batch * seq, d)
    mem2 = mem.reshape(batch * mem_len, d)
    row = lambda g: g.reshape(1, -1).astype(F32)
    final_gain = row(final_norm)

    for l in range(depth):
        last = l == depth - 1
        h, (w_mix,) = _ffn_f32(h, row(ffn1_norm[l]), ffn1_w_gate[l], ffn1_w_up[l], ffn1_w_down[l], final_gain,
                               final_norm=False, cast_weights=[w_in[l]])
        (dmat, xi_tile, zeta_tile), chunk_decay = _retention_tables()
        rq, rqx, rk, rkz, rv, rg, sq, sk4, sv4_t = _inproj(h, row(mix_norm[l]), w_mix, row(ret_gn_gain[l]),
                                                           xi_tile, zeta_tile, seq)
        ret, swa, (wo_mix, wq, wo) = _mixers(
            rq, rqx, rk, rkz, rv, rg, dmat, chunk_decay, sq, sk4, sv4_t, swa_sinks[l].astype(F32), batch, seq,
            [w_out[l], xa_wq[l], xa_wo[l]])
        h, hn, mkv = _outproj(h, ret, swa, wo_mix, row(xa_norm[l]), mem2, row(mem_norm[l]), xa_wkv[l])
        h, (w2_down,) = _xattn(hn, h, wq, wo, mkv, seq, mem_len, [ffn2_w_down[l]])
        h, _ = _ffn_f32(h, row(ffn2_norm[l]), ffn2_w_gate[l], ffn2_w_up[l], w2_down, final_gain, final_norm=last)
    if depth == 0:
        raise ValueError("depth must be at least 1")
    return h.reshape(batch, seq, d)
```

```python
import functools

import numpy as np
import jax
import jax.numpy as jnp
from jax import lax
from jax.experimental import pallas as pl
from jax.experimental.pallas import tpu as pltpu

F32 = jnp.float32
BF16 = jnp.bfloat16

D_MODEL = 2048
RET_HEADS = 8
RET_DK = 128
RET_DV = 128
RET_WIDTH = RET_HEADS * RET_DV
RET_CHUNK = 128
SWA_HEADS = 16
SWA_KV_HEADS = 2
SWA_HEAD_DIM = 64
SWA_WIDTH = SWA_HEADS * SWA_HEAD_DIM
SWA_KV_WIDTH = SWA_KV_HEADS * SWA_HEAD_DIM
WINDOW = 128
XA_HEADS = 4
XA_HEAD_DIM = D_MODEL // XA_HEADS
ROPE_THETA = 10000.0
EPS = 1e-6
IN_SIZES = (RET_WIDTH, RET_WIDTH, RET_WIDTH, RET_WIDTH, SWA_WIDTH, SWA_KV_WIDTH, SWA_KV_WIDTH)
IN_COLS = sum(IN_SIZES)

LANES = 128
BF16_SUBLANES = 16
VMEM_LIMIT_BYTES = 60000 * 1024

FFN_TM = 1024
FFN_TF = 512
FFN_HEAD_TF = 256
ROW_TM = 512
NORM_ROWS = 512
MIXER_PARTS = 2
MIXER_BLOCKS = 4
NEG_INF = float(np.finfo(np.float32).min)
LOG2E = float(np.log2(np.e))


def _params(n_axes):
    return pltpu.CompilerParams(
        dimension_semantics=("arbitrary",) * n_axes,
        vmem_limit_bytes=VMEM_LIMIT_BYTES,
    )


def _resident(shape):
    zeros = (0,) * len(shape)
    return pl.BlockSpec(shape, lambda *_: zeros, pipeline_mode=pl.Buffered(1))


def _rms_rows(x, gain):
    ms = jnp.mean(x * x, axis=-1, keepdims=True)
    return x * lax.rsqrt(ms + EPS) * gain


def _dot(a, b):
    return lax.dot_general(a, b, (((1,), (0,)), ((), ())), preferred_element_type=F32)


def _dot_nt(a, b):
    return lax.dot_general(a, b, (((1,), (1,)), ((), ())), preferred_element_type=F32)


def _dot_tn(a, b):
    return lax.dot_general(a, b, (((0,), (0,)), ((), ())), preferred_element_type=F32)


def _cast_block(shape, n_steps):
    rows, cols = shape
    for col_splits in (1, 2, 4, 8):
        row_blocks, rem = divmod(n_steps, col_splits)
        if rem or rows % row_blocks or cols % col_splits:
            continue
        br, bc = rows // row_blocks, cols // col_splits
        if br % BF16_SUBLANES == 0 and bc % LANES == 0:
            return br, bc, col_splits
    raise ValueError(f"no aligned {n_steps}-way split of {shape}")


def _with_casts(kernel_fn, n_in, n_out, n_cast):
    def wrapped(*refs):
        ins, rest = refs[:n_in], refs[n_in:]
        cast_in, rest = rest[:n_cast], rest[n_cast:]
        outs, rest = rest[:n_out], rest[n_out:]
        cast_out, scratch = rest[:n_cast], rest[n_cast:]
        kernel_fn(*ins, *outs, *scratch)
        for src, dst in zip(cast_in, cast_out):
            dst[...] = src[...].astype(BF16)
    return wrapped


def _cast_specs(weights, grid):
    n_steps = int(np.prod(grid))
    specs, shapes = [], []
    for w in weights:
        br, bc, col_splits = _cast_block(w.shape, n_steps)

        def index_map(*idx, col_splits=col_splits):
            step = idx[0]
            for size, i in zip(grid[1:], idx[1:]):
                step = step * size + i
            return step // col_splits, step % col_splits
        specs.append(pl.BlockSpec((br, bc), index_map))
        shapes.append(jax.ShapeDtypeStruct(w.shape, BF16))
    return specs, shapes


def _cast_specs_tiled(weights, grid):
    n_outer, n_inner = grid
    specs, shapes = [], []
    for w in weights:
        rows, cols = w.shape
        bc = cols // n_outer
        row_blocks = max(r for r in range(1, n_inner + 1) if rows % r == 0 and (rows // r) % BF16_SUBLANES == 0)
        if cols % n_outer or bc % LANES:
            raise ValueError(f"no aligned split of {w.shape} over {grid}")

        def index_map(i, j, row_blocks=row_blocks):
            return jnp.minimum(j, row_blocks - 1), i
        specs.append(pl.BlockSpec((rows // row_blocks, bc), index_map))
        shapes.append(jax.ShapeDtypeStruct(w.shape, BF16))
    return specs, shapes


def _rows_loop(n_rows, fn):
    def body(r, carry):
        fn(pl.ds(pl.multiple_of(r * NORM_ROWS, NORM_ROWS), NORM_ROWS))
        return carry
    lax.fori_loop(0, n_rows // NORM_ROWS, body, 0)


def _ffn_prologue(h_ref, g_ref, xb_ref, r_ref):
    def scale(rows):
        h = h_ref[rows, :]
        xb_ref[rows, :] = (h * g_ref[...]).astype(BF16)
        r_ref[rows, :] = lax.rsqrt(jnp.mean(h * h, axis=-1, keepdims=True) + EPS)
    _rows_loop(h_ref.shape[0], scale)


def _ffn_prologue_full(h_ref, g_ref, o_ref, xn_ref):
    def norm(rows):
        h = h_ref[rows, :]
        xn_ref[rows, :] = _rms_rows(h, g_ref[...]).astype(BF16)
        o_ref[rows, :] = h
    _rows_loop(h_ref.shape[0], norm)


def _ffn_step(xb_ref, wg, wu, wd, o_ref, *, first=None, h_ref=None, r_ref=None):
    xb = xb_ref[...]
    g = _dot(xb, wg)
    u = _dot(xb, wu)
    if r_ref is not None:
        r = r_ref[...]
        g, u = g * r, u * r
    a = (g * (0.5 / (1.0 + jnp.exp(-g))) * u).astype(BF16)
    for c in range(0, o_ref.shape[1], FFN_TF):
        cols = slice(c, c + FFN_TF)
        acc = o_ref[:, cols] if first is None else jnp.where(first, h_ref[:, cols], o_ref[:, cols])
        o_ref[:, cols] = acc + _dot(a, wd[:, cols])


def _ffn_kernel(*refs, n_ff, final_norm, aliased):
    if aliased:
        refs = refs[1:]
    h_ref, g_ref, wg_ref, wu_ref, wd_ref, fg_ref, o_ref, xb_ref, r_ref = refs
    j = pl.program_id(1)

    @pl.when(j == 0)
    def _():
        _ffn_prologue(h_ref, g_ref, xb_ref, r_ref)

    _ffn_step(xb_ref, wg_ref[...], wu_ref[...], wd_ref[...], o_ref, first=j == 0, h_ref=h_ref, r_ref=r_ref)

    if final_norm:
        @pl.when(j == n_ff - 1)
        def _():
            def norm(rows):
                o_ref[rows, :] = _rms_rows(o_ref[rows, :], fg_ref[...])
            _rows_loop(o_ref.shape[0], norm)


def _ffn(h, gain, wg, wu, wd, final_gain, *, final_norm, first_tile=0, into=None, cast_weights=()):
    t, d = h.shape
    f = wg.shape[1]
    n_ff = f // FFN_TF
    grid = (t // FFN_TM - first_tile, n_ff)
    aliased = into is not None
    tile = lambda i, j: (i + first_tile, 0)
    in_specs = [
        pl.BlockSpec((FFN_TM, d), tile),
        pl.BlockSpec((1, d), lambda i, j: (0, 0)),
        pl.BlockSpec((d, FFN_TF), lambda i, j: (0, j)),
        pl.BlockSpec((d, FFN_TF), lambda i, j: (0, j)),
        pl.BlockSpec((FFN_TF, d), lambda i, j: (j, 0)),
        pl.BlockSpec((1, d), lambda i, j: (0, 0)),
    ]
    args = (h, gain, wg, wu, wd, final_gain)
    if aliased:
        in_specs = [pl.BlockSpec(memory_space=pl.ANY)] + in_specs
        args = (into,) + args
    cast_specs, cast_shapes = _cast_specs_tiled(cast_weights, grid)
    host = functools.partial(_ffn_kernel, n_ff=n_ff, final_norm=final_norm, aliased=aliased)
    out = pl.pallas_call(
        _with_casts(host, len(in_specs), 1, len(cast_weights)),
        out_shape=[jax.ShapeDtypeStruct((t, d), F32)] + cast_shapes,
        grid=grid,
        in_specs=in_specs + cast_specs,
        out_specs=[pl.BlockSpec((FFN_TM, d), tile)] + cast_specs,
        scratch_shapes=[pltpu.VMEM((FFN_TM, d), BF16), pltpu.VMEM((FFN_TM, 1), F32)],
        input_output_aliases={0: 0} if aliased else {},
        compiler_params=_params(2),
        name="ffn",
    )(*args, *cast_weights)
    return out[0], out[1:]


def _ffn_head_kernel(h_ref, g_ref, wg_ref, wu_ref, wd_ref, fg_ref, o_ref, wg_out, wu_out, wd_out, xn_ref,
                     *, n_ff, final_norm):
    j = pl.program_id(0)

    @pl.when(j == 0)
    def _():
        _ffn_prologue_full(h_ref, g_ref, o_ref, xn_ref)

    wg, wu, wd = (w[...].astype(BF16) for w in (wg_ref, wu_ref, wd_ref))
    wg_out[...] = wg
    wu_out[...] = wu
    wd_out[...] = wd
    _ffn_step(xn_ref, wg, wu, wd, o_ref)

    if final_norm:
        @pl.when(j == n_ff - 1)
        def _():
            def norm(rows):
                o_ref[rows, :] = _rms_rows(o_ref[rows, :], fg_ref[...])
            _rows_loop(o_ref.shape[0], norm)


def _ffn_head(h, gain, wg, wu, wd, final_gain, *, final_norm):
    t, d = h.shape
    f = wg.shape[1]
    tf = FFN_HEAD_TF
    col = pl.BlockSpec((d, tf), lambda j: (0, j))
    row = pl.BlockSpec((tf, d), lambda j: (j, 0))
    vec = pl.BlockSpec((1, d), lambda j: (0, 0))
    tile0 = pl.BlockSpec((FFN_TM, d), lambda j: (0, 0))
    return pl.pallas_call(
        functools.partial(_ffn_head_kernel, n_ff=f // tf, final_norm=final_norm),
        out_shape=[jax.ShapeDtypeStruct((t, d), F32), jax.ShapeDtypeStruct(wg.shape, BF16),
                   jax.ShapeDtypeStruct(wu.shape, BF16), jax.ShapeDtypeStruct(wd.shape, BF16)],
        grid=(f // tf,),
        in_specs=[tile0, vec, col, col, row, vec],
        out_specs=[tile0, col, col, row],
        scratch_shapes=[pltpu.VMEM((FFN_TM, d), BF16)],
        compiler_params=_params(1),
        name="ffn_head",
    )(h, gain, wg, wu, wd, final_gain)


def _ffn_f32(h, gain, wg, wu, wd, final_gain, *, final_norm, cast_weights=()):
    head, wg16, wu16, wd16 = _ffn_head(h, gain, wg, wu, wd, final_gain, final_norm=final_norm)
    return _ffn(h, gain, wg16, wu16, wd16, final_gain, final_norm=final_norm, first_tile=1, into=head,
                cast_weights=cast_weights)


def _rope_tables(seq):
    pos = np.arange(seq, dtype=np.float32)

    def angles(d):
        inv = np.float32(ROPE_THETA) ** (-np.arange(0, d, 2, dtype=np.float32) / np.float32(d))
        return (pos[:, None] * inv[None, :].astype(np.float32)).astype(np.float32).astype(np.float64)

    a128 = angles(RET_DK)
    cos_r = np.concatenate([np.cos(a128), np.cos(a128)], -1)
    sin_r = np.concatenate([-np.sin(a128), np.sin(a128)], -1)
    a64 = angles(SWA_HEAD_DIM)
    c, s, z = np.cos(a64), np.sin(a64), np.zeros_like(a64)
    cos_s = np.concatenate([c, c, c, c], -1)
    sin_lo = np.concatenate([-s, z, -s, z], -1)
    sin_hi = np.concatenate([z, s, z, s], -1)
    return [jnp.asarray(v, dtype=F32) for v in (cos_r, sin_r, cos_s, sin_lo, sin_hi)]


def _retention_tables():
    c = RET_CHUNK
    heads = np.arange(RET_HEADS, dtype=np.float64)
    log_gamma = np.log1p(-np.exp2(-5.0 - heads))
    idx = np.arange(c, dtype=np.float64)
    diff = idx[:, None] - idx[None, :]
    scale = RET_DK ** -0.5
    dmat = np.where(diff[None] >= 0, np.exp(np.maximum(diff, 0.0)[None] * log_gamma[:, None, None]), 0.0)
    zeta = np.exp((c - 1.0 - idx)[None, :] * log_gamma[:, None])
    xi = np.exp((idx + 1.0)[None, :] * log_gamma[:, None])
    chunk_decay = tuple(float(v) for v in np.exp(c * log_gamma))

    def token_tile(tab):
        return np.tile(np.repeat(tab.T, RET_DK, axis=1), (ROW_TM // c, 1))
    tabs = [jnp.asarray(v, dtype=F32) for v in (dmat * scale, token_tile(xi), token_tile(zeta * scale))]
    return tabs, chunk_decay


def _inproj_kernel(h_ref, g_ref, w_ref, cr_ref, sr_ref, cs_ref, sl_ref, sh_ref, xi_ref, zeta_ref, gn_ref,
                   rq_ref, rqx_ref, rk_ref, rkz_ref, rv_ref, rg_ref, sq_ref, sk_ref, sv_ref, xn_ref):
    tm = h_ref.shape[0]

    def norm(rows):
        xn_ref[rows, :] = _rms_rows(h_ref[rows, :], g_ref[...]).astype(BF16)
    _rows_loop(tm, norm)

    xn = xn_ref[...]
    cr, sr = cr_ref[...], sr_ref[...]
    cs, sl, sh = cs_ref[...], sl_ref[...], sh_ref[...]
    half = LANES // 2
    slabs = [slice(s * LANES, (s + 1) * LANES) for s in range(RET_WIDTH // LANES)]

    def rope_ret(x):
        return x * cr + pltpu.roll(x, half, 1) * sr

    def rope_swa(x):
        return x * cs + pltpu.roll(x, LANES - half // 2, 1) * sl + pltpu.roll(x, half // 2, 1) * sh

    def project(col0, width):
        return _dot(xn, w_ref[:, col0:col0 + width])

    def ret_rotary(out_ref, scaled_ref, tab_ref):
        def epilogue(y):
            for sl_ in slabs:
                x = rope_ret(y[:, sl_])
                out_ref[:, sl_] = x.astype(BF16)
                scaled_ref[:, sl_] = (x * tab_ref[:, sl_]).astype(BF16)
        return epilogue

    def ret_values(y):
        for sl_ in slabs:
            rv_ref[:, sl_] = y[:, sl_].astype(BF16)

    def ret_gate(y):
        for sl_ in slabs:
            g = y[:, sl_]
            rg_ref[:, sl_] = (g * (1.0 / (1.0 + jnp.exp(-g))) * gn_ref[:, sl_]).astype(BF16)

    def swa_queries(y):
        scale = SWA_HEAD_DIM ** -0.5 * LOG2E
        for sl_ in slabs:
            sq_ref[:, sl_] = (rope_swa(y[:, sl_]) * scale).astype(BF16)

    def swa_keys_values(y):
        lo = lax.broadcasted_iota(jnp.int32, (tm, LANES), 1) < half
        for transposed, out_ref, x in ((False, sk_ref, rope_swa(y[:, :LANES])), (True, sv_ref, y[:, LANES:])):
            xr = pltpu.roll(x, half, 1)
            variants = (jnp.where(lo, x, 0.0), jnp.where(lo, 0.0, xr),
                        jnp.where(lo, xr, 0.0), jnp.where(lo, 0.0, x))
            for s, v in enumerate(variants):
                if transposed:
                    out_ref[s * LANES:(s + 1) * LANES, :] = v.T.astype(BF16)
                else:
                    out_ref[:, s * LANES:(s + 1) * LANES] = v.astype(BF16)

    epilogues = (ret_rotary(rq_ref, rqx_ref, xi_ref), ret_rotary(rk_ref, rkz_ref, zeta_ref), ret_values, ret_gate,
                 swa_queries, swa_keys_values)
    col0 = np.cumsum((0,) + IN_SIZES[:-2])
    widths = IN_SIZES[:-2] + (2 * SWA_KV_WIDTH,)
    order = (5, 0, 1, 3, 4, 2)
    for r in order:
        epilogues[r](project(int(col0[r]), widths[r]))


def _inproj(h, gain, w_in, gn_gain, xi_tile, zeta_tile, seq):
    t, d = h.shape
    tables = _rope_tables(seq)
    tiles_per_seq = seq // ROW_TM
    tab_spec = pl.BlockSpec((ROW_TM, LANES), lambda i: (i % tiles_per_seq, 0))

    def row_spec(width):
        return pl.BlockSpec((ROW_TM, width), lambda i: (i, 0))

    widths = (RET_WIDTH,) * 6 + (SWA_WIDTH, 4 * LANES)
    return pl.pallas_call(
        _inproj_kernel,
        out_shape=[jax.ShapeDtypeStruct((t, w), BF16) for w in widths]
        + [jax.ShapeDtypeStruct((4 * LANES, t), BF16)],
        grid=(t // ROW_TM,),
        in_specs=[row_spec(d), _resident((1, d)), _resident((d, IN_COLS))] + [tab_spec] * 5
        + [_resident((ROW_TM, RET_WIDTH))] * 2 + [_resident((1, RET_WIDTH))],
        out_specs=[row_spec(w) for w in widths] + [pl.BlockSpec((4 * LANES, ROW_TM), lambda i: (0, i))],
        scratch_shapes=[pltpu.VMEM((ROW_TM, d), BF16)],
        compiler_params=_params(1),
        name="inproj",
    )(h, gain, w_in, *tables, xi_tile, zeta_tile, gn_gain)


def _mixers_kernel(q_ref, qx_ref, k_ref, kz_ref, v_ref, g_ref, dmat_ref,
                   sink_ref, sq_ref, kp_ref, kc_ref, vp_ref, vc_ref, ret_ref, swa_ref, state_ref, *, chunk_decay):
    n = pl.program_id(1)
    w = WINDOW

    @pl.when(n == 0)
    def _():
        state_ref[...] = jnp.zeros_like(state_ref)

    pairs = SWA_HEADS // SWA_KV_HEADS // 2
    key = lax.broadcasted_iota(jnp.int32, (2 * w, pairs * w), 0)
    qry = lax.broadcasted_iota(jnp.int32, (2 * w, pairs * w), 1) % w
    band = (key > qry) & (key <= qry + w)
    bias = jnp.where(band, 0.0, NEG_INF)
    bias_first = jnp.where(band & (key >= jnp.where(n == 0, w, 0)), 0.0, NEG_INF)

    hs = [slice(h * RET_DK, (h + 1) * RET_DK) for h in range(RET_HEADS)]
    lane_blk = lambda c: slice(c * LANES, (c + 1) * LANES)
    slabs = {g: [g * pairs + p for p in range(pairs)] for g in range(SWA_KV_HEADS)}

    for blk in range(MIXER_BLOCKS):
        rows = slice(blk * w, (blk + 1) * w)
        mask_bias = bias if blk else bias_first

        def prev_keys(c):
            return kc_ref[(blk - 1) * w:blk * w, lane_blk(c)] if blk else kp_ref[:, lane_blk(c)]

        def prev_values_t(c):
            return vc_ref[lane_blk(c), (blk - 1) * w:blk * w] if blk else vp_ref[lane_blk(c), :]

        for part in range(MIXER_PARTS):
            heads = range(part * RET_HEADS // MIXER_PARTS, (part + 1) * RET_HEADS // MIXER_PARTS)
            groups = range(part * SWA_KV_HEADS // MIXER_PARTS, (part + 1) * SWA_KV_HEADS // MIXER_PARTS)
            phases = [(g, e) for g in groups for e in range(2)]

            s_ret = {h: _dot_nt(q_ref[rows, hs[h]], k_ref[rows, hs[h]]) for h in heads}
            q_swa = {g: jnp.concatenate([sq_ref[rows, lane_blk(sl)] for sl in slabs[g]], 0) for g in groups}
            s_swa = []
            for g, e in phases:
                c = 2 * g + e
                k = jnp.concatenate([prev_keys(c), kc_ref[rows, lane_blk(c)]], 0)
                s_swa.append(_dot_nt(k, q_swa[g]) + mask_bias)

            states = {h: state_ref[h] for h in heads}
            ret = {}
            for h in heads:
                lhs = jnp.concatenate([(s_ret[h] * dmat_ref[h]).astype(BF16), qx_ref[rows, hs[h]]], 1)
                rhs = jnp.concatenate([v_ref[rows, hs[h]], states[h].astype(BF16)], 0)
                ret[h] = _dot(lhs, rhs)
            for h in heads:
                state_ref[h] = states[h] * chunk_decay[h] + _dot_tn(kz_ref[rows, hs[h]], v_ref[rows, hs[h]])

            pv = {}
            for (g, e), s in zip(phases, s_swa):
                c = 2 * g + e
                sink = jnp.concatenate([jnp.full((1, w), sink_ref[2 * sl + e] * LOG2E, F32) for sl in slabs[g]], 1)
                m = jnp.maximum(jnp.max(s, axis=0, keepdims=True), sink)
                p = jnp.exp2(s - m)
                inv = 1.0 / (jnp.sum(p, axis=0, keepdims=True) + jnp.exp2(sink - m))
                v_t = jnp.concatenate([prev_values_t(c), vc_ref[lane_blk(c), rows]], 1)
                pv[g, e] = _dot(v_t, p.astype(BF16)) * inv

            for h in heads:
                mu = jnp.mean(ret[h], axis=-1, keepdims=True)
                cen = ret[h] - mu
                var = jnp.mean(cen * cen, axis=-1, keepdims=True)
                ret_ref[rows, hs[h]] = (cen * lax.rsqrt(var + EPS) * g_ref[rows, hs[h]].astype(F32)).astype(BF16)

            for g in groups:
                acc = pv[g, 0] + pv[g, 1]
                for i, sl in enumerate(slabs[g]):
                    swa_ref[lane_blk(sl), rows] = acc[:, i * w:(i + 1) * w].astype(BF16)


def _mixers(rq, rqx, rk, rkz, rv, rg, dmat, chunk_decay, sq, sk4, sv4_t, sinks, batch, seq, cast_weights):
    assert RET_CHUNK == WINDOW
    t = rq.shape[0]
    rows = MIXER_BLOCKS * WINDOW
    n_steps = seq // rows
    grid = (batch, n_steps)
    cur = lambda b, n: (b * n_steps + n, 0)
    cur_t = lambda b, n: (0, b * n_steps + n)
    prev_idx = lambda b, n: b * n_steps * MIXER_BLOCKS + jnp.maximum(n * MIXER_BLOCKS - 1, 0)
    blk = pl.BlockSpec((rows, RET_WIDTH), cur)
    in_specs = [
        blk, blk, blk, blk, blk, blk, _resident((RET_HEADS, RET_CHUNK, RET_CHUNK)),
        pl.BlockSpec(memory_space=pltpu.SMEM),
        pl.BlockSpec((rows, SWA_WIDTH), cur),
        pl.BlockSpec((WINDOW, 4 * LANES), lambda b, n: (prev_idx(b, n), 0)),
        pl.BlockSpec((rows, 4 * LANES), cur),
        pl.BlockSpec((4 * LANES, WINDOW), lambda b, n: (0, prev_idx(b, n))),
        pl.BlockSpec((4 * LANES, rows), cur_t),
    ]
    cast_specs, cast_shapes = _cast_specs(cast_weights, grid)
    host = functools.partial(_mixers_kernel, chunk_decay=chunk_decay)
    out = pl.pallas_call(
        _with_casts(host, len(in_specs), 2, len(cast_weights)),
        out_shape=[jax.ShapeDtypeStruct((t, RET_WIDTH), BF16), jax.ShapeDtypeStruct((SWA_WIDTH, t), BF16)]
        + cast_shapes,
        grid=grid,
        in_specs=in_specs + cast_specs,
        out_specs=[blk, pl.BlockSpec((SWA_WIDTH, rows), cur_t)] + cast_specs,
        scratch_shapes=[pltpu.VMEM((RET_HEADS, RET_DK, RET_DV), F32)],
        compiler_params=_params(2),
        name="mixers",
    )(rq, rqx, rk, rkz, rv, rg, dmat, sinks, sq, sk4, sk4, sv4_t, sv4_t, *cast_weights)
    return out[0], out[1], out[2:]


def _outproj_kernel(h_ref, ret_ref, swa_ref, w_ref, g_ref, mem_ref, mem_gain_ref, wkv_ref,
                    h2_ref, hn_ref, mkv_ref, memn_ref):
    tm = h_ref.shape[0]

    @pl.when(pl.program_id(0) == 0)
    def _():
        memn_ref[...] = _rms_rows(mem_ref[...], mem_gain_ref[...]).astype(BF16)

    y = _dot(ret_ref[...], w_ref[:RET_WIDTH, :]) + _dot_tn(swa_ref[...], w_ref[RET_WIDTH:, :])
    h2_ref[...] = h_ref[...] + y
    mkv_ref[...] = _dot(memn_ref[...], wkv_ref[...].astype(BF16)).astype(BF16)

    def body(r, carry):
        rows = pl.ds(pl.multiple_of(r * NORM_ROWS, NORM_ROWS), NORM_ROWS)
        hn_ref[rows, :] = _rms_rows(h2_ref[rows, :], g_ref[...]).astype(BF16)
        return carry
    lax.fori_loop(0, tm // NORM_ROWS, body, 0)


def _outproj(h, ret, swa, w_out, gain, mem, mem_gain, wkv):
    t, d = h.shape
    n_steps = t // ROW_TM
    kv_cols = wkv.shape[1] // n_steps
    row = lambda width: pl.BlockSpec((ROW_TM, width), lambda i: (i, 0))
    return pl.pallas_call(
        _outproj_kernel,
        out_shape=[jax.ShapeDtypeStruct((t, d), F32), jax.ShapeDtypeStruct((t, d), BF16),
                   jax.ShapeDtypeStruct((mem.shape[0], wkv.shape[1]), BF16)],
        grid=(n_steps,),
        in_specs=[row(d), row(RET_WIDTH), pl.BlockSpec((SWA_WIDTH, ROW_TM), lambda i: (0, i)),
                  _resident(w_out.shape), _resident((1, d)),
                  _resident(mem.shape), _resident(mem_gain.shape),
                  pl.BlockSpec((wkv.shape[0], kv_cols), lambda i: (0, i))],
        out_specs=[row(d), row(d), pl.BlockSpec((mem.shape[0], kv_cols), lambda i: (0, i))],
        scratch_shapes=[pltpu.VMEM(mem.shape, BF16)],
        compiler_params=_params(1),
        name="outproj",
    )(h, ret, swa, w_out, gain, mem, mem_gain, wkv)


def _xattn_kernel(hn_ref, h_ref, wq_ref, wo_ref, k_ref, v_ref, o_ref):
    q = _dot(hn_ref[...], wq_ref[...]).astype(BF16)
    scale = XA_HEAD_DIM ** -0.5
    heads = [slice(hd * XA_HEAD_DIM, (hd + 1) * XA_HEAD_DIM) for hd in range(XA_HEADS)]
    scores = [_dot_nt(q[:, hs], k_ref[:, hs]) * scale for hs in heads]
    probs, invs = [], []
    for s in scores:
        p = jnp.exp(s - jnp.max(s, axis=-1, keepdims=True))
        probs.append(p.astype(BF16))
        invs.append(1.0 / jnp.sum(p, axis=-1, keepdims=True))
    out = h_ref[...]
    for hs, p, inv in zip(heads, probs, invs):
        att = (_dot(p, v_ref[:, hs]) * inv).astype(BF16)
        out = out + _dot(att, wo_ref[hs, :])
    o_ref[...] = out


def _xattn(hn, h, wq, wo, mkv, seq, mem_len):
    t, d = h.shape
    tiles_per_seq = seq // ROW_TM
    row = lambda i: (i, 0)
    return pl.pallas_call(
        _xattn_kernel,
        out_shape=jax.ShapeDtypeStruct((t, d), F32),
        grid=(t // ROW_TM,),
        in_specs=[
            pl.BlockSpec((ROW_TM, d), row),
            pl.BlockSpec((ROW_TM, d), row),
            _resident((d, d)),
            _resident((d, d)),
            pl.BlockSpec((mem_len, d), lambda i: (i // tiles_per_seq, 0)),
            pl.BlockSpec((mem_len, d), lambda i: (i // tiles_per_seq, 1)),
        ],
        out_specs=pl.BlockSpec((ROW_TM, d), row),
        compiler_params=_params(1),
        name="xattn",
    )(hn, h, wq, wo, mkv, mkv)


def kernel(x, mem, ffn1_norm, ffn1_w_gate, ffn1_w_up, ffn1_w_down, mix_norm, w_in, ret_gn_gain, swa_sinks,
           w_out, xa_norm, mem_norm, xa_wq, xa_wkv, xa_wo, ffn2_norm, ffn2_w_gate, ffn2_w_up, ffn2_w_down,
           final_norm):
    batch, seq, d = x.shape
    mem_len = mem.shape[1]
    depth = ffn1_norm.shape[0]
    h = x.reshape(batch * seq, d)
    mem2 = mem.reshape(batch * mem_len, d)
    row = lambda g: g.reshape(1, -1).astype(F32)
    final_gain = row(final_norm)

    for l in range(depth):
        last = l == depth - 1
        h, (w_mix,) = _ffn_f32(h, row(ffn1_norm[l]), ffn1_w_gate[l], ffn1_w_up[l], ffn1_w_down[l], final_gain,
                               final_norm=False, cast_weights=[w_in[l]])
        (dmat, xi_tile, zeta_tile), chunk_decay = _retention_tables()
        rq, rqx, rk, rkz, rv, rg, sq, sk4, sv4_t = _inproj(h, row(mix_norm[l]), w_mix, row(ret_gn_gain[l]),
                                                           xi_tile, zeta_tile, seq)
        ret, swa, (wo_mix, wq, wo) = _mixers(
            rq, rqx, rk, rkz, rv, rg, dmat, chunk_decay, sq, sk4, sv4_t, swa_sinks[l].astype(F32), batch, seq,
            [w_out[l], xa_wq[l], xa_wo[l]])
        h, hn, mkv = _outproj(h, ret, swa, wo_mix, row(xa_norm[l]), mem2, row(mem_norm[l]), xa_wkv[l])
        h = _xattn(hn, h, wq, wo, mkv, seq, mem_len)
        h, _ = _ffn_f32(h, row(ffn2_norm[l]), ffn2_w_gate[l], ffn2_w_up[l], ffn2_w_down[l], final_gain,
                        final_norm=last)
    if depth == 0:
        raise ValueError("depth must be at least 1")
    return h.reshape(batch, seq, d)
```

```python
import functools

import numpy as np
import jax
import jax.numpy as jnp
from jax import lax
from jax.experimental import pallas as pl
from jax.experimental.pallas import tpu as pltpu

F32 = jnp.float32
BF16 = jnp.bfloat16

D_MODEL = 2048
RET_HEADS = 8
RET_DK = 128
RET_DV = 128
RET_WIDTH = RET_HEADS * RET_DV
RET_CHUNK = 128
SWA_HEADS = 16
SWA_KV_HEADS = 2
SWA_HEAD_DIM = 64
SWA_WIDTH = SWA_HEADS * SWA_HEAD_DIM
SWA_KV_WIDTH = SWA_KV_HEADS * SWA_HEAD_DIM
WINDOW = 128
XA_HEADS = 4
XA_HEAD_DIM = D_MODEL // XA_HEADS
ROPE_THETA = 10000.0
EPS = 1e-6
IN_SIZES = (RET_WIDTH, RET_WIDTH, RET_WIDTH, RET_WIDTH, SWA_WIDTH, SWA_KV_WIDTH, SWA_KV_WIDTH)
IN_COLS = sum(IN_SIZES)

LANES = 128
BF16_SUBLANES = 16
VMEM_LIMIT_BYTES = 60000 * 1024

FFN_TM = 1024
FFN_TF = 512
FFN_HEAD_TF = 256
ROW_TM = 512
NORM_ROWS = 512
MIXER_PARTS = 2
MIXER_BLOCKS = 4
NEG_INF = float(np.finfo(np.float32).min)
LOG2E = float(np.log2(np.e))


def _params(n_axes):
    return pltpu.CompilerParams(
        dimension_semantics=("arbitrary",) * n_axes,
        vmem_limit_bytes=VMEM_LIMIT_BYTES,
    )


def _resident(shape):
    zeros = (0,) * len(shape)
    return pl.BlockSpec(shape, lambda *_: zeros, pipeline_mode=pl.Buffered(1))


def _rms_rows(x, gain):
    ms = jnp.mean(x * x, axis=-1, keepdims=True)
    return x * lax.rsqrt(ms + EPS) * gain


def _dot(a, b):
    return lax.dot_general(a, b, (((1,), (0,)), ((), ())), preferred_element_type=F32)


def _dot_nt(a, b):
    return lax.dot_general(a, b, (((1,), (1,)), ((), ())), preferred_element_type=F32)


def _dot_tn(a, b):
    return lax.dot_general(a, b, (((0,), (0,)), ((), ())), preferred_element_type=F32)


def _cast_block(shape, n_steps):
    rows, cols = shape
    for col_splits in (1, 2, 4, 8):
        row_blocks, rem = divmod(n_steps, col_splits)
        if rem or rows % row_blocks or cols % col_splits:
            continue
        br, bc = rows // row_blocks, cols // col_splits
        if br % BF16_SUBLANES == 0 and bc % LANES == 0:
            return br, bc, col_splits
    raise ValueError(f"no aligned {n_steps}-way split of {shape}")


def _with_casts(kernel_fn, n_in, n_out, n_cast):
    def wrapped(*refs):
        ins, rest = refs[:n_in], refs[n_in:]
        cast_in, rest = rest[:n_cast], rest[n_cast:]
        outs, rest = rest[:n_out], rest[n_out:]
        cast_out, scratch = rest[:n_cast], rest[n_cast:]
        kernel_fn(*ins, *outs, *scratch)
        for src, dst in zip(cast_in, cast_out):
            dst[...] = src[...].astype(BF16)
    return wrapped


def _cast_specs(weights, grid):
    n_steps = int(np.prod(grid))
    specs, shapes = [], []
    for w in weights:
        br, bc, col_splits = _cast_block(w.shape, n_steps)

        def index_map(*idx, col_splits=col_splits):
            step = idx[0]
            for size, i in zip(grid[1:], idx[1:]):
                step = step * size + i
            return step // col_splits, step % col_splits
        specs.append(pl.BlockSpec((br, bc), index_map))
        shapes.append(jax.ShapeDtypeStruct(w.shape, BF16))
    return specs, shapes


def _cast_specs_tiled(weights, grid):
    n_outer, n_inner = grid
    specs, shapes = [], []
    for w in weights:
        rows, cols = w.shape
        band = cols // n_outer
        if cols % n_outer or band % LANES:
            raise ValueError(f"no aligned split of {w.shape} over {grid}")
        splits = [(r, c) for r in range(1, n_inner + 1) for c in range(1, n_inner // r + 1)
                  if rows % r == 0 and (rows // r) % BF16_SUBLANES == 0 and band % c == 0 and (band // c) % LANES == 0]
        row_blocks, col_blocks = max(splits, key=lambda rc: (rc[0] * rc[1], -rc[1]))

        def index_map(i, j, n=row_blocks * col_blocks, col_blocks=col_blocks):
            k = jnp.minimum(j, n - 1)
            return k // col_blocks, i * col_blocks + k % col_blocks
        specs.append(pl.BlockSpec((rows // row_blocks, band // col_blocks), index_map))
        shapes.append(jax.ShapeDtypeStruct(w.shape, BF16))
    return specs, shapes


def _rows_loop(n_rows, fn):
    def body(r, carry):
        fn(pl.ds(pl.multiple_of(r * NORM_ROWS, NORM_ROWS), NORM_ROWS))
        return carry
    lax.fori_loop(0, n_rows // NORM_ROWS, body, 0)


def _ffn_prologue(h_ref, g_ref, xb_ref, r_ref):
    def scale(rows):
        h = h_ref[rows, :]
        xb_ref[rows, :] = (h * g_ref[...]).astype(BF16)
        r_ref[rows, :] = lax.rsqrt(jnp.mean(h * h, axis=-1, keepdims=True) + EPS)
    _rows_loop(h_ref.shape[0], scale)


def _ffn_prologue_full(h_ref, g_ref, o_ref, xn_ref):
    def norm(rows):
        h = h_ref[rows, :]
        xn_ref[rows, :] = _rms_rows(h, g_ref[...]).astype(BF16)
        o_ref[rows, :] = h
    _rows_loop(h_ref.shape[0], norm)


def _ffn_act(xb_ref, wg, wu, r_ref=None):
    xb = xb_ref[...]
    g = _dot(xb, wg)
    u = _dot(xb, wu)
    if r_ref is not None:
        r = r_ref[...]
        g, u = g * r, u * r
    return (g * (0.5 / (1.0 + jnp.exp(-g))) * u).astype(BF16)


def _ffn_down(a, wd, o_ref, *, first=None, h_ref=None):
    for c in range(0, o_ref.shape[1], FFN_TF):
        cols = slice(c, c + FFN_TF)
        if first is None:
            acc = o_ref[:, cols]
        elif first is True:
            acc = h_ref[:, cols]
        else:
            acc = jnp.where(first, h_ref[:, cols], o_ref[:, cols])
        o_ref[:, cols] = acc + _dot(a, wd[:, cols])


def _ffn_kernel(*refs, n_ff, final_norm, aliased):
    if aliased:
        refs = refs[1:]
    h_ref, g_ref, wg_ref, wu_ref, wd_ref, fg_ref, o_ref, xb_ref, r_ref, a0_ref, a1_ref = refs
    j = pl.program_id(1)

    @pl.when(j == 0)
    def _():
        _ffn_prologue(h_ref, g_ref, xb_ref, r_ref)
        a0_ref[...] = _ffn_act(xb_ref, wg_ref[...], wu_ref[...], r_ref)

    def middle(a_prev, a_next, first):
        a_next[...] = _ffn_act(xb_ref, wg_ref[...], wu_ref[...], r_ref)
        _ffn_down(a_prev[...], wd_ref[...], o_ref, first=first, h_ref=h_ref)

    inner = (j > 0) & (j < n_ff)

    @pl.when(inner & (j % 2 == 1))
    def _():
        middle(a0_ref, a1_ref, j == 1)

    @pl.when(inner & (j % 2 == 0))
    def _():
        middle(a1_ref, a0_ref, None)

    @pl.when(j == n_ff)
    def _():
        a_last = a0_ref if n_ff % 2 else a1_ref
        _ffn_down(a_last[...], wd_ref[...], o_ref, first=True if n_ff == 1 else None, h_ref=h_ref)
        if final_norm:
            def norm(rows):
                o_ref[rows, :] = _rms_rows(o_ref[rows, :], fg_ref[...])
            _rows_loop(o_ref.shape[0], norm)


def _ffn(h, gain, wg, wu, wd, final_gain, *, final_norm, first_tile=0, into=None, cast_weights=()):
    t, d = h.shape
    f = wg.shape[1]
    n_ff = f // FFN_TF
    grid = (t // FFN_TM - first_tile, n_ff + 1)
    aliased = into is not None
    tile = lambda i, j: (i + first_tile, 0)
    col = pl.BlockSpec((d, FFN_TF), lambda i, j: (0, jnp.minimum(j, n_ff - 1)))
    in_specs = [
        pl.BlockSpec((FFN_TM, d), tile),
        pl.BlockSpec((1, d), lambda i, j: (0, 0)),
        col,
        col,
        pl.BlockSpec((FFN_TF, d), lambda i, j: (jnp.maximum(j - 1, 0), 0)),
        pl.BlockSpec((1, d), lambda i, j: (0, 0)),
    ]
    args = (h, gain, wg, wu, wd, final_gain)
    if aliased:
        in_specs = [pl.BlockSpec(memory_space=pl.ANY)] + in_specs
        args = (into,) + args
    cast_specs, cast_shapes = _cast_specs_tiled(cast_weights, grid)
    host = functools.partial(_ffn_kernel, n_ff=n_ff, final_norm=final_norm, aliased=aliased)
    out = pl.pallas_call(
        _with_casts(host, len(in_specs), 1, len(cast_weights)),
        out_shape=[jax.ShapeDtypeStruct((t, d), F32)] + cast_shapes,
        grid=grid,
        in_specs=in_specs + cast_specs,
        out_specs=[pl.BlockSpec((FFN_TM, d), tile)] + cast_specs,
        scratch_shapes=[pltpu.VMEM((FFN_TM, d), BF16), pltpu.VMEM((FFN_TM, 1), F32),
                        pltpu.VMEM((FFN_TM, FFN_TF), BF16), pltpu.VMEM((FFN_TM, FFN_TF), BF16)],
        input_output_aliases={0: 0} if aliased else {},
        compiler_params=_params(2),
        name="ffn",
    )(*args, *cast_weights)
    return out[0], out[1:]


def _ffn_head_kernel(h_ref, g_ref, wg_ref, wu_ref, wd_ref, fg_ref, o_ref, wg_out, wu_out, wd_out, xn_ref,
                     *, n_ff, final_norm):
    j = pl.program_id(0)

    @pl.when(j == 0)
    def _():
        _ffn_prologue_full(h_ref, g_ref, o_ref, xn_ref)

    wg, wu, wd = (w[...].astype(BF16) for w in (wg_ref, wu_ref, wd_ref))
    wg_out[...] = wg
    wu_out[...] = wu
    wd_out[...] = wd
    _ffn_down(_ffn_act(xn_ref, wg, wu), wd, o_ref)

    if final_norm:
        @pl.when(j == n_ff - 1)
        def _():
            def norm(rows):
                o_ref[rows, :] = _rms_rows(o_ref[rows, :], fg_ref[...])
            _rows_loop(o_ref.shape[0], norm)


def _ffn_head(h, gain, wg, wu, wd, final_gain, *, final_norm):
    t, d = h.shape
    f = wg.shape[1]
    tf = FFN_HEAD_TF
    col = pl.BlockSpec((d, tf), lambda j: (0, j))
    row = pl.BlockSpec((tf, d), lambda j: (j, 0))
    vec = pl.BlockSpec((1, d), lambda j: (0, 0))
    tile0 = pl.BlockSpec((FFN_TM, d), lambda j: (0, 0))
    return pl.pallas_call(
        functools.partial(_ffn_head_kernel, n_ff=f // tf, final_norm=final_norm),
        out_shape=[jax.ShapeDtypeStruct((t, d), F32), jax.ShapeDtypeStruct(wg.shape, BF16),
                   jax.ShapeDtypeStruct(wu.shape, BF16), jax.ShapeDtypeStruct(wd.shape, BF16)],
        grid=(f // tf,),
        in_specs=[tile0, vec, col, col, row, vec],
        out_specs=[tile0, col, col, row],
        scratch_shapes=[pltpu.VMEM((FFN_TM, d), BF16)],
        compiler_params=_params(1),
        name="ffn_head",
    )(h, gain, wg, wu, wd, final_gain)


def _ffn_f32(h, gain, wg, wu, wd, final_gain, *, final_norm, cast_weights=()):
    head, wg16, wu16, wd16 = _ffn_head(h, gain, wg, wu, wd, final_gain, final_norm=final_norm)
    return _ffn(h, gain, wg16, wu16, wd16, final_gain, final_norm=final_norm, first_tile=1, into=head,
                cast_weights=cast_weights)


def _rope_tables(seq):
    pos = np.arange(seq, dtype=np.float32)

    def angles(d):
        inv = np.float32(ROPE_THETA) ** (-np.arange(0, d, 2, dtype=np.float32) / np.float32(d))
        return (pos[:, None] * inv[None, :].astype(np.float32)).astype(np.float32).astype(np.float64)

    a128 = angles(RET_DK)
    cos_r = np.concatenate([np.cos(a128), np.cos(a128)], -1)
    sin_r = np.concatenate([-np.sin(a128), np.sin(a128)], -1)
    a64 = angles(SWA_HEAD_DIM)
    c, s, z = np.cos(a64), np.sin(a64), np.zeros_like(a64)
    cos_s = np.concatenate([c, c, c, c], -1)
    sin_lo = np.concatenate([-s, z, -s, z], -1)
    sin_hi = np.concatenate([z, s, z, s], -1)
    return [jnp.asarray(v, dtype=F32) for v in (cos_r, sin_r, cos_s, sin_lo, sin_hi)]


def _retention_tables():
    c = RET_CHUNK
    heads = np.arange(RET_HEADS, dtype=np.float64)
    log_gamma = np.log1p(-np.exp2(-5.0 - heads))
    idx = np.arange(c, dtype=np.float64)
    diff = idx[:, None] - idx[None, :]
    scale = RET_DK ** -0.5
    dmat = np.where(diff[None] >= 0, np.exp(np.maximum(diff, 0.0)[None] * log_gamma[:, None, None]), 0.0)
    zeta = np.exp((c - 1.0 - idx)[None, :] * log_gamma[:, None])
    xi = np.exp((idx + 1.0)[None, :] * log_gamma[:, None])
    chunk_decay = tuple(float(v) for v in np.exp(c * log_gamma))

    def token_tile(tab):
        return np.tile(np.repeat(tab.T, RET_DK, axis=1), (ROW_TM // c, 1))
    tabs = [jnp.asarray(v, dtype=F32) for v in (dmat * scale, token_tile(xi), token_tile(zeta * scale))]
    return tabs, chunk_decay


def _inproj_kernel(h_ref, g_ref, w_ref, cr_ref, sr_ref, cs_ref, sl_ref, sh_ref, xi_ref, zeta_ref, gn_ref,
                   rq_ref, rqx_ref, rk_ref, rkz_ref, rv_ref, rg_ref, sq_ref, sk_ref, sv_ref, xn_ref):
    tm = h_ref.shape[0]

    def norm(rows):
        xn_ref[rows, :] = _rms_rows(h_ref[rows, :], g_ref[...]).astype(BF16)
    _rows_loop(tm, norm)

    xn = xn_ref[...]
    cr, sr = cr_ref[...], sr_ref[...]
    cs, sl, sh = cs_ref[...], sl_ref[...], sh_ref[...]
    half = LANES // 2
    slabs = [slice(s * LANES, (s + 1) * LANES) for s in range(RET_WIDTH // LANES)]

    def rope_ret(x):
        return x * cr + pltpu.roll(x, half, 1) * sr

    def rope_swa(x):
        return x * cs + pltpu.roll(x, LANES - half // 2, 1) * sl + pltpu.roll(x, half // 2, 1) * sh

    def project(col0, width):
        return _dot(xn, w_ref[:, col0:col0 + width])

    def ret_rotary(out_ref, scaled_ref, tab_ref):
        def epilogue(y):
            for sl_ in slabs:
                x = rope_ret(y[:, sl_])
                out_ref[:, sl_] = x.astype(BF16)
                scaled_ref[:, sl_] = (x * tab_ref[:, sl_]).astype(BF16)
        return epilogue

    def ret_values(y):
        for sl_ in slabs:
            rv_ref[:, sl_] = y[:, sl_].astype(BF16)

    def ret_gate(y):
        for sl_ in slabs:
            g = y[:, sl_]
            rg_ref[:, sl_] = (g * (1.0 / (1.0 + jnp.exp(-g))) * gn_ref[:, sl_]).astype(BF16)

    def swa_queries(y):
        scale = SWA_HEAD_DIM ** -0.5 * LOG2E
        for sl_ in slabs:
            sq_ref[:, sl_] = (rope_swa(y[:, sl_]) * scale).astype(BF16)

    def swa_keys_values(y):
        lo = lax.broadcasted_iota(jnp.int32, (tm, LANES), 1) < half
        for transposed, out_ref, x in ((False, sk_ref, rope_swa(y[:, :LANES])), (True, sv_ref, y[:, LANES:])):
            xr = pltpu.roll(x, half, 1)
            variants = (jnp.where(lo, x, 0.0), jnp.where(lo, 0.0, xr),
                        jnp.where(lo, xr, 0.0), jnp.where(lo, 0.0, x))
            for s, v in enumerate(variants):
                if transposed:
                    out_ref[s * LANES:(s + 1) * LANES, :] = v.T.astype(BF16)
                else:
                    out_ref[:, s * LANES:(s + 1) * LANES] = v.astype(BF16)

    epilogues = (ret_rotary(rq_ref, rqx_ref, xi_ref), ret_rotary(rk_ref, rkz_ref, zeta_ref), ret_values, ret_gate,
                 swa_queries, swa_keys_values)
    col0 = np.cumsum((0,) + IN_SIZES[:-2])
    widths = IN_SIZES[:-2] + (2 * SWA_KV_WIDTH,)
    order = (5, 0, 1, 3, 4, 2)
    for r in order:
        epilogues[r](project(int(col0[r]), widths[r]))


def _inproj(h, gain, w_in, gn_gain, xi_tile, zeta_tile, seq):
    t, d = h.shape
    tables = _rope_tables(seq)
    tiles_per_seq = seq // ROW_TM
    tab_spec = pl.BlockSpec((ROW_TM, LANES), lambda i: (i % tiles_per_seq, 0))

    def row_spec(width):
        return pl.BlockSpec((ROW_TM, width), lambda i: (i, 0))

    widths = (RET_WIDTH,) * 6 + (SWA_WIDTH, 4 * LANES)
    return pl.pallas_call(
        _inproj_kernel,
        out_shape=[jax.ShapeDtypeStruct((t, w), BF16) for w in widths]
        + [jax.ShapeDtypeStruct((4 * LANES, t), BF16)],
        grid=(t // ROW_TM,),
        in_specs=[row_spec(d), _resident((1, d)), _resident((d, IN_COLS))] + [tab_spec] * 5
        + [_resident((ROW_TM, RET_WIDTH))] * 2 + [_resident((1, RET_WIDTH))],
        out_specs=[row_spec(w) for w in widths] + [pl.BlockSpec((4 * LANES, ROW_TM), lambda i: (0, i))],
        scratch_shapes=[pltpu.VMEM((ROW_TM, d), BF16)],
        compiler_params=_params(1),
        name="inproj",
    )(h, gain, w_in, *tables, xi_tile, zeta_tile, gn_gain)


def _mixers_kernel(q_ref, qx_ref, k_ref, kz_ref, v_ref, g_ref, dmat_ref,
                   sink_ref, sq_ref, kp_ref, kc_ref, vp_ref, vc_ref, ret_ref, swa_ref, state_ref, *, chunk_decay):
    n = pl.program_id(1)
    w = WINDOW

    @pl.when(n == 0)
    def _():
        state_ref[...] = jnp.zeros_like(state_ref)

    pairs = SWA_HEADS // SWA_KV_HEADS // 2
    key = lax.broadcasted_iota(jnp.int32, (2 * w, pairs * w), 0)
    qry = lax.broadcasted_iota(jnp.int32, (2 * w, pairs * w), 1) % w
    band = (key > qry) & (key <= qry + w)
    bias = jnp.where(band, 0.0, NEG_INF)
    bias_first = jnp.where(band & (key >= jnp.where(n == 0, w, 0)), 0.0, NEG_INF)

    hs = [slice(h * RET_DK, (h + 1) * RET_DK) for h in range(RET_HEADS)]
    lane_blk = lambda c: slice(c * LANES, (c + 1) * LANES)
    slabs = {g: [g * pairs + p for p in range(pairs)] for g in range(SWA_KV_HEADS)}

    for blk in range(MIXER_BLOCKS):
        rows = slice(blk * w, (blk + 1) * w)
        mask_bias = bias if blk else bias_first

        def prev_keys(c):
            return kc_ref[(blk - 1) * w:blk * w, lane_blk(c)] if blk else kp_ref[:, lane_blk(c)]

        def prev_values_t(c):
            return vc_ref[lane_blk(c), (blk - 1) * w:blk * w] if blk else vp_ref[lane_blk(c), :]

        for part in range(MIXER_PARTS):
            heads = range(part * RET_HEADS // MIXER_PARTS, (part + 1) * RET_HEADS // MIXER_PARTS)
            groups = range(part * SWA_KV_HEADS // MIXER_PARTS, (part + 1) * SWA_KV_HEADS // MIXER_PARTS)
            phases = [(g, e) for g in groups for e in range(2)]

            s_ret = {h: _dot_nt(q_ref[rows, hs[h]], k_ref[rows, hs[h]]) for h in heads}
            q_swa = {g: jnp.concatenate([sq_ref[rows, lane_blk(sl)] for sl in slabs[g]], 0) for g in groups}
            s_swa = []
            for g, e in phases:
                c = 2 * g + e
                k = jnp.concatenate([prev_keys(c), kc_ref[rows, lane_blk(c)]], 0)
                s_swa.append(_dot_nt(k, q_swa[g]) + mask_bias)

            states = {h: state_ref[h] for h in heads}
            ret = {}
            for h in heads:
                lhs = jnp.concatenate([(s_ret[h] * dmat_ref[h]).astype(BF16), qx_ref[rows, hs[h]]], 1)
                rhs = jnp.concatenate([v_ref[rows, hs[h]], states[h].astype(BF16)], 0)
                ret[h] = _dot(lhs, rhs)
            for h in heads:
                state_ref[h] = states[h] * chunk_decay[h] + _dot_tn(kz_ref[rows, hs[h]], v_ref[rows, hs[h]])

            pv = {}
            for (g, e), s in zip(phases, s_swa):
                c = 2 * g + e
                sink = jnp.concatenate([jnp.full((1, w), sink_ref[2 * sl + e] * LOG2E, F32) for sl in slabs[g]], 1)
                m = jnp.maximum(jnp.max(s, axis=0, keepdims=True), sink)
                p = jnp.exp2(s - m)
                inv = 1.0 / (jnp.sum(p, axis=0, keepdims=True) + jnp.exp2(sink - m))
                v_t = jnp.concatenate([prev_values_t(c), vc_ref[lane_blk(c), rows]], 1)
                pv[g, e] = _dot(v_t, p.astype(BF16)) * inv

            for h in heads:
                mu = jnp.mean(ret[h], axis=-1, keepdims=True)
                cen = ret[h] - mu
                var = jnp.mean(cen * cen, axis=-1, keepdims=True)
                ret_ref[rows, hs[h]] = (cen * lax.rsqrt(var + EPS) * g_ref[rows, hs[h]].astype(F32)).astype(BF16)

            for g in groups:
                acc = pv[g, 0] + pv[g, 1]
                for i, sl in enumerate(slabs[g]):
                    swa_ref[lane_blk(sl), rows] = acc[:, i * w:(i + 1) * w].astype(BF16)


def _mixers(rq, rqx, rk, rkz, rv, rg, dmat, chunk_decay, sq, sk4, sv4_t, sinks, batch, seq, cast_weights):
    assert RET_CHUNK == WINDOW
    t = rq.shape[0]
    rows = MIXER_BLOCKS * WINDOW
    n_steps = seq // rows
    grid = (batch, n_steps)
    cur = lambda b, n: (b * n_steps + n, 0)
    cur_t = lambda b, n: (0, b * n_steps + n)
    prev_idx = lambda b, n: b * n_steps * MIXER_BLOCKS + jnp.maximum(n * MIXER_BLOCKS - 1, 0)
    blk = pl.BlockSpec((rows, RET_WIDTH), cur)
    in_specs = [
        blk, blk, blk, blk, blk, blk, _resident((RET_HEADS, RET_CHUNK, RET_CHUNK)),
        pl.BlockSpec(memory_space=pltpu.SMEM),
        pl.BlockSpec((rows, SWA_WIDTH), cur),
        pl.BlockSpec((WINDOW, 4 * LANES), lambda b, n: (prev_idx(b, n), 0)),
        pl.BlockSpec((rows, 4 * LANES), cur),
        pl.BlockSpec((4 * LANES, WINDOW), lambda b, n: (0, prev_idx(b, n))),
        pl.BlockSpec((4 * LANES, rows), cur_t),
    ]
    cast_specs, cast_shapes = _cast_specs(cast_weights, grid)
    host = functools.partial(_mixers_kernel, chunk_decay=chunk_decay)
    out = pl.pallas_call(
        _with_casts(host, len(in_specs), 2, len(cast_weights)),
        out_shape=[jax.ShapeDtypeStruct((t, RET_WIDTH), BF16), jax.ShapeDtypeStruct((SWA_WIDTH, t), BF16)]
        + cast_shapes,
        grid=grid,
        in_specs=in_specs + cast_specs,
        out_specs=[blk, pl.BlockSpec((SWA_WIDTH, rows), cur_t)] + cast_specs,
        scratch_shapes=[pltpu.VMEM((RET_HEADS, RET_DK, RET_DV), F32)],
        compiler_params=_params(2),
        name="mixers",
    )(rq, rqx, rk, rkz, rv, rg, dmat, sinks, sq, sk4, sk4, sv4_t, sv4_t, *cast_weights)
    return out[0], out[1], out[2:]


def _outproj_kernel(h_ref, ret_ref, swa_ref, w_ref, g_ref, mem_ref, mem_gain_ref, wkv_ref,
                    h2_ref, hn_ref, mkv_ref, memn_ref):
    tm = h_ref.shape[0]

    @pl.when(pl.program_id(0) == 0)
    def _():
        memn_ref[...] = _rms_rows(mem_ref[...], mem_gain_ref[...]).astype(BF16)

    y = _dot(ret_ref[...], w_ref[:RET_WIDTH, :]) + _dot_tn(swa_ref[...], w_ref[RET_WIDTH:, :])
    h2_ref[...] = h_ref[...] + y
    mkv_ref[...] = _dot(memn_ref[...], wkv_ref[...].astype(BF16)).astype(BF16)

    def body(r, carry):
        rows = pl.ds(pl.multiple_of(r * NORM_ROWS, NORM_ROWS), NORM_ROWS)
        hn_ref[rows, :] = _rms_rows(h2_ref[rows, :], g_ref[...]).astype(BF16)
        return carry
    lax.fori_loop(0, tm // NORM_ROWS, body, 0)


def _outproj(h, ret, swa, w_out, gain, mem, mem_gain, wkv):
    t, d = h.shape
    n_steps = t // ROW_TM
    kv_cols = wkv.shape[1] // n_steps
    row = lambda width: pl.BlockSpec((ROW_TM, width), lambda i: (i, 0))
    return pl.pallas_call(
        _outproj_kernel,
        out_shape=[jax.ShapeDtypeStruct((t, d), F32), jax.ShapeDtypeStruct((t, d), BF16),
                   jax.ShapeDtypeStruct((mem.shape[0], wkv.shape[1]), BF16)],
        grid=(n_steps,),
        in_specs=[row(d), row(RET_WIDTH), pl.BlockSpec((SWA_WIDTH, ROW_TM), lambda i: (0, i)),
                  _resident(w_out.shape), _resident((1, d)),
                  _resident(mem.shape), _resident(mem_gain.shape),
                  pl.BlockSpec((wkv.shape[0], kv_cols), lambda i: (0, i))],
        out_specs=[row(d), row(d), pl.BlockSpec((mem.shape[0], kv_cols), lambda i: (0, i))],
        scratch_shapes=[pltpu.VMEM(mem.shape, BF16)],
        compiler_params=_params(1),
        name="outproj",
    )(h, ret, swa, w_out, gain, mem, mem_gain, wkv)


def _xattn_kernel(hn_ref, h_ref, wq_ref, wo_ref, k_ref, v_ref, o_ref):
    q = _dot(hn_ref[...], wq_ref[...]).astype(BF16)
    scale = XA_HEAD_DIM ** -0.5
    heads = [slice(hd * XA_HEAD_DIM, (hd + 1) * XA_HEAD_DIM) for hd in range(XA_HEADS)]
    scores = [_dot_nt(q[:, hs], k_ref[:, hs]) * scale for hs in heads]
    probs, invs = [], []
    for s in scores:
        p = jnp.exp(s - jnp.max(s, axis=-1, keepdims=True))
        probs.append(p.astype(BF16))
        invs.append(1.0 / jnp.sum(p, axis=-1, keepdims=True))
    out = h_ref[...]
    for hs, p, inv in zip(heads, probs, invs):
        att = (_dot(p, v_ref[:, hs]) * inv).astype(BF16)
        out = out + _dot(att, wo_ref[hs, :])
    o_ref[...] = out


def _xattn(hn, h, wq, wo, mkv, seq, mem_len):
    t, d = h.shape
    tiles_per_seq = seq // ROW_TM
    row = lambda i: (i, 0)
    return pl.pallas_call(
        _xattn_kernel,
        out_shape=jax.ShapeDtypeStruct((t, d), F32),
        grid=(t // ROW_TM,),
        in_specs=[
            pl.BlockSpec((ROW_TM, d), row),
            pl.BlockSpec((ROW_TM, d), row),
            _resident((d, d)),
            _resident((d, d)),
            pl.BlockSpec((mem_len, d), lambda i: (i // tiles_per_seq, 0)),
            pl.BlockSpec((mem_len, d), lambda i: (i // tiles_per_seq, 1)),
        ],
        out_specs=pl.BlockSpec((ROW_TM, d), row),
        compiler_params=_params(1),
        name="xattn",
    )(hn, h, wq, wo, mkv, mkv)


def kernel(x, mem, ffn1_norm, ffn1_w_gate, ffn1_w_up, ffn1_w_down, mix_norm, w_in, ret_gn_gain, swa_sinks,
           w_out, xa_norm, mem_norm, xa_wq, xa_wkv, xa_wo, ffn2_norm, ffn2_w_gate, ffn2_w_up, ffn2_w_down,
           final_norm):
    batch, seq, d = x.shape
    mem_len = mem.shape[1]
    depth = ffn1_norm.shape[0]
    h = x.reshape(batch * seq, d)
    mem2 = mem.reshape(batch * mem_len, d)
    row = lambda g: g.reshape(1, -1).astype(F32)
    final_gain = row(final_norm)

    for l in range(depth):
        last = l == depth - 1
        h, (w_mix,) = _ffn_f32(h, row(ffn1_norm[l]), ffn1_w_gate[l], ffn1_w_up[l], ffn1_w_down[l], final_gain,
                               final_norm=False, cast_weights=[w_in[l]])
        (dmat, xi_tile, zeta_tile), chunk_decay = _retention_tables()
        rq, rqx, rk, rkz, rv, rg, sq, sk4, sv4_t = _inproj(h, row(mix_norm[l]), w_mix, row(ret_gn_gain[l]),
                                                           xi_tile, zeta_tile, seq)
        ret, swa, (wo_mix, wq, wo) = _mixers(
            rq, rqx, rk, rkz, rv, rg, dmat, chunk_decay, sq, sk4, sv4_t, swa_sinks[l].astype(F32), batch, seq,
            [w_out[l], xa_wq[l], xa_wo[l]])
        h, hn, mkv = _outproj(h, ret, swa, wo_mix, row(xa_norm[l]), mem2, row(mem_norm[l]), xa_wkv[l])
        h = _xattn(hn, h, wq, wo, mkv, seq, mem_len)
        h, _ = _ffn_f32(h, row(ffn2_norm[l]), ffn2_w_gate[l], ffn2_w_up[l], ffn2_w_down[l], final_gain,
                        final_norm=last)
    if depth == 0:
        raise ValueError("depth must be at least 1")
    return h.reshape(batch, seq, d)
```

```python
import functools

import numpy as np
import jax
import jax.numpy as jnp
from jax import lax
from jax.experimental import pallas as pl
from jax.experimental.pallas import tpu as pltpu

F32 = jnp.float32
BF16 = jnp.bfloat16

D_MODEL = 2048
RET_HEADS = 8
RET_DK = 128
RET_DV = 128
RET_WIDTH = RET_HEADS * RET_DV
RET_CHUNK = 128
SWA_HEADS = 16
SWA_KV_HEADS = 2
SWA_HEAD_DIM = 64
SWA_WIDTH = SWA_HEADS * SWA_HEAD_DIM
SWA_KV_WIDTH = SWA_KV_HEADS * SWA_HEAD_DIM
WINDOW = 128
XA_HEADS = 4
XA_HEAD_DIM = D_MODEL // XA_HEADS
ROPE_THETA = 10000.0
EPS = 1e-6
IN_SIZES = (RET_WIDTH, RET_WIDTH, RET_WIDTH, RET_WIDTH, SWA_WIDTH, SWA_KV_WIDTH, SWA_KV_WIDTH)
IN_COLS = sum(IN_SIZES)

LANES = 128
BF16_SUBLANES = 16
VMEM_LIMIT_BYTES = 60000 * 1024

FFN_TM = 1024
FFN_TF = 512
FFN_HEAD_TF = 256
ROW_TM = 512
NORM_ROWS = 512
MIXER_PARTS = 2
MIXER_BLOCKS = 4
NEG_INF = float(np.finfo(np.float32).min)
LOG2E = float(np.log2(np.e))


def _params(n_axes):
    return pltpu.CompilerParams(
        dimension_semantics=("arbitrary",) * n_axes,
        vmem_limit_bytes=VMEM_LIMIT_BYTES,
    )


def _resident(shape):
    zeros = (0,) * len(shape)
    return pl.BlockSpec(shape, lambda *_: zeros, pipeline_mode=pl.Buffered(1))


def _rms_rows(x, gain):
    ms = jnp.mean(x * x, axis=-1, keepdims=True)
    return x * lax.rsqrt(ms + EPS) * gain


def _dot(a, b):
    return lax.dot_general(a, b, (((1,), (0,)), ((), ())), preferred_element_type=F32)


def _dot_nt(a, b):
    return lax.dot_general(a, b, (((1,), (1,)), ((), ())), preferred_element_type=F32)


def _dot_tn(a, b):
    return lax.dot_general(a, b, (((0,), (0,)), ((), ())), preferred_element_type=F32)


def _cast_block(shape, n_steps):
    rows, cols = shape
    for col_splits in (1, 2, 4, 8):
        row_blocks, rem = divmod(n_steps, col_splits)
        if rem or rows % row_blocks or cols % col_splits:
            continue
        br, bc = rows // row_blocks, cols // col_splits
        if br % BF16_SUBLANES == 0 and bc % LANES == 0:
            return br, bc, col_splits
    raise ValueError(f"no aligned {n_steps}-way split of {shape}")


def _with_casts(kernel_fn, n_in, n_out, n_cast):
    def wrapped(*refs):
        ins, rest = refs[:n_in], refs[n_in:]
        cast_in, rest = rest[:n_cast], rest[n_cast:]
        outs, rest = rest[:n_out], rest[n_out:]
        cast_out, scratch = rest[:n_cast], rest[n_cast:]
        kernel_fn(*ins, *outs, *scratch)
        for src, dst in zip(cast_in, cast_out):
            dst[...] = src[...].astype(BF16)
    return wrapped


def _cast_specs(weights, grid):
    n_steps = int(np.prod(grid))
    specs, shapes = [], []
    for w in weights:
        br, bc, col_splits = _cast_block(w.shape, n_steps)

        def index_map(*idx, col_splits=col_splits):
            step = idx[0]
            for size, i in zip(grid[1:], idx[1:]):
                step = step * size + i
            return step // col_splits, step % col_splits
        specs.append(pl.BlockSpec((br, bc), index_map))
        shapes.append(jax.ShapeDtypeStruct(w.shape, BF16))
    return specs, shapes


def _cast_specs_tiled(weights, grid):
    n_outer, n_inner = grid
    specs, shapes = [], []
    for w in weights:
        rows, cols = w.shape
        bc = cols // n_outer
        row_blocks = max(r for r in range(1, n_inner + 1) if rows % r == 0 and (rows // r) % BF16_SUBLANES == 0)
        if cols % n_outer or bc % LANES:
            raise ValueError(f"no aligned split of {w.shape} over {grid}")

        def index_map(i, j, row_blocks=row_blocks):
            return jnp.minimum(j, row_blocks - 1), i
        specs.append(pl.BlockSpec((rows // row_blocks, bc), index_map))
        shapes.append(jax.ShapeDtypeStruct(w.shape, BF16))
    return specs, shapes


def _rows_loop(n_rows, fn):
    def body(r, carry):
        fn(pl.ds(pl.multiple_of(r * NORM_ROWS, NORM_ROWS), NORM_ROWS))
        return carry
    lax.fori_loop(0, n_rows // NORM_ROWS, body, 0)


def _ffn_prologue(h_ref, g_ref, xb_ref, r_ref):
    def scale(rows):
        h = h_ref[rows, :]
        xb_ref[rows, :] = (h * g_ref[...]).astype(BF16)
        r_ref[rows, :] = lax.rsqrt(jnp.mean(h * h, axis=-1, keepdims=True) + EPS)
    _rows_loop(h_ref.shape[0], scale)


def _ffn_prologue_full(h_ref, g_ref, o_ref, xn_ref):
    def norm(rows):
        h = h_ref[rows, :]
        xn_ref[rows, :] = _rms_rows(h, g_ref[...]).astype(BF16)
        o_ref[rows, :] = h
    _rows_loop(h_ref.shape[0], norm)


def _ffn_step(xb_ref, wg, wu, wd, o_ref, *, first=None, h_ref=None, r_ref=None):
    xb = xb_ref[...]
    g = _dot(xb, wg)
    u = _dot(xb, wu)
    if r_ref is not None:
        r = r_ref[...]
        g, u = g * r, u * r
    a = (g * (0.5 / (1.0 + jnp.exp(-g))) * u).astype(BF16)
    for c in range(0, o_ref.shape[1], FFN_TF):
        cols = slice(c, c + FFN_TF)
        acc = o_ref[:, cols] if first is None else jnp.where(first, h_ref[:, cols], o_ref[:, cols])
        o_ref[:, cols] = acc + _dot(a, wd[:, cols])


def _ffn_kernel(*refs, n_ff, final_norm, aliased):
    if aliased:
        refs = refs[1:]
    h_ref, g_ref, wg_ref, wu_ref, wd_ref, fg_ref, o_ref, xb_ref, r_ref = refs
    j = pl.program_id(1)

    @pl.when(j == 0)
    def _():
        _ffn_prologue(h_ref, g_ref, xb_ref, r_ref)

    _ffn_step(xb_ref, wg_ref[...], wu_ref[...], wd_ref[...], o_ref, first=j == 0, h_ref=h_ref, r_ref=r_ref)

    if final_norm:
        @pl.when(j == n_ff - 1)
        def _():
            def norm(rows):
                o_ref[rows, :] = _rms_rows(o_ref[rows, :], fg_ref[...])
            _rows_loop(o_ref.shape[0], norm)


def _ffn(h, gain, wg, wu, wd, final_gain, *, final_norm, first_tile=0, into=None, cast_weights=()):
    t, d = h.shape
    f = wg.shape[1]
    n_ff = f // FFN_TF
    grid = (t // FFN_TM - first_tile, n_ff)
    aliased = into is not None
    tile = lambda i, j: (i + first_tile, 0)
    in_specs = [
        pl.BlockSpec((FFN_TM, d), tile),
        pl.BlockSpec((1, d), lambda i, j: (0, 0)),
        pl.BlockSpec((d, FFN_TF), lambda i, j: (0, j)),
        pl.BlockSpec((d, FFN_TF), lambda i, j: (0, j)),
        pl.BlockSpec((FFN_TF, d), lambda i, j: (j, 0)),
        pl.BlockSpec((1, d), lambda i, j: (0, 0)),
    ]
    args = (h, gain, wg, wu, wd, final_gain)
    if aliased:
        in_specs = [pl.BlockSpec(memory_space=pl.ANY)] + in_specs
        args = (into,) + args
    cast_specs, cast_shapes = _cast_specs_tiled(cast_weights, grid)
    host = functools.partial(_ffn_kernel, n_ff=n_ff, final_norm=final_norm, aliased=aliased)
    out = pl.pallas_call(
        _with_casts(host, len(in_specs), 1, len(cast_weights)),
        out_shape=[jax.ShapeDtypeStruct((t, d), F32)] + cast_shapes,
        grid=grid,
        in_specs=in_specs + cast_specs,
        out_specs=[pl.BlockSpec((FFN_TM, d), tile)] + cast_specs,
        scratch_shapes=[pltpu.VMEM((FFN_TM, d), BF16), pltpu.VMEM((FFN_TM, 1), F32)],
        input_output_aliases={0: 0} if aliased else {},
        compiler_params=_params(2),
        name="ffn",
    )(*args, *cast_weights)
    return out[0], out[1:]


def _ffn_head_kernel(h_ref, g_ref, wg_ref, wu_ref, wd_ref, fg_ref, o_ref, wg_out, wu_out, wd_out, xn_ref,
                     *, n_ff, final_norm):
    j = pl.program_id(0)

    @pl.when(j == 0)
    def _():
        _ffn_prologue_full(h_ref, g_ref, o_ref, xn_ref)

    wg, wu, wd = (w[...].astype(BF16) for w in (wg_ref, wu_ref, wd_ref))
    wg_out[...] = wg
    wu_out[...] = wu
    wd_out[...] = wd
    _ffn_step(xn_ref, wg, wu, wd, o_ref)

    if final_norm:
        @pl.when(j == n_ff - 1)
        def _():
            def norm(rows):
                o_ref[rows, :] = _rms_rows(o_ref[rows, :], fg_ref[...])
            _rows_loop(o_ref.shape[0], norm)


def _ffn_head(h, gain, wg, wu, wd, final_gain, *, final_norm):
    t, d = h.shape
    f = wg.shape[1]
    tf = FFN_HEAD_TF
    col = pl.BlockSpec((d, tf), lambda j: (0, j))
    row = pl.BlockSpec((tf, d), lambda j: (j, 0))
    vec = pl.BlockSpec((1, d), lambda j: (0, 0))
    tile0 = pl.BlockSpec((FFN_TM, d), lambda j: (0, 0))
    return pl.pallas_call(
        functools.partial(_ffn_head_kernel, n_ff=f // tf, final_norm=final_norm),
        out_shape=[jax.ShapeDtypeStruct((t, d), F32), jax.ShapeDtypeStruct(wg.shape, BF16),
                   jax.ShapeDtypeStruct(wu.shape, BF16), jax.ShapeDtypeStruct(wd.shape, BF16)],
        grid=(f // tf,),
        in_specs=[tile0, vec, col, col, row, vec],
        out_specs=[tile0, col, col, row],
        scratch_shapes=[pltpu.VMEM((FFN_TM, d), BF16)],
        compiler_params=_params(1),
        name="ffn_head",
    )(h, gain, wg, wu, wd, final_gain)


def _ffn_f32(h, gain, wg, wu, wd, final_gain, *, final_norm, cast_weights=()):
    head, wg16, wu16, wd16 = _ffn_head(h, gain, wg, wu, wd, final_gain, final_norm=final_norm)
    return _ffn(h, gain, wg16, wu16, wd16, final_gain, final_norm=final_norm, first_tile=1, into=head,
                cast_weights=cast_weights)


def _rope_tables(seq):
    pos = np.arange(seq, dtype=np.float32)

    def angles(d):
        inv = np.float32(ROPE_THETA) ** (-np.arange(0, d, 2, dtype=np.float32) / np.float32(d))
        return (pos[:, None] * inv[None, :].astype(np.float32)).astype(np.float32).astype(np.float64)

    a128 = angles(RET_DK)
    cos_r = np.concatenate([np.cos(a128), np.cos(a128)], -1)
    sin_r = np.concatenate([-np.sin(a128), np.sin(a128)], -1)
    a64 = angles(SWA_HEAD_DIM)
    c, s, z = np.cos(a64), np.sin(a64), np.zeros_like(a64)
    cos_s = np.concatenate([c, c, c, c], -1)
    sin_lo = np.concatenate([-s, z, -s, z], -1)
    sin_hi = np.concatenate([z, s, z, s], -1)
    return [jnp.asarray(v, dtype=F32) for v in (cos_r, sin_r, cos_s, sin_lo, sin_hi)]


def _retention_tables():
    c = RET_CHUNK
    heads = np.arange(RET_HEADS, dtype=np.float64)
    log_gamma = np.log1p(-np.exp2(-5.0 - heads))
    idx = np.arange(c, dtype=np.float64)
    diff = idx[:, None] - idx[None, :]
    scale = RET_DK ** -0.5
    dmat = np.where(diff[None] >= 0, np.exp(np.maximum(diff, 0.0)[None] * log_gamma[:, None, None]), 0.0)
    zeta = np.exp((c - 1.0 - idx)[None, :] * log_gamma[:, None])
    xi = np.exp((idx + 1.0)[None, :] * log_gamma[:, None])
    chunk_decay = tuple(float(v) for v in np.exp(c * log_gamma))

    def token_tile(tab):
        return np.tile(np.repeat(tab.T, RET_DK, axis=1), (ROW_TM // c, 1))
    tabs = [jnp.asarray(v, dtype=F32) for v in (dmat * scale, token_tile(xi), token_tile(zeta * scale))]
    return tabs, chunk_decay


def _inproj_kernel(h_ref, g_ref, w_ref, cr_ref, sr_ref, cs_ref, sl_ref, sh_ref, xi_ref, zeta_ref, gn_ref,
                   rq_ref, rqx_ref, rk_ref, rkz_ref, rv_ref, rg_ref, sq_ref, sk_ref, sv_ref, xn_ref):
    tm = h_ref.shape[0]

    def norm(rows):
        xn_ref[rows, :] = _rms_rows(h_ref[rows, :], g_ref[...]).astype(BF16)
    _rows_loop(tm, norm)

    xn = xn_ref[...]
    cr, sr = cr_ref[...], sr_ref[...]
    cs, sl, sh = cs_ref[...], sl_ref[...], sh_ref[...]
    half = LANES // 2
    slabs = [slice(s * LANES, (s + 1) * LANES) for s in range(RET_WIDTH // LANES)]

    def rope_ret(x):
        return x * cr + pltpu.roll(x, half, 1) * sr

    def rope_swa(x):
        return x * cs + pltpu.roll(x, LANES - half // 2, 1) * sl + pltpu.roll(x, half // 2, 1) * sh

    def project(col0, width):
        return _dot(xn, w_ref[:, col0:col0 + width])

    def ret_rotary(out_ref, scaled_ref, tab_ref):
        def epilogue(y):
            for sl_ in slabs:
                x = rope_ret(y[:, sl_])
                out_ref[:, sl_] = x.astype(BF16)
                scaled_ref[:, sl_] = (x * tab_ref[:, sl_]).astype(BF16)
        return epilogue

    def ret_values(y):
        for sl_ in slabs:
            rv_ref[:, sl_] = y[:, sl_].astype(BF16)

    def ret_gate(y):
        for sl_ in slabs:
            g = y[:, sl_]
            rg_ref[:, sl_] = (g * (1.0 / (1.0 + jnp.exp(-g))) * gn_ref[:, sl_]).astype(BF16)

    def swa_queries(y):
        scale = SWA_HEAD_DIM ** -0.5 * LOG2E
        for sl_ in slabs:
            sq_ref[:, sl_] = (rope_swa(y[:, sl_]) * scale).astype(BF16)

    def swa_keys_values(y):
        lo = lax.broadcasted_iota(jnp.int32, (tm, LANES), 1) < half
        for transposed, out_ref, x in ((False, sk_ref, rope_swa(y[:, :LANES])), (True, sv_ref, y[:, LANES:])):
            xr = pltpu.roll(x, half, 1)
            variants = (jnp.where(lo, x, 0.0), jnp.where(lo, 0.0, xr),
                        jnp.where(lo, xr, 0.0), jnp.where(lo, 0.0, x))
            for s, v in enumerate(variants):
                if transposed:
                    out_ref[s * LANES:(s + 1) * LANES, :] = v.T.astype(BF16)
                else:
                    out_ref[:, s * LANES:(s + 1) * LANES] = v.astype(BF16)

    epilogues = (ret_rotary(rq_ref, rqx_ref, xi_ref), ret_rotary(rk_ref, rkz_ref, zeta_ref), ret_values, ret_gate,
                 swa_queries, swa_keys_values)
    col0 = np.cumsum((0,) + IN_SIZES[:-2])
    widths = IN_SIZES[:-2] + (2 * SWA_KV_WIDTH,)
    order = (5, 0, 1, 3, 4, 2)
    for r in order:
        epilogues[r](project(int(col0[r]), widths[r]))


def _inproj(h, gain, w_in, gn_gain, xi_tile, zeta_tile, seq):
    t, d = h.shape
    tables = _rope_tables(seq)
    tiles_per_seq = seq // ROW_TM
    tab_spec = pl.BlockSpec((ROW_TM, LANES), lambda i: (i % tiles_per_seq, 0))

    def row_spec(width):
        return pl.BlockSpec((ROW_TM, width), lambda i: (i, 0))

    widths = (RET_WIDTH,) * 6 + (SWA_WIDTH, 4 * LANES)
    return pl.pallas_call(
        _inproj_kernel,
        out_shape=[jax.ShapeDtypeStruct((t, w), BF16) for w in widths]
        + [jax.ShapeDtypeStruct((4 * LANES, t), BF16)],
        grid=(t // ROW_TM,),
        in_specs=[row_spec(d), _resident((1, d)), _resident((d, IN_COLS))] + [tab_spec] * 5
        + [_resident((ROW_TM, RET_WIDTH))] * 2 + [_resident((1, RET_WIDTH))],
        out_specs=[row_spec(w) for w in widths] + [pl.BlockSpec((4 * LANES, ROW_TM), lambda i: (0, i))],
        scratch_shapes=[pltpu.VMEM((ROW_TM, d), BF16)],
        compiler_params=_params(1),
        name="inproj",
    )(h, gain, w_in, *tables, xi_tile, zeta_tile, gn_gain)


def _mixers_kernel(q_ref, qx_ref, k_ref, kz_ref, v_ref, g_ref, dmat_ref,
                   sink_ref, sq_ref, kp_ref, kc_ref, vp_ref, vc_ref, ret_ref, swa_ref, state_ref, *, chunk_decay):
    n = pl.program_id(1)
    w = WINDOW

    @pl.when(n == 0)
    def _():
        state_ref[...] = jnp.zeros_like(state_ref)

    pairs = SWA_HEADS // SWA_KV_HEADS // 2
    key = lax.broadcasted_iota(jnp.int32, (2 * w, pairs * w), 0)
    qry = lax.broadcasted_iota(jnp.int32, (2 * w, pairs * w), 1) % w
    band = (key > qry) & (key <= qry + w)
    bias = jnp.where(band, 0.0, NEG_INF)
    bias_first = jnp.where(band & (key >= jnp.where(n == 0, w, 0)), 0.0, NEG_INF)

    hs = [slice(h * RET_DK, (h + 1) * RET_DK) for h in range(RET_HEADS)]
    lane_blk = lambda c: slice(c * LANES, (c + 1) * LANES)
    slabs = {g: [g * pairs + p for p in range(pairs)] for g in range(SWA_KV_HEADS)}

    for blk in range(MIXER_BLOCKS):
        rows = slice(blk * w, (blk + 1) * w)
        mask_bias = bias if blk else bias_first

        def prev_keys(c):
            return kc_ref[(blk - 1) * w:blk * w, lane_blk(c)] if blk else kp_ref[:, lane_blk(c)]

        def prev_values_t(c):
            return vc_ref[lane_blk(c), (blk - 1) * w:blk * w] if blk else vp_ref[lane_blk(c), :]

        for part in range(MIXER_PARTS):
            heads = range(part * RET_HEADS // MIXER_PARTS, (part + 1) * RET_HEADS // MIXER_PARTS)
            groups = range(part * SWA_KV_HEADS // MIXER_PARTS, (part + 1) * SWA_KV_HEADS // MIXER_PARTS)
            phases = [(g, e) for g in groups for e in range(2)]

            s_ret = {h: _dot_nt(q_ref[rows, hs[h]], k_ref[rows, hs[h]]) for h in heads}
            q_swa = {g: jnp.concatenate([sq_ref[rows, lane_blk(sl)] for sl in slabs[g]], 0) for g in groups}
            s_swa = []
            for g, e in phases:
                c = 2 * g + e
                k = jnp.concatenate([prev_keys(c), kc_ref[rows, lane_blk(c)]], 0)
                s_swa.append(_dot_nt(k, q_swa[g]) + mask_bias)

            states = {h: state_ref[h] for h in heads}
            ret = {}
            for h in heads:
                lhs = jnp.concatenate([(s_ret[h] * dmat_ref[h]).astype(BF16), qx_ref[rows, hs[h]]], 1)
                rhs = jnp.concatenate([v_ref[rows, hs[h]], states[h].astype(BF16)], 0)
                ret[h] = _dot(lhs, rhs)
            for h in heads:
                state_ref[h] = states[h] * chunk_decay[h] + _dot_tn(kz_ref[rows, hs[h]], v_ref[rows, hs[h]])

            pv = {}
            for (g, e), s in zip(phases, s_swa):
                c = 2 * g + e
                sink = jnp.concatenate([jnp.full((1, w), sink_ref[2 * sl + e] * LOG2E, F32) for sl in slabs[g]], 1)
                m = jnp.maximum(jnp.max(s, axis=0, keepdims=True), sink)
                p = jnp.exp2(s - m)
                inv = 1.0 / (jnp.sum(p, axis=0, keepdims=True) + jnp.exp2(sink - m))
                v_t = jnp.concatenate([prev_values_t(c), vc_ref[lane_blk(c), rows]], 1)
                pv[g, e] = _dot(v_t, p.astype(BF16)) * inv

            for h in heads:
                mu = jnp.mean(ret[h], axis=-1, keepdims=True)
                cen = ret[h] - mu
                var = jnp.mean(cen * cen, axis=-1, keepdims=True)
                ret_ref[rows, hs[h]] = (cen * lax.rsqrt(var + EPS) * g_ref[rows, hs[h]].astype(F32)).astype(BF16)

            for g in groups:
                acc = pv[g, 0] + pv[g, 1]
                for i, sl in enumerate(slabs[g]):
                    swa_ref[lane_blk(sl), rows] = acc[:, i * w:(i + 1) * w].astype(BF16)


def _mixers(rq, rqx, rk, rkz, rv, rg, dmat, chunk_decay, sq, sk4, sv4_t, sinks, batch, seq, cast_weights):
    assert RET_CHUNK == WINDOW
    t = rq.shape[0]
    rows = MIXER_BLOCKS * WINDOW
    n_steps = seq // rows
    grid = (batch, n_steps)
    cur = lambda b, n: (b * n_steps + n, 0)
    cur_t = lambda b, n: (0, b * n_steps + n)
    prev_idx = lambda b, n: b * n_steps * MIXER_BLOCKS + jnp.maximum(n * MIXER_BLOCKS - 1, 0)
    blk = pl.BlockSpec((rows, RET_WIDTH), cur)
    in_specs = [
        blk, blk, blk, blk, blk, blk, _resident((RET_HEADS, RET_CHUNK, RET_CHUNK)),
        pl.BlockSpec(memory_space=pltpu.SMEM),
        pl.BlockSpec((rows, SWA_WIDTH), cur),
        pl.BlockSpec((WINDOW, 4 * LANES), lambda b, n: (prev_idx(b, n), 0)),
        pl.BlockSpec((rows, 4 * LANES), cur),
        pl.BlockSpec((4 * LANES, WINDOW), lambda b, n: (0, prev_idx(b, n))),
        pl.BlockSpec((4 * LANES, rows), cur_t),
    ]
    cast_specs, cast_shapes = _cast_specs(cast_weights, grid)
    host = functools.partial(_mixers_kernel, chunk_decay=chunk_decay)
    out = pl.pallas_call(
        _with_casts(host, len(in_specs), 2, len(cast_weights)),
        out_shape=[jax.ShapeDtypeStruct((t, RET_WIDTH), BF16), jax.ShapeDtypeStruct((SWA_WIDTH, t), BF16)]
        + cast_shapes,
        grid=grid,
        in_specs=in_specs + cast_specs,
        out_specs=[blk, pl.BlockSpec((SWA_WIDTH, rows), cur_t)] + cast_specs,
        scratch_shapes=[pltpu.VMEM((RET_HEADS, RET_DK, RET_DV), F32)],
        compiler_params=_params(2),
        name="mixers",
    )(rq, rqx, rk, rkz, rv, rg, dmat, sinks, sq, sk4, sk4, sv4_t, sv4_t, *cast_weights)
    return out[0], out[1], out[2:]


def _outproj_kernel(h_ref, ret_ref, swa_ref, w_ref, g_ref, mem_ref, mem_gain_ref, wkv_ref,
                    h2_ref, hn_ref, mkv_ref, memn_ref):
    tm = h_ref.shape[0]

    @pl.when(pl.program_id(0) == 0)
    def _():
        memn_ref[...] = _rms_rows(mem_ref[...], mem_gain_ref[...]).astype(BF16)

    y = _dot(ret_ref[...], w_ref[:RET_WIDTH, :]) + _dot_tn(swa_ref[...], w_ref[RET_WIDTH:, :])
    h2_ref[...] = h_ref[...] + y
    mkv_ref[...] = _dot(memn_ref[...], wkv_ref[...].astype(BF16)).astype(BF16)

    def body(r, carry):
        rows = pl.ds(pl.multiple_of(r * NORM_ROWS, NORM_ROWS), NORM_ROWS)
        hn_ref[rows, :] = _rms_rows(h2_ref[rows, :], g_ref[...]).astype(BF16)
        return carry
    lax.fori_loop(0, tm // NORM_ROWS, body, 0)


def _outproj(h, ret, swa, w_out, gain, mem, mem_gain, wkv, cast_weights):
    t, d = h.shape
    n_steps = t // ROW_TM
    kv_cols = wkv.shape[1] // n_steps
    row = lambda width: pl.BlockSpec((ROW_TM, width), lambda i: (i, 0))
    in_specs = [row(d), row(RET_WIDTH), pl.BlockSpec((SWA_WIDTH, ROW_TM), lambda i: (0, i)),
                _resident(w_out.shape), _resident((1, d)),
                _resident(mem.shape), _resident(mem_gain.shape),
                pl.BlockSpec((wkv.shape[0], kv_cols), lambda i: (0, i))]
    cast_specs, cast_shapes = _cast_specs(cast_weights, (n_steps,))
    out = pl.pallas_call(
        _with_casts(_outproj_kernel, len(in_specs), 3, len(cast_weights)),
        out_shape=[jax.ShapeDtypeStruct((t, d), F32), jax.ShapeDtypeStruct((t, d), BF16),
                   jax.ShapeDtypeStruct((mem.shape[0], wkv.shape[1]), BF16)] + cast_shapes,
        grid=(n_steps,),
        in_specs=in_specs + cast_specs,
        out_specs=[row(d), row(d), pl.BlockSpec((mem.shape[0], kv_cols), lambda i: (0, i))] + cast_specs,
        scratch_shapes=[pltpu.VMEM(mem.shape, BF16)],
        compiler_params=_params(1),
        name="outproj",
    )(h, ret, swa, w_out, gain, mem, mem_gain, wkv, *cast_weights)
    return out[0], out[1], out[2], out[3:]


def _xattn_kernel(hn_ref, h_ref, wq_ref, wo_ref, k_ref, v_ref, o_ref):
    q = _dot(hn_ref[...], wq_ref[...]).astype(BF16)
    scale = XA_HEAD_DIM ** -0.5
    heads = [slice(hd * XA_HEAD_DIM, (hd + 1) * XA_HEAD_DIM) for hd in range(XA_HEADS)]
    scores = [_dot_nt(q[:, hs], k_ref[:, hs]) * scale for hs in heads]
    probs, invs = [], []
    for s in scores:
        p = jnp.exp(s - jnp.max(s, axis=-1, keepdims=True))
        probs.append(p.astype(BF16))
        invs.append(1.0 / jnp.sum(p, axis=-1, keepdims=True))
    out = h_ref[...]
    for hs, p, inv in zip(heads, probs, invs):
        att = (_dot(p, v_ref[:, hs]) * inv).astype(BF16)
        out = out + _dot(att, wo_ref[hs, :])
    o_ref[...] = out


def _xattn(hn, h, wq, wo, mkv, seq, mem_len):
    t, d = h.shape
    tiles_per_seq = seq // ROW_TM
    row = lambda i: (i, 0)
    return pl.pallas_call(
        _xattn_kernel,
        out_shape=jax.ShapeDtypeStruct((t, d), F32),
        grid=(t // ROW_TM,),
        in_specs=[
            pl.BlockSpec((ROW_TM, d), row),
            pl.BlockSpec((ROW_TM, d), row),
            _resident((d, d)),
            _resident((d, d)),
            pl.BlockSpec((mem_len, d), lambda i: (i // tiles_per_seq, 0)),
            pl.BlockSpec((mem_len, d), lambda i: (i // tiles_per_seq, 1)),
        ],
        out_specs=pl.BlockSpec((ROW_TM, d), row),
        compiler_params=_params(1),
        name="xattn",
    )(hn, h, wq, wo, mkv, mkv)


def kernel(x, mem, ffn1_norm, ffn1_w_gate, ffn1_w_up, ffn1_w_down, mix_norm, w_in, ret_gn_gain, swa_sinks,
           w_out, xa_norm, mem_norm, xa_wq, xa_wkv, xa_wo, ffn2_norm, ffn2_w_gate, ffn2_w_up, ffn2_w_down,
           final_norm):
    batch, seq, d = x.shape
    mem_len = mem.shape[1]
    depth = ffn1_norm.shape[0]
    h = x.reshape(batch * seq, d)
    mem2 = mem.reshape(batch * mem_len, d)
    row = lambda g: g.reshape(1, -1).astype(F32)
    final_gain = row(final_norm)

    for l in range(depth):
        last = l == depth - 1
        h, (w_mix,) = _ffn_f32(h, row(ffn1_norm[l]), ffn1_w_gate[l], ffn1_w_up[l], ffn1_w_down[l], final_gain,
                               final_norm=False, cast_weights=[w_in[l]])
        (dmat, xi_tile, zeta_tile), chunk_decay = _retention_tables()
        rq, rqx, rk, rkz, rv, rg, sq, sk4, sv4_t = _inproj(h, row(mix_norm[l]), w_mix, row(ret_gn_gain[l]),
                                                           xi_tile, zeta_tile, seq)
        ret, swa, (wo_mix,) = _mixers(
            rq, rqx, rk, rkz, rv, rg, dmat, chunk_decay, sq, sk4, sv4_t, swa_sinks[l].astype(F32), batch, seq,
            [w_out[l]])
        h, hn, mkv, (wq, wo) = _outproj(h, ret, swa, wo_mix, row(xa_norm[l]), mem2, row(mem_norm[l]), xa_wkv[l],
                                        [xa_wq[l], xa_wo[l]])
        h = _xattn(hn, h, wq, wo, mkv, seq, mem_len)
        h, _ = _ffn_f32(h, row(ffn2_norm[l]), ffn2_w_gate[l], ffn2_w_up[l], ffn2_w_down[l], final_gain,
                        final_norm=last)
    if depth == 0:
        raise ValueError("depth must be at least 1")
    return h.reshape(batch, seq, d)
```

```python
import functools

import numpy as np
import jax
import jax.numpy as jnp
from jax import lax
from jax.experimental import pallas as pl
from jax.experimental.pallas import tpu as pltpu

F32 = jnp.float32
BF16 = jnp.bfloat16

D_MODEL = 2048
RET_HEADS = 8
RET_DK = 128
RET_DV = 128
RET_WIDTH = RET_HEADS * RET_DV
RET_CHUNK = 128
SWA_HEADS = 16
SWA_KV_HEADS = 2
SWA_HEAD_DIM = 64
SWA_WIDTH = SWA_HEADS * SWA_HEAD_DIM
SWA_KV_WIDTH = SWA_KV_HEADS * SWA_HEAD_DIM
WINDOW = 128
XA_HEADS = 4
XA_HEAD_DIM = D_MODEL // XA_HEADS
ROPE_THETA = 10000.0
EPS = 1e-6
IN_SIZES = (RET_WIDTH, RET_WIDTH, RET_WIDTH, RET_WIDTH, SWA_WIDTH, SWA_KV_WIDTH, SWA_KV_WIDTH)
IN_COLS = sum(IN_SIZES)

LANES = 128
BF16_SUBLANES = 16
VMEM_LIMIT_BYTES = 60000 * 1024

FFN_TM = 1024
FFN_TF = 512
FFN_HEAD_TF = 256
ROW_TM = 512
NORM_ROWS = 512
MIXER_PARTS = 2
MIXER_BLOCKS = 4
NEG_INF = float(np.finfo(np.float32).min)
LOG2E = float(np.log2(np.e))


def _params(n_axes):
    return pltpu.CompilerParams(
        dimension_semantics=("arbitrary",) * n_axes,
        vmem_limit_bytes=VMEM_LIMIT_BYTES,
    )


def _resident(shape):
    zeros = (0,) * len(shape)
    return pl.BlockSpec(shape, lambda *_: zeros, pipeline_mode=pl.Buffered(1))


def _rms_rows(x, gain):
    ms = jnp.mean(x * x, axis=-1, keepdims=True)
    return x * lax.rsqrt(ms + EPS) * gain


def _dot(a, b):
    return lax.dot_general(a, b, (((1,), (0,)), ((), ())), preferred_element_type=F32)


def _dot_nt(a, b):
    return lax.dot_general(a, b, (((1,), (1,)), ((), ())), preferred_element_type=F32)


def _dot_tn(a, b):
    return lax.dot_general(a, b, (((0,), (0,)), ((), ())), preferred_element_type=F32)


def _cast_block(shape, n_steps):
    rows, cols = shape
    for col_splits in (1, 2, 4, 8):
        row_blocks, rem = divmod(n_steps, col_splits)
        if rem or rows % row_blocks or cols % col_splits:
            continue
        br, bc = rows // row_blocks, cols // col_splits
        if br % BF16_SUBLANES == 0 and bc % LANES == 0:
            return br, bc, col_splits
    raise ValueError(f"no aligned {n_steps}-way split of {shape}")


def _with_casts(kernel_fn, n_in, n_out, n_cast):
    def wrapped(*refs):
        ins, rest = refs[:n_in], refs[n_in:]
        cast_in, rest = rest[:n_cast], rest[n_cast:]
        outs, rest = rest[:n_out], rest[n_out:]
        cast_out, scratch = rest[:n_cast], rest[n_cast:]
        kernel_fn(*ins, *outs, *scratch)
        for src, dst in zip(cast_in, cast_out):
            dst[...] = src[...].astype(BF16)
    return wrapped


def _cast_specs(weights, grid):
    n_steps = int(np.prod(grid))
    specs, shapes = [], []
    for w in weights:
        br, bc, col_splits = _cast_block(w.shape, n_steps)

        def index_map(*idx, col_splits=col_splits):
            step = idx[0]
            for size, i in zip(grid[1:], idx[1:]):
                step = step * size + i
            return step // col_splits, step % col_splits
        specs.append(pl.BlockSpec((br, bc), index_map))
        shapes.append(jax.ShapeDtypeStruct(w.shape, BF16))
    return specs, shapes


def _cast_specs_tiled(weights, grid):
    n_outer, n_inner = grid
    specs, shapes = [], []
    for w in weights:
        rows, cols = w.shape
        bc = cols // n_outer
        row_blocks = max(r for r in range(1, n_inner + 1) if rows % r == 0 and (rows // r) % BF16_SUBLANES == 0)
        if cols % n_outer or bc % LANES:
            raise ValueError(f"no aligned split of {w.shape} over {grid}")

        def index_map(i, j, row_blocks=row_blocks):
            return jnp.minimum(j, row_blocks - 1), i
        specs.append(pl.BlockSpec((rows // row_blocks, bc), index_map))
        shapes.append(jax.ShapeDtypeStruct(w.shape, BF16))
    return specs, shapes


def _rows_loop(n_rows, fn):
    def body(r, carry):
        fn(pl.ds(pl.multiple_of(r * NORM_ROWS, NORM_ROWS), NORM_ROWS))
        return carry
    lax.fori_loop(0, n_rows // NORM_ROWS, body, 0)


def _ffn_prologue(h_ref, g_ref, xb_ref, r_ref):
    def scale(rows):
        h = h_ref[rows, :]
        xb_ref[rows, :] = (h * g_ref[...]).astype(BF16)
        r_ref[rows, :] = lax.rsqrt(jnp.mean(h * h, axis=-1, keepdims=True) + EPS)
    _rows_loop(h_ref.shape[0], scale)


def _ffn_prologue_full(h_ref, g_ref, o_ref, xn_ref):
    def norm(rows):
        h = h_ref[rows, :]
        xn_ref[rows, :] = _rms_rows(h, g_ref[...]).astype(BF16)
        o_ref[rows, :] = h
    _rows_loop(h_ref.shape[0], norm)


def _ffn_step(xb_ref, wg, wu, wd, o_ref, *, first=None, h_ref=None, r_ref=None):
    xb = xb_ref[...]
    g = _dot(xb, wg)
    u = _dot(xb, wu)
    if r_ref is not None:
        r = r_ref[...]
        g, u = g * r, u * r
    a = (g * (0.5 / (1.0 + jnp.exp(-g))) * u).astype(BF16)
    for c in range(0, o_ref.shape[1], FFN_TF):
        cols = slice(c, c + FFN_TF)
        acc = o_ref[:, cols] if first is None else jnp.where(first, h_ref[:, cols], o_ref[:, cols])
        o_ref[:, cols] = acc + _dot(a, wd[:, cols])


def _ffn_kernel(*refs, n_ff, final_norm, aliased):
    if aliased:
        refs = refs[1:]
    h_ref, g_ref, wg_ref, wu_ref, wd_ref, fg_ref, o_ref, xb_ref, r_ref = refs
    j = pl.program_id(1)

    @pl.when(j == 0)
    def _():
        _ffn_prologue(h_ref, g_ref, xb_ref, r_ref)

    _ffn_step(xb_ref, wg_ref[...], wu_ref[...], wd_ref[...], o_ref, first=j == 0, h_ref=h_ref, r_ref=r_ref)

    if final_norm:
        @pl.when(j == n_ff - 1)
        def _():
            def norm(rows):
                o_ref[rows, :] = _rms_rows(o_ref[rows, :], fg_ref[...])
            _rows_loop(o_ref.shape[0], norm)


def _ffn(h, gain, wg, wu, wd, final_gain, *, final_norm, first_tile=0, into=None, cast_weights=()):
    t, d = h.shape
    f = wg.shape[1]
    n_ff = f // FFN_TF
    grid = (t // FFN_TM - first_tile, n_ff)
    aliased = into is not None
    tile = lambda i, j: (i + first_tile, 0)
    in_specs = [
        pl.BlockSpec((FFN_TM, d), tile),
        pl.BlockSpec((1, d), lambda i, j: (0, 0)),
        pl.BlockSpec((d, FFN_TF), lambda i, j: (0, j)),
        pl.BlockSpec((d, FFN_TF), lambda i, j: (0, j)),
        pl.BlockSpec((FFN_TF, d), lambda i, j: (j, 0)),
        pl.BlockSpec((1, d), lambda i, j: (0, 0)),
    ]
    args = (h, gain, wg, wu, wd, final_gain)
    if aliased:
        in_specs = [pl.BlockSpec(memory_space=pl.ANY)] + in_specs
        args = (into,) + args
    cast_specs, cast_shapes = _cast_specs_tiled(cast_weights, grid)
    host = functools.partial(_ffn_kernel, n_ff=n_ff, final_norm=final_norm, aliased=aliased)
    out = pl.pallas_call(
        _with_casts(host, len(in_specs), 1, len(cast_weights)),
        out_shape=[jax.ShapeDtypeStruct((t, d), F32)] + cast_shapes,
        grid=grid,
        in_specs=in_specs + cast_specs,
        out_specs=[pl.BlockSpec((FFN_TM, d), tile)] + cast_specs,
        scratch_shapes=[pltpu.VMEM((FFN_TM, d), BF16), pltpu.VMEM((FFN_TM, 1), F32)],
        input_output_aliases={0: 0} if aliased else {},
        compiler_params=_params(2),
        name="ffn",
    )(*args, *cast_weights)
    return out[0], out[1:]


def _ffn_head_kernel(h_ref, g_ref, wg_ref, wu_ref, wd_ref, fg_ref, o_ref, wg_out, wu_out, wd_out, xn_ref,
                     *, n_ff, final_norm):
    j = pl.program_id(0)

    @pl.when(j == 0)
    def _():
        _ffn_prologue_full(h_ref, g_ref, o_ref, xn_ref)

    wg, wu, wd = (w[...].astype(BF16) for w in (wg_ref, wu_ref, wd_ref))
    wg_out[...] = wg
    wu_out[...] = wu
    wd_out[...] = wd
    _ffn_step(xn_ref, wg, wu, wd, o_ref)

    if final_norm:
        @pl.when(j == n_ff - 1)
        def _():
            def norm(rows):
                o_ref[rows, :] = _rms_rows(o_ref[rows, :], fg_ref[...])
            _rows_loop(o_ref.shape[0], norm)


def _ffn_head(h, gain, wg, wu, wd, final_gain, *, final_norm):
    t, d = h.shape
    f = wg.shape[1]
    tf = FFN_HEAD_TF
    col = pl.BlockSpec((d, tf), lambda j: (0, j))
    row = pl.BlockSpec((tf, d), lambda j: (j, 0))
    vec = pl.BlockSpec((1, d), lambda j: (0, 0))
    tile0 = pl.BlockSpec((FFN_TM, d), lambda j: (0, 0))
    return pl.pallas_call(
        functools.partial(_ffn_head_kernel, n_ff=f // tf, final_norm=final_norm),
        out_shape=[jax.ShapeDtypeStruct((t, d), F32), jax.ShapeDtypeStruct(wg.shape, BF16),
                   jax.ShapeDtypeStruct(wu.shape, BF16), jax.ShapeDtypeStruct(wd.shape, BF16)],
        grid=(f // tf,),
        in_specs=[tile0, vec, col, col, row, vec],
        out_specs=[tile0, col, col, row],
        scratch_shapes=[pltpu.VMEM((FFN_TM, d), BF16)],
        compiler_params=_params(1),
        name="ffn_head",
    )(h, gain, wg, wu, wd, final_gain)


def _ffn_f32(h, gain, wg, wu, wd, final_gain, *, final_norm, cast_weights=()):
    head, wg16, wu16, wd16 = _ffn_head(h, gain, wg, wu, wd, final_gain, final_norm=final_norm)
    return _ffn(h, gain, wg16, wu16, wd16, final_gain, final_norm=final_norm, first_tile=1, into=head,
                cast_weights=cast_weights)


def _rope_tables(seq):
    pos = np.arange(seq, dtype=np.float32)

    def angles(d):
        inv = np.float32(ROPE_THETA) ** (-np.arange(0, d, 2, dtype=np.float32) / np.float32(d))
        return (pos[:, None] * inv[None, :].astype(np.float32)).astype(np.float32).astype(np.float64)

    a128 = angles(RET_DK)
    cos_r = np.concatenate([np.cos(a128), np.cos(a128)], -1)
    sin_r = np.concatenate([-np.sin(a128), np.sin(a128)], -1)
    a64 = angles(SWA_HEAD_DIM)
    c, s, z = np.cos(a64), np.sin(a64), np.zeros_like(a64)
    cos_s = np.concatenate([c, c, c, c], -1)
    sin_lo = np.concatenate([-s, z, -s, z], -1)
    sin_hi = np.concatenate([z, s, z, s], -1)
    return [jnp.asarray(v, dtype=F32) for v in (cos_r, sin_r, cos_s, sin_lo, sin_hi)]


def _retention_tables():
    c = RET_CHUNK
    heads = np.arange(RET_HEADS, dtype=np.float64)
    log_gamma = np.log1p(-np.exp2(-5.0 - heads))
    idx = np.arange(c, dtype=np.float64)
    diff = idx[:, None] - idx[None, :]
    scale = RET_DK ** -0.5
    dmat = np.where(diff[None] >= 0, np.exp(np.maximum(diff, 0.0)[None] * log_gamma[:, None, None]), 0.0)
    zeta = np.exp((c - 1.0 - idx)[None, :] * log_gamma[:, None])
    xi = np.exp((idx + 1.0)[None, :] * log_gamma[:, None])
    chunk_decay = tuple(float(v) for v in np.exp(c * log_gamma))

    def token_tile(tab):
        return np.tile(np.repeat(tab.T, RET_DK, axis=1), (ROW_TM // c, 1))
    tabs = [jnp.asarray(v, dtype=F32) for v in (dmat * scale, token_tile(xi), token_tile(zeta * scale))]
    return tabs, chunk_decay


def _inproj_kernel(h_ref, g_ref, w_ref, cr_ref, sr_ref, cs_ref, sl_ref, sh_ref, xi_ref, zeta_ref, gn_ref,
                   rq_ref, rqx_ref, rk_ref, rkz_ref, rv_ref, rg_ref, sq_ref, sk_ref, sv_ref, xn_ref):
    tm = h_ref.shape[0]

    def norm(rows):
        xn_ref[rows, :] = _rms_rows(h_ref[rows, :], g_ref[...]).astype(BF16)
    _rows_loop(tm, norm)

    xn = xn_ref[...]
    cr, sr = cr_ref[...], sr_ref[...]
    cs, sl, sh = cs_ref[...], sl_ref[...], sh_ref[...]
    half = LANES // 2
    slabs = [slice(s * LANES, (s + 1) * LANES) for s in range(RET_WIDTH // LANES)]

    def rope_ret(x):
        return x * cr + pltpu.roll(x, half, 1) * sr

    def rope_swa(x):
        return x * cs + pltpu.roll(x, LANES - half // 2, 1) * sl + pltpu.roll(x, half // 2, 1) * sh

    def project(col0, width):
        return _dot(xn, w_ref[:, col0:col0 + width])

    def ret_rotary(out_ref, scaled_ref, tab_ref):
        def epilogue(y):
            for sl_ in slabs:
                x = rope_ret(y[:, sl_])
                out_ref[:, sl_] = x.astype(BF16)
                scaled_ref[:, sl_] = (x * tab_ref[:, sl_]).astype(BF16)
        return epilogue

    def ret_values(y):
        for sl_ in slabs:
            rv_ref[:, sl_] = y[:, sl_].astype(BF16)

    def ret_gate(y):
        for sl_ in slabs:
            g = y[:, sl_]
            rg_ref[:, sl_] = (g * (1.0 / (1.0 + jnp.exp(-g))) * gn_ref[:, sl_]).astype(BF16)

    def swa_queries(y):
        scale = SWA_HEAD_DIM ** -0.5 * LOG2E
        for sl_ in slabs:
            sq_ref[:, sl_] = (rope_swa(y[:, sl_]) * scale).astype(BF16)

    def swa_keys_values(y):
        lo = lax.broadcasted_iota(jnp.int32, (tm, LANES), 1) < half
        for transposed, out_ref, x in ((False, sk_ref, rope_swa(y[:, :LANES])), (True, sv_ref, y[:, LANES:])):
            xr = pltpu.roll(x, half, 1)
            variants = (jnp.where(lo, x, 0.0), jnp.where(lo, 0.0, xr),
                        jnp.where(lo, xr, 0.0), jnp.where(lo, 0.0, x))
            for s, v in enumerate(variants):
                if transposed:
                    out_ref[s * LANES:(s + 1) * LANES, :] = v.T.astype(BF16)
                else:
                    out_ref[:, s * LANES:(s + 1) * LANES] = v.astype(BF16)

    epilogues = (ret_rotary(rq_ref, rqx_ref, xi_ref), ret_rotary(rk_ref, rkz_ref, zeta_ref), ret_values, ret_gate,
                 swa_queries, swa_keys_values)
    col0 = np.cumsum((0,) + IN_SIZES[:-2])
    widths = IN_SIZES[:-2] + (2 * SWA_KV_WIDTH,)
    order = (5, 0, 1, 3, 4, 2)
    for r in order:
        epilogues[r](project(int(col0[r]), widths[r]))


def _inproj(h, gain, w_in, gn_gain, xi_tile, zeta_tile, seq):
    t, d = h.shape
    tables = _rope_tables(seq)
    tiles_per_seq = seq // ROW_TM
    tab_spec = pl.BlockSpec((ROW_TM, LANES), lambda i: (i % tiles_per_seq, 0))

    def row_spec(width):
        return pl.BlockSpec((ROW_TM, width), lambda i: (i, 0))

    widths = (RET_WIDTH,) * 6 + (SWA_WIDTH, 4 * LANES)
    return pl.pallas_call(
        _inproj_kernel,
        out_shape=[jax.ShapeDtypeStruct((t, w), BF16) for w in widths]
        + [jax.ShapeDtypeStruct((4 * LANES, t), BF16)],
        grid=(t // ROW_TM,),
        in_specs=[row_spec(d), _resident((1, d)), _resident((d, IN_COLS))] + [tab_spec] * 5
        + [_resident((ROW_TM, RET_WIDTH))] * 2 + [_resident((1, RET_WIDTH))],
        out_specs=[row_spec(w) for w in widths] + [pl.BlockSpec((4 * LANES, ROW_TM), lambda i: (0, i))],
        scratch_shapes=[pltpu.VMEM((ROW_TM, d), BF16)],
        compiler_params=_params(1),
        name="inproj",
    )(h, gain, w_in, *tables, xi_tile, zeta_tile, gn_gain)


def _mixers_kernel(q_ref, qx_ref, k_ref, kz_ref, v_ref, g_ref, dmat_ref,
                   sink_ref, sq_ref, kp_ref, kc_ref, vp_ref, vc_ref, ret_ref, swa_ref, state_ref, *, chunk_decay):
    n = pl.program_id(1)
    w = WINDOW

    @pl.when(n == 0)
    def _():
        state_ref[...] = jnp.zeros_like(state_ref)

    pairs = SWA_HEADS // SWA_KV_HEADS // 2
    key = lax.broadcasted_iota(jnp.int32, (2 * w, pairs * w), 0)
    qry = lax.broadcasted_iota(jnp.int32, (2 * w, pairs * w), 1) % w
    band = (key > qry) & (key <= qry + w)
    bias = jnp.where(band, 0.0, NEG_INF)
    bias_first = jnp.where(band & (key >= jnp.where(n == 0, w, 0)), 0.0, NEG_INF)

    hs = [slice(h * RET_DK, (h + 1) * RET_DK) for h in range(RET_HEADS)]
    lane_blk = lambda c: slice(c * LANES, (c + 1) * LANES)
    slabs = {g: [g * pairs + p for p in range(pairs)] for g in range(SWA_KV_HEADS)}

    for blk in range(MIXER_BLOCKS):
        rows = slice(blk * w, (blk + 1) * w)
        mask_bias = bias if blk else bias_first

        def prev_keys(c):
            return kc_ref[(blk - 1) * w:blk * w, lane_blk(c)] if blk else kp_ref[:, lane_blk(c)]

        def prev_values_t(c):
            return vc_ref[lane_blk(c), (blk - 1) * w:blk * w] if blk else vp_ref[lane_blk(c), :]

        for part in range(MIXER_PARTS):
            heads = range(part * RET_HEADS // MIXER_PARTS, (part + 1) * RET_HEADS // MIXER_PARTS)
            groups = range(part * SWA_KV_HEADS // MIXER_PARTS, (part + 1) * SWA_KV_HEADS // MIXER_PARTS)
            phases = [(g, e) for g in groups for e in range(2)]

            s_ret = {h: _dot_nt(q_ref[rows, hs[h]], k_ref[rows, hs[h]]) for h in heads}
            q_swa = {g: jnp.concatenate([sq_ref[rows, lane_blk(sl)] for sl in slabs[g]], 0) for g in groups}
            s_swa = []
            for g, e in phases:
                c = 2 * g + e
                k = jnp.concatenate([prev_keys(c), kc_ref[rows, lane_blk(c)]], 0)
                s_swa.append(_dot_nt(k, q_swa[g]) + mask_bias)

            states = {h: state_ref[h] for h in heads}
            ret = {}
            for h in heads:
                lhs = jnp.concatenate([(s_ret[h] * dmat_ref[h]).astype(BF16), qx_ref[rows, hs[h]]], 1)
                rhs = jnp.concatenate([v_ref[rows, hs[h]], states[h].astype(BF16)], 0)
                ret[h] = _dot(lhs, rhs)
            for h in heads:
                state_ref[h] = states[h] * chunk_decay[h] + _dot_tn(kz_ref[rows, hs[h]], v_ref[rows, hs[h]])

            pv = {}
            for (g, e), s in zip(phases, s_swa):
                c = 2 * g + e
                sink = jnp.concatenate([jnp.full((1, w), sink_ref[2 * sl + e] * LOG2E, F32) for sl in slabs[g]], 1)
                m = jnp.maximum(jnp.max(s, axis=0, keepdims=True), sink)
                p = jnp.exp2(s - m)
                inv = 1.0 / (jnp.sum(p, axis=0, keepdims=True) + jnp.exp2(sink - m))
                v_t = jnp.concatenate([prev_values_t(c), vc_ref[lane_blk(c), rows]], 1)
                pv[g, e] = _dot(v_t, p.astype(BF16)) * inv

            for h in heads:
                mu = jnp.mean(ret[h], axis=-1, keepdims=True)
                cen = ret[h] - mu
                var = jnp.mean(cen * cen, axis=-1, keepdims=True)
                ret_ref[rows, hs[h]] = (cen * lax.rsqrt(var + EPS) * g_ref[rows, hs[h]].astype(F32)).astype(BF16)

            for g in groups:
                acc = pv[g, 0] + pv[g, 1]
                for i, sl in enumerate(slabs[g]):
                    swa_ref[lane_blk(sl), rows] = acc[:, i * w:(i + 1) * w].astype(BF16)


def _mixers(rq, rqx, rk, rkz, rv, rg, dmat, chunk_decay, sq, sk4, sv4_t, sinks, batch, seq, cast_weights):
    assert RET_CHUNK == WINDOW
    t = rq.shape[0]
    rows = MIXER_BLOCKS * WINDOW
    n_steps = seq // rows
    grid = (batch, n_steps)
    cur = lambda b, n: (b * n_steps + n, 0)
    cur_t = lambda b, n: (0, b * n_steps + n)
    prev_idx = lambda b, n: b * n_steps * MIXER_BLOCKS + jnp.maximum(n * MIXER_BLOCKS - 1, 0)
    blk = pl.BlockSpec((rows, RET_WIDTH), cur)
    in_specs = [
        blk, blk, blk, blk, blk, blk, _resident((RET_HEADS, RET_CHUNK, RET_CHUNK)),
        pl.BlockSpec(memory_space=pltpu.SMEM),
        pl.BlockSpec((rows, SWA_WIDTH), cur),
        pl.BlockSpec((WINDOW, 4 * LANES), lambda b, n: (prev_idx(b, n), 0)),
        pl.BlockSpec((rows, 4 * LANES), cur),
        pl.BlockSpec((4 * LANES, WINDOW), lambda b, n: (0, prev_idx(b, n))),
        pl.BlockSpec((4 * LANES, rows), cur_t),
    ]
    cast_specs, cast_shapes = _cast_specs(cast_weights, grid)
    host = functools.partial(_mixers_kernel, chunk_decay=chunk_decay)
    out = pl.pallas_call(
        _with_casts(host, len(in_specs), 2, len(cast_weights)),
        out_shape=[jax.ShapeDtypeStruct((t, RET_WIDTH), BF16), jax.ShapeDtypeStruct((SWA_WIDTH, t), BF16)]
        + cast_shapes,
        grid=grid,
        in_specs=in_specs + cast_specs,
        out_specs=[blk, pl.BlockSpec((SWA_WIDTH, rows), cur_t)] + cast_specs,
        scratch_shapes=[pltpu.VMEM((RET_HEADS, RET_DK, RET_DV), F32)],
        compiler_params=_params(2),
        name="mixers",
    )(rq, rqx, rk, rkz, rv, rg, dmat, sinks, sq, sk4, sk4, sv4_t, sv4_t, *cast_weights)
    return out[0], out[1], out[2:]


def _outproj_kernel(h_ref, ret_ref, swa_ref, w_ref, mem_ref, mem_gain_ref, wkv_ref, h2_ref, mkv_ref, memn_ref):
    @pl.when(pl.program_id(0) == 0)
    def _():
        memn_ref[...] = _rms_rows(mem_ref[...], mem_gain_ref[...]).astype(BF16)

    y = _dot(ret_ref[...], w_ref[:RET_WIDTH, :]) + _dot_tn(swa_ref[...], w_ref[RET_WIDTH:, :])
    h2_ref[...] = h_ref[...] + y
    mkv_ref[...] = _dot(memn_ref[...], wkv_ref[...].astype(BF16)).astype(BF16)


def _outproj(h, ret, swa, w_out, mem, mem_gain, wkv):
    t, d = h.shape
    n_steps = t // ROW_TM
    kv_cols = wkv.shape[1] // n_steps
    row = lambda width: pl.BlockSpec((ROW_TM, width), lambda i: (i, 0))
    return pl.pallas_call(
        _outproj_kernel,
        out_shape=[jax.ShapeDtypeStruct((t, d), F32), jax.ShapeDtypeStruct((mem.shape[0], wkv.shape[1]), BF16)],
        grid=(n_steps,),
        in_specs=[row(d), row(RET_WIDTH), pl.BlockSpec((SWA_WIDTH, ROW_TM), lambda i: (0, i)),
                  _resident(w_out.shape),
                  _resident(mem.shape), _resident(mem_gain.shape),
                  pl.BlockSpec((wkv.shape[0], kv_cols), lambda i: (0, i))],
        out_specs=[row(d), pl.BlockSpec((mem.shape[0], kv_cols), lambda i: (0, i))],
        scratch_shapes=[pltpu.VMEM(mem.shape, BF16)],
        compiler_params=_params(1),
        name="outproj",
    )(h, ret, swa, w_out, mem, mem_gain, wkv)


def _xattn_kernel(h_ref, g_ref, wq_ref, wo_ref, k_ref, v_ref, o_ref):
    hn = _rms_rows(h_ref[...], g_ref[...]).astype(BF16)
    q = _dot(hn, wq_ref[...]).astype(BF16)
    scale = XA_HEAD_DIM ** -0.5
    heads = [slice(hd * XA_HEAD_DIM, (hd + 1) * XA_HEAD_DIM) for hd in range(XA_HEADS)]
    scores = [_dot_nt(q[:, hs], k_ref[:, hs]) * scale for hs in heads]
    probs, invs = [], []
    for s in scores:
        p = jnp.exp(s - jnp.max(s, axis=-1, keepdims=True))
        probs.append(p.astype(BF16))
        invs.append(1.0 / jnp.sum(p, axis=-1, keepdims=True))
    out = h_ref[...]
    for hs, p, inv in zip(heads, probs, invs):
        att = (_dot(p, v_ref[:, hs]) * inv).astype(BF16)
        out = out + _dot(att, wo_ref[hs, :])
    o_ref[...] = out


def _xattn(h, gain, wq, wo, mkv, seq, mem_len):
    t, d = h.shape
    tiles_per_seq = seq // ROW_TM
    row = lambda i: (i, 0)
    return pl.pallas_call(
        _xattn_kernel,
        out_shape=jax.ShapeDtypeStruct((t, d), F32),
        grid=(t // ROW_TM,),
        in_specs=[
            pl.BlockSpec((ROW_TM, d), row),
            _resident((1, d)),
            _resident((d, d)),
            _resident((d, d)),
            pl.BlockSpec((mem_len, d), lambda i: (i // tiles_per_seq, 0)),
            pl.BlockSpec((mem_len, d), lambda i: (i // tiles_per_seq, 1)),
        ],
        out_specs=pl.BlockSpec((ROW_TM, d), row),
        compiler_params=_params(1),
        name="xattn",
    )(h, gain, wq, wo, mkv, mkv)


def kernel(x, mem, ffn1_norm, ffn1_w_gate, ffn1_w_up, ffn1_w_down, mix_norm, w_in, ret_gn_gain, swa_sinks,
           w_out, xa_norm, mem_norm, xa_wq, xa_wkv, xa_wo, ffn2_norm, ffn2_w_gate, ffn2_w_up, ffn2_w_down,
           final_norm):
    batch, seq, d = x.shape
    mem_len = mem.shape[1]
    depth = ffn1_norm.shape[0]
    h = x.reshape(batch * seq, d)
    mem2 = mem.reshape(batch * mem_len, d)
    row = lambda g: g.reshape(1, -1).astype(F32)
    final_gain = row(final_norm)

    for l in range(depth):
        last = l == depth - 1
        h, (w_mix,) = _ffn_f32(h, row(ffn1_norm[l]), ffn1_w_gate[l], ffn1_w_up[l], ffn1_w_down[l], final_gain,
                               final_norm=False, cast_weights=[w_in[l]])
        (dmat, xi_tile, zeta_tile), chunk_decay = _retention_tables()
        rq, rqx, rk, rkz, rv, rg, sq, sk4, sv4_t = _inproj(h, row(mix_norm[l]), w_mix, row(ret_gn_gain[l]),
                                                           xi_tile, zeta_tile, seq)
        ret, swa, (wo_mix, wq, wo) = _mixers(
            rq, rqx, rk, rkz, rv, rg, dmat, chunk_decay, sq, sk4, sv4_t, swa_sinks[l].astype(F32), batch, seq,
            [w_out[l], xa_wq[l], xa_wo[l]])
        h, mkv = _outproj(h, ret, swa, wo_mix, mem2, row(mem_norm[l]), xa_wkv[l])
        h = _xattn(h, row(xa_norm[l]), wq, wo, mkv, seq, mem_len)
        h, _ = _ffn_f32(h, row(ffn2_norm[l]), ffn2_w_gate[l], ffn2_w_up[l], ffn2_w_down[l], final_gain,
                        final_norm=last)
    if depth == 0:
        raise ValueError("depth must be at least 1")
    return h.reshape(batch, seq, d)
```

```python
import functools

import numpy as np
import jax
import jax.numpy as jnp
from jax import lax
from jax.experimental import pallas as pl
from jax.experimental.pallas import tpu as pltpu

F32 = jnp.float32
BF16 = jnp.bfloat16

D_MODEL = 2048
RET_HEADS = 8
RET_DK = 128
RET_DV = 128
RET_WIDTH = RET_HEADS * RET_DV
RET_CHUNK = 128
SWA_HEADS = 16
SWA_KV_HEADS = 2
SWA_HEAD_DIM = 64
SWA_WIDTH = SWA_HEADS * SWA_HEAD_DIM
SWA_KV_WIDTH = SWA_KV_HEADS * SWA_HEAD_DIM
WINDOW = 128
XA_HEADS = 4
XA_HEAD_DIM = D_MODEL // XA_HEADS
ROPE_THETA = 10000.0
EPS = 1e-6
IN_SIZES = (RET_WIDTH, RET_WIDTH, RET_WIDTH, RET_WIDTH, SWA_WIDTH, SWA_KV_WIDTH, SWA_KV_WIDTH)
IN_COLS = sum(IN_SIZES)

LANES = 128
BF16_SUBLANES = 16
VMEM_LIMIT_BYTES = 60000 * 1024

FFN_TM = 1024
FFN_TF = 512
FFN_HEAD_TF = 256
ROW_TM = 512
NORM_ROWS = 512
MIXER_PARTS = 2
MIXER_BLOCKS = 4
NEG_INF = float(np.finfo(np.float32).min)
LOG2E = float(np.log2(np.e))


def _params(n_axes):
    return pltpu.CompilerParams(
        dimension_semantics=("arbitrary",) * n_axes,
        vmem_limit_bytes=VMEM_LIMIT_BYTES,
    )


def _resident(shape):
    zeros = (0,) * len(shape)
    return pl.BlockSpec(shape, lambda *_: zeros, pipeline_mode=pl.Buffered(1))


def _rms_rows(x, gain):
    ms = jnp.mean(x * x, axis=-1, keepdims=True)
    return x * lax.rsqrt(ms + EPS) * gain


def _dot(a, b):
    return lax.dot_general(a, b, (((1,), (0,)), ((), ())), preferred_element_type=F32)


def _dot_nt(a, b):
    return lax.dot_general(a, b, (((1,), (1,)), ((), ())), preferred_element_type=F32)


def _dot_tn(a, b):
    return lax.dot_general(a, b, (((0,), (0,)), ((), ())), preferred_element_type=F32)


def _cast_block(shape, n_steps):
    rows, cols = shape
    for col_splits in (1, 2, 4, 8):
        row_blocks, rem = divmod(n_steps, col_splits)
        if rem or rows % row_blocks or cols % col_splits:
            continue
        br, bc = rows // row_blocks, cols // col_splits
        if br % BF16_SUBLANES == 0 and bc % LANES == 0:
            return br, bc, col_splits
    raise ValueError(f"no aligned {n_steps}-way split of {shape}")


def _with_casts(kernel_fn, n_in, n_out, n_cast):
    def wrapped(*refs):
        ins, rest = refs[:n_in], refs[n_in:]
        cast_in, rest = rest[:n_cast], rest[n_cast:]
        outs, rest = rest[:n_out], rest[n_out:]
        cast_out, scratch = rest[:n_cast], rest[n_cast:]
        kernel_fn(*ins, *outs, *scratch)
        for src, dst in zip(cast_in, cast_out):
            dst[...] = src[...].astype(BF16)
    return wrapped


def _cast_specs(weights, grid):
    n_steps = int(np.prod(grid))
    specs, shapes = [], []
    for w in weights:
        br, bc, col_splits = _cast_block(w.shape, n_steps)

        def index_map(*idx, col_splits=col_splits):
            step = idx[0]
            for size, i in zip(grid[1:], idx[1:]):
                step = step * size + i
            return step // col_splits, step % col_splits
        specs.append(pl.BlockSpec((br, bc), index_map))
        shapes.append(jax.ShapeDtypeStruct(w.shape, BF16))
    return specs, shapes


def _cast_specs_tiled(weights, grid):
    n_outer, n_inner = grid
    specs, shapes = [], []
    for w in weights:
        rows, cols = w.shape
        bc = cols // n_outer
        row_blocks = max(r for r in range(1, n_inner + 1) if rows % r == 0 and (rows // r) % BF16_SUBLANES == 0)
        if cols % n_outer or bc % LANES:
            raise ValueError(f"no aligned split of {w.shape} over {grid}")

        def index_map(i, j, row_blocks=row_blocks):
            return jnp.minimum(j, row_blocks - 1), i
        specs.append(pl.BlockSpec((rows // row_blocks, bc), index_map))
        shapes.append(jax.ShapeDtypeStruct(w.shape, BF16))
    return specs, shapes


def _rows_loop(n_rows, fn):
    def body(r, carry):
        fn(pl.ds(pl.multiple_of(r * NORM_ROWS, NORM_ROWS), NORM_ROWS))
        return carry
    lax.fori_loop(0, n_rows // NORM_ROWS, body, 0)


def _ffn_prologue(h_ref, g_ref, xb_ref, r_ref):
    def scale(rows):
        h = h_ref[rows, :]
        xb_ref[rows, :] = (h * g_ref[...]).astype(BF16)
        r_ref[rows, :] = lax.rsqrt(jnp.mean(h * h, axis=-1, keepdims=True) + EPS)
    _rows_loop(h_ref.shape[0], scale)


def _ffn_prologue_full(h_ref, g_ref, o_ref, xn_ref):
    def norm(rows):
        h = h_ref[rows, :]
        xn_ref[rows, :] = _rms_rows(h, g_ref[...]).astype(BF16)
        o_ref[rows, :] = h
    _rows_loop(h_ref.shape[0], norm)


def _ffn_step(xb_ref, wg, wu, wd, o_ref, *, first=None, h_ref=None, r_ref=None):
    xb = xb_ref[...]
    g = _dot(xb, wg)
    u = _dot(xb, wu)
    if r_ref is not None:
        r = r_ref[...]
        g, u = g * r, u * r
    a = (g * (0.5 / (1.0 + jnp.exp(-g))) * u).astype(BF16)
    for c in range(0, o_ref.shape[1], FFN_TF):
        cols = slice(c, c + FFN_TF)
        acc = o_ref[:, cols] if first is None else jnp.where(first, h_ref[:, cols], o_ref[:, cols])
        o_ref[:, cols] = acc + _dot(a, wd[:, cols])


def _ffn_kernel(*refs, n_ff, final_norm, aliased):
    if aliased:
        refs = refs[1:]
    h_ref, g_ref, wg_ref, wu_ref, wd_ref, fg_ref, o_ref, xb_ref, r_ref = refs
    j = pl.program_id(1)

    @pl.when(j == 0)
    def _():
        _ffn_prologue(h_ref, g_ref, xb_ref, r_ref)

    _ffn_step(xb_ref, wg_ref[...], wu_ref[...], wd_ref[...], o_ref, first=j == 0, h_ref=h_ref, r_ref=r_ref)

    if final_norm:
        @pl.when(j == n_ff - 1)
        def _():
            def norm(rows):
                o_ref[rows, :] = _rms_rows(o_ref[rows, :], fg_ref[...])
            _rows_loop(o_ref.shape[0], norm)


def _ffn(h, gain, wg, wu, wd, final_gain, *, final_norm, first_tile=0, into=None, cast_weights=()):
    t, d = h.shape
    f = wg.shape[1]
    n_ff = f // FFN_TF
    grid = (t // FFN_TM - first_tile, n_ff)
    aliased = into is not None
    tile = lambda i, j: (i + first_tile, 0)
    in_specs = [
        pl.BlockSpec((FFN_TM, d), tile),
        pl.BlockSpec((1, d), lambda i, j: (0, 0)),
        pl.BlockSpec((d, FFN_TF), lambda i, j: (0, j)),
        pl.BlockSpec((d, FFN_TF), lambda i, j: (0, j)),
        pl.BlockSpec((FFN_TF, d), lambda i, j: (j, 0)),
        pl.BlockSpec((1, d), lambda i, j: (0, 0)),
    ]
    args = (h, gain, wg, wu, wd, final_gain)
    if aliased:
        in_specs = [pl.BlockSpec(memory_space=pl.ANY)] + in_specs
        args = (into,) + args
    cast_specs, cast_shapes = _cast_specs_tiled(cast_weights, grid)
    host = functools.partial(_ffn_kernel, n_ff=n_ff, final_norm=final_norm, aliased=aliased)
    out = pl.pallas_call(
        _with_casts(host, len(in_specs), 1, len(cast_weights)),
        out_shape=[jax.ShapeDtypeStruct((t, d), F32)] + cast_shapes,
        grid=grid,
        in_specs=in_specs + cast_specs,
        out_specs=[pl.BlockSpec((FFN_TM, d), tile)] + cast_specs,
        scratch_shapes=[pltpu.VMEM((FFN_TM, d), BF16), pltpu.VMEM((FFN_TM, 1), F32)],
        input_output_aliases={0: 0} if aliased else {},
        compiler_params=_params(2),
        name="ffn",
    )(*args, *cast_weights)
    return out[0], out[1:]


def _ffn_head_kernel(h_ref, g_ref, wg_ref, wu_ref, wd_ref, fg_ref, o_ref, wg_out, wu_out, wd_out, xn_ref,
                     *, n_ff, final_norm):
    j = pl.program_id(0)

    @pl.when(j == 0)
    def _():
        _ffn_prologue_full(h_ref, g_ref, o_ref, xn_ref)

    wg, wu, wd = (w[...].astype(BF16) for w in (wg_ref, wu_ref, wd_ref))
    wg_out[...] = wg
    wu_out[...] = wu
    wd_out[...] = wd
    _ffn_step(xn_ref, wg, wu, wd, o_ref)

    if final_norm:
        @pl.when(j == n_ff - 1)
        def _():
            def norm(rows):
                o_ref[rows, :] = _rms_rows(o_ref[rows, :], fg_ref[...])
            _rows_loop(o_ref.shape[0], norm)


def _ffn_head(h, gain, wg, wu, wd, final_gain, *, final_norm):
    t, d = h.shape
    f = wg.shape[1]
    tf = FFN_HEAD_TF
    col = pl.BlockSpec((d, tf), lambda j: (0, j))
    row = pl.BlockSpec((tf, d), lambda j: (j, 0))
    vec = pl.BlockSpec((1, d), lambda j: (0, 0))
    tile0 = pl.BlockSpec((FFN_TM, d), lambda j: (0, 0))
    return pl.pallas_call(
        functools.partial(_ffn_head_kernel, n_ff=f // tf, final_norm=final_norm),
        out_shape=[jax.ShapeDtypeStruct((t, d), F32), jax.ShapeDtypeStruct(wg.shape, BF16),
                   jax.ShapeDtypeStruct(wu.shape, BF16), jax.ShapeDtypeStruct(wd.shape, BF16)],
        grid=(f // tf,),
        in_specs=[tile0, vec, col, col, row, vec],
        out_specs=[tile0, col, col, row],
        scratch_shapes=[pltpu.VMEM((FFN_TM, d), BF16)],
        compiler_params=_params(1),
        name="ffn_head",
    )(h, gain, wg, wu, wd, final_gain)


def _ffn_f32(h, gain, wg, wu, wd, final_gain, *, final_norm, cast_weights=()):
    head, wg16, wu16, wd16 = _ffn_head(h, gain, wg, wu, wd, final_gain, final_norm=final_norm)
    return _ffn(h, gain, wg16, wu16, wd16, final_gain, final_norm=final_norm, first_tile=1, into=head,
                cast_weights=cast_weights)


def _rope_tables(seq):
    pos = np.arange(seq, dtype=np.float32)

    def angles(d):
        inv = np.float32(ROPE_THETA) ** (-np.arange(0, d, 2, dtype=np.float32) / np.float32(d))
        return (pos[:, None] * inv[None, :].astype(np.float32)).astype(np.float32).astype(np.float64)

    a128 = angles(RET_DK)
    cos_r = np.concatenate([np.cos(a128), np.cos(a128)], -1)
    sin_r = np.concatenate([-np.sin(a128), np.sin(a128)], -1)
    a64 = angles(SWA_HEAD_DIM)
    c, s, z = np.cos(a64), np.sin(a64), np.zeros_like(a64)
    cos_s = np.concatenate([c, c, c, c], -1)
    sin_lo = np.concatenate([-s, z, -s, z], -1)
    sin_hi = np.concatenate([z, s, z, s], -1)
    return [jnp.asarray(v, dtype=F32) for v in (cos_r, sin_r, cos_s, sin_lo, sin_hi)]


def _retention_tables():
    c = RET_CHUNK
    heads = np.arange(RET_HEADS, dtype=np.float64)
    log_gamma = np.log1p(-np.exp2(-5.0 - heads))
    idx = np.arange(c, dtype=np.float64)
    diff = idx[:, None] - idx[None, :]
    scale = RET_DK ** -0.5
    dmat = np.where(diff[None] >= 0, np.exp(np.maximum(diff, 0.0)[None] * log_gamma[:, None, None]), 0.0)
    zeta = np.exp((c - 1.0 - idx)[None, :] * log_gamma[:, None])
    xi = np.exp((idx + 1.0)[None, :] * log_gamma[:, None])
    chunk_decay = tuple(float(v) for v in np.exp(c * log_gamma))

    def token_tile(tab):
        return np.tile(np.repeat(tab.T, RET_DK, axis=1), (ROW_TM // c, 1))
    tabs = [jnp.asarray(v, dtype=F32) for v in (dmat * scale, token_tile(xi), token_tile(zeta * scale))]
    return tabs, chunk_decay


def _inproj_kernel(h_ref, g_ref, w_ref, cr_ref, sr_ref, cs_ref, sl_ref, sh_ref, xi_ref, zeta_ref, gn_ref,
                   rq_ref, rqx_ref, rk_ref, rkz_ref, rv_ref, rg_ref, sq_ref, sk_ref, sv_ref, xn_ref):
    tm = h_ref.shape[0]

    def norm(rows):
        xn_ref[rows, :] = _rms_rows(h_ref[rows, :], g_ref[...]).astype(BF16)
    _rows_loop(tm, norm)

    xn = xn_ref[...]
    cr, sr = cr_ref[...], sr_ref[...]
    cs, sl, sh = cs_ref[...], sl_ref[...], sh_ref[...]
    half = LANES // 2
    slabs = [slice(s * LANES, (s + 1) * LANES) for s in range(RET_WIDTH // LANES)]

    def rope_ret(x):
        return x * cr + pltpu.roll(x, half, 1) * sr

    def rope_swa(x):
        return x * cs + pltpu.roll(x, LANES - half // 2, 1) * sl + pltpu.roll(x, half // 2, 1) * sh

    def project(col0, width):
        return _dot(xn, w_ref[:, col0:col0 + width])

    def ret_rotary(out_ref, scaled_ref, tab_ref):
        def epilogue(y):
            for sl_ in slabs:
                x = rope_ret(y[:, sl_])
                out_ref[:, sl_] = x.astype(BF16)
                scaled_ref[:, sl_] = (x * tab_ref[:, sl_]).astype(BF16)
        return epilogue

    def ret_values(y):
        for sl_ in slabs:
            rv_ref[:, sl_] = y[:, sl_].astype(BF16)

    def ret_gate(y):
        for sl_ in slabs:
            g = y[:, sl_]
            rg_ref[:, sl_] = (g * (1.0 / (1.0 + jnp.exp(-g))) * gn_ref[:, sl_]).astype(BF16)

    def swa_queries(y):
        scale = SWA_HEAD_DIM ** -0.5 * LOG2E
        for sl_ in slabs:
            sq_ref[:, sl_] = (rope_swa(y[:, sl_]) * scale).astype(BF16)

    def swa_keys_values(y):
        lo = lax.broadcasted_iota(jnp.int32, (tm, LANES), 1) < half
        for transposed, out_ref, x in ((False, sk_ref, rope_swa(y[:, :LANES])), (True, sv_ref, y[:, LANES:])):
            xr = pltpu.roll(x, half, 1)
            variants = (jnp.where(lo, x, 0.0), jnp.where(lo, 0.0, xr),
                        jnp.where(lo, xr, 0.0), jnp.where(lo, 0.0, x))
            for s, v in enumerate(variants):
                if transposed:
                    out_ref[s * LANES:(s + 1) * LANES, :] = v.T.astype(BF16)
                else:
                    out_ref[:, s * LANES:(s + 1) * LANES] = v.astype(BF16)

    epilogues = (ret_rotary(rq_ref, rqx_ref, xi_ref), ret_rotary(rk_ref, rkz_ref, zeta_ref), ret_values, ret_gate,
                 swa_queries, swa_keys_values)
    col0 = np.cumsum((0,) + IN_SIZES[:-2])
    widths = IN_SIZES[:-2] + (2 * SWA_KV_WIDTH,)
    order = (5, 0, 1, 3, 4, 2)
    for r in order:
        epilogues[r](project(int(col0[r]), widths[r]))


def _inproj(h, gain, w_in, gn_gain, xi_tile, zeta_tile, seq):
    t, d = h.shape
    tables = _rope_tables(seq)
    tiles_per_seq = seq // ROW_TM
    tab_spec = pl.BlockSpec((ROW_TM, LANES), lambda i: (i % tiles_per_seq, 0))

    def row_spec(width):
        return pl.BlockSpec((ROW_TM, width), lambda i: (i, 0))

    widths = (RET_WIDTH,) * 6 + (SWA_WIDTH, 4 * LANES)
    return pl.pallas_call(
        _inproj_kernel,
        out_shape=[jax.ShapeDtypeStruct((t, w), BF16) for w in widths]
        + [jax.ShapeDtypeStruct((4 * LANES, t), BF16)],
        grid=(t // ROW_TM,),
        in_specs=[row_spec(d), _resident((1, d)), _resident((d, IN_COLS))] + [tab_spec] * 5
        + [_resident((ROW_TM, RET_WIDTH))] * 2 + [_resident((1, RET_WIDTH))],
        out_specs=[row_spec(w) for w in widths] + [pl.BlockSpec((4 * LANES, ROW_TM), lambda i: (0, i))],
        scratch_shapes=[pltpu.VMEM((ROW_TM, d), BF16)],
        compiler_params=_params(1),
        name="inproj",
    )(h, gain, w_in, *tables, xi_tile, zeta_tile, gn_gain)


def _mixers_kernel(q_ref, qx_ref, k_ref, kz_ref, v_ref, g_ref, dmat_ref,
                   sink_ref, sq_ref, kp_ref, kc_ref, vp_ref, vc_ref, ret_ref, swa_ref, state_ref, *, chunk_decay):
    n = pl.program_id(1)
    w = WINDOW

    @pl.when(n == 0)
    def _():
        state_ref[...] = jnp.zeros_like(state_ref)

    pairs = SWA_HEADS // SWA_KV_HEADS // 2
    key = lax.broadcasted_iota(jnp.int32, (2 * w, pairs * w), 0)
    qry = lax.broadcasted_iota(jnp.int32, (2 * w, pairs * w), 1) % w
    band = (key > qry) & (key <= qry + w)
    bias = jnp.where(band, 0.0, NEG_INF)
    bias_first = jnp.where(band & (key >= jnp.where(n == 0, w, 0)), 0.0, NEG_INF)

    hs = [slice(h * RET_DK, (h + 1) * RET_DK) for h in range(RET_HEADS)]
    lane_blk = lambda c: slice(c * LANES, (c + 1) * LANES)
    slabs = {g: [g * pairs + p for p in range(pairs)] for g in range(SWA_KV_HEADS)}

    for blk in range(MIXER_BLOCKS):
        rows = slice(blk * w, (blk + 1) * w)
        mask_bias = bias if blk else bias_first

        def prev_keys(c):
            return kc_ref[(blk - 1) * w:blk * w, lane_blk(c)] if blk else kp_ref[:, lane_blk(c)]

        def prev_values_t(c):
            return vc_ref[lane_blk(c), (blk - 1) * w:blk * w] if blk else vp_ref[lane_blk(c), :]

        for part in range(MIXER_PARTS):
            heads = range(part * RET_HEADS // MIXER_PARTS, (part + 1) * RET_HEADS // MIXER_PARTS)
            groups = range(part * SWA_KV_HEADS // MIXER_PARTS, (part + 1) * SWA_KV_HEADS // MIXER_PARTS)
            phases = [(g, e) for g in groups for e in range(2)]

            s_ret = {h: _dot_nt(q_ref[rows, hs[h]], k_ref[rows, hs[h]]) for h in heads}
            q_swa = {g: jnp.concatenate([sq_ref[rows, lane_blk(sl)] for sl in slabs[g]], 0) for g in groups}
            s_swa = []
            for g, e in phases:
                c = 2 * g + e
                k = jnp.concatenate([prev_keys(c), kc_ref[rows, lane_blk(c)]], 0)
                s_swa.append(_dot_nt(k, q_swa[g]) + mask_bias)

            states = {h: state_ref[h] for h in heads}
            ret = {}
            for h in heads:
                lhs = jnp.concatenate([(s_ret[h] * dmat_ref[h]).astype(BF16), qx_ref[rows, hs[h]]], 1)
                rhs = jnp.concatenate([v_ref[rows, hs[h]], states[h].astype(BF16)], 0)
                ret[h] = _dot(lhs, rhs)
            for h in heads:
                state_ref[h] = states[h] * chunk_decay[h] + _dot_tn(kz_ref[rows, hs[h]], v_ref[rows, hs[h]])

            pv = {}
            for (g, e), s in zip(phases, s_swa):
                c = 2 * g + e
                sink = jnp.concatenate([jnp.full((1, w), sink_ref[2 * sl + e] * LOG2E, F32) for sl in slabs[g]], 1)
                m = jnp.maximum(jnp.max(s, axis=0, keepdims=True), sink)
                p = jnp.exp2(s - m)
                inv = 1.0 / (jnp.sum(p, axis=0, keepdims=True) + jnp.exp2(sink - m))
                v_t = jnp.concatenate([prev_values_t(c), vc_ref[lane_blk(c), rows]], 1)
                pv[g, e] = _dot(v_t, p.astype(BF16)) * inv

            for h in heads:
                mu = jnp.mean(ret[h], axis=-1, keepdims=True)
                cen = ret[h] - mu
                var = jnp.mean(cen * cen, axis=-1, keepdims=True)
                ret_ref[rows, hs[h]] = (cen * lax.rsqrt(var + EPS) * g_ref[rows, hs[h]].astype(F32)).astype(BF16)

            for g in groups:
                acc = pv[g, 0] + pv[g, 1]
                for i, sl in enumerate(slabs[g]):
                    swa_ref[lane_blk(sl), rows] = acc[:, i * w:(i + 1) * w].astype(BF16)


def _mixers(rq, rqx, rk, rkz, rv, rg, dmat, chunk_decay, sq, sk4, sv4_t, sinks, batch, seq, cast_weights):
    assert RET_CHUNK == WINDOW
    t = rq.shape[0]
    rows = MIXER_BLOCKS * WINDOW
    n_steps = seq // rows
    grid = (batch, n_steps)
    cur = lambda b, n: (b * n_steps + n, 0)
    cur_t = lambda b, n: (0, b * n_steps + n)
    prev_idx = lambda b, n: b * n_steps * MIXER_BLOCKS + jnp.maximum(n * MIXER_BLOCKS - 1, 0)
    blk = pl.BlockSpec((rows, RET_WIDTH), cur)
    in_specs = [
        blk, blk, blk, blk, blk, blk, _resident((RET_HEADS, RET_CHUNK, RET_CHUNK)),
        pl.BlockSpec(memory_space=pltpu.SMEM),
        pl.BlockSpec((rows, SWA_WIDTH), cur),
        pl.BlockSpec((WINDOW, 4 * LANES), lambda b, n: (prev_idx(b, n), 0)),
        pl.BlockSpec((rows, 4 * LANES), cur),
        pl.BlockSpec((4 * LANES, WINDOW), lambda b, n: (0, prev_idx(b, n))),
        pl.BlockSpec((4 * LANES, rows), cur_t),
    ]
    cast_specs, cast_shapes = _cast_specs(cast_weights, grid)
    host = functools.partial(_mixers_kernel, chunk_decay=chunk_decay)
    out = pl.pallas_call(
        _with_casts(host, len(in_specs), 2, len(cast_weights)),
        out_shape=[jax.ShapeDtypeStruct((t, RET_WIDTH), BF16), jax.ShapeDtypeStruct((SWA_WIDTH, t), BF16)]
        + cast_shapes,
        grid=grid,
        in_specs=in_specs + cast_specs,
        out_specs=[blk, pl.BlockSpec((SWA_WIDTH, rows), cur_t)] + cast_specs,
        scratch_shapes=[pltpu.VMEM((RET_HEADS, RET_DK, RET_DV), F32)],
        compiler_params=_params(2),
        name="mixers",
    )(rq, rqx, rk, rkz, rv, rg, dmat, sinks, sq, sk4, sk4, sv4_t, sv4_t, *cast_weights)
    return out[0], out[1], out[2:]


def _outproj_kernel(h_ref, ret_ref, swa_ref, w_ref, mem_ref, mem_gain_ref, wkv_ref, h2_ref, mkv_ref, memn_ref):
    @pl.when(pl.program_id(0) == 0)
    def _():
        memn_ref[...] = _rms_rows(mem_ref[...], mem_gain_ref[...]).astype(BF16)

    y = _dot(ret_ref[...], w_ref[:RET_WIDTH, :]) + _dot_tn(swa_ref[...], w_ref[RET_WIDTH:, :])
    h2_ref[...] = h_ref[...] + y
    mkv_ref[...] = _dot(memn_ref[...], wkv_ref[...].astype(BF16)).astype(BF16)


def _outproj(h, ret, swa, w_out, mem, mem_gain, wkv):
    t, d = h.shape
    n_steps = t // ROW_TM
    kv_cols = wkv.shape[1] // n_steps
    row = lambda width: pl.BlockSpec((ROW_TM, width), lambda i: (i, 0))
    return pl.pallas_call(
        _outproj_kernel,
        out_shape=[jax.ShapeDtypeStruct((t, d), F32), jax.ShapeDtypeStruct((mem.shape[0], wkv.shape[1]), BF16)],
        grid=(n_steps,),
        in_specs=[row(d), row(RET_WIDTH), pl.BlockSpec((SWA_WIDTH, ROW_TM), lambda i: (0, i)),
                  _resident(w_out.shape),
                  _resident(mem.shape), _resident(mem_gain.shape),
                  pl.BlockSpec((wkv.shape[0], kv_cols), lambda i: (0, i))],
        out_specs=[row(d), pl.BlockSpec((mem.shape[0], kv_cols), lambda i: (0, i))],
        scratch_shapes=[pltpu.VMEM(mem.shape, BF16)],
        compiler_params=_params(1),
        name="outproj",
    )(h, ret, swa, w_out, mem, mem_gain, wkv)


def _xattn_kernel(h_ref, g_ref, wq_ref, wo_ref, k_ref, v_ref, o_ref):
    h = h_ref[...]
    r = lax.rsqrt(jnp.mean(h * h, axis=-1, keepdims=True) + EPS)
    q = (_dot((h * g_ref[...]).astype(BF16), wq_ref[...]) * r).astype(BF16)
    scale = XA_HEAD_DIM ** -0.5
    heads = [slice(hd * XA_HEAD_DIM, (hd + 1) * XA_HEAD_DIM) for hd in range(XA_HEADS)]
    scores = [_dot_nt(q[:, hs], k_ref[:, hs]) * scale for hs in heads]
    probs, invs = [], []
    for s in scores:
        p = jnp.exp(s - jnp.max(s, axis=-1, keepdims=True))
        probs.append(p.astype(BF16))
        invs.append(1.0 / jnp.sum(p, axis=-1, keepdims=True))
    out = h_ref[...]
    for hs, p, inv in zip(heads, probs, invs):
        att = (_dot(p, v_ref[:, hs]) * inv).astype(BF16)
        out = out + _dot(att, wo_ref[hs, :])
    o_ref[...] = out


def _xattn(h, gain, wq, wo, mkv, seq, mem_len):
    t, d = h.shape
    tiles_per_seq = seq // ROW_TM
    row = lambda i: (i, 0)
    return pl.pallas_call(
        _xattn_kernel,
        out_shape=jax.ShapeDtypeStruct((t, d), F32),
        grid=(t // ROW_TM,),
        in_specs=[
            pl.BlockSpec((ROW_TM, d), row),
            _resident((1, d)),
            _resident((d, d)),
            _resident((d, d)),
            pl.BlockSpec((mem_len, d), lambda i: (i // tiles_per_seq, 0)),
            pl.BlockSpec((mem_len, d), lambda i: (i // tiles_per_seq, 1)),
        ],
        out_specs=pl.BlockSpec((ROW_TM, d), row),
        compiler_params=_params(1),
        name="xattn",
    )(h, gain, wq, wo, mkv, mkv)


def kernel(x, mem, ffn1_norm, ffn1_w_gate, ffn1_w_up, ffn1_w_down, mix_norm, w_in, ret_gn_gain, swa_sinks,
           w_out, xa_norm, mem_norm, xa_wq, xa_wkv, xa_wo, ffn2_norm, ffn2_w_gate, ffn2_w_up, ffn2_w_down,
           final_norm):
    batch, seq, d = x.shape
    mem_len = mem.shape[1]
    depth = ffn1_norm.shape[0]
    h = x.reshape(batch * seq, d)
    mem2 = mem.reshape(batch * mem_len, d)
    row = lambda g: g.reshape(1, -1).astype(F32)
    final_gain = row(final_norm)

    for l in range(depth):
        last = l == depth - 1
        h, (w_mix,) = _ffn_f32(h, row(ffn1_norm[l]), ffn1_w_gate[l], ffn1_w_up[l], ffn1_w_down[l], final_gain,
                               final_norm=False, cast_weights=[w_in[l]])
        (dmat, xi_tile, zeta_tile), chunk_decay = _retention_tables()
        rq, rqx, rk, rkz, rv, rg, sq, sk4, sv4_t = _inproj(h, row(mix_norm[l]), w_mix, row(ret_gn_gain[l]),
                                                           xi_tile, zeta_tile, seq)
        ret, swa, (wo_mix, wq, wo) = _mixers(
            rq, rqx, rk, rkz, rv, rg, dmat, chunk_decay, sq, sk4, sv4_t, swa_sinks[l].astype(F32), batch, seq,
            [w_out[l], xa_wq[l], xa_wo[l]])
        h, mkv = _outproj(h, ret, swa, wo_mix, mem2, row(mem_norm[l]), xa_wkv[l])
        h = _xattn(h, row(xa_norm[l]), wq, wo, mkv, seq, mem_len)
        h, _ = _ffn_f32(h, row(ffn2_norm[l]), ffn2_w_gate[l], ffn2_w_up[l], ffn2_w_down[l], final_gain,
                        final_norm=last)
    if depth == 0:
        raise ValueError("depth must be at least 1")
    return h.reshape(batch, seq, d)
```

```python
import functools

import numpy as np
import jax
import jax.numpy as jnp
from jax import lax
from jax.experimental import pallas as pl
from jax.experimental.pallas import tpu as pltpu

F32 = jnp.float32
BF16 = jnp.bfloat16

D_MODEL = 2048
RET_HEADS = 8
RET_DK = 128
RET_DV = 128
RET_WIDTH = RET_HEADS * RET_DV
RET_CHUNK = 128
SWA_HEADS = 16
SWA_KV_HEADS = 2
SWA_HEAD_DIM = 64
SWA_WIDTH = SWA_HEADS * SWA_HEAD_DIM
SWA_KV_WIDTH = SWA_KV_HEADS * SWA_HEAD_DIM
WINDOW = 128
XA_HEADS = 4
XA_HEAD_DIM = D_MODEL // XA_HEADS
ROPE_THETA = 10000.0
EPS = 1e-6
IN_SIZES = (RET_WIDTH, RET_WIDTH, RET_WIDTH, RET_WIDTH, SWA_WIDTH, SWA_KV_WIDTH, SWA_KV_WIDTH)
IN_COLS = sum(IN_SIZES)

LANES = 128
BF16_SUBLANES = 16
VMEM_LIMIT_BYTES = 60000 * 1024

FFN_TM = 1024
FFN_TF = 512
FFN_HEAD_TF = 256
ROW_TM = 512
NORM_ROWS = 512
MIXER_PARTS = 2
MIXER_BLOCKS = 4
NEG_INF = float(np.finfo(np.float32).min)
LOG2E = float(np.log2(np.e))


def _params(n_axes):
    return pltpu.CompilerParams(
        dimension_semantics=("arbitrary",) * n_axes,
        vmem_limit_bytes=VMEM_LIMIT_BYTES,
    )


def _resident(shape):
    zeros = (0,) * len(shape)
    return pl.BlockSpec(shape, lambda *_: zeros, pipeline_mode=pl.Buffered(1))


def _rms_rows(x, gain):
    ms = jnp.mean(x * x, axis=-1, keepdims=True)
    return x * lax.rsqrt(ms + EPS) * gain


def _dot(a, b):
    return lax.dot_general(a, b, (((1,), (0,)), ((), ())), preferred_element_type=F32)


def _dot_nt(a, b):
    return lax.dot_general(a, b, (((1,), (1,)), ((), ())), preferred_element_type=F32)


def _dot_tn(a, b):
    return lax.dot_general(a, b, (((0,), (0,)), ((), ())), preferred_element_type=F32)


def _cast_block(shape, n_steps):
    rows, cols = shape
    for col_splits in (1, 2, 4, 8):
        row_blocks, rem = divmod(n_steps, col_splits)
        if rem or rows % row_blocks or cols % col_splits:
            continue
        br, bc = rows // row_blocks, cols // col_splits
        if br % BF16_SUBLANES == 0 and bc % LANES == 0:
            return br, bc, col_splits
    raise ValueError(f"no aligned {n_steps}-way split of {shape}")


def _with_casts(kernel_fn, n_in, n_out, n_cast):
    def wrapped(*refs):
        ins, rest = refs[:n_in], refs[n_in:]
        cast_in, rest = rest[:n_cast], rest[n_cast:]
        outs, rest = rest[:n_out], rest[n_out:]
        cast_out, scratch = rest[:n_cast], rest[n_cast:]
        kernel_fn(*ins, *outs, *scratch)
        for src, dst in zip(cast_in, cast_out):
            dst[...] = src[...].astype(BF16)
    return wrapped


def _cast_specs(weights, grid):
    n_steps = int(np.prod(grid))
    specs, shapes = [], []
    for w in weights:
        br, bc, col_splits = _cast_block(w.shape, n_steps)

        def index_map(*idx, col_splits=col_splits):
            step = idx[0]
            for size, i in zip(grid[1:], idx[1:]):
                step = step * size + i
            return step // col_splits, step % col_splits
        specs.append(pl.BlockSpec((br, bc), index_map))
        shapes.append(jax.ShapeDtypeStruct(w.shape, BF16))
    return specs, shapes


def _cast_specs_tiled(weights, grid):
    n_outer, n_inner = grid
    specs, shapes = [], []
    for w in weights:
        rows, cols = w.shape
        bc = cols // n_outer
        row_blocks = max(r for r in range(1, n_inner + 1) if rows % r == 0 and (rows // r) % BF16_SUBLANES == 0)
        if cols % n_outer or bc % LANES:
            raise ValueError(f"no aligned split of {w.shape} over {grid}")

        def index_map(i, j, row_blocks=row_blocks):
            return jnp.minimum(j, row_blocks - 1), i
        specs.append(pl.BlockSpec((rows // row_blocks, bc), index_map))
        shapes.append(jax.ShapeDtypeStruct(w.shape, BF16))
    return specs, shapes


def _rows_loop(n_rows, fn):
    def body(r, carry):
        fn(pl.ds(pl.multiple_of(r * NORM_ROWS, NORM_ROWS), NORM_ROWS))
        return carry
    lax.fori_loop(0, n_rows // NORM_ROWS, body, 0)


def _ffn_prologue(h_ref, g_ref, xb_ref, r_ref):
    def scale(rows):
        h = h_ref[rows, :]
        xb_ref[rows, :] = (h * g_ref[...]).astype(BF16)
        r_ref[rows, :] = lax.rsqrt(jnp.mean(h * h, axis=-1, keepdims=True) + EPS)
    _rows_loop(h_ref.shape[0], scale)


def _ffn_prologue_full(h_ref, g_ref, o_ref, xn_ref):
    def norm(rows):
        h = h_ref[rows, :]
        xn_ref[rows, :] = _rms_rows(h, g_ref[...]).astype(BF16)
        o_ref[rows, :] = h
    _rows_loop(h_ref.shape[0], norm)


def _ffn_step(xb, wg, wu, wd, o_ref, *, first=False, h_ref=None, r=None):
    g = _dot(xb, wg)
    u = _dot(xb, wu)
    if r is not None:
        g, u = g * r, u * r
    a = (g * (0.5 / (1.0 + jnp.exp(-g))) * u).astype(BF16)
    for c in range(0, o_ref.shape[1], FFN_TF):
        cols = slice(c, c + FFN_TF)
        acc = h_ref[:, cols] if first else o_ref[:, cols]
        o_ref[:, cols] = acc + _dot(a, wd[:, cols])


def _ffn_kernel(*refs, n_ff, final_norm, aliased):
    if aliased:
        refs = refs[1:]
    h_ref, g_ref, wg_ref, wu_ref, wd_ref, fg_ref, o_ref, xb_ref, r_ref = refs
    j = pl.program_id(1)

    @pl.when(j == 0)
    def _():
        h = h_ref[...]
        xb = (h * g_ref[...]).astype(BF16)
        r = lax.rsqrt(jnp.mean(h * h, axis=-1, keepdims=True) + EPS)
        xb_ref[...] = xb
        r_ref[...] = r
        _ffn_step(xb, wg_ref[...], wu_ref[...], wd_ref[...], o_ref, first=True, h_ref=h_ref, r=r)

    @pl.when(j > 0)
    def _():
        _ffn_step(xb_ref[...], wg_ref[...], wu_ref[...], wd_ref[...], o_ref, r=r_ref[...])

    if final_norm:
        @pl.when(j == n_ff - 1)
        def _():
            def norm(rows):
                o_ref[rows, :] = _rms_rows(o_ref[rows, :], fg_ref[...])
            _rows_loop(o_ref.shape[0], norm)


def _ffn(h, gain, wg, wu, wd, final_gain, *, final_norm, first_tile=0, into=None, cast_weights=()):
    t, d = h.shape
    f = wg.shape[1]
    n_ff = f // FFN_TF
    grid = (t // FFN_TM - first_tile, n_ff)
    aliased = into is not None
    tile = lambda i, j: (i + first_tile, 0)
    in_specs = [
        pl.BlockSpec((FFN_TM, d), tile),
        pl.BlockSpec((1, d), lambda i, j: (0, 0)),
        pl.BlockSpec((d, FFN_TF), lambda i, j: (0, j)),
        pl.BlockSpec((d, FFN_TF), lambda i, j: (0, j)),
        pl.BlockSpec((FFN_TF, d), lambda i, j: (j, 0)),
        pl.BlockSpec((1, d), lambda i, j: (0, 0)),
    ]
    args = (h, gain, wg, wu, wd, final_gain)
    if aliased:
        in_specs = [pl.BlockSpec(memory_space=pl.ANY)] + in_specs
        args = (into,) + args
    cast_specs, cast_shapes = _cast_specs_tiled(cast_weights, grid)
    host = functools.partial(_ffn_kernel, n_ff=n_ff, final_norm=final_norm, aliased=aliased)
    out = pl.pallas_call(
        _with_casts(host, len(in_specs), 1, len(cast_weights)),
        out_shape=[jax.ShapeDtypeStruct((t, d), F32)] + cast_shapes,
        grid=grid,
        in_specs=in_specs + cast_specs,
        out_specs=[pl.BlockSpec((FFN_TM, d), tile)] + cast_specs,
        scratch_shapes=[pltpu.VMEM((FFN_TM, d), BF16), pltpu.VMEM((FFN_TM, 1), F32)],
        input_output_aliases={0: 0} if aliased else {},
        compiler_params=_params(2),
        name="ffn",
    )(*args, *cast_weights)
    return out[0], out[1:]


def _ffn_head_kernel(h_ref, g_ref, wg_ref, wu_ref, wd_ref, fg_ref, o_ref, wg_out, wu_out, wd_out, xn_ref,
                     *, n_ff, final_norm):
    j = pl.program_id(0)

    @pl.when(j == 0)
    def _():
        _ffn_prologue_full(h_ref, g_ref, o_ref, xn_ref)

    wg, wu, wd = (w[...].astype(BF16) for w in (wg_ref, wu_ref, wd_ref))
    wg_out[...] = wg
    wu_out[...] = wu
    wd_out[...] = wd
    _ffn_step(xn_ref[...], wg, wu, wd, o_ref)

    if final_norm:
        @pl.when(j == n_ff - 1)
        def _():
            def norm(rows):
                o_ref[rows, :] = _rms_rows(o_ref[rows, :], fg_ref[...])
            _rows_loop(o_ref.shape[0], norm)


def _ffn_head(h, gain, wg, wu, wd, final_gain, *, final_norm):
    t, d = h.shape
    f = wg.shape[1]
    tf = FFN_HEAD_TF
    col = pl.BlockSpec((d, tf), lambda j: (0, j))
    row = pl.BlockSpec((tf, d), lambda j: (j, 0))
    vec = pl.BlockSpec((1, d), lambda j: (0, 0))
    tile0 = pl.BlockSpec((FFN_TM, d), lambda j: (0, 0))
    return pl.pallas_call(
        functools.partial(_ffn_head_kernel, n_ff=f // tf, final_norm=final_norm),
        out_shape=[jax.ShapeDtypeStruct((t, d), F32), jax.ShapeDtypeStruct(wg.shape, BF16),
                   jax.ShapeDtypeStruct(wu.shape, BF16), jax.ShapeDtypeStruct(wd.shape, BF16)],
        grid=(f // tf,),
        in_specs=[tile0, vec, col, col, row, vec],
        out_specs=[tile0, col, col, row],
        scratch_shapes=[pltpu.VMEM((FFN_TM, d), BF16)],
        compiler_params=_params(1),
        name="ffn_head",
    )(h, gain, wg, wu, wd, final_gain)


def _ffn_f32(h, gain, wg, wu, wd, final_gain, *, final_norm, cast_weights=()):
    head, wg16, wu16, wd16 = _ffn_head(h, gain, wg, wu, wd, final_gain, final_norm=final_norm)
    return _ffn(h, gain, wg16, wu16, wd16, final_gain, final_norm=final_norm, first_tile=1, into=head,
                cast_weights=cast_weights)


def _rope_tables(seq):
    pos = np.arange(seq, dtype=np.float32)

    def angles(d):
        inv = np.float32(ROPE_THETA) ** (-np.arange(0, d, 2, dtype=np.float32) / np.float32(d))
        return (pos[:, None] * inv[None, :].astype(np.float32)).astype(np.float32).astype(np.float64)

    a128 = angles(RET_DK)
    cos_r = np.concatenate([np.cos(a128), np.cos(a128)], -1)
    sin_r = np.concatenate([-np.sin(a128), np.sin(a128)], -1)
    a64 = angles(SWA_HEAD_DIM)
    c, s, z = np.cos(a64), np.sin(a64), np.zeros_like(a64)
    cos_s = np.concatenate([c, c, c, c], -1)
    sin_lo = np.concatenate([-s, z, -s, z], -1)
    sin_hi = np.concatenate([z, s, z, s], -1)
    return [jnp.asarray(v, dtype=F32) for v in (cos_r, sin_r, cos_s, sin_lo, sin_hi)]


def _retention_tables():
    c = RET_CHUNK
    heads = np.arange(RET_HEADS, dtype=np.float64)
    log_gamma = np.log1p(-np.exp2(-5.0 - heads))
    idx = np.arange(c, dtype=np.float64)
    diff = idx[:, None] - idx[None, :]
    scale = RET_DK ** -0.5
    dmat = np.where(diff[None] >= 0, np.exp(np.maximum(diff, 0.0)[None] * log_gamma[:, None, None]), 0.0)
    zeta = np.exp((c - 1.0 - idx)[None, :] * log_gamma[:, None])
    xi = np.exp((idx + 1.0)[None, :] * log_gamma[:, None])
    chunk_decay = tuple(float(v) for v in np.exp(c * log_gamma))

    def token_tile(tab):
        return np.tile(np.repeat(tab.T, RET_DK, axis=1), (ROW_TM // c, 1))
    tabs = [jnp.asarray(v, dtype=F32) for v in (dmat * scale, token_tile(xi), token_tile(zeta * scale))]
    return tabs, chunk_decay


def _inproj_kernel(h_ref, g_ref, w_ref, cr_ref, sr_ref, cs_ref, sl_ref, sh_ref, xi_ref, zeta_ref, gn_ref,
                   rq_ref, rqx_ref, rk_ref, rkz_ref, rv_ref, rg_ref, sq_ref, sk_ref, sv_ref, xn_ref):
    tm = h_ref.shape[0]

    def norm(rows):
        xn_ref[rows, :] = _rms_rows(h_ref[rows, :], g_ref[...]).astype(BF16)
    _rows_loop(tm, norm)

    xn = xn_ref[...]
    cr, sr = cr_ref[...], sr_ref[...]
    cs, sl, sh = cs_ref[...], sl_ref[...], sh_ref[...]
    half = LANES // 2
    slabs = [slice(s * LANES, (s + 1) * LANES) for s in range(RET_WIDTH // LANES)]

    def rope_ret(x):
        return x * cr + pltpu.roll(x, half, 1) * sr

    def rope_swa(x):
        return x * cs + pltpu.roll(x, LANES - half // 2, 1) * sl + pltpu.roll(x, half // 2, 1) * sh

    def project(col0, width):
        return _dot(xn, w_ref[:, col0:col0 + width])

    def ret_rotary(out_ref, scaled_ref, tab_ref):
        def epilogue(y):
            for sl_ in slabs:
                x = rope_ret(y[:, sl_])
                out_ref[:, sl_] = x.astype(BF16)
                scaled_ref[:, sl_] = (x * tab_ref[:, sl_]).astype(BF16)
        return epilogue

    def ret_values(y):
        for sl_ in slabs:
            rv_ref[:, sl_] = y[:, sl_].astype(BF16)

    def ret_gate(y):
        for sl_ in slabs:
            g = y[:, sl_]
            rg_ref[:, sl_] = (g * (1.0 / (1.0 + jnp.exp(-g))) * gn_ref[:, sl_]).astype(BF16)

    def swa_queries(y):
        scale = SWA_HEAD_DIM ** -0.5 * LOG2E
        for sl_ in slabs:
            sq_ref[:, sl_] = (rope_swa(y[:, sl_]) * scale).astype(BF16)

    def swa_keys_values(y):
        lo = lax.broadcasted_iota(jnp.int32, (tm, LANES), 1) < half
        for transposed, out_ref, x in ((False, sk_ref, rope_swa(y[:, :LANES])), (True, sv_ref, y[:, LANES:])):
            xr = pltpu.roll(x, half, 1)
            variants = (jnp.where(lo, x, 0.0), jnp.where(lo, 0.0, xr),
                        jnp.where(lo, xr, 0.0), jnp.where(lo, 0.0, x))
            for s, v in enumerate(variants):
                if transposed:
                    out_ref[s * LANES:(s + 1) * LANES, :] = v.T.astype(BF16)
                else:
                    out_ref[:, s * LANES:(s + 1) * LANES] = v.astype(BF16)

    epilogues = (ret_rotary(rq_ref, rqx_ref, xi_ref), ret_rotary(rk_ref, rkz_ref, zeta_ref), ret_values, ret_gate,
                 swa_queries, swa_keys_values)
    col0 = np.cumsum((0,) + IN_SIZES[:-2])
    widths = IN_SIZES[:-2] + (2 * SWA_KV_WIDTH,)
    order = (5, 0, 1, 3, 4, 2)
    for r in order:
        epilogues[r](project(int(col0[r]), widths[r]))


def _inproj(h, gain, w_in, gn_gain, xi_tile, zeta_tile, seq):
    t, d = h.shape
    tables = _rope_tables(seq)
    tiles_per_seq = seq // ROW_TM
    tab_spec = pl.BlockSpec((ROW_TM, LANES), lambda i: (i % tiles_per_seq, 0))

    def row_spec(width):
        return pl.BlockSpec((ROW_TM, width), lambda i: (i, 0))

    widths = (RET_WIDTH,) * 6 + (SWA_WIDTH, 4 * LANES)
    return pl.pallas_call(
        _inproj_kernel,
        out_shape=[jax.ShapeDtypeStruct((t, w), BF16) for w in widths]
        + [jax.ShapeDtypeStruct((4 * LANES, t), BF16)],
        grid=(t // ROW_TM,),
        in_specs=[row_spec(d), _resident((1, d)), _resident((d, IN_COLS))] + [tab_spec] * 5
        + [_resident((ROW_TM, RET_WIDTH))] * 2 + [_resident((1, RET_WIDTH))],
        out_specs=[row_spec(w) for w in widths] + [pl.BlockSpec((4 * LANES, ROW_TM), lambda i: (0, i))],
        scratch_shapes=[pltpu.VMEM((ROW_TM, d), BF16)],
        compiler_params=_params(1),
        name="inproj",
    )(h, gain, w_in, *tables, xi_tile, zeta_tile, gn_gain)


def _mixers_kernel(q_ref, qx_ref, k_ref, kz_ref, v_ref, g_ref, dmat_ref,
                   sink_ref, sq_ref, kp_ref, kc_ref, vp_ref, vc_ref, ret_ref, swa_ref, state_ref, *, chunk_decay):
    n = pl.program_id(1)
    w = WINDOW

    @pl.when(n == 0)
    def _():
        state_ref[...] = jnp.zeros_like(state_ref)

    pairs = SWA_HEADS // SWA_KV_HEADS // 2
    key = lax.broadcasted_iota(jnp.int32, (2 * w, pairs * w), 0)
    qry = lax.broadcasted_iota(jnp.int32, (2 * w, pairs * w), 1) % w
    band = (key > qry) & (key <= qry + w)
    bias = jnp.where(band, 0.0, NEG_INF)
    bias_first = jnp.where(band & (key >= jnp.where(n == 0, w, 0)), 0.0, NEG_INF)

    hs = [slice(h * RET_DK, (h + 1) * RET_DK) for h in range(RET_HEADS)]
    lane_blk = lambda c: slice(c * LANES, (c + 1) * LANES)
    slabs = {g: [g * pairs + p for p in range(pairs)] for g in range(SWA_KV_HEADS)}

    for blk in range(MIXER_BLOCKS):
        rows = slice(blk * w, (blk + 1) * w)
        mask_bias = bias if blk else bias_first

        def prev_keys(c):
            return kc_ref[(blk - 1) * w:blk * w, lane_blk(c)] if blk else kp_ref[:, lane_blk(c)]

        def prev_values_t(c):
            return vc_ref[lane_blk(c), (blk - 1) * w:blk * w] if blk else vp_ref[lane_blk(c), :]

        for part in range(MIXER_PARTS):
            heads = range(part * RET_HEADS // MIXER_PARTS, (part + 1) * RET_HEADS // MIXER_PARTS)
            groups = range(part * SWA_KV_HEADS // MIXER_PARTS, (part + 1) * SWA_KV_HEADS // MIXER_PARTS)
            phases = [(g, e) for g in groups for e in range(2)]

            s_ret = {h: _dot_nt(q_ref[rows, hs[h]], k_ref[rows, hs[h]]) for h in heads}
            q_swa = {g: jnp.concatenate([sq_ref[rows, lane_blk(sl)] for sl in slabs[g]], 0) for g in groups}
            s_swa = []
            for g, e in phases:
                c = 2 * g + e
                k = jnp.concatenate([prev_keys(c), kc_ref[rows, lane_blk(c)]], 0)
                s_swa.append(_dot_nt(k, q_swa[g]) + mask_bias)

            states = {h: state_ref[h] for h in heads}
            ret = {}
            for h in heads:
                lhs = jnp.concatenate([(s_ret[h] * dmat_ref[h]).astype(BF16), qx_ref[rows, hs[h]]], 1)
                rhs = jnp.concatenate([v_ref[rows, hs[h]], states[h].astype(BF16)], 0)
                ret[h] = _dot(lhs, rhs)
            for h in heads:
                state_ref[h] = states[h] * chunk_decay[h] + _dot_tn(kz_ref[rows, hs[h]], v_ref[rows, hs[h]])

            pv = {}
            for (g, e), s in zip(phases, s_swa):
                c = 2 * g + e
                sink = jnp.concatenate([jnp.full((1, w), sink_ref[2 * sl + e] * LOG2E, F32) for sl in slabs[g]], 1)
                m = jnp.maximum(jnp.max(s, axis=0, keepdims=True), sink)
                p = jnp.exp2(s - m)
                inv = 1.0 / (jnp.sum(p, axis=0, keepdims=True) + jnp.exp2(sink - m))
                v_t = jnp.concatenate([prev_values_t(c), vc_ref[lane_blk(c), rows]], 1)
                pv[g, e] = _dot(v_t, p.astype(BF16)) * inv

            for h in heads:
                mu = jnp.mean(ret[h], axis=-1, keepdims=True)
                cen = ret[h] - mu
                var = jnp.mean(cen * cen, axis=-1, keepdims=True)
                ret_ref[rows, hs[h]] = (cen * lax.rsqrt(var + EPS) * g_ref[rows, hs[h]].astype(F32)).astype(BF16)

            for g in groups:
                acc = pv[g, 0] + pv[g, 1]
                for i, sl in enumerate(slabs[g]):
                    swa_ref[lane_blk(sl), rows] = acc[:, i * w:(i + 1) * w].astype(BF16)


def _mixers(rq, rqx, rk, rkz, rv, rg, dmat, chunk_decay, sq, sk4, sv4_t, sinks, batch, seq, cast_weights):
    assert RET_CHUNK == WINDOW
    t = rq.shape[0]
    rows = MIXER_BLOCKS * WINDOW
    n_steps = seq // rows
    grid = (batch, n_steps)
    cur = lambda b, n: (b * n_steps + n, 0)
    cur_t = lambda b, n: (0, b * n_steps + n)
    prev_idx = lambda b, n: b * n_steps * MIXER_BLOCKS + jnp.maximum(n * MIXER_BLOCKS - 1, 0)
    blk = pl.BlockSpec((rows, RET_WIDTH), cur)
    in_specs = [
        blk, blk, blk, blk, blk, blk, _resident((RET_HEADS, RET_CHUNK, RET_CHUNK)),
        pl.BlockSpec(memory_space=pltpu.SMEM),
        pl.BlockSpec((rows, SWA_WIDTH), cur),
        pl.BlockSpec((WINDOW, 4 * LANES), lambda b, n: (prev_idx(b, n), 0)),
        pl.BlockSpec((rows, 4 * LANES), cur),
        pl.BlockSpec((4 * LANES, WINDOW), lambda b, n: (0, prev_idx(b, n))),
        pl.BlockSpec((4 * LANES, rows), cur_t),
    ]
    cast_specs, cast_shapes = _cast_specs(cast_weights, grid)
    host = functools.partial(_mixers_kernel, chunk_decay=chunk_decay)
    out = pl.pallas_call(
        _with_casts(host, len(in_specs), 2, len(cast_weights)),
        out_shape=[jax.ShapeDtypeStruct((t, RET_WIDTH), BF16), jax.ShapeDtypeStruct((SWA_WIDTH, t), BF16)]
        + cast_shapes,
        grid=grid,
        in_specs=in_specs + cast_specs,
        out_specs=[blk, pl.BlockSpec((SWA_WIDTH, rows), cur_t)] + cast_specs,
        scratch_shapes=[pltpu.VMEM((RET_HEADS, RET_DK, RET_DV), F32)],
        compiler_params=_params(2),
        name="mixers",
    )(rq, rqx, rk, rkz, rv, rg, dmat, sinks, sq, sk4, sk4, sv4_t, sv4_t, *cast_weights)
    return out[0], out[1], out[2:]


def _outproj_kernel(h_ref, ret_ref, swa_ref, w_ref, mem_ref, mem_gain_ref, wkv_ref, h2_ref, mkv_ref, memn_ref):
    @pl.when(pl.program_id(0) == 0)
    def _():
        memn_ref[...] = _rms_rows(mem_ref[...], mem_gain_ref[...]).astype(BF16)

    y = _dot(ret_ref[...], w_ref[:RET_WIDTH, :]) + _dot_tn(swa_ref[...], w_ref[RET_WIDTH:, :])
    h2_ref[...] = h_ref[...] + y
    mkv_ref[...] = _dot(memn_ref[...], wkv_ref[...].astype(BF16)).astype(BF16)


def _outproj(h, ret, swa, w_out, mem, mem_gain, wkv):
    t, d = h.shape
    n_steps = t // ROW_TM
    kv_cols = wkv.shape[1] // n_steps
    row = lambda width: pl.BlockSpec((ROW_TM, width), lambda i: (i, 0))
    return pl.pallas_call(
        _outproj_kernel,
        out_shape=[jax.ShapeDtypeStruct((t, d), F32), jax.ShapeDtypeStruct((mem.shape[0], wkv.shape[1]), BF16)],
        grid=(n_steps,),
        in_specs=[row(d), row(RET_WIDTH), pl.BlockSpec((SWA_WIDTH, ROW_TM), lambda i: (0, i)),
                  _resident(w_out.shape),
                  _resident(mem.shape), _resident(mem_gain.shape),
                  pl.BlockSpec((wkv.shape[0], kv_cols), lambda i: (0, i))],
        out_specs=[row(d), pl.BlockSpec((mem.shape[0], kv_cols), lambda i: (0, i))],
        scratch_shapes=[pltpu.VMEM(mem.shape, BF16)],
        compiler_params=_params(1),
        name="outproj",
    )(h, ret, swa, w_out, mem, mem_gain, wkv)


def _xattn_kernel(h_ref, g_ref, wq_ref, wo_ref, k_ref, v_ref, o_ref):
    h = h_ref[...]
    r = lax.rsqrt(jnp.mean(h * h, axis=-1, keepdims=True) + EPS)
    q = (_dot((h * g_ref[...]).astype(BF16), wq_ref[...]) * r).astype(BF16)
    scale = XA_HEAD_DIM ** -0.5
    heads = [slice(hd * XA_HEAD_DIM, (hd + 1) * XA_HEAD_DIM) for hd in range(XA_HEADS)]
    scores = [_dot_nt(q[:, hs], k_ref[:, hs]) * scale for hs in heads]
    probs, invs = [], []
    for s in scores:
        p = jnp.exp(s - jnp.max(s, axis=-1, keepdims=True))
        probs.append(p.astype(BF16))
        invs.append(1.0 / jnp.sum(p, axis=-1, keepdims=True))
    out = h_ref[...]
    for hs, p, inv in zip(heads, probs, invs):
        att = (_dot(p, v_ref[:, hs]) * inv).astype(BF16)
        out = out + _dot(att, wo_ref[hs, :])
    o_ref[...] = out


def _xattn(h, gain, wq, wo, mkv, seq, mem_len):
    t, d = h.shape
    tiles_per_seq = seq // ROW_TM
    row = lambda i: (i, 0)
    return pl.pallas_call(
        _xattn_kernel,
        out_shape=jax.ShapeDtypeStruct((t, d), F32),
        grid=(t // ROW_TM,),
        in_specs=[
            pl.BlockSpec((ROW_TM, d), row),
            _resident((1, d)),
            _resident((d, d)),
            _resident((d, d)),
            pl.BlockSpec((mem_len, d), lambda i: (i // tiles_per_seq, 0)),
            pl.BlockSpec((mem_len, d), lambda i: (i // tiles_per_seq, 1)),
        ],
        out_specs=pl.BlockSpec((ROW_TM, d), row),
        compiler_params=_params(1),
        name="xattn",
    )(h, gain, wq, wo, mkv, mkv)


def kernel(x, mem, ffn1_norm, ffn1_w_gate, ffn1_w_up, ffn1_w_down, mix_norm, w_in, ret_gn_gain, swa_sinks,
           w_out, xa_norm, mem_norm, xa_wq, xa_wkv, xa_wo, ffn2_norm, ffn2_w_gate, ffn2_w_up, ffn2_w_down,
           final_norm):
    batch, seq, d = x.shape
    mem_len = mem.shape[1]
    depth = ffn1_norm.shape[0]
    h = x.reshape(batch * seq, d)
    mem2 = mem.reshape(batch * mem_len, d)
    row = lambda g: g.reshape(1, -1).astype(F32)
    final_gain = row(final_norm)

    for l in range(depth):
        last = l == depth - 1
        h, (w_mix,) = _ffn_f32(h, row(ffn1_norm[l]), ffn1_w_gate[l], ffn1_w_up[l], ffn1_w_down[l], final_gain,
                               final_norm=False, cast_weights=[w_in[l]])
        (dmat, xi_tile, zeta_tile), chunk_decay = _retention_tables()
        rq, rqx, rk, rkz, rv, rg, sq, sk4, sv4_t = _inproj(h, row(mix_norm[l]), w_mix, row(ret_gn_gain[l]),
                                                           xi_tile, zeta_tile, seq)
        ret, swa, (wo_mix, wq, wo) = _mixers(
            rq, rqx, rk, rkz, rv, rg, dmat, chunk_decay, sq, sk4, sv4_t, swa_sinks[l].astype(F32), batch, seq,
            [w_out[l], xa_wq[l], xa_wo[l]])
        h, mkv = _outproj(h, ret, swa, wo_mix, mem2, row(mem_norm[l]), xa_wkv[l])
        h = _xattn(h, row(xa_norm[l]), wq, wo, mkv, seq, mem_len)
        h, _ = _ffn_f32(h, row(ffn2_norm[l]), ffn2_w_gate[l], ffn2_w_up[l], ffn2_w_down[l], final_gain,
                        final_norm=last)
    if depth == 0:
        raise ValueError("depth must be at least 1")
    return h.reshape(batch, seq, d)
```

```python
import functools

import numpy as np
import jax
import jax.numpy as jnp
from jax import lax
from jax.experimental import pallas as pl
from jax.experimental.pallas import tpu as pltpu

F32 = jnp.float32
BF16 = jnp.bfloat16

D_MODEL = 2048
RET_HEADS = 8
RET_DK = 128
RET_DV = 128
RET_WIDTH = RET_HEADS * RET_DV
RET_CHUNK = 128
SWA_HEADS = 16
SWA_KV_HEADS = 2
SWA_HEAD_DIM = 64
SWA_WIDTH = SWA_HEADS * SWA_HEAD_DIM
SWA_KV_WIDTH = SWA_KV_HEADS * SWA_HEAD_DIM
WINDOW = 128
XA_HEADS = 4
XA_HEAD_DIM = D_MODEL // XA_HEADS
ROPE_THETA = 10000.0
EPS = 1e-6
IN_SIZES = (RET_WIDTH, RET_WIDTH, RET_WIDTH, RET_WIDTH, SWA_WIDTH, SWA_KV_WIDTH, SWA_KV_WIDTH)
IN_COLS = sum(IN_SIZES)

LANES = 128
BF16_SUBLANES = 16
VMEM_LIMIT_BYTES = 60000 * 1024

FFN_TM = 1024
FFN_TF = 512
FFN_HEAD_TF = 256
ROW_TM = 512
NORM_ROWS = 512
MIXER_PARTS = 2
MIXER_BLOCKS = 4
NEG_INF = float(np.finfo(np.float32).min)
LOG2E = float(np.log2(np.e))


def _params(n_axes):
    return pltpu.CompilerParams(
        dimension_semantics=("arbitrary",) * n_axes,
        vmem_limit_bytes=VMEM_LIMIT_BYTES,
    )


def _resident(shape):
    zeros = (0,) * len(shape)
    return pl.BlockSpec(shape, lambda *_: zeros, pipeline_mode=pl.Buffered(1))


def _rms_rows(x, gain):
    ms = jnp.mean(x * x, axis=-1, keepdims=True)
    return x * lax.rsqrt(ms + EPS) * gain


def _dot(a, b):
    return lax.dot_general(a, b, (((1,), (0,)), ((), ())), preferred_element_type=F32)


def _dot_nt(a, b):
    return lax.dot_general(a, b, (((1,), (1,)), ((), ())), preferred_element_type=F32)


def _dot_tn(a, b):
    return lax.dot_general(a, b, (((0,), (0,)), ((), ())), preferred_element_type=F32)


def _cast_block(shape, n_steps):
    rows, cols = shape
    for col_splits in (1, 2, 4, 8):
        row_blocks, rem = divmod(n_steps, col_splits)
        if rem or rows % row_blocks or cols % col_splits:
            continue
        br, bc = rows // row_blocks, cols // col_splits
        if br % BF16_SUBLANES == 0 and bc % LANES == 0:
            return br, bc, col_splits
    raise ValueError(f"no aligned {n_steps}-way split of {shape}")


def _with_casts(kernel_fn, n_in, n_out, n_cast):
    def wrapped(*refs):
        ins, rest = refs[:n_in], refs[n_in:]
        cast_in, rest = rest[:n_cast], rest[n_cast:]
        outs, rest = rest[:n_out], rest[n_out:]
        cast_out, scratch = rest[:n_cast], rest[n_cast:]
        kernel_fn(*ins, *outs, *scratch)
        for src, dst in zip(cast_in, cast_out):
            dst[...] = src[...].astype(BF16)
    return wrapped


def _cast_specs(weights, grid):
    n_steps = int(np.prod(grid))
    specs, shapes = [], []
    for w in weights:
        br, bc, col_splits = _cast_block(w.shape, n_steps)

        def index_map(*idx, col_splits=col_splits):
            step = idx[0]
            for size, i in zip(grid[1:], idx[1:]):
                step = step * size + i
            return step // col_splits, step % col_splits
        specs.append(pl.BlockSpec((br, bc), index_map))
        shapes.append(jax.ShapeDtypeStruct(w.shape, BF16))
    return specs, shapes


def _cast_specs_tiled(weights, grid):
    n_outer, n_inner = grid
    specs, shapes = [], []
    for w in weights:
        rows, cols = w.shape
        bc = cols // n_outer
        row_blocks = max(r for r in range(1, n_inner + 1) if rows % r == 0 and (rows // r) % BF16_SUBLANES == 0)
        if cols % n_outer or bc % LANES:
            raise ValueError(f"no aligned split of {w.shape} over {grid}")

        def index_map(i, j, row_blocks=row_blocks):
            return jnp.minimum(j, row_blocks - 1), i
        specs.append(pl.BlockSpec((rows // row_blocks, bc), index_map))
        shapes.append(jax.ShapeDtypeStruct(w.shape, BF16))
    return specs, shapes


def _rows_loop(n_rows, fn):
    def body(r, carry):
        fn(pl.ds(pl.multiple_of(r * NORM_ROWS, NORM_ROWS), NORM_ROWS))
        return carry
    lax.fori_loop(0, n_rows // NORM_ROWS, body, 0)


def _ffn_step(xb, wg, wu, wd, o_ref, *, first=False, h_ref=None, r=None):
    g = _dot(xb, wg)
    u = _dot(xb, wu)
    if r is not None:
        g, u = g * r, u * r
    a = (g * (0.5 / (1.0 + jnp.exp(-g))) * u).astype(BF16)
    for c in range(0, o_ref.shape[1], FFN_TF):
        cols = slice(c, c + FFN_TF)
        acc = h_ref[:, cols] if first else o_ref[:, cols]
        o_ref[:, cols] = acc + _dot(a, wd[:, cols])


def _ffn_kernel(*refs, n_ff, final_norm, aliased):
    if aliased:
        refs = refs[1:]
    h_ref, g_ref, wg_ref, wu_ref, wd_ref, fg_ref, o_ref, xb_ref, r_ref = refs
    j = pl.program_id(1)

    @pl.when(j == 0)
    def _():
        h = h_ref[...]
        xb = (h * g_ref[...]).astype(BF16)
        r = lax.rsqrt(jnp.mean(h * h, axis=-1, keepdims=True) + EPS)
        xb_ref[...] = xb
        r_ref[...] = r
        _ffn_step(xb, wg_ref[...], wu_ref[...], wd_ref[...], o_ref, first=True, h_ref=h_ref, r=r)

    @pl.when(j > 0)
    def _():
        _ffn_step(xb_ref[...], wg_ref[...], wu_ref[...], wd_ref[...], o_ref, r=r_ref[...])

    if final_norm:
        @pl.when(j == n_ff - 1)
        def _():
            def norm(rows):
                o_ref[rows, :] = _rms_rows(o_ref[rows, :], fg_ref[...])
            _rows_loop(o_ref.shape[0], norm)


def _ffn(h, gain, wg, wu, wd, final_gain, *, final_norm, first_tile=0, into=None, cast_weights=()):
    t, d = h.shape
    f = wg.shape[1]
    n_ff = f // FFN_TF
    grid = (t // FFN_TM - first_tile, n_ff)
    aliased = into is not None
    tile = lambda i, j: (i + first_tile, 0)
    in_specs = [
        pl.BlockSpec((FFN_TM, d), tile),
        pl.BlockSpec((1, d), lambda i, j: (0, 0)),
        pl.BlockSpec((d, FFN_TF), lambda i, j: (0, j)),
        pl.BlockSpec((d, FFN_TF), lambda i, j: (0, j)),
        pl.BlockSpec((FFN_TF, d), lambda i, j: (j, 0)),
        pl.BlockSpec((1, d), lambda i, j: (0, 0)),
    ]
    args = (h, gain, wg, wu, wd, final_gain)
    if aliased:
        in_specs = [pl.BlockSpec(memory_space=pl.ANY)] + in_specs
        args = (into,) + args
    cast_specs, cast_shapes = _cast_specs_tiled(cast_weights, grid)
    host = functools.partial(_ffn_kernel, n_ff=n_ff, final_norm=final_norm, aliased=aliased)
    out = pl.pallas_call(
        _with_casts(host, len(in_specs), 1, len(cast_weights)),
        out_shape=[jax.ShapeDtypeStruct((t, d), F32)] + cast_shapes,
        grid=grid,
        in_specs=in_specs + cast_specs,
        out_specs=[pl.BlockSpec((FFN_TM, d), tile)] + cast_specs,
        scratch_shapes=[pltpu.VMEM((FFN_TM, d), BF16), pltpu.VMEM((FFN_TM, 1), F32)],
        input_output_aliases={0: 0} if aliased else {},
        compiler_params=_params(2),
        name="ffn",
    )(*args, *cast_weights)
    return out[0], out[1:]


def _ffn_head_kernel(h_ref, g_ref, wg_ref, wu_ref, wd_ref, fg_ref, o_ref, wg_out, wu_out, wd_out, xn_ref,
                     *, n_ff, final_norm):
    j = pl.program_id(0)

    def weights():
        wg, wu, wd = (w[...].astype(BF16) for w in (wg_ref, wu_ref, wd_ref))
        wg_out[...] = wg
        wu_out[...] = wu
        wd_out[...] = wd
        return wg, wu, wd

    @pl.when(j == 0)
    def _():
        xn = _rms_rows(h_ref[...], g_ref[...]).astype(BF16)
        xn_ref[...] = xn
        _ffn_step(xn, *weights(), o_ref, first=True, h_ref=h_ref)

    @pl.when(j > 0)
    def _():
        _ffn_step(xn_ref[...], *weights(), o_ref)

    if final_norm:
        @pl.when(j == n_ff - 1)
        def _():
            def norm(rows):
                o_ref[rows, :] = _rms_rows(o_ref[rows, :], fg_ref[...])
            _rows_loop(o_ref.shape[0], norm)


def _ffn_head(h, gain, wg, wu, wd, final_gain, *, final_norm):
    t, d = h.shape
    f = wg.shape[1]
    tf = FFN_HEAD_TF
    col = pl.BlockSpec((d, tf), lambda j: (0, j))
    row = pl.BlockSpec((tf, d), lambda j: (j, 0))
    vec = pl.BlockSpec((1, d), lambda j: (0, 0))
    tile0 = pl.BlockSpec((FFN_TM, d), lambda j: (0, 0))
    return pl.pallas_call(
        functools.partial(_ffn_head_kernel, n_ff=f // tf, final_norm=final_norm),
        out_shape=[jax.ShapeDtypeStruct((t, d), F32), jax.ShapeDtypeStruct(wg.shape, BF16),
                   jax.ShapeDtypeStruct(wu.shape, BF16), jax.ShapeDtypeStruct(wd.shape, BF16)],
        grid=(f // tf,),
        in_specs=[tile0, vec, col, col, row, vec],
        out_specs=[tile0, col, col, row],
        scratch_shapes=[pltpu.VMEM((FFN_TM, d), BF16)],
        compiler_params=_params(1),
        name="ffn_head",
    )(h, gain, wg, wu, wd, final_gain)


def _ffn_f32(h, gain, wg, wu, wd, final_gain, *, final_norm, cast_weights=()):
    head, wg16, wu16, wd16 = _ffn_head(h, gain, wg, wu, wd, final_gain, final_norm=final_norm)
    return _ffn(h, gain, wg16, wu16, wd16, final_gain, final_norm=final_norm, first_tile=1, into=head,
                cast_weights=cast_weights)


def _rope_tables(seq):
    pos = np.arange(seq, dtype=np.float32)

    def angles(d):
        inv = np.float32(ROPE_THETA) ** (-np.arange(0, d, 2, dtype=np.float32) / np.float32(d))
        return (pos[:, None] * inv[None, :].astype(np.float32)).astype(np.float32).astype(np.float64)

    a128 = angles(RET_DK)
    cos_r = np.concatenate([np.cos(a128), np.cos(a128)], -1)
    sin_r = np.concatenate([-np.sin(a128), np.sin(a128)], -1)
    a64 = angles(SWA_HEAD_DIM)
    c, s, z = np.cos(a64), np.sin(a64), np.zeros_like(a64)
    cos_s = np.concatenate([c, c, c, c], -1)
    sin_lo = np.concatenate([-s, z, -s, z], -1)
    sin_hi = np.concatenate([z, s, z, s], -1)
    return [jnp.asarray(v, dtype=F32) for v in (cos_r, sin_r, cos_s, sin_lo, sin_hi)]


def _retention_tables():
    c = RET_CHUNK
    heads = np.arange(RET_HEADS, dtype=np.float64)
    log_gamma = np.log1p(-np.exp2(-5.0 - heads))
    idx = np.arange(c, dtype=np.float64)
    diff = idx[:, None] - idx[None, :]
    scale = RET_DK ** -0.5
    dmat = np.where(diff[None] >= 0, np.exp(np.maximum(diff, 0.0)[None] * log_gamma[:, None, None]), 0.0)
    zeta = np.exp((c - 1.0 - idx)[None, :] * log_gamma[:, None])
    xi = np.exp((idx + 1.0)[None, :] * log_gamma[:, None])
    chunk_decay = tuple(float(v) for v in np.exp(c * log_gamma))

    def token_tile(tab):
        return np.tile(np.repeat(tab.T, RET_DK, axis=1), (ROW_TM // c, 1))
    tabs = [jnp.asarray(v, dtype=F32) for v in (dmat * scale, token_tile(xi), token_tile(zeta * scale))]
    return tabs, chunk_decay


def _inproj_kernel(h_ref, g_ref, w_ref, cr_ref, sr_ref, cs_ref, sl_ref, sh_ref, xi_ref, zeta_ref, gn_ref,
                   rq_ref, rqx_ref, rk_ref, rkz_ref, rv_ref, rg_ref, sq_ref, sk_ref, sv_ref):
    tm = h_ref.shape[0]
    xn = _rms_rows(h_ref[...], g_ref[...]).astype(BF16)
    cr, sr = cr_ref[...], sr_ref[...]
    cs, sl, sh = cs_ref[...], sl_ref[...], sh_ref[...]
    half = LANES // 2
    slabs = [slice(s * LANES, (s + 1) * LANES) for s in range(RET_WIDTH // LANES)]

    def rope_ret(x):
        return x * cr + pltpu.roll(x, half, 1) * sr

    def rope_swa(x):
        return x * cs + pltpu.roll(x, LANES - half // 2, 1) * sl + pltpu.roll(x, half // 2, 1) * sh

    def project(col0, width):
        return _dot(xn, w_ref[:, col0:col0 + width])

    def ret_rotary(out_ref, scaled_ref, tab_ref):
        def epilogue(y):
            for sl_ in slabs:
                x = rope_ret(y[:, sl_])
                out_ref[:, sl_] = x.astype(BF16)
                scaled_ref[:, sl_] = (x * tab_ref[:, sl_]).astype(BF16)
        return epilogue

    def ret_values(y):
        for sl_ in slabs:
            rv_ref[:, sl_] = y[:, sl_].astype(BF16)

    def ret_gate(y):
        for sl_ in slabs:
            g = y[:, sl_]
            rg_ref[:, sl_] = (g * (1.0 / (1.0 + jnp.exp(-g))) * gn_ref[:, sl_]).astype(BF16)

    def swa_queries(y):
        scale = SWA_HEAD_DIM ** -0.5 * LOG2E
        for sl_ in slabs:
            sq_ref[:, sl_] = (rope_swa(y[:, sl_]) * scale).astype(BF16)

    def swa_keys_values(y):
        lo = lax.broadcasted_iota(jnp.int32, (tm, LANES), 1) < half
        for transposed, out_ref, x in ((False, sk_ref, rope_swa(y[:, :LANES])), (True, sv_ref, y[:, LANES:])):
            xr = pltpu.roll(x, half, 1)
            variants = (jnp.where(lo, x, 0.0), jnp.where(lo, 0.0, xr),
                        jnp.where(lo, xr, 0.0), jnp.where(lo, 0.0, x))
            for s, v in enumerate(variants):
                if transposed:
                    out_ref[s * LANES:(s + 1) * LANES, :] = v.T.astype(BF16)
                else:
                    out_ref[:, s * LANES:(s + 1) * LANES] = v.astype(BF16)

    epilogues = (ret_rotary(rq_ref, rqx_ref, xi_ref), ret_rotary(rk_ref, rkz_ref, zeta_ref), ret_values, ret_gate,
                 swa_queries, swa_keys_values)
    col0 = np.cumsum((0,) + IN_SIZES[:-2])
    widths = IN_SIZES[:-2] + (2 * SWA_KV_WIDTH,)
    order = (5, 0, 1, 3, 4, 2)
    for r in order:
        epilogues[r](project(int(col0[r]), widths[r]))


def _inproj(h, gain, w_in, gn_gain, xi_tile, zeta_tile, seq):
    t, d = h.shape
    tables = _rope_tables(seq)
    tiles_per_seq = seq // ROW_TM
    tab_spec = pl.BlockSpec((ROW_TM, LANES), lambda i: (i % tiles_per_seq, 0))

    def row_spec(width):
        return pl.BlockSpec((ROW_TM, width), lambda i: (i, 0))

    widths = (RET_WIDTH,) * 6 + (SWA_WIDTH, 4 * LANES)
    return pl.pallas_call(
        _inproj_kernel,
        out_shape=[jax.ShapeDtypeStruct((t, w), BF16) for w in widths]
        + [jax.ShapeDtypeStruct((4 * LANES, t), BF16)],
        grid=(t // ROW_TM,),
        in_specs=[row_spec(d), _resident((1, d)), _resident((d, IN_COLS))] + [tab_spec] * 5
        + [_resident((ROW_TM, RET_WIDTH))] * 2 + [_resident((1, RET_WIDTH))],
        out_specs=[row_spec(w) for w in widths] + [pl.BlockSpec((4 * LANES, ROW_TM), lambda i: (0, i))],
        compiler_params=_params(1),
        name="inproj",
    )(h, gain, w_in, *tables, xi_tile, zeta_tile, gn_gain)


def _mixers_kernel(q_ref, qx_ref, k_ref, kz_ref, v_ref, g_ref, dmat_ref,
                   sink_ref, sq_ref, kp_ref, kc_ref, vp_ref, vc_ref, ret_ref, swa_ref, state_ref, *, chunk_decay):
    n = pl.program_id(1)
    w = WINDOW

    @pl.when(n == 0)
    def _():
        state_ref[...] = jnp.zeros_like(state_ref)

    pairs = SWA_HEADS // SWA_KV_HEADS // 2
    key = lax.broadcasted_iota(jnp.int32, (2 * w, pairs * w), 0)
    qry = lax.broadcasted_iota(jnp.int32, (2 * w, pairs * w), 1) % w
    band = (key > qry) & (key <= qry + w)
    bias = jnp.where(band, 0.0, NEG_INF)
    bias_first = jnp.where(band & (key >= jnp.where(n == 0, w, 0)), 0.0, NEG_INF)

    hs = [slice(h * RET_DK, (h + 1) * RET_DK) for h in range(RET_HEADS)]
    lane_blk = lambda c: slice(c * LANES, (c + 1) * LANES)
    slabs = {g: [g * pairs + p for p in range(pairs)] for g in range(SWA_KV_HEADS)}

    for blk in range(MIXER_BLOCKS):
        rows = slice(blk * w, (blk + 1) * w)
        mask_bias = bias if blk else bias_first

        def prev_keys(c):
            return kc_ref[(blk - 1) * w:blk * w, lane_blk(c)] if blk else kp_ref[:, lane_blk(c)]

        def prev_values_t(c):
            return vc_ref[lane_blk(c), (blk - 1) * w:blk * w] if blk else vp_ref[lane_blk(c), :]

        for part in range(MIXER_PARTS):
            heads = range(part * RET_HEADS // MIXER_PARTS, (part + 1) * RET_HEADS // MIXER_PARTS)
            groups = range(part * SWA_KV_HEADS // MIXER_PARTS, (part + 1) * SWA_KV_HEADS // MIXER_PARTS)
            phases = [(g, e) for g in groups for e in range(2)]

            s_ret = {h: _dot_nt(q_ref[rows, hs[h]], k_ref[rows, hs[h]]) for h in heads}
            q_swa = {g: jnp.concatenate([sq_ref[rows, lane_blk(sl)] for sl in slabs[g]], 0) for g in groups}
            s_swa = []
            for g, e in phases:
                c = 2 * g + e
                k = jnp.concatenate([prev_keys(c), kc_ref[rows, lane_blk(c)]], 0)
                s_swa.append(_dot_nt(k, q_swa[g]) + mask_bias)

            states = {h: state_ref[h] for h in heads}
            ret = {}
            for h in heads:
                lhs = jnp.concatenate([(s_ret[h] * dmat_ref[h]).astype(BF16), qx_ref[rows, hs[h]]], 1)
                rhs = jnp.concatenate([v_ref[rows, hs[h]], states[h].astype(BF16)], 0)
                ret[h] = _dot(lhs, rhs)
            for h in heads:
                state_ref[h] = states[h] * chunk_decay[h] + _dot_tn(kz_ref[rows, hs[h]], v_ref[rows, hs[h]])

            pv = {}
            for (g, e), s in zip(phases, s_swa):
                c = 2 * g + e
                sink = jnp.concatenate([jnp.full((1, w), sink_ref[2 * sl + e] * LOG2E, F32) for sl in slabs[g]], 1)
                m = jnp.maximum(jnp.max(s, axis=0, keepdims=True), sink)
                p = jnp.exp2(s - m)
                inv = 1.0 / (jnp.sum(p, axis=0, keepdims=True) + jnp.exp2(sink - m))
                v_t = jnp.concatenate([prev_values_t(c), vc_ref[lane_blk(c), rows]], 1)
                pv[g, e] = _dot(v_t, p.astype(BF16)) * inv

            for h in heads:
                mu = jnp.mean(ret[h], axis=-1, keepdims=True)
                cen = ret[h] - mu
                var = jnp.mean(cen * cen, axis=-1, keepdims=True)
                ret_ref[rows, hs[h]] = (cen * lax.rsqrt(var + EPS) * g_ref[rows, hs[h]].astype(F32)).astype(BF16)

            for g in groups:
                acc = pv[g, 0] + pv[g, 1]
                for i, sl in enumerate(slabs[g]):
                    swa_ref[lane_blk(sl), rows] = acc[:, i * w:(i + 1) * w].astype(BF16)


def _mixers(rq, rqx, rk, rkz, rv, rg, dmat, chunk_decay, sq, sk4, sv4_t, sinks, batch, seq, cast_weights):
    assert RET_CHUNK == WINDOW
    t = rq.shape[0]
    rows = MIXER_BLOCKS * WINDOW
    n_steps = seq // rows
    grid = (batch, n_steps)
    cur = lambda b, n: (b * n_steps + n, 0)
    cur_t = lambda b, n: (0, b * n_steps + n)
    prev_idx = lambda b, n: b * n_steps * MIXER_BLOCKS + jnp.maximum(n * MIXER_BLOCKS - 1, 0)
    blk = pl.BlockSpec((rows, RET_WIDTH), cur)
    in_specs = [
        blk, blk, blk, blk, blk, blk, _resident((RET_HEADS, RET_CHUNK, RET_CHUNK)),
        pl.BlockSpec(memory_space=pltpu.SMEM),
        pl.BlockSpec((rows, SWA_WIDTH), cur),
        pl.BlockSpec((WINDOW, 4 * LANES), lambda b, n: (prev_idx(b, n), 0)),
        pl.BlockSpec((rows, 4 * LANES), cur),
        pl.BlockSpec((4 * LANES, WINDOW), lambda b, n: (0, prev_idx(b, n))),
        pl.BlockSpec((4 * LANES, rows), cur_t),
    ]
    cast_specs, cast_shapes = _cast_specs(cast_weights, grid)
    host = functools.partial(_mixers_kernel, chunk_decay=chunk_decay)
    out = pl.pallas_call(
        _with_casts(host, len(in_specs), 2, len(cast_weights)),
        out_shape=[jax.ShapeDtypeStruct((t, RET_WIDTH), BF16), jax.ShapeDtypeStruct((SWA_WIDTH, t), BF16)]
        + cast_shapes,
        grid=grid,
        in_specs=in_specs + cast_specs,
        out_specs=[blk, pl.BlockSpec((SWA_WIDTH, rows), cur_t)] + cast_specs,
        scratch_shapes=[pltpu.VMEM((RET_HEADS, RET_DK, RET_DV), F32)],
        compiler_params=_params(2),
        name="mixers",
    )(rq, rqx, rk, rkz, rv, rg, dmat, sinks, sq, sk4, sk4, sv4_t, sv4_t, *cast_weights)
    return out[0], out[1], out[2:]


def _outproj_kernel(h_ref, ret_ref, swa_ref, w_ref, mem_ref, mem_gain_ref, wkv_ref, h2_ref, mkv_ref, memn_ref):
    @pl.when(pl.program_id(0) == 0)
    def _():
        memn_ref[...] = _rms_rows(mem_ref[...], mem_gain_ref[...]).astype(BF16)

    y = _dot(ret_ref[...], w_ref[:RET_WIDTH, :]) + _dot_tn(swa_ref[...], w_ref[RET_WIDTH:, :])
    h2_ref[...] = h_ref[...] + y
    mkv_ref[...] = _dot(memn_ref[...], wkv_ref[...].astype(BF16)).astype(BF16)


def _outproj(h, ret, swa, w_out, mem, mem_gain, wkv):
    t, d = h.shape
    n_steps = t // ROW_TM
    kv_cols = wkv.shape[1] // n_steps
    row = lambda width: pl.BlockSpec((ROW_TM, width), lambda i: (i, 0))
    return pl.pallas_call(
        _outproj_kernel,
        out_shape=[jax.ShapeDtypeStruct((t, d), F32), jax.ShapeDtypeStruct((mem.shape[0], wkv.shape[1]), BF16)],
        grid=(n_steps,),
        in_specs=[row(d), row(RET_WIDTH), pl.BlockSpec((SWA_WIDTH, ROW_TM), lambda i: (0, i)),
                  _resident(w_out.shape),
                  _resident(mem.shape), _resident(mem_gain.shape),
                  pl.BlockSpec((wkv.shape[0], kv_cols), lambda i: (0, i))],
        out_specs=[row(d), pl.BlockSpec((mem.shape[0], kv_cols), lambda i: (0, i))],
        scratch_shapes=[pltpu.VMEM(mem.shape, BF16)],
        compiler_params=_params(1),
        name="outproj",
    )(h, ret, swa, w_out, mem, mem_gain, wkv)


def _xattn_kernel(h_ref, g_ref, wq_ref, wo_ref, k_ref, v_ref, o_ref):
    h = h_ref[...]
    r = lax.rsqrt(jnp.mean(h * h, axis=-1, keepdims=True) + EPS)
    q = (_dot((h * g_ref[...]).astype(BF16), wq_ref[...]) * r).astype(BF16)
    scale = XA_HEAD_DIM ** -0.5
    heads = [slice(hd * XA_HEAD_DIM, (hd + 1) * XA_HEAD_DIM) for hd in range(XA_HEADS)]
    scores = [_dot_nt(q[:, hs], k_ref[:, hs]) * scale for hs in heads]
    probs, invs = [], []
    for s in scores:
        p = jnp.exp(s - jnp.max(s, axis=-1, keepdims=True))
        probs.append(p.astype(BF16))
        invs.append(1.0 / jnp.sum(p, axis=-1, keepdims=True))
    out = h_ref[...]
    for hs, p, inv in zip(heads, probs, invs):
        att = (_dot(p, v_ref[:, hs]) * inv).astype(BF16)
        out = out + _dot(att, wo_ref[hs, :])
    o_ref[...] = out


def _xattn(h, gain, wq, wo, mkv, seq, mem_len):
    t, d = h.shape
    tiles_per_seq = seq // ROW_TM
    row = lambda i: (i, 0)
    return pl.pallas_call(
        _xattn_kernel,
        out_shape=jax.ShapeDtypeStruct((t, d), F32),
        grid=(t // ROW_TM,),
        in_specs=[
            pl.BlockSpec((ROW_TM, d), row),
            _resident((1, d)),
            _resident((d, d)),
            _resident((d, d)),
            pl.BlockSpec((mem_len, d), lambda i: (i // tiles_per_seq, 0)),
            pl.BlockSpec((mem_len, d), lambda i: (i // tiles_per_seq, 1)),
        ],
        out_specs=pl.BlockSpec((ROW_TM, d), row),
        compiler_params=_params(1),
        name="xattn",
    )(h, gain, wq, wo, mkv, mkv)


def kernel(x, mem, ffn1_norm, ffn1_w_gate, ffn1_w_up, ffn1_w_down, mix_norm, w_in, ret_gn_gain, swa_sinks,
           w_out, xa_norm, mem_norm, xa_wq, xa_wkv, xa_wo, ffn2_norm, ffn2_w_gate, ffn2_w_up, ffn2_w_down,
           final_norm):
    batch, seq, d = x.shape
    mem_len = mem.shape[1]
    depth = ffn1_norm.shape[0]
    h = x.reshape(batch * seq, d)
    mem2 = mem.reshape(batch * mem_len, d)
    row = lambda g: g.reshape(1, -1).astype(F32)
    final_gain = row(final_norm)

    for l in range(depth):
        last = l == depth - 1
        h, (w_mix,) = _ffn_f32(h, row(ffn1_norm[l]), ffn1_w_gate[l], ffn1_w_up[l], ffn1_w_down[l], final_gain,
                               final_norm=False, cast_weights=[w_in[l]])
        (dmat, xi_tile, zeta_tile), chunk_decay = _retention_tables()
        rq, rqx, rk, rkz, rv, rg, sq, sk4, sv4_t = _inproj(h, row(mix_norm[l]), w_mix, row(ret_gn_gain[l]),
                                                           xi_tile, zeta_tile, seq)
        ret, swa, (wo_mix, wq, wo) = _mixers(
            rq, rqx, rk, rkz, rv, rg, dmat, chunk_decay, sq, sk4, sv4_t, swa_sinks[l].astype(F32), batch, seq,
            [w_out[l], xa_wq[l], xa_wo[l]])
        h, mkv = _outproj(h, ret, swa, wo_mix, mem2, row(mem_norm[l]), xa_wkv[l])
        h = _xattn(h, row(xa_norm[l]), wq, wo, mkv, seq, mem_len)
        h, _ = _ffn_f32(h, row(ffn2_norm[l]), ffn2_w_gate[l], ffn2_w_up[l], ffn2_w_down[l], final_gain,
                        final_norm=last)
    if depth == 0:
        raise ValueError("depth must be at least 1")
    return h.reshape(batch, seq, d)
```
